```python
import math
import jax, jax.numpy as jnp
from jax import lax
import numpy as np

D_MODEL = 2048
BATCH = 2
SEQ = 16384
DEPTH = 1

N_MEM = 256
MIX_WIDTH = D_MODEL
MLSTM_WIDTH = MIX_WIDTH // 2
MLSTM_HEADS = 4
MLSTM_DV = MLSTM_WIDTH // MLSTM_HEADS
MLSTM_DQK = MLSTM_DV // 2
MLSTM_QK = MLSTM_HEADS * MLSTM_DQK
MLSTM_CHUNK = 64
CONV_WIDTH = 4
NSA_WIDTH = MIX_WIDTH - MLSTM_WIDTH
NSA_HEAD_DIM = 128
NSA_HEADS = NSA_WIDTH // NSA_HEAD_DIM
NSA_KV_HEADS = 2
NSA_GROUP = NSA_HEADS // NSA_KV_HEADS
NSA_KV = NSA_KV_HEADS * NSA_HEAD_DIM
CMP_BLOCK = 32
CMP_STRIDE = 16
CMP_HIDDEN = 2 * NSA_HEAD_DIM
SLC_BLOCK = 64
SLC_TOPN = 16
WINDOW = 512
Q_BLOCK = 128
XA_HEADS = 4
XA_HEAD_DIM = D_MODEL // XA_HEADS
N_EXPERTS = 64
TOP_K = 6
D_EXPERT = D_MODEL * 11 // 16
D_SHARED = 2 * D_EXPERT
ROUTED_SCALE = 2.446
EXPERT_BLOCK = 128
ROPE_THETA = 10000.0
LN_EPS = 1e-5
DN_ALPHA = (2.0 * DEPTH) ** 0.25
DN_BETA = (8.0 * DEPTH) ** -0.25

IN_SIZES = (MLSTM_QK, MLSTM_QK, MLSTM_WIDTH, MLSTM_WIDTH, MLSTM_HEADS, MLSTM_HEADS,
            NSA_WIDTH, NSA_KV, NSA_KV, NSA_KV, NSA_KV, NSA_KV, NSA_KV, 3 * NSA_HEADS)
IN_COLS = sum(IN_SIZES)
IN_SPLITS = tuple(int(s) for s in np.cumsum(IN_SIZES)[:-1])

kernel_name = "hybrid_mlstm_nsa_memxattn_moe"


def layer_norm(x, g, b):
    xf = x.astype(jnp.float32)
    mu = xf.mean(-1, keepdims=True)
    var = jnp.square(xf - mu).mean(-1, keepdims=True)
    return ((xf - mu) * lax.rsqrt(var + LN_EPS) * g + b).astype(x.dtype)


def rope_tables(positions, dim):
    half = dim // 2
    inv_freq = ROPE_THETA ** (-jnp.arange(half, dtype=jnp.float32) / half)
    ang = positions.astype(jnp.float32)[..., None] * inv_freq
    return jnp.cos(ang), jnp.sin(ang)


def apply_rope(x, cos, sin):
    half = x.shape[-1] // 2
    shape = (cos.shape[0],) + (1,) * (x.ndim - 3) + cos.shape[1:]
    c, s = cos.reshape(shape), sin.reshape(shape)
    xf = x.astype(jnp.float32)
    x1, x2 = xf[..., :half], xf[..., half:]
    return jnp.concatenate([x1 * c - x2 * s, x2 * c + x1 * s], -1).astype(x.dtype)


def causal_conv(x, w, b):
    C = x.shape[-1]
    y = lax.conv_general_dilated(x, w[:, None, :].astype(x.dtype), window_strides=(1,),
                                 padding=[(CONV_WIDTH - 1, 0)],
                                 dimension_numbers=('NWC', 'WIO', 'NWC'), feature_group_count=C)
    return y + b


def masked_softmax(s, mask):
    s = jnp.where(mask, s.astype(jnp.float32), -jnp.inf)
    m = jnp.max(s, axis=-1, keepdims=True)
    m = jnp.where(jnp.isfinite(m), m, 0.0)
    p = jnp.exp(s - m)
    return p / jnp.maximum(jnp.sum(p, axis=-1, keepdims=True), 1e-30)


def mlstm_chunkwise(q, k, v, i_pre, f_pre):
    B, H, S, dk = q.shape
    dv = v.shape[-1]
    L = MLSTM_CHUNK
    nc = S // L
    f32 = jnp.float32

    def chunks(t):
        t = t.astype(f32)
        return jnp.moveaxis(t.reshape((B, H, nc, L) + t.shape[3:]), 2, 0)

    qs, kss, vs = chunks(q), chunks(k.astype(f32) * dk ** -0.5), chunks(v)
    ic = chunks(i_pre)
    lf = chunks(jax.nn.log_sigmoid(f_pre.astype(f32)))
    causal = jnp.tril(jnp.ones((L, L), dtype=bool))

    def step(carry, inp):
        C, n, m = carry
        qc, kc, vc, igc, lfc = inp
        b = jnp.cumsum(lfc, axis=-1)
        dmat = jnp.where(causal, b[..., :, None] - b[..., None, :] + igc[..., None, :], -jnp.inf)
        inter = b + m[..., None]
        m_t = jnp.maximum(inter, dmat.max(-1))
        s = jnp.einsum('bhtd,bhsd->bhts', qc, kc) * jnp.exp(dmat - m_t[..., None])
        a_inter = jnp.exp(inter - m_t)
        num = jnp.einsum('bhts,bhsv->bhtv', s, vc) + a_inter[..., None] * jnp.einsum('bhtd,bhdv->bhtv', qc, C)
        den = s.sum(-1) + a_inter * jnp.einsum('bhtd,bhd->bht', qc, n)
        hc = num / jnp.maximum(jnp.abs(den), jnp.exp(-m_t))[..., None]
        g = b[..., -1:] - b + igc
        m_new = jnp.maximum(b[..., -1] + m, g.max(-1))
        wk = jnp.exp(g - m_new[..., None])
        decay = jnp.exp(b[..., -1] + m - m_new)
        C = decay[..., None, None] * C + jnp.einsum('bhs,bhsd,bhsv->bhdv', wk, kc, vc)
        n = decay[..., None] * n + jnp.einsum('bhs,bhsd->bhd', wk, kc)
        return (C, n, m_new), hc

    init = (jnp.zeros((B, H, dk, dv), f32), jnp.zeros((B, H, dk), f32), jnp.zeros((B, H), f32))
    _, hs = lax.scan(step, init, (qs, kss, vs, ic, lf))
    return jnp.moveaxis(hs, 0, 2).reshape(B, H, S, dv)


def compress_blocks(k, pos_emb, w1, w2):
    B, Hkv, S, dh = k.shape
    n_cmp = (S - CMP_BLOCK) // CMP_STRIDE + 1
    idx = jnp.arange(n_cmp)[:, None] * CMP_STRIDE + jnp.arange(CMP_BLOCK)[None, :]
    blocks = k[:, :, idx, :] + pos_emb
    flat = blocks.reshape(B, Hkv, n_cmp, CMP_BLOCK * dh)
    return jax.nn.gelu(flat @ w1) @ w2


def nsa_attention(q, k_c, v_c, k_s, v_s, k_w, v_w, gates, cmp_pos, w1k, w2k, w1v, w2v):
    B, Hkv, G, S, dh = q.shape
    scale = dh ** -0.5
    kc = compress_blocks(k_c, cmp_pos, w1k, w2k)
    vc = compress_blocks(v_c, cmp_pos, w1v, w2v)
    n_cmp = kc.shape[2]
    cmp_end = jnp.arange(n_cmp) * CMP_STRIDE + CMP_BLOCK - 1
    n_slc = S // SLC_BLOCK
    topn = min(SLC_TOPN, n_slc)
    c_lo = jnp.arange(n_cmp)[:, None] * CMP_STRIDE
    j_lo = jnp.arange(n_slc)[None, :] * SLC_BLOCK
    overlap = ((c_lo <= j_lo + SLC_BLOCK - 1) & (c_lo + CMP_BLOCK - 1 >= j_lo)).astype(jnp.float32)
    ks_blk = k_s.reshape(B, Hkv, n_slc, SLC_BLOCK, dh)
    vs_blk = v_s.reshape(B, Hkv, n_slc, SLC_BLOCK, dh)
    pad = ((0, 0), (0, 0), (WINDOW, 0), (0, 0))
    kw_pad, vw_pad = jnp.pad(k_w, pad), jnp.pad(v_w, pad)
    b_ix = jnp.arange(B)[:, None, None, None]
    g_ix = jnp.arange(Hkv)[None, :, None, None]
    blk = jnp.arange(n_slc)

    def one_block(i):
        q0 = i * Q_BLOCK
        t = q0 + jnp.arange(Q_BLOCK)
        qb = lax.dynamic_slice_in_dim(q, q0, Q_BLOCK, axis=3)
        gb = lax.dynamic_slice_in_dim(gates, q0, Q_BLOCK, axis=3)
        s_c = jnp.einsum('bghqd,bgcd->bghqc', qb, kc) * scale
        p_c = masked_softmax(s_c, cmp_end[None, :] <= t[:, None])
        o_c = jnp.einsum('bghqc,bgcd->bghqd', p_c, vc)
        imp = jnp.einsum('bghqc,cj->bgqj', p_c, overlap)
        cur = t // SLC_BLOCK
        causal_blk = blk[None, :] <= cur[:, None]
        forced = (blk[None, :] == 0) | (blk[None, :] == cur[:, None]) | (blk[None, :] == cur[:, None] - 1)
        score = jnp.where(causal_blk & forced, jnp.inf, jnp.where(causal_blk, imp, -jnp.inf))
        _, sel = lax.top_k(score, topn)
        k_sel = ks_blk[b_ix, g_ix, sel].reshape(B, Hkv, Q_BLOCK, topn * SLC_BLOCK, dh)
        v_sel = vs_blk[b_ix, g_ix, sel].reshape(B, Hkv, Q_BLOCK, topn * SLC_BLOCK, dh)
        kpos = (sel[..., None] * SLC_BLOCK + jnp.arange(SLC_BLOCK)).reshape(B, Hkv, Q_BLOCK, topn * SLC_BLOCK)
        s_s = jnp.einsum('bghqd,bgqkd->bghqk', qb, k_sel) * scale
        p_s = masked_softmax(s_s, (kpos <= t[:, None])[:, :, None])
        o_s = jnp.einsum('bghqk,bgqkd->bghqd', p_s, v_sel)
        k_win = lax.dynamic_slice_in_dim(kw_pad, q0, Q_BLOCK + WINDOW, axis=2)
        v_win = lax.dynamic_slice_in_dim(vw_pad, q0, Q_BLOCK + WINDOW, axis=2)
        wpos = q0 - WINDOW + jnp.arange(Q_BLOCK + WINDOW)
        wmask = (wpos[None, :] <= t[:, None]) & (wpos[None, :] > t[:, None] - WINDOW) & (wpos[None, :] >= 0)
        s_w = jnp.einsum('bghqd,bgkd->bghqk', qb, k_win) * scale
        p_w = masked_softmax(s_w, wmask)
        o_w = jnp.einsum('bghqk,bgkd->bghqd', p_w, v_win)
        return gb[..., 0:1] * o_c + gb[..., 1:2] * o_s + gb[..., 2:3] * o_w

    out = lax.map(one_block, jnp.arange(S // Q_BLOCK))
    return jnp.transpose(out, (1, 0, 4, 2, 3, 5)).reshape(B, S, Hkv * G * dh)


def hybrid_mixer(h, cos, sin, w_in, conv_w, conv_b, igate_b, fgate_b, mlstm_norm_g,
                 cmp_pos, cmp_w1k, cmp_w2k, cmp_w1v, cmp_w2v, nsa_gate_b, w_out):
    B, S, _ = h.shape
    u = h @ w_in
    (q_m, k_m, v_m, o_m, i_m, f_m, q_n, kc, vc, ks, vs, kw, vw, g_n) = jnp.split(u, IN_SPLITS, axis=-1)
    qk = jax.nn.silu(causal_conv(jnp.concatenate([q_m, k_m], -1), conv_w, conv_b))
    q_m, k_m = jnp.split(qk, 2, axis=-1)
    heads = lambda t, d: t.reshape(B, S, -1, d).transpose(0, 2, 1, 3)
    hm = mlstm_chunkwise(heads(q_m, MLSTM_DQK), heads(k_m, MLSTM_DQK), heads(v_m, MLSTM_DV),
                         (i_m + igate_b).transpose(0, 2, 1), (f_m + fgate_b).transpose(0, 2, 1))
    mu = hm.mean(-1, keepdims=True)
    var = jnp.square(hm - mu).mean(-1, keepdims=True)
    hm = (hm - mu) * lax.rsqrt(var + LN_EPS) * mlstm_norm_g.reshape(MLSTM_HEADS, 1, MLSTM_DV)
    hm = hm.transpose(0, 2, 1, 3).reshape(B, S, MLSTM_WIDTH)
    hm = (jax.nn.sigmoid(o_m.astype(jnp.float32)) * hm).astype(h.dtype)
    qn = q_n.reshape(B, S, NSA_KV_HEADS, NSA_GROUP, NSA_HEAD_DIM).transpose(0, 2, 3, 1, 4)
    qn = apply_rope(qn, cos, sin)
    kvh = lambda t: t.reshape(B, S, NSA_KV_HEADS, NSA_HEAD_DIM).transpose(0, 2, 1, 3)
    gates = jax.nn.sigmoid((g_n + nsa_gate_b).astype(jnp.float32))
    gates = gates.reshape(B, S, NSA_KV_HEADS, NSA_GROUP, 3).transpose(0, 2, 3, 1, 4)
    hn = nsa_attention(qn, apply_rope(kvh(kc), cos, sin), kvh(vc), apply_rope(kvh(ks), cos, sin), kvh(vs),
                       apply_rope(kvh(kw), cos, sin), kvh(vw), gates,
                       cmp_pos, cmp_w1k, cmp_w2k, cmp_w1v, cmp_w2v).astype(h.dtype)
    return jnp.concatenate([hm, hn], -1) @ w_out


def memory_cross_attention(h, mem, wq, wk, wv, wo):
    B, S, D = h.shape
    M = mem.shape[1]
    q = (h @ wq).reshape(B, S, XA_HEADS, XA_HEAD_DIM)
    k = (mem @ wk).reshape(B, M, XA_HEADS, XA_HEAD_DIM)
    v = (mem @ wv).reshape(B, M, XA_HEADS, XA_HEAD_DIM)
    s = jnp.einsum('bshd,bmhd->bhsm', q, k).astype(jnp.float32) * XA_HEAD_DIM ** -0.5
    p = jax.nn.softmax(s, axis=-1)
    o = jnp.einsum('bhsm,bmhd->bshd', p, v.astype(jnp.float32)).astype(h.dtype)
    return o.reshape(B, S, D) @ wo


def swiglu(x, wg, wu, wd):
    return (jax.nn.silu(x @ wg) * (x @ wu)) @ wd


def routed_experts(hf, top_idx, top_w, w_gate, w_up, w_down):
    T, D = hf.shape
    A = T * TOP_K
    flat_e = top_idx.reshape(A).astype(jnp.int32)
    flat_w = top_w.reshape(A)
    flat_tok = jnp.arange(A, dtype=jnp.int32) // TOP_K
    order = jnp.argsort(flat_e)
    e_sorted, tok_sorted, w_sorted = flat_e[order], flat_tok[order], flat_w[order]
    counts = jnp.zeros((N_EXPERTS,), jnp.int32).at[flat_e].add(1)
    padded = (counts + EXPERT_BLOCK - 1) // EXPERT_BLOCK * EXPERT_BLOCK
    pad_end = jnp.cumsum(padded)
    pad_start = pad_end - padded
    grp_start = jnp.cumsum(counts) - counts
    dest = pad_start[e_sorted] + jnp.arange(A, dtype=jnp.int32) - grp_start[e_sorted]
    n_blocks = -(-A // EXPERT_BLOCK) + N_EXPERTS
    P = n_blocks * EXPERT_BLOCK
    row_tok = jnp.full((P,), T, jnp.int32).at[dest].set(tok_sorted)
    row_w = jnp.zeros((P,), hf.dtype).at[dest].set(w_sorted)
    block_e = jnp.minimum(jnp.searchsorted(pad_end, jnp.arange(n_blocks, dtype=jnp.int32) * EXPERT_BLOCK,
                                           side='right'), N_EXPERTS - 1)
    h_pad = jnp.concatenate([hf, jnp.zeros((1, D), hf.dtype)], 0)

    def body(i, y):
        toks = lax.dynamic_slice_in_dim(row_tok, i * EXPERT_BLOCK, EXPERT_BLOCK)
        ws = lax.dynamic_slice_in_dim(row_w, i * EXPERT_BLOCK, EXPERT_BLOCK)
        e = block_e[i]
        out = swiglu(h_pad[toks], w_gate[e], w_up[e], w_down[e])
        return y.at[toks].add(out * ws[:, None])

    y = lax.fori_loop(0, n_blocks, body, jnp.zeros((T + 1, D), hf.dtype))
    return y[:T]


def moe_ffn(h, router_w, router_bias, w_gate, w_up, w_down, sh_w_gate, sh_w_up, sh_w_down):
    B, S, D = h.shape
    hf = h.reshape(B * S, D)
    scores = jax.nn.sigmoid((hf @ router_w).astype(jnp.float32))
    _, top_idx = lax.top_k(scores + router_bias.astype(jnp.float32), TOP_K)
    top_s = jnp.take_along_axis(scores, top_idx, axis=-1)
    top_w = (top_s / jnp.sum(top_s, -1, keepdims=True) * ROUTED_SCALE).astype(h.dtype)
    routed = routed_experts(hf, top_idx, top_w, w_gate, w_up, w_down)
    shared = swiglu(hf, sh_w_gate, sh_w_up, sh_w_down)
    return (routed + shared).reshape(B, S, D)


def setup_inputs(seed: int = 0) -> dict:
    key = jax.random.key(seed)
    ks = iter(jax.random.split(key, 40))
    f32 = jnp.float32
    nrm = lambda shape, scale: jax.random.normal(next(ks), shape, f32) * scale
    gain = lambda shape: 1.0 + nrm(shape, 0.02)
    L = DEPTH
    x = nrm((BATCH, SEQ, D_MODEL), 1.0)
    mem = nrm((BATCH, N_MEM, D_MODEL), 1.0)
    positions = (jax.random.randint(next(ks), (BATCH, 1), 0, 4096, dtype=jnp.int32)
                 + jnp.arange(SEQ, dtype=jnp.int32)[None, :])
    return {
        "x": x,
        "mem": mem,
        "positions": positions,
        "ln0_g": gain((D_MODEL,)),
        "ln0_b": nrm((D_MODEL,), 0.02),
        "w_in": nrm((L, D_MODEL, IN_COLS), D_MODEL ** -0.5),
        "conv_w": nrm((L, CONV_WIDTH, 2 * MLSTM_QK), CONV_WIDTH ** -0.5),
        "conv_b": nrm((L, 2 * MLSTM_QK), 0.02),
        "igate_b": nrm((L, MLSTM_HEADS), 0.1),
        "fgate_b": jnp.linspace(3.0, 6.0, MLSTM_HEADS, dtype=f32)[None, :] + nrm((L, MLSTM_HEADS), 0.1),
        "mlstm_norm_g": gain((L, MLSTM_WIDTH)),
        "cmp_pos": nrm((L, CMP_BLOCK, NSA_HEAD_DIM), 0.02),
        "cmp_w1k": nrm((L, CMP_BLOCK * NSA_HEAD_DIM, CMP_HIDDEN), (CMP_BLOCK * NSA_HEAD_DIM) ** -0.5),
        "cmp_w2k": nrm((L, CMP_HIDDEN, NSA_HEAD_DIM), CMP_HIDDEN ** -0.5),
        "cmp_w1v": nrm((L, CMP_BLOCK * NSA_HEAD_DIM, CMP_HIDDEN), (CMP_BLOCK * NSA_HEAD_DIM) ** -0.5),
        "cmp_w2v": nrm((L, CMP_HIDDEN, NSA_HEAD_DIM), CMP_HIDDEN ** -0.5),
        "nsa_gate_b": nrm((L, 3 * NSA_HEADS), 0.1),
        "w_out": nrm((L, MIX_WIDTH, D_MODEL), MIX_WIDTH ** -0.5 * DN_BETA),
        "ln1_g": gain((L, D_MODEL)),
        "ln1_b": nrm((L, D_MODEL), 0.02),
        "xa_wq": nrm((L, D_MODEL, D_MODEL), D_MODEL ** -0.5),
        "xa_wk": nrm((L, D_MODEL, D_MODEL), D_MODEL ** -0.5),
        "xa_wv": nrm((L, D_MODEL, D_MODEL), D_MODEL ** -0.5),
        "xa_wo": nrm((L, D_MODEL, D_MODEL), D_MODEL ** -0.5 * DN_BETA),
        "ln2_g": gain((L, D_MODEL)),
        "ln2_b": nrm((L, D_MODEL), 0.02),
        "router_w": nrm((L, D_MODEL, N_EXPERTS), D_MODEL ** -0.5),
        "router_bias": nrm((L, N_EXPERTS), 0.01),
        "moe_w_gate": nrm((L, N_EXPERTS, D_MODEL, D_EXPERT), D_MODEL ** -0.5),
        "moe_w_up": nrm((L, N_EXPERTS, D_MODEL, D_EXPERT), D_MODEL ** -0.5),
        "moe_w_down": nrm((L, N_EXPERTS, D_EXPERT, D_MODEL), D_EXPERT ** -0.5 * DN_BETA),
        "sh_w_gate": nrm((L, D_MODEL, D_SHARED), D_MODEL ** -0.5),
        "sh_w_up": nrm((L, D_MODEL, D_SHARED), D_MODEL ** -0.5),
        "sh_w_down": nrm((L, D_SHARED, D_MODEL), D_SHARED ** -0.5 * DN_BETA),
        "ln3_g": gain((L, D_MODEL)),
        "ln3_b": nrm((L, D_MODEL), 0.02),
    }


def reference(x, mem, positions, ln0_g, ln0_b, w_in, conv_w, conv_b, igate_b, fgate_b, mlstm_norm_g,
              cmp_pos, cmp_w1k, cmp_w2k, cmp_w1v, cmp_w2v, nsa_gate_b, w_out, ln1_g, ln1_b,
              xa_wq, xa_wk, xa_wv, xa_wo, ln2_g, ln2_b, router_w, router_bias,
              moe_w_gate, moe_w_up, moe_w_down, sh_w_gate, sh_w_up, sh_w_down, ln3_g, ln3_b):
    cos, sin = rope_tables(positions, NSA_HEAD_DIM)
    h = layer_norm(x, ln0_g, ln0_b)
    for l in range(DEPTH):
        mix = hybrid_mixer(h, cos, sin, w_in[l], conv_w[l], conv_b[l], igate_b[l], fgate_b[l], mlstm_norm_g[l],
                           cmp_pos[l], cmp_w1k[l], cmp_w2k[l], cmp_w1v[l], cmp_w2v[l], nsa_gate_b[l], w_out[l])
        h = layer_norm(DN_ALPHA * h + mix, ln1_g[l], ln1_b[l])
        xa = memory_cross_attention(h, mem, xa_wq[l], xa_wk[l], xa_wv[l], xa_wo[l])
        h = layer_norm(DN_ALPHA * h + xa, ln2_g[l], ln2_b[l])
        ff = moe_ffn(h, router_w[l], router_bias[l], moe_w_gate[l], moe_w_up[l], moe_w_down[l],
                     sh_w_gate[l], sh_w_up[l], sh_w_down[l])
        h = layer_norm(DN_ALPHA * h + ff, ln3_g[l], ln3_b[l])
    return h
```

```python
import functools

import numpy as np
import jax
import jax.numpy as jnp
from jax import lax
from jax.experimental import pallas as pl
from jax.experimental.pallas import tpu as pltpu

F32 = jnp.float32
BF16 = jnp.bfloat16
I32 = jnp.int32

D_MODEL = 2048
MLSTM_HEADS = 4
MLSTM_DV = 256
MLSTM_DQK = 128
MLSTM_QK = MLSTM_HEADS * MLSTM_DQK
MLSTM_WIDTH = MLSTM_HEADS * MLSTM_DV
CONV_WIDTH = 4
NSA_HEAD_DIM = 128
NSA_HEADS = 8
NSA_KV_HEADS = 2
NSA_GROUP = 4
NSA_WIDTH = NSA_HEADS * NSA_HEAD_DIM
CMP_BLOCK = 32
CMP_STRIDE = 16
CMP_HIDDEN = 256
SLC_BLOCK = 64
SLC_TOPN = 16
WINDOW = 512
Q_BLOCK = 128
XA_HEADS = 4
XA_HEAD_DIM = 512
N_EXPERTS = 64
TOP_K = 6
D_EXPERT = 1408
D_SHARED = 2816
ROUTED_SCALE = 2.446
ROPE_THETA = 10000.0
LN_EPS = 1e-5
DEPTH = 1
DN_ALPHA = (2.0 * DEPTH) ** 0.25

U_MLSTM = 0
U_NSA = 3072
U_GATE = 5632
U_COLS = 5760

LANE = 128
NEG = -1e30
MLSTM_L = 256
VMEM_LIMIT = 56 * 1024 * 1024
EXPERT_BM = 256


def _cparams(sem):
    return pltpu.CompilerParams(dimension_semantics=sem, vmem_limit_bytes=VMEM_LIMIT)


def _dot(a, b):
    return jnp.dot(a, b, preferred_element_type=F32)


def _dot_nt(a, b):
    return lax.dot_general(a, b, (((1,), (1,)), ((), ())), preferred_element_type=F32)


def _layer_norm(z, g, b):
    mu = jnp.mean(z, -1, keepdims=True)
    zc = z - mu
    var = jnp.mean(zc * zc, -1, keepdims=True)
    return zc * lax.rsqrt(var + LN_EPS) * g + b


def _split3(x):
    hi = x.astype(BF16)
    r = x - hi.astype(F32)
    mid = r.astype(BF16)
    lo = (r - mid.astype(F32)).astype(BF16)
    return hi, mid, lo


def _split2(x):
    hi = x.astype(BF16)
    lo = (x - hi.astype(F32)).astype(BF16)
    return hi, lo


def _ln_inproj_kernel(x_ref, g_ref, b_ref, w_ref, h_ref, u_ref, hb_ref):
    @pl.when(pl.program_id(1) == 0)
    def _():
        hn = _layer_norm(x_ref[...], g_ref[...], b_ref[...])
        h_ref[...] = hn
        hb_ref[...] = hn.astype(BF16)

    u_ref[...] = _dot(hb_ref[...], w_ref[...])


def _ln_inproj(x2, g, b, w):
    T = x2.shape[0]
    tm, tn = 512, 640
    return pl.pallas_call(
        _ln_inproj_kernel,
        grid=(T // tm, U_COLS // tn),
        in_specs=[pl.BlockSpec((tm, D_MODEL), lambda i, j: (i, 0)),
                  pl.BlockSpec((1, D_MODEL), lambda i, j: (0, 0)),
                  pl.BlockSpec((1, D_MODEL), lambda i, j: (0, 0)),
                  pl.BlockSpec((D_MODEL, tn), lambda i, j: (0, j))],
        out_specs=[pl.BlockSpec((tm, D_MODEL), lambda i, j: (i, 0)),
                   pl.BlockSpec((tm, tn), lambda i, j: (i, j))],
        out_shape=[jax.ShapeDtypeStruct((T, D_MODEL), F32),
                   jax.ShapeDtypeStruct((T, U_COLS), F32)],
        scratch_shapes=[pltpu.VMEM((tm, D_MODEL), BF16)],
        compiler_params=_cparams(("parallel", "arbitrary")),
        name="ln_inproj",
    )(x2, g, b, w)


def _log_sigmoid(x):
    return jnp.minimum(x, 0.0) - jnp.log1p(jnp.exp(-jnp.abs(x)))


def _mlstm_kernel(qk_ref, v_ref, o_ref, gt_ref, cw_ref, cb_ref, gb_ref, ng_ref, out_ref,
                  prev_ref, c_ref, n_ref, m_ref):
    L = MLSTM_L

    @pl.when(pl.program_id(1) == 0)
    def _():
        prev_ref[...] = jnp.zeros_like(prev_ref)
        c_ref[...] = jnp.zeros_like(c_ref)
        n_ref[...] = jnp.zeros_like(n_ref)
        m_ref[...] = jnp.zeros_like(m_ref)

    x = qk_ref[...]
    prev = prev_ref[...]
    row = lax.broadcasted_iota(I32, (L, 1), 0)
    cw = cw_ref[...]
    y = cb_ref[...] + cw[CONV_WIDTH - 1:CONV_WIDTH, :] * x
    for j in range(1, CONV_WIDTH):
        shifted = jnp.where(row < j, pltpu.roll(prev, j, 0), pltpu.roll(x, j, 0))
        y = y + cw[CONV_WIDTH - 1 - j:CONV_WIDTH - j, :] * shifted
    prev_ref[...] = x
    qk = y * jax.nn.sigmoid(y)

    gpre = gt_ref[...] + gb_ref[...]
    gpre_t = gpre.T
    r_i = lax.broadcasted_iota(I32, (L, L), 0)
    c_i = lax.broadcasted_iota(I32, (L, L), 1)
    causal = r_i >= c_i
    tril = jnp.where(causal, 1.0, 0.0).astype(BF16)
    triu = jnp.where(r_i <= c_i, 1.0, 0.0).astype(BF16)
    lf = _log_sigmoid(gpre)
    lf_t = _log_sigmoid(gpre_t)
    b_cols = sum(_dot(tril, part) for part in _split3(lf))
    b_rows = sum(_dot(part, triu) for part in _split3(lf_t))

    for h in range(MLSTM_HEADS):
        i_col = gpre[:, h:h + 1]
        i_row = gpre_t[h:h + 1, :]
        b_col = b_cols[:, MLSTM_HEADS + h:MLSTM_HEADS + h + 1]
        b_row = b_rows[MLSTM_HEADS + h:MLSTM_HEADS + h + 1, :]
        m_prev = m_ref[h:h + 1, 0:1]
        n_prev = n_ref[h:h + 1, :]
        c_prev = c_ref[h]

        q = qk[:, h * MLSTM_DQK:(h + 1) * MLSTM_DQK]
        k = qk[:, MLSTM_QK + h * MLSTM_DQK:MLSTM_QK + (h + 1) * MLSTM_DQK] * (MLSTM_DQK ** -0.5)
        v = v_ref[:, h * MLSTM_DV:(h + 1) * MLSTM_DV].astype(BF16)
        qb = q.astype(BF16)

        dmat = jnp.where(causal, b_col - b_row + i_row, -jnp.inf)
        inter = b_col + m_prev
        m_t = jnp.maximum(inter, jnp.max(dmat, -1, keepdims=True))
        s = _dot_nt(qb, k.astype(BF16)) * jnp.exp(dmat - m_t)
        a_inter = jnp.exp(inter - m_t)
        num = _dot(s.astype(BF16), v) + a_inter * _dot(qb, c_prev.astype(BF16))
        den = jnp.sum(s, -1, keepdims=True) + a_inter * jnp.sum(q * n_prev, -1, keepdims=True)
        hc = num / jnp.maximum(jnp.abs(den), jnp.exp(-m_t))

        mu = jnp.mean(hc, -1, keepdims=True)
        hcc = hc - mu
        var = jnp.mean(hcc * hcc, -1, keepdims=True)
        hn = hcc * lax.rsqrt(var + LN_EPS) * ng_ref[:, h * MLSTM_DV:(h + 1) * MLSTM_DV]
        og = jax.nn.sigmoid(o_ref[:, h * MLSTM_DV:(h + 1) * MLSTM_DV])
        out_ref[:, h * MLSTM_DV:(h + 1) * MLSTM_DV] = (og * hn).astype(out_ref.dtype)

        b_last = b_col[L - 1:L, :]
        g_col = b_last - b_col + i_col
        m_new = jnp.maximum(b_last + m_prev, jnp.max(g_col, 0, keepdims=True))
        kw = k * jnp.exp(g_col - m_new)
        decay = jnp.exp(b_last + m_prev - m_new)
        c_ref[h] = decay * c_prev + _dot(kw.T.astype(BF16), v)
        n_ref[h:h + 1, :] = decay * n_prev + jnp.sum(kw, 0, keepdims=True)
        m_ref[h:h + 1, :] = jnp.broadcast_to(m_new, (1, LANE))


def _mlstm(u, conv_w, conv_b, gate_b, norm_g, B, S):
    T = B * S
    L = MLSTM_L
    nc = S // L
    row = lambda b, c: b * nc + c
    return pl.pallas_call(
        _mlstm_kernel,
        grid=(B, nc),
        in_specs=[pl.BlockSpec((L, 2 * MLSTM_QK), lambda b, c: (row(b, c), 0)),
                  pl.BlockSpec((L, MLSTM_WIDTH), lambda b, c: (row(b, c), 1)),
                  pl.BlockSpec((L, MLSTM_WIDTH), lambda b, c: (row(b, c), 2)),
                  pl.BlockSpec((L, LANE), lambda b, c: (row(b, c), U_GATE // LANE)),
                  pl.BlockSpec((CONV_WIDTH, 2 * MLSTM_QK), lambda b, c: (0, 0)),
                  pl.BlockSpec((1, 2 * MLSTM_QK), lambda b, c: (0, 0)),
                  pl.BlockSpec((1, LANE), lambda b, c: (0, 0)),
                  pl.BlockSpec((1, MLSTM_WIDTH), lambda b, c: (0, 0))],
        out_specs=pl.BlockSpec((L, MLSTM_WIDTH), lambda b, c: (row(b, c), 0)),
        out_shape=jax.ShapeDtypeStruct((T, MLSTM_WIDTH), BF16),
        scratch_shapes=[pltpu.VMEM((L, 2 * MLSTM_QK), F32),
                        pltpu.VMEM((MLSTM_HEADS, MLSTM_DQK, MLSTM_DV), F32),
                        pltpu.VMEM((8, MLSTM_DQK), F32),
                        pltpu.VMEM((8, LANE), F32)],
        compiler_params=_cparams(("parallel", "arbitrary")),
        name="mlstm",
    )(u, u, u, u, conv_w, conv_b, gate_b, norm_g)


def _nsa_prep_kernel(q_ref, kv0_ref, kv1_ref, kv2_ref, pos_ref, invf_ref, sgn_ref,
                     qo_ref, kc_ref, vc_ref, ks_ref, vs_ref, kw_ref, vw_ref):
    ang = pos_ref[...].astype(F32) * invf_ref[...]
    cos = jnp.cos(ang)
    sin = jnp.sin(ang) * sgn_ref[...]

    def rope(x):
        return x * cos + pltpu.roll(x, NSA_HEAD_DIM // 2, 1) * sin

    scale = NSA_HEAD_DIM ** -0.5
    for g in range(NSA_KV_HEADS):
        for h in range(NSA_GROUP):
            c0 = (g * NSA_GROUP + h) * NSA_HEAD_DIM
            qo_ref[g, h] = (rope(q_ref[:, c0:c0 + NSA_HEAD_DIM]) * scale).astype(BF16)
        c0 = g * NSA_HEAD_DIM
        c1 = NSA_KV_HEADS * NSA_HEAD_DIM + g * NSA_HEAD_DIM
        for kv_ref, ko_ref, vo_ref in ((kv0_ref, kc_ref, vc_ref), (kv1_ref, ks_ref, vs_ref),
                                       (kv2_ref, kw_ref, vw_ref)):
            ko_ref[g] = rope(kv_ref[:, c0:c0 + NSA_HEAD_DIM]).astype(BF16)
            vo_ref[g] = kv_ref[:, c1:c1 + NSA_HEAD_DIM].astype(BF16)


def _nsa_prep(u, pos_col, invf2, sgn, T):
    ts = 512
    kvw = 2 * NSA_KV_HEADS * NSA_HEAD_DIM
    kv_spec = lambda n: pl.BlockSpec((ts, kvw), lambda i: (i, (U_NSA + NSA_WIDTH) // kvw + n))
    kv_out = pl.BlockSpec((NSA_KV_HEADS, ts, NSA_HEAD_DIM), lambda i: (0, i, 0))
    kv_shape = jax.ShapeDtypeStruct((NSA_KV_HEADS, T, NSA_HEAD_DIM), BF16)
    return pl.pallas_call(
        _nsa_prep_kernel,
        grid=(T // ts,),
        in_specs=[pl.BlockSpec((ts, NSA_WIDTH), lambda i: (i, U_NSA // NSA_WIDTH)),
                  kv_spec(0), kv_spec(1), kv_spec(2),
                  pl.BlockSpec((ts, 1), lambda i: (i, 0)),
                  pl.BlockSpec((1, LANE), lambda i: (0, 0)),
                  pl.BlockSpec((1, LANE), lambda i: (0, 0))],
        out_specs=[pl.BlockSpec((NSA_KV_HEADS, NSA_GROUP, ts, NSA_HEAD_DIM), lambda i: (0, 0, i, 0))]
        + [kv_out] * 6,
        out_shape=[jax.ShapeDtypeStruct((NSA_KV_HEADS, NSA_GROUP, T, NSA_HEAD_DIM), BF16)]
        + [kv_shape] * 6,
        compiler_params=_cparams(("parallel",)),
        name="nsa_prep",
    )(u, u, u, u, pos_col, invf2, sgn)


def _gelu_tanh(x):
    return 0.5 * x * (1.0 + jnp.tanh(0.7978845608028654 * (x + 0.044715 * x * x * x)))


def _compress_kernel(r_ref, w1_ref, pos_ref, w2_ref, out_ref):
    r = r_ref[0]
    nr = r.shape[0]
    a = _dot(r, w1_ref[0])
    b = _dot(r, w1_ref[1])
    c0 = _dot(pos_ref[0], w1_ref[0]) + _dot(pos_ref[1], w1_ref[1])
    pre = a + pltpu.roll(b, nr - 1, 0) + c0[0:1, :]
    out = _dot(_gelu_tanh(pre).astype(BF16), w2_ref[...])
    row = lax.broadcasted_iota(I32, (nr, 1), 0)
    out_ref[0] = jnp.where(row < nr - 1, out, 0.0).astype(out_ref.dtype)


def _compress(kv, w1, pos8, w2, B, S):
    nr = S // CMP_STRIDE
    half = CMP_STRIDE * NSA_HEAD_DIM
    r = kv.reshape(NSA_KV_HEADS, B * nr, half)
    return pl.pallas_call(
        _compress_kernel,
        grid=(NSA_KV_HEADS, B),
        in_specs=[pl.BlockSpec((1, nr, half), lambda g, b: (g, b, 0)),
                  pl.BlockSpec((2, half, CMP_HIDDEN), lambda g, b: (0, 0, 0)),
                  pl.BlockSpec((2, 8, half), lambda g, b: (0, 0, 0)),
                  pl.BlockSpec((CMP_HIDDEN, NSA_HEAD_DIM), lambda g, b: (0, 0))],
        out_specs=pl.BlockSpec((1, nr, NSA_HEAD_DIM), lambda g, b: (g, b, 0)),
        out_shape=jax.ShapeDtypeStruct((NSA_KV_HEADS, B * nr, NSA_HEAD_DIM), BF16),
        compiler_params=_cparams(("parallel", "parallel")),
        name="nsa_compress",
    )(r, w1, pos8, w2)


def _nsa_cmp_kernel(q_ref, kc_ref, vc_ref, ov_ref, oc_ref, sel_ref, *, n_slc, topn):
    i = pl.program_id(2)
    nc = kc_ref.shape[1]
    q = q_ref[0].reshape(NSA_GROUP * Q_BLOCK, NSA_HEAD_DIM)
    t = i * Q_BLOCK + lax.broadcasted_iota(I32, (Q_BLOCK, 1), 0)
    c_idx = lax.broadcasted_iota(I32, (1, nc), 1)
    valid = (c_idx * CMP_STRIDE + CMP_BLOCK - 1 <= t) & (c_idx < nc - 1)
    bias = jnp.where(valid, 0.0, NEG)
    s = _dot_nt(q, kc_ref[0]) + jnp.concatenate([bias] * NSA_GROUP, 0)
    m = jnp.max(s, -1, keepdims=True)
    p = jnp.where(m > 0.5 * NEG, jnp.exp(s - m), 0.0)
    p = p / jnp.maximum(jnp.sum(p, -1, keepdims=True), 1e-30)
    o = _dot(p.astype(BF16), vc_ref[0])
    psum = p[0:Q_BLOCK]
    for h in range(NSA_GROUP):
        oc_ref[:, h * NSA_HEAD_DIM:(h + 1) * NSA_HEAD_DIM] = (
            o[h * Q_BLOCK:(h + 1) * Q_BLOCK].astype(oc_ref.dtype))
        if h:
            psum = psum + p[h * Q_BLOCK:(h + 1) * Q_BLOCK]
    imp = sum(_dot(part, ov_ref[...]) for part in _split2(psum))

    lane = lax.broadcasted_iota(I32, (Q_BLOCK, n_slc), 1)
    lane_f = lane.astype(F32)
    cur = t // SLC_BLOCK
    causal_blk = lane <= cur
    forced = (lane == 0) | (lane == cur) | (lane == cur - 1)
    score = jnp.where(causal_blk, jnp.where(forced, jnp.inf, imp), -jnp.inf)
    sel = jnp.zeros((Q_BLOCK, n_slc), F32)
    for _ in range(topn):
        mx = jnp.max(score, -1, keepdims=True)
        first = jnp.min(jnp.where(score == mx, lane_f, float(n_slc)), -1, keepdims=True)
        hit = lane_f == first
        sel = jnp.where(hit, 1.0, sel)
        score = jnp.where(hit, -jnp.inf, score)
    sel_ref[0] = jnp.where(causal_blk, sel, 0.0).astype(sel_ref.dtype)


def _nsa_cmp(qn, kcc, vcc, ov, B, S):
    T = B * S
    nq = S // Q_BLOCK
    nc = S // CMP_STRIDE
    n_slc = S // SLC_BLOCK
    kern = functools.partial(_nsa_cmp_kernel, n_slc=n_slc, topn=min(SLC_TOPN, n_slc))
    return pl.pallas_call(
        kern,
        grid=(NSA_KV_HEADS, B, nq),
        in_specs=[pl.BlockSpec((1, NSA_GROUP, Q_BLOCK, NSA_HEAD_DIM), lambda g, b, i: (g, 0, b * nq + i, 0)),
                  pl.BlockSpec((1, nc, NSA_HEAD_DIM), lambda g, b, i: (g, b, 0)),
                  pl.BlockSpec((1, nc, NSA_HEAD_DIM), lambda g, b, i: (g, b, 0)),
                  pl.BlockSpec((nc, n_slc), lambda g, b, i: (0, 0))],
        out_specs=[pl.BlockSpec((Q_BLOCK, NSA_GROUP * NSA_HEAD_DIM), lambda g, b, i: (b * nq + i, g)),
                   pl.BlockSpec((1, Q_BLOCK, n_slc), lambda g, b, i: (g, b * nq + i, 0))],
        out_shape=[jax.ShapeDtypeStruct((T, NSA_WIDTH), BF16),
                   jax.ShapeDtypeStruct((NSA_KV_HEADS, T, n_slc), BF16)],
        compiler_params=_cparams(("parallel", "parallel", "arbitrary")),
        name="nsa_cmp_select",
    )(qn, kcc, vcc, ov)


SLC_CHUNK = 256


def _nsa_slc_kernel(q_ref, sel_ref, ks_ref, vs_ref, os_ref, m_ref, l_ref, acc_ref, *, n_slc):
    i = pl.program_id(2)
    rows = NSA_GROUP * Q_BLOCK
    q = q_ref[0].reshape(rows, NSA_HEAD_DIM)
    sel = sel_ref[0]
    t = i * Q_BLOCK + lax.broadcasted_iota(I32, (Q_BLOCK, 1), 0)
    m_ref[...] = jnp.full_like(m_ref, NEG)
    l_ref[...] = jnp.zeros_like(l_ref)
    acc_ref[...] = jnp.zeros_like(acc_ref)
    blk_j = lax.broadcasted_iota(I32, (n_slc, SLC_CHUNK), 0)
    blk_k = lax.broadcasted_iota(I32, (n_slc, SLC_CHUNK), 1) // SLC_BLOCK
    k_idx = lax.broadcasted_iota(I32, (1, SLC_CHUNK), 1)

    def body(c, carry):
        start = pl.multiple_of(c * SLC_CHUNK, SLC_CHUNK)
        k = ks_ref[0, pl.ds(start, SLC_CHUNK), :]
        v = vs_ref[0, pl.ds(start, SLC_CHUNK), :]
        expand = jnp.where(blk_j == blk_k + c * (SLC_CHUNK // SLC_BLOCK), 1.0, 0.0).astype(BF16)
        picked = _dot(sel, expand)
        ok = (picked > 0.5) & (k_idx + c * SLC_CHUNK <= t)
        bias = jnp.where(ok, 0.0, NEG)
        s = _dot_nt(q, k) + jnp.concatenate([bias] * NSA_GROUP, 0)
        m_old = m_ref[...]
        m_new = jnp.maximum(m_old, jnp.max(s, -1, keepdims=True))
        alpha = jnp.exp(m_old - m_new)
        p = jnp.exp(s - m_new)
        l_ref[...] = alpha * l_ref[...] + jnp.sum(p, -1, keepdims=True)
        acc_ref[...] = alpha * acc_ref[...] + _dot(p.astype(BF16), v)
        m_ref[...] = m_new
        return carry

    n_chunks = ((i + 1) * Q_BLOCK + SLC_CHUNK - 1) // SLC_CHUNK
    lax.fori_loop(0, n_chunks, body, 0)
    o = jnp.where(m_ref[...] > 0.5 * NEG, acc_ref[...] / l_ref[...], 0.0)
    for h in range(NSA_GROUP):
        os_ref[:, h * NSA_HEAD_DIM:(h + 1) * NSA_HEAD_DIM] = (
            o[h * Q_BLOCK:(h + 1) * Q_BLOCK].astype(os_ref.dtype))


def _nsa_slc(qn, sel, ks, vs, B, S):
    T = B * S
    nq = S // Q_BLOCK
    n_slc = S // SLC_BLOCK
    rows = NSA_GROUP * Q_BLOCK
    return pl.pallas_call(
        functools.partial(_nsa_slc_kernel, n_slc=n_slc),
        grid=(NSA_KV_HEADS, B, nq),
        in_specs=[pl.BlockSpec((1, NSA_GROUP, Q_BLOCK, NSA_HEAD_DIM), lambda g, b, i: (g, 0, b * nq + i, 0)),
                  pl.BlockSpec((1, Q_BLOCK, n_slc), lambda g, b, i: (g, b * nq + i, 0)),
                  pl.BlockSpec((1, S, NSA_HEAD_DIM), lambda g, b, i: (g, b, 0)),
                  pl.BlockSpec((1, S, NSA_HEAD_DIM), lambda g, b, i: (g, b, 0))],
        out_specs=pl.BlockSpec((Q_BLOCK, NSA_GROUP * NSA_HEAD_DIM), lambda g, b, i: (b * nq + i, g)),
        out_shape=jax.ShapeDtypeStruct((T, NSA_WIDTH), BF16),
        scratch_shapes=[pltpu.VMEM((rows, 1), F32), pltpu.VMEM((rows, 1), F32),
                        pltpu.VMEM((rows, NSA_HEAD_DIM), F32)],
        compiler_params=_cparams(("parallel", "parallel", "arbitrary")),
        name="nsa_selected",
    )(qn, sel, ks, vs)


N_WIN_BLOCKS = WINDOW // Q_BLOCK + 1


def _nsa_win_kernel(q_ref, *refs):
    k_refs = refs[:N_WIN_BLOCKS]
    v_refs = refs[N_WIN_BLOCKS:2 * N_WIN_BLOCKS]
    ow_ref = refs[2 * N_WIN_BLOCKS]
    i = pl.program_id(2)
    q = q_ref[0].reshape(NSA_GROUP * Q_BLOCK, NSA_HEAD_DIM)
    k = jnp.concatenate([r[0] for r in k_refs], 0)
    v = jnp.concatenate([r[0] for r in v_refs], 0)
    nk = N_WIN_BLOCKS * Q_BLOCK
    t = i * Q_BLOCK + lax.broadcasted_iota(I32, (Q_BLOCK, 1), 0)
    wpos = i * Q_BLOCK - WINDOW + lax.broadcasted_iota(I32, (1, nk), 1)
    ok = (wpos <= t) & (wpos > t - WINDOW) & (wpos >= 0)
    bias = jnp.where(ok, 0.0, NEG)
    s = _dot_nt(q, k) + jnp.concatenate([bias] * NSA_GROUP, 0)
    m = jnp.max(s, -1, keepdims=True)
    p = jnp.exp(s - m)
    o = _dot(p.astype(BF16), v) / jnp.sum(p, -1, keepdims=True)
    for h in range(NSA_GROUP):
        ow_ref[:, h * NSA_HEAD_DIM:(h + 1) * NSA_HEAD_DIM] = (
            o[h * Q_BLOCK:(h + 1) * Q_BLOCK].astype(ow_ref.dtype))


def _nsa_win(qn, kw, vw, B, S):
    T = B * S
    nq = S // Q_BLOCK

    def kv_spec(c):
        back = N_WIN_BLOCKS - 1 - c
        return pl.BlockSpec((1, Q_BLOCK, NSA_HEAD_DIM),
                            lambda g, b, i: (g, b * nq + jnp.maximum(i - back, 0), 0))

    specs = [kv_spec(c) for c in range(N_WIN_BLOCKS)]
    return pl.pallas_call(
        _nsa_win_kernel,
        grid=(NSA_KV_HEADS, B, nq),
        in_specs=[pl.BlockSpec((1, NSA_GROUP, Q_BLOCK, NSA_HEAD_DIM), lambda g, b, i: (g, 0, b * nq + i, 0))]
        + specs + specs,
        out_specs=pl.BlockSpec((Q_BLOCK, NSA_GROUP * NSA_HEAD_DIM), lambda g, b, i: (b * nq + i, g)),
        out_shape=jax.ShapeDtypeStruct((T, NSA_WIDTH), BF16),
        compiler_params=_cparams(("parallel", "parallel", "arbitrary")),
        name="nsa_window",
    )(qn, *([kw] * N_WIN_BLOCKS), *([vw] * N_WIN_BLOCKS))


def _outproj_kernel(hm_ref, oc_ref, os_ref, ow_ref, gt_ref, gb_ref, ge_ref, h_ref, w_ref, g_ref, b_ref,
                    out_ref):
    gates = jax.nn.sigmoid(gt_ref[...] + gb_ref[...]).astype(BF16)
    gx = _dot(gates, ge_ref[...])
    hn = (gx[:, 0:NSA_WIDTH] * oc_ref[...].astype(F32)
          + gx[:, NSA_WIDTH:2 * NSA_WIDTH] * os_ref[...].astype(F32)
          + gx[:, 2 * NSA_WIDTH:3 * NSA_WIDTH] * ow_ref[...].astype(F32))
    mix = (_dot(hm_ref[...], w_ref[0:MLSTM_WIDTH, :])
           + _dot(hn.astype(BF16), w_ref[MLSTM_WIDTH:MLSTM_WIDTH + NSA_WIDTH, :]))
    out_ref[...] = _layer_norm(DN_ALPHA * h_ref[...] + mix, g_ref[...], b_ref[...])


def _outproj(hm, oc, os_, ow, u, gate_b, gate_expand, h, w_out, g, b):
    T = h.shape[0]
    tm = 256
    row = lambda i: (i, 0)
    fixed = lambda i: (0, 0)
    return pl.pallas_call(
        _outproj_kernel,
        grid=(T // tm,),
        in_specs=[pl.BlockSpec((tm, MLSTM_WIDTH), row), pl.BlockSpec((tm, NSA_WIDTH), row),
                  pl.BlockSpec((tm, NSA_WIDTH), row), pl.BlockSpec((tm, NSA_WIDTH), row),
                  pl.BlockSpec((tm, LANE), lambda i: (i, U_GATE // LANE)),
                  pl.BlockSpec((1, LANE), fixed),
                  pl.BlockSpec((LANE, 3 * NSA_WIDTH), fixed),
                  pl.BlockSpec((tm, D_MODEL), row),
                  pl.BlockSpec((D_MODEL, D_MODEL), fixed),
                  pl.BlockSpec((1, D_MODEL), fixed), pl.BlockSpec((1, D_MODEL), fixed)],
        out_specs=pl.BlockSpec((tm, D_MODEL), row),
        out_shape=jax.ShapeDtypeStruct((T, D_MODEL), F32),
        compiler_params=_cparams(("parallel",)),
        name="mixer_outproj_ln",
    )(hm, oc, os_, ow, u, gate_b, gate_expand, h, w_out, g, b)


def _matmul_kernel(x_ref, w_ref, o_ref):
    o_ref[...] = _dot(x_ref[...].astype(BF16), w_ref[...]).astype(o_ref.dtype)


def _mem_kv(mem2, wkv):
    M = mem2.shape[0]
    N = wkv.shape[1]
    tn = 512
    return pl.pallas_call(
        _matmul_kernel,
        grid=(N // tn,),
        in_specs=[pl.BlockSpec((M, D_MODEL), lambda j: (0, 0)),
                  pl.BlockSpec((D_MODEL, tn), lambda j: (0, j))],
        out_specs=pl.BlockSpec((M, tn), lambda j: (0, j)),
        out_shape=jax.ShapeDtypeStruct((M, N), BF16),
        compiler_params=_cparams(("parallel",)),
        name="mem_kv_proj",
    )(mem2, wkv)


def _xattn_kernel(h_ref, wq_ref, kv_ref, o_ref):
    hb = h_ref[...].astype(BF16)
    for hd in range(XA_HEADS):
        c0 = hd * XA_HEAD_DIM
        q = (_dot(hb, wq_ref[:, c0:c0 + XA_HEAD_DIM]) * (XA_HEAD_DIM ** -0.5)).astype(BF16)
        s = _dot_nt(q, kv_ref[:, c0:c0 + XA_HEAD_DIM])
        m = jnp.max(s, -1, keepdims=True)
        p = jnp.exp(s - m)
        o = _dot(p.astype(BF16), kv_ref[:, D_MODEL + c0:D_MODEL + c0 + XA_HEAD_DIM])
        o_ref[:, c0:c0 + XA_HEAD_DIM] = (o / jnp.sum(p, -1, keepdims=True)).astype(o_ref.dtype)


def _xattn(h1, wq, kv, B, S):
    T = B * S
    tm = 256
    n_mem = kv.shape[0] // B
    per_b = S // tm
    return pl.pallas_call(
        _xattn_kernel,
        grid=(T // tm,),
        in_specs=[pl.BlockSpec((tm, D_MODEL), lambda i: (i, 0)),
                  pl.BlockSpec((D_MODEL, D_MODEL), lambda i: (0, 0)),
                  pl.BlockSpec((n_mem, 2 * D_MODEL), lambda i: (i // per_b, 0))],
        out_specs=pl.BlockSpec((tm, D_MODEL), lambda i: (i, 0)),
        out_shape=jax.ShapeDtypeStruct((T, D_MODEL), BF16),
        compiler_params=_cparams(("parallel",)),
        name="mem_xattn",
    )(h1, wq, kv)


def _xa_out_router_kernel(o_ref, wo_ref, h_ref, g_ref, b_ref, rw_ref, rb_ref,
                          h2_ref, idx_ref, wgt_ref):
    xa = _dot(o_ref[...], wo_ref[...])
    h2 = _layer_norm(DN_ALPHA * h_ref[...] + xa, g_ref[...], b_ref[...])
    h2_ref[...] = h2
    hi, lo = _split2(h2)
    logits = _dot(hi, rw_ref[0]) + _dot(hi, rw_ref[1]) + _dot(lo, rw_ref[0])
    scores = jax.nn.sigmoid(logits)
    lane = lax.broadcasted_iota(I32, scores.shape, 1)
    lane_f = lane.astype(F32)
    biased = jnp.where(lane < N_EXPERTS, scores + rb_ref[...], -jnp.inf)
    idx_mat = jnp.zeros(scores.shape, F32)
    w_mat = jnp.zeros(scores.shape, F32)
    for kk in range(TOP_K):
        mx = jnp.max(biased, -1, keepdims=True)
        first = jnp.min(jnp.where(biased == mx, lane_f, float(LANE)), -1, keepdims=True)
        hit = lane_f == first
        top_s = jnp.sum(jnp.where(hit, scores, 0.0), -1, keepdims=True)
        idx_mat = jnp.where(lane == kk, first, idx_mat)
        w_mat = jnp.where(lane == kk, top_s, w_mat)
        biased = jnp.where(hit, -jnp.inf, biased)
    idx_ref[...] = idx_mat.astype(I32)
    wgt_ref[...] = w_mat / jnp.sum(w_mat, -1, keepdims=True) * ROUTED_SCALE


def _xa_out_router(o, wo, h1, g, b, rw2, rb):
    T = h1.shape[0]
    tm = 256
    row = lambda i: (i, 0)
    fixed = lambda i: (0, 0)
    return pl.pallas_call(
        _xa_out_router_kernel,
        grid=(T // tm,),
        in_specs=[pl.BlockSpec((tm, D_MODEL), row),
                  pl.BlockSpec((D_MODEL, D_MODEL), fixed),
                  pl.BlockSpec((tm, D_MODEL), row),
                  pl.BlockSpec((1, D_MODEL), fixed), pl.BlockSpec((1, D_MODEL), fixed),
                  pl.BlockSpec((2, D_MODEL, LANE), lambda i: (0, 0, 0)),
                  pl.BlockSpec((1, LANE), fixed)],
        out_specs=[pl.BlockSpec((tm, D_MODEL), row), pl.BlockSpec((tm, LANE), row),
                   pl.BlockSpec((tm, LANE), row)],
        out_shape=[jax.ShapeDtypeStruct((T, D_MODEL), F32),
                   jax.ShapeDtypeStruct((T, LANE), I32),
                   jax.ShapeDtypeStruct((T, LANE), F32)],
        compiler_params=_cparams(("parallel",)),
        name="xattn_out_ln_router",
    )(o, wo, h1, g, b, rw2, rb)


DISPATCH_TM = 512


def _dispatch_kernel(dest_hbm, h_ref, xs_in_ref, xs_ref, idx_smem, isem, sem):
    del xs_in_ref
    i = pl.program_id(0)
    n = DISPATCH_TM * TOP_K
    cp = pltpu.make_async_copy(dest_hbm.at[i], idx_smem, isem)
    cp.start()
    cp.wait()

    def issue(r, carry):
        for kk in range(TOP_K):
            d = idx_smem[r * TOP_K + kk]
            pltpu.make_async_copy(h_ref.at[pl.ds(r, 1)], xs_ref.at[pl.ds(d, 1)], sem).start()
        return carry

    lax.fori_loop(0, DISPATCH_TM, issue, 0)
    for _ in range(TOP_K):
        pltpu.make_async_copy(h_ref, xs_ref.at[pl.ds(0, DISPATCH_TM)], sem).wait()
    del n


def _dispatch(dest2, h2, xs_zero):
    T = h2.shape[0]
    P = xs_zero.shape[0]
    tm = DISPATCH_TM
    return pl.pallas_call(
        _dispatch_kernel,
        grid=(T // tm,),
        in_specs=[pl.BlockSpec(memory_space=pl.ANY),
                  pl.BlockSpec((tm, D_MODEL), lambda i: (i, 0)),
                  pl.BlockSpec(memory_space=pl.ANY)],
        out_specs=pl.BlockSpec(memory_space=pl.ANY),
        out_shape=jax.ShapeDtypeStruct((P, D_MODEL), F32),
        scratch_shapes=[pltpu.SMEM((tm * TOP_K,), I32), pltpu.SemaphoreType.DMA, pltpu.SemaphoreType.DMA],
        input_output_aliases={2: 0},
        compiler_params=_cparams(("arbitrary",)),
        name="moe_dispatch",
    )(dest2, h2, xs_zero)


def _experts_kernel(be_ref, nu_ref, x_ref, wg_ref, wu_ref, wd_ref, y_ref):
    used = pl.program_id(0) < nu_ref[0]

    @pl.when(used)
    def _():
        xb = x_ref[...].astype(BF16)
        a = _dot(xb, wg_ref[0])
        act = (a * jax.nn.sigmoid(a)) * _dot(xb, wu_ref[0])
        y_ref[...] = _dot(act.astype(BF16), wd_ref[0])

    @pl.when(jnp.logical_not(used))
    def _():
        y_ref[...] = jnp.zeros_like(y_ref)


def _experts(block_e, n_used, xs, wg, wu, wd):
    P = xs.shape[0]
    bm = EXPERT_BM
    rowmap = lambda j, be, nu: (jnp.minimum(j, nu[0] - 1), 0)
    wmap = lambda j, be, nu: (be[j], 0, 0)
    return pl.pallas_call(
        _experts_kernel,
        grid_spec=pltpu.PrefetchScalarGridSpec(
            num_scalar_prefetch=2,
            grid=(P // bm,),
            in_specs=[pl.BlockSpec((bm, D_MODEL), rowmap),
                      pl.BlockSpec((1, D_MODEL, D_EXPERT), wmap),
                      pl.BlockSpec((1, D_MODEL, D_EXPERT), wmap),
                      pl.BlockSpec((1, D_EXPERT, D_MODEL), wmap)],
            out_specs=pl.BlockSpec((bm, D_MODEL), lambda j, be, nu: (j, 0))),
        out_shape=jax.ShapeDtypeStruct((P, D_MODEL), F32),
        compiler_params=_cparams(("arbitrary",)),
        name="moe_experts",
    )(block_e, n_used, xs, wg, wu, wd)


def _shared_ffn_kernel(x_ref, wg_ref, wu_ref, wd_ref, o_ref, xb_ref):
    f = pl.program_id(1)

    @pl.when(f == 0)
    def _():
        xb_ref[...] = x_ref[...].astype(BF16)
        o_ref[...] = jnp.zeros_like(o_ref)

    xb = xb_ref[...]
    a = _dot(xb, wg_ref[...])
    act = (a * jax.nn.sigmoid(a)) * _dot(xb, wu_ref[...])
    o_ref[...] += _dot(act.astype(BF16), wd_ref[...])


def _shared_ffn(h2, wg, wu, wd):
    T = h2.shape[0]
    tm, tf = 512, 256
    return pl.pallas_call(
        _shared_ffn_kernel,
        grid=(T // tm, D_SHARED // tf),
        in_specs=[pl.BlockSpec((tm, D_MODEL), lambda i, f: (i, 0)),
                  pl.BlockSpec((D_MODEL, tf), lambda i, f: (0, f)),
                  pl.BlockSpec((D_MODEL, tf), lambda i, f: (0, f)),
                  pl.BlockSpec((tf, D_MODEL), lambda i, f: (f, 0))],
        out_specs=pl.BlockSpec((tm, D_MODEL), lambda i, f: (i, 0)),
        out_shape=jax.ShapeDtypeStruct((T, D_MODEL), F32),
        scratch_shapes=[pltpu.VMEM((tm, D_MODEL), BF16)],
        compiler_params=_cparams(("parallel", "arbitrary")),
        name="shared_ffn",
    )(h2, wg, wu, wd)


COMBINE_TM = 512
COMBINE_SUB = 128


def _combine_kernel(dest_hbm, ys_hbm, w_ref, sh_ref, h_ref, g_ref, b_ref, out_ref,
                    idx_smem, gbuf, isem, sem):
    i = pl.program_id(0)
    cp = pltpu.make_async_copy(dest_hbm.at[i], idx_smem, isem)
    cp.start()
    cp.wait()
    for sub in range(COMBINE_TM // COMBINE_SUB):
        base = sub * COMBINE_SUB

        def issue(r, carry):
            for kk in range(TOP_K):
                d = idx_smem[(base + r) * TOP_K + kk]
                pltpu.make_async_copy(ys_hbm.at[pl.ds(d, 1)], gbuf.at[kk, pl.ds(r, 1)], sem).start()
            return carry

        lax.fori_loop(0, COMBINE_SUB, issue, 0)
        for kk in range(TOP_K):
            pltpu.make_async_copy(ys_hbm.at[pl.ds(0, COMBINE_SUB)], gbuf.at[kk], sem).wait()
        rs = slice(base, base + COMBINE_SUB)
        w = w_ref[rs, :]
        routed = w[:, 0:1] * gbuf[0]
        for kk in range(1, TOP_K):
            routed = routed + w[:, kk:kk + 1] * gbuf[kk]
        z = DN_ALPHA * h_ref[rs, :] + (routed + sh_ref[rs, :])
        out_ref[rs, :] = _layer_norm(z, g_ref[...], b_ref[...])


def _combine(dest2, ys, top_w, sh, h2, g, b):
    T = h2.shape[0]
    tm = COMBINE_TM
    row = lambda i: (i, 0)
    fixed = lambda i: (0, 0)
    return pl.pallas_call(
        _combine_kernel,
        grid=(T // tm,),
        in_specs=[pl.BlockSpec(memory_space=pl.ANY), pl.BlockSpec(memory_space=pl.ANY),
                  pl.BlockSpec((tm, LANE), row), pl.BlockSpec((tm, D_MODEL), row),
                  pl.BlockSpec((tm, D_MODEL), row),
                  pl.BlockSpec((1, D_MODEL), fixed), pl.BlockSpec((1, D_MODEL), fixed)],
        out_specs=pl.BlockSpec((tm, D_MODEL), row),
        out_shape=jax.ShapeDtypeStruct((T, D_MODEL), F32),
        scratch_shapes=[pltpu.SMEM((tm * TOP_K,), I32),
                        pltpu.VMEM((TOP_K, COMBINE_SUB, D_MODEL), F32),
                        pltpu.SemaphoreType.DMA, pltpu.SemaphoreType.DMA],
        compiler_params=_cparams(("arbitrary",)),
        name="moe_combine_ln",
    )(dest2, ys, top_w, sh, h2, g, b)


def _route_plan(top_idx, T):
    bm = EXPERT_BM
    onehot = (top_idx[:, :, None] == jnp.arange(N_EXPERTS, dtype=I32)[None, None, :]).astype(I32).sum(1)
    counts = onehot.sum(0)
    rank = jnp.cumsum(onehot, 0) - onehot
    padded = (counts + bm - 1) // bm * bm
    pad_end = jnp.cumsum(padded)
    pad_start = pad_end - padded
    dest = jnp.take_along_axis(pad_start[None, :] + rank, top_idx, axis=1)
    n_blocks = T * TOP_K // bm + N_EXPERTS
    block_e = jnp.minimum(jnp.searchsorted(pad_end, jnp.arange(n_blocks, dtype=I32) * bm, side='right'),
                          N_EXPERTS - 1).astype(I32)
    n_used = (pad_end[-1:] // bm).astype(I32)
    return dest.astype(I32), block_e, n_used, n_blocks * bm


def _overlap_matrix(S):
    n_cmp_rows = S // CMP_STRIDE
    n_slc = S // SLC_BLOCK
    c_lo = np.arange(n_cmp_rows)[:, None] * CMP_STRIDE
    j_lo = np.arange(n_slc)[None, :] * SLC_BLOCK
    ov = (c_lo <= j_lo + SLC_BLOCK - 1) & (c_lo + CMP_BLOCK - 1 >= j_lo)
    return jnp.asarray(ov.astype(np.float32), dtype=BF16)


def _gate_expand_matrix():
    ge = np.zeros((LANE, 3 * NSA_WIDTH), np.float32)
    for hh in range(NSA_HEADS):
        for br in range(3):
            ge[8 + hh * 3 + br, br * NSA_WIDTH + hh * NSA_HEAD_DIM:br * NSA_WIDTH + (hh + 1) * NSA_HEAD_DIM] = 1.0
    return jnp.asarray(ge, dtype=BF16)


def kernel(x, mem, positions, ln0_g, ln0_b, w_in, conv_w, conv_b, igate_b, fgate_b, mlstm_norm_g, cmp_pos, cmp_w1k, cmp_w2k, cmp_w1v, cmp_w2v, nsa_gate_b, w_out, ln1_g, ln1_b, xa_wq, xa_wk, xa_wv, xa_wo, ln2_g, ln2_b, router_w, router_bias, moe_w_gate, moe_w_up, moe_w_down, sh_w_gate, sh_w_up, sh_w_down, ln3_g, ln3_b):
    B, S, D = x.shape
    T = B * S
    assert D == D_MODEL and w_in.shape[0] == DEPTH == 1
    assert S % MLSTM_L == 0 and S % 512 == 0 and T % COMBINE_TM == 0 and (T * TOP_K) % EXPERT_BM == 0
    row = lambda a: a.reshape(1, -1)

    w = w_in[0]
    w_r = jnp.concatenate([w[:, :3072], w[:, 3080:5640], w[:, 3072:3080], w[:, 5640:5664],
                           jnp.zeros((D, U_COLS - 5664), F32)], 1).astype(BF16)
    gate_b = jnp.concatenate([igate_b[0], fgate_b[0], nsa_gate_b[0],
                              jnp.zeros((LANE - 2 * MLSTM_HEADS - 3 * NSA_HEADS,), F32)]).reshape(1, LANE)
    half = NSA_HEAD_DIM // 2
    inv_freq = ROPE_THETA ** (-jnp.arange(half, dtype=F32) / half)
    invf2 = jnp.concatenate([inv_freq, inv_freq]).reshape(1, LANE)
    sgn = jnp.concatenate([-jnp.ones((half,), F32), jnp.ones((half,), F32)]).reshape(1, LANE)
    pos8 = jnp.zeros((2, 8, CMP_STRIDE * NSA_HEAD_DIM), F32).at[:, 0, :].set(
        cmp_pos[0].reshape(2, CMP_STRIDE * NSA_HEAD_DIM)).astype(BF16)
    w1k = cmp_w1k[0].reshape(2, CMP_STRIDE * NSA_HEAD_DIM, CMP_HIDDEN).astype(BF16)
    w1v = cmp_w1v[0].reshape(2, CMP_STRIDE * NSA_HEAD_DIM, CMP_HIDDEN).astype(BF16)
    rw = jnp.pad(router_w[0], ((0, 0), (0, LANE - N_EXPERTS)))
    rw_hi = rw.astype(BF16)
    rw2 = jnp.stack([rw_hi, (rw - rw_hi.astype(F32)).astype(BF16)])
    rb = jnp.pad(router_bias[0], (0, LANE - N_EXPERTS)).reshape(1, LANE)

    h, u = _ln_inproj(x.reshape(T, D), row(ln0_g), row(ln0_b), w_r)
    hm = _mlstm(u, conv_w[0], row(conv_b[0]), gate_b, row(mlstm_norm_g[0]), B, S)
    qn, kc, vc, ks, vs, kw, vw = _nsa_prep(u, positions.reshape(T, 1), invf2, sgn, T)
    kcc = _compress(kc, w1k, pos8, cmp_w2k[0].astype(BF16), B, S)
    vcc = _compress(vc, w1v, pos8, cmp_w2v[0].astype(BF16), B, S)
    oc, sel = _nsa_cmp(qn, kcc, vcc, _overlap_matrix(S), B, S)
    os_ = _nsa_slc(qn, sel, ks, vs, B, S)
    ow = _nsa_win(qn, kw, vw, B, S)
    h1 = _outproj(hm, oc, os_, ow, u, gate_b, _gate_expand_matrix(), h, w_out[0].astype(BF16),
                  row(ln1_g[0]), row(ln1_b[0]))

    wkv = jnp.concatenate([xa_wk[0], xa_wv[0]], 1).astype(BF16)
    kv = _mem_kv(mem.reshape(-1, D), wkv)
    xo = _xattn(h1, xa_wq[0].astype(BF16), kv, B, S)
    h2, top_idx, top_w = _xa_out_router(xo, xa_wo[0].astype(BF16), h1, row(ln2_g[0]), row(ln2_b[0]), rw2, rb)

    dest, block_e, n_used, P = _route_plan(top_idx[:, :TOP_K], T)
    dest2 = dest.reshape(T // COMBINE_TM, COMBINE_TM * TOP_K)
    xs = _dispatch(dest2, h2, jnp.zeros((P, D), F32))
    ys = _experts(block_e, n_used, xs, moe_w_gate[0].astype(BF16), moe_w_up[0].astype(BF16),
                  moe_w_down[0].astype(BF16))
    sh = _shared_ffn(h2, sh_w_gate[0].astype(BF16), sh_w_up[0].astype(BF16), sh_w_down[0].astype(BF16))
    out = _combine(dest2, ys, top_w, sh, h2, row(ln3_g[0]), row(ln3_b[0]))
    return out.reshape(B, S, D)
```

```python
import functools

import numpy as np
import jax
import jax.numpy as jnp
from jax import lax
from jax.experimental import pallas as pl
from jax.experimental.pallas import tpu as pltpu

F32 = jnp.float32
BF16 = jnp.bfloat16
I32 = jnp.int32

D_MODEL = 2048
MLSTM_HEADS = 4
MLSTM_DV = 256
MLSTM_DQK = 128
MLSTM_QK = MLSTM_HEADS * MLSTM_DQK
MLSTM_WIDTH = MLSTM_HEADS * MLSTM_DV
CONV_WIDTH = 4
NSA_HEAD_DIM = 128
NSA_HEADS = 8
NSA_KV_HEADS = 2
NSA_GROUP = 4
NSA_WIDTH = NSA_HEADS * NSA_HEAD_DIM
CMP_BLOCK = 32
CMP_STRIDE = 16
CMP_HIDDEN = 256
SLC_BLOCK = 64
SLC_TOPN = 16
WINDOW = 512
Q_BLOCK = 128
XA_HEADS = 4
XA_HEAD_DIM = 512
N_EXPERTS = 64
TOP_K = 6
D_EXPERT = 1408
D_SHARED = 2816
ROUTED_SCALE = 2.446
ROPE_THETA = 10000.0
LN_EPS = 1e-5
DEPTH = 1
DN_ALPHA = (2.0 * DEPTH) ** 0.25

U_MLSTM = 0
U_NSA = 3072
U_GATE = 5632
U_COLS = 5760

LANE = 128
SUBLANE = 8
NEG = -1e30
MLSTM_L = 256
VMEM_LIMIT = 56 * 1024 * 1024
EXPERT_BM = 256


def _cparams(sem):
    return pltpu.CompilerParams(dimension_semantics=sem, vmem_limit_bytes=VMEM_LIMIT)


def _dot(a, b):
    return jnp.dot(a, b, preferred_element_type=F32)


def _dot_nt(a, b):
    return lax.dot_general(a, b, (((1,), (1,)), ((), ())), preferred_element_type=F32)


def _layer_norm(z, g, b):
    mu = jnp.mean(z, -1, keepdims=True)
    zc = z - mu
    var = jnp.mean(zc * zc, -1, keepdims=True)
    return zc * lax.rsqrt(var + LN_EPS) * g + b


def _split3(x):
    hi = x.astype(BF16)
    r = x - hi.astype(F32)
    mid = r.astype(BF16)
    lo = (r - mid.astype(F32)).astype(BF16)
    return hi, mid, lo


def _split2(x):
    hi = x.astype(BF16)
    lo = (x - hi.astype(F32)).astype(BF16)
    return hi, lo


def _ln_inproj_kernel(x_ref, g_ref, b_ref, w_ref, h_ref, u_ref, hb_ref):
    @pl.when(pl.program_id(1) == 0)
    def _():
        hn = _layer_norm(x_ref[...], g_ref[...], b_ref[...])
        h_ref[...] = hn
        hb_ref[...] = hn.astype(BF16)

    u_ref[...] = _dot(hb_ref[...], w_ref[...])


def _ln_inproj(x2, g, b, w):
    T = x2.shape[0]
    tm, tn = 512, 640
    return pl.pallas_call(
        _ln_inproj_kernel,
        grid=(T // tm, U_COLS // tn),
        in_specs=[pl.BlockSpec((tm, D_MODEL), lambda i, j: (i, 0)),
                  pl.BlockSpec((1, D_MODEL), lambda i, j: (0, 0)),
                  pl.BlockSpec((1, D_MODEL), lambda i, j: (0, 0)),
                  pl.BlockSpec((D_MODEL, tn), lambda i, j: (0, j))],
        out_specs=[pl.BlockSpec((tm, D_MODEL), lambda i, j: (i, 0)),
                   pl.BlockSpec((tm, tn), lambda i, j: (i, j))],
        out_shape=[jax.ShapeDtypeStruct((T, D_MODEL), F32),
                   jax.ShapeDtypeStruct((T, U_COLS), F32)],
        scratch_shapes=[pltpu.VMEM((tm, D_MODEL), BF16)],
        compiler_params=_cparams(("parallel", "arbitrary")),
        name="ln_inproj",
    )(x2, g, b, w)


def _log_sigmoid(x):
    return jnp.minimum(x, 0.0) - jnp.log1p(jnp.exp(-jnp.abs(x)))


def _mlstm_kernel(qk_ref, v_ref, o_ref, gt_ref, cw_ref, cb_ref, gb_ref, ng_ref, out_ref,
                  prev_ref, c_ref, n_ref, m_ref):
    L = MLSTM_L

    @pl.when(pl.program_id(1) == 0)
    def _():
        prev_ref[...] = jnp.zeros_like(prev_ref)
        c_ref[...] = jnp.zeros_like(c_ref)
        n_ref[...] = jnp.zeros_like(n_ref)
        m_ref[...] = jnp.zeros_like(m_ref)

    x = qk_ref[...]
    prev = prev_ref[...]
    row = lax.broadcasted_iota(I32, (L, 1), 0)
    cw = cw_ref[...]
    y = cb_ref[...] + cw[CONV_WIDTH - 1:CONV_WIDTH, :] * x
    for j in range(1, CONV_WIDTH):
        shifted = jnp.where(row < j, pltpu.roll(prev, j, 0), pltpu.roll(x, j, 0))
        y = y + cw[CONV_WIDTH - 1 - j:CONV_WIDTH - j, :] * shifted
    prev_ref[...] = x
    qk = y * jax.nn.sigmoid(y)

    gpre = gt_ref[...] + gb_ref[...]
    gpre_t = gpre.T
    r_i = lax.broadcasted_iota(I32, (L, L), 0)
    c_i = lax.broadcasted_iota(I32, (L, L), 1)
    causal = r_i >= c_i
    tril = jnp.where(causal, 1.0, 0.0).astype(BF16)
    triu = jnp.where(r_i <= c_i, 1.0, 0.0).astype(BF16)
    lf = _log_sigmoid(gpre)
    lf_t = _log_sigmoid(gpre_t)
    b_cols = sum(_dot(tril, part) for part in _split3(lf))
    b_rows = sum(_dot(part, triu) for part in _split3(lf_t))

    for h in range(MLSTM_HEADS):
        i_col = gpre[:, h:h + 1]
        i_row = gpre_t[h:h + 1, :]
        b_col = b_cols[:, MLSTM_HEADS + h:MLSTM_HEADS + h + 1]
        b_row = b_rows[MLSTM_HEADS + h:MLSTM_HEADS + h + 1, :]
        m_prev = m_ref[h:h + 1, 0:1]
        n_prev = n_ref[h:h + 1, :]
        c_prev = c_ref[h]

        q = qk[:, h * MLSTM_DQK:(h + 1) * MLSTM_DQK]
        k = qk[:, MLSTM_QK + h * MLSTM_DQK:MLSTM_QK + (h + 1) * MLSTM_DQK] * (MLSTM_DQK ** -0.5)
        v = v_ref[:, h * MLSTM_DV:(h + 1) * MLSTM_DV].astype(BF16)
        qb = q.astype(BF16)

        dmat = jnp.where(causal, b_col - b_row + i_row, -jnp.inf)
        inter = b_col + m_prev
        m_t = jnp.maximum(inter, jnp.max(dmat, -1, keepdims=True))
        s = _dot_nt(qb, k.astype(BF16)) * jnp.exp(dmat - m_t)
        a_inter = jnp.exp(inter - m_t)
        num = _dot(s.astype(BF16), v) + a_inter * _dot(qb, c_prev.astype(BF16))
        den = jnp.sum(s, -1, keepdims=True) + a_inter * jnp.sum(q * n_prev, -1, keepdims=True)
        hc = num / jnp.maximum(jnp.abs(den), jnp.exp(-m_t))

        mu = jnp.mean(hc, -1, keepdims=True)
        hcc = hc - mu
        var = jnp.mean(hcc * hcc, -1, keepdims=True)
        hn = hcc * lax.rsqrt(var + LN_EPS) * ng_ref[:, h * MLSTM_DV:(h + 1) * MLSTM_DV]
        og = jax.nn.sigmoid(o_ref[:, h * MLSTM_DV:(h + 1) * MLSTM_DV])
        out_ref[:, h * MLSTM_DV:(h + 1) * MLSTM_DV] = (og * hn).astype(out_ref.dtype)

        b_last = b_col[L - 1:L, :]
        g_col = b_last - b_col + i_col
        m_new = jnp.maximum(b_last + m_prev, jnp.max(g_col, 0, keepdims=True))
        kw = k * jnp.exp(g_col - m_new)
        decay = jnp.exp(b_last + m_prev - m_new)
        c_ref[h] = decay * c_prev + _dot(kw.T.astype(BF16), v)
        n_ref[h:h + 1, :] = decay * n_prev + jnp.sum(kw, 0, keepdims=True)
        m_ref[h:h + 1, :] = jnp.broadcast_to(m_new, (1, LANE))


def _mlstm(u, conv_w, conv_b, gate_b, norm_g, B, S):
    T = B * S
    L = MLSTM_L
    nc = S // L
    row = lambda b, c: b * nc + c
    return pl.pallas_call(
        _mlstm_kernel,
        grid=(B, nc),
        in_specs=[pl.BlockSpec((L, 2 * MLSTM_QK), lambda b, c: (row(b, c), 0)),
                  pl.BlockSpec((L, MLSTM_WIDTH), lambda b, c: (row(b, c), 1)),
                  pl.BlockSpec((L, MLSTM_WIDTH), lambda b, c: (row(b, c), 2)),
                  pl.BlockSpec((L, LANE), lambda b, c: (row(b, c), U_GATE // LANE)),
                  pl.BlockSpec((CONV_WIDTH, 2 * MLSTM_QK), lambda b, c: (0, 0)),
                  pl.BlockSpec((1, 2 * MLSTM_QK), lambda b, c: (0, 0)),
                  pl.BlockSpec((1, LANE), lambda b, c: (0, 0)),
                  pl.BlockSpec((1, MLSTM_WIDTH), lambda b, c: (0, 0))],
        out_specs=pl.BlockSpec((L, MLSTM_WIDTH), lambda b, c: (row(b, c), 0)),
        out_shape=jax.ShapeDtypeStruct((T, MLSTM_WIDTH), BF16),
        scratch_shapes=[pltpu.VMEM((L, 2 * MLSTM_QK), F32),
                        pltpu.VMEM((MLSTM_HEADS, MLSTM_DQK, MLSTM_DV), F32),
                        pltpu.VMEM((8, MLSTM_DQK), F32),
                        pltpu.VMEM((8, LANE), F32)],
        compiler_params=_cparams(("parallel", "arbitrary")),
        name="mlstm",
    )(u, u, u, u, conv_w, conv_b, gate_b, norm_g)


def _nsa_prep_kernel(q_ref, kv0_ref, kv1_ref, kv2_ref, pos_ref, invf_ref, sgn_ref,
                     qt_ref, kc_ref, vc_ref, ks_ref, vst_ref, kw_ref, vwt_ref):
    ang = pos_ref[...].astype(F32) * invf_ref[...]
    cos = jnp.cos(ang)
    sin = jnp.sin(ang) * sgn_ref[...]

    def rope(x):
        return x * cos + pltpu.roll(x, NSA_HEAD_DIM // 2, 1) * sin

    scale = NSA_HEAD_DIM ** -0.5
    for g in range(NSA_KV_HEADS):
        for h in range(NSA_GROUP):
            c0 = (g * NSA_GROUP + h) * NSA_HEAD_DIM
            qt_ref[g, h] = (rope(q_ref[:, c0:c0 + NSA_HEAD_DIM]) * scale).T.astype(BF16)
        c0 = g * NSA_HEAD_DIM
        c1 = NSA_KV_HEADS * NSA_HEAD_DIM + g * NSA_HEAD_DIM
        kc_ref[g] = rope(kv0_ref[:, c0:c0 + NSA_HEAD_DIM]).astype(BF16)
        vc_ref[g] = kv0_ref[:, c1:c1 + NSA_HEAD_DIM].astype(BF16)
        ks_ref[g] = rope(kv1_ref[:, c0:c0 + NSA_HEAD_DIM]).astype(BF16)
        vst_ref[g] = kv1_ref[:, c1:c1 + NSA_HEAD_DIM].T.astype(BF16)
        kw_ref[g] = rope(kv2_ref[:, c0:c0 + NSA_HEAD_DIM]).astype(BF16)
        vwt_ref[g] = kv2_ref[:, c1:c1 + NSA_HEAD_DIM].T.astype(BF16)


def _nsa_prep(u, pos_col, invf2, sgn, T):
    ts = 512
    kvw = 2 * NSA_KV_HEADS * NSA_HEAD_DIM
    kv_spec = lambda n: pl.BlockSpec((ts, kvw), lambda i: (i, (U_NSA + NSA_WIDTH) // kvw + n))
    row_out = pl.BlockSpec((NSA_KV_HEADS, ts, NSA_HEAD_DIM), lambda i: (0, i, 0))
    row_shape = jax.ShapeDtypeStruct((NSA_KV_HEADS, T, NSA_HEAD_DIM), BF16)
    col_out = pl.BlockSpec((NSA_KV_HEADS, NSA_HEAD_DIM, ts), lambda i: (0, 0, i))
    col_shape = jax.ShapeDtypeStruct((NSA_KV_HEADS, NSA_HEAD_DIM, T), BF16)
    return pl.pallas_call(
        _nsa_prep_kernel,
        grid=(T // ts,),
        in_specs=[pl.BlockSpec((ts, NSA_WIDTH), lambda i: (i, U_NSA // NSA_WIDTH)),
                  kv_spec(0), kv_spec(1), kv_spec(2),
                  pl.BlockSpec((ts, 1), lambda i: (i, 0)),
                  pl.BlockSpec((1, LANE), lambda i: (0, 0)),
                  pl.BlockSpec((1, LANE), lambda i: (0, 0))],
        out_specs=[pl.BlockSpec((NSA_KV_HEADS, NSA_GROUP, NSA_HEAD_DIM, ts), lambda i: (0, 0, 0, i)),
                   row_out, row_out, row_out, col_out, row_out, col_out],
        out_shape=[jax.ShapeDtypeStruct((NSA_KV_HEADS, NSA_GROUP, NSA_HEAD_DIM, T), BF16),
                   row_shape, row_shape, row_shape, col_shape, row_shape, col_shape],
        compiler_params=_cparams(("parallel",)),
        name="nsa_prep",
    )(u, u, u, u, pos_col, invf2, sgn)


def _gelu_tanh(x):
    return 0.5 * x * (1.0 + jnp.tanh(0.7978845608028654 * (x + 0.044715 * x * x * x)))


def _compress_kernel(r_ref, w1_ref, pos_ref, w2_ref, out_ref, *, transpose_out):
    r = r_ref[0]
    nr = r.shape[0]
    a = _dot(r, w1_ref[0])
    b = _dot(r, w1_ref[1])
    c0 = _dot(pos_ref[0], w1_ref[0]) + _dot(pos_ref[1], w1_ref[1])
    pre = a + pltpu.roll(b, nr - 1, 0) + c0[0:1, :]
    out = _dot(_gelu_tanh(pre).astype(BF16), w2_ref[...])
    row = lax.broadcasted_iota(I32, (nr, 1), 0)
    out = jnp.where(row < nr - 1, out, 0.0)
    out_ref[0] = (out.T if transpose_out else out).astype(out_ref.dtype)


def _compress(kv, w1, pos8, w2, B, S, transpose_out):
    nr = S // CMP_STRIDE
    half = CMP_STRIDE * NSA_HEAD_DIM
    r = kv.reshape(NSA_KV_HEADS, B * nr, half)
    if transpose_out:
        out_spec = pl.BlockSpec((1, NSA_HEAD_DIM, nr), lambda g, b: (g, 0, b))
        out_shape = jax.ShapeDtypeStruct((NSA_KV_HEADS, NSA_HEAD_DIM, B * nr), BF16)
    else:
        out_spec = pl.BlockSpec((1, nr, NSA_HEAD_DIM), lambda g, b: (g, b, 0))
        out_shape = jax.ShapeDtypeStruct((NSA_KV_HEADS, B * nr, NSA_HEAD_DIM), BF16)
    return pl.pallas_call(
        functools.partial(_compress_kernel, transpose_out=transpose_out),
        grid=(NSA_KV_HEADS, B),
        in_specs=[pl.BlockSpec((1, nr, half), lambda g, b: (g, b, 0)),
                  pl.BlockSpec((2, half, CMP_HIDDEN), lambda g, b: (0, 0, 0)),
                  pl.BlockSpec((2, 8, half), lambda g, b: (0, 0, 0)),
                  pl.BlockSpec((CMP_HIDDEN, NSA_HEAD_DIM), lambda g, b: (0, 0))],
        out_specs=out_spec,
        out_shape=out_shape,
        compiler_params=_cparams(("parallel", "parallel")),
        name="nsa_compress",
    )(r, w1, pos8, w2)


NSA_COLS = NSA_GROUP * Q_BLOCK


def _load_qt(q_ref):
    return jnp.concatenate([q_ref[0, h] for h in range(NSA_GROUP)], 1)


def _store_heads(o_t, out_ref):
    for h in range(NSA_GROUP):
        out_ref[:, h * NSA_HEAD_DIM:(h + 1) * NSA_HEAD_DIM] = (
            o_t[:, h * Q_BLOCK:(h + 1) * Q_BLOCK].T.astype(out_ref.dtype))


def _tile_heads(x):
    return jnp.concatenate([x] * NSA_GROUP, 1)


def _nsa_cmp_kernel(q_ref, kc_ref, vct_ref, ovt_ref, oc_ref, nb_ref, *, n_slc, topn):
    i = pl.program_id(2)
    nc = kc_ref.shape[1]
    qt = _load_qt(q_ref)
    t = i * Q_BLOCK + lax.broadcasted_iota(I32, (1, Q_BLOCK), 1)
    c_idx = lax.broadcasted_iota(I32, (nc, 1), 0)
    valid = (c_idx * CMP_STRIDE + CMP_BLOCK - 1 <= t) & (c_idx < nc - 1)
    s = _dot(kc_ref[0], qt) + _tile_heads(jnp.where(valid, 0.0, NEG))
    m = jnp.max(s, 0, keepdims=True)
    p = jnp.where(m > 0.5 * NEG, jnp.exp(s - m), 0.0)
    p = p * (1.0 / jnp.maximum(jnp.sum(p, 0, keepdims=True), 1e-30))
    _store_heads(_dot(vct_ref[0], p.astype(BF16)), oc_ref)
    psum = p[:, 0:Q_BLOCK]
    for h in range(1, NSA_GROUP):
        psum = psum + p[:, h * Q_BLOCK:(h + 1) * Q_BLOCK]
    imp = sum(_dot(ovt_ref[...], part) for part in _split2(psum))

    blk = lax.broadcasted_iota(I32, (n_slc, Q_BLOCK), 0)
    blk_f = blk.astype(F32)
    cur = t // SLC_BLOCK
    causal_blk = blk <= cur
    forced = (blk == 0) | (blk == cur) | (blk == cur - 1)
    score = jnp.where(causal_blk, jnp.where(forced, jnp.inf, imp), -jnp.inf)
    sel = jnp.zeros((n_slc, Q_BLOCK), F32)
    for _ in range(topn):
        mx = jnp.max(score, 0, keepdims=True)
        first = jnp.min(jnp.where(score == mx, blk_f, float(n_slc)), 0, keepdims=True)
        hit = blk_f == first
        sel = jnp.where(hit, 1.0, sel)
        score = jnp.where(hit, -jnp.inf, score)
    nb_ref[0, 0] = jnp.where(causal_blk & (sel > 0.5), 0.0, NEG)


def _nsa_cmp(qt, kcc, vcct, ovt, B, S):
    T = B * S
    nq = S // Q_BLOCK
    nc = S // CMP_STRIDE
    n_slc = S // SLC_BLOCK
    kern = functools.partial(_nsa_cmp_kernel, n_slc=n_slc, topn=min(SLC_TOPN, n_slc))
    return pl.pallas_call(
        kern,
        grid=(NSA_KV_HEADS, B, nq),
        in_specs=[pl.BlockSpec((1, NSA_GROUP, NSA_HEAD_DIM, Q_BLOCK), lambda g, b, i: (g, 0, 0, b * nq + i)),
                  pl.BlockSpec((1, nc, NSA_HEAD_DIM), lambda g, b, i: (g, b, 0)),
                  pl.BlockSpec((1, NSA_HEAD_DIM, nc), lambda g, b, i: (g, 0, b)),
                  pl.BlockSpec((n_slc, nc), lambda g, b, i: (0, 0))],
        out_specs=[pl.BlockSpec((Q_BLOCK, NSA_GROUP * NSA_HEAD_DIM), lambda g, b, i: (b * nq + i, g)),
                   pl.BlockSpec((1, 1, n_slc, Q_BLOCK), lambda g, b, i: (g, b * nq + i, 0, 0))],
        out_shape=[jax.ShapeDtypeStruct((T, NSA_WIDTH), BF16),
                   jax.ShapeDtypeStruct((NSA_KV_HEADS, B * nq, n_slc, Q_BLOCK), F32)],
        compiler_params=_cparams(("parallel", "parallel", "arbitrary")),
        name="nsa_cmp_select",
    )(qt, kcc, vcct, ovt)


SLC_CHUNK = 512
SLC_CHUNK_BLOCKS = SLC_CHUNK // SLC_BLOCK


def _nsa_slc_kernel(q_ref, nb_ref, ks_ref, vst_ref, os_ref, m_ref, l_ref, acc_ref):
    i = pl.program_id(2)
    qt = _load_qt(q_ref)
    t = i * Q_BLOCK + lax.broadcasted_iota(I32, (1, Q_BLOCK), 1)
    m_ref[...] = jnp.full_like(m_ref, NEG)
    l_ref[...] = jnp.zeros_like(l_ref)
    acc_ref[...] = jnp.zeros_like(acc_ref)

    def step(c, diagonal):
        start = pl.multiple_of(c * SLC_CHUNK, SLC_CHUNK)
        k = ks_ref[0, pl.ds(start, SLC_CHUNK), :]
        vt = vst_ref[0, :, pl.ds(start, SLC_CHUNK)]
        nb8 = nb_ref[0, 0, pl.ds(pl.multiple_of(c * SLC_CHUNK_BLOCKS, SLC_CHUNK_BLOCKS), SLC_CHUNK_BLOCKS), :]
        bias = jnp.concatenate(
            [jnp.broadcast_to(nb8[j:j + 1, :], (SLC_BLOCK, Q_BLOCK)) for j in range(SLC_CHUNK_BLOCKS)], 0)
        if diagonal:
            key = start + lax.broadcasted_iota(I32, (SLC_CHUNK, 1), 0)
            bias = bias + jnp.where(key <= t, 0.0, NEG)
        s = _dot(k, qt) + _tile_heads(bias)
        m_old = m_ref[...]
        m_new = jnp.maximum(m_old, jnp.max(s, 0, keepdims=True))
        alpha = jnp.exp(m_old - m_new)
        p = jnp.exp(s - m_new)
        l_ref[...] = alpha * l_ref[...] + jnp.sum(p, 0, keepdims=True)
        acc_ref[...] = alpha * acc_ref[...] + _dot(vt, p.astype(BF16))
        m_ref[...] = m_new

    n_full = (i * Q_BLOCK) // SLC_CHUNK

    def body(c, carry):
        step(c, False)
        return carry

    lax.fori_loop(0, n_full, body, 0)
    step(n_full, True)
    ok = m_ref[...] > 0.5 * NEG
    inv = jnp.where(ok, 1.0 / jnp.where(ok, l_ref[...], 1.0), 0.0)
    _store_heads(acc_ref[...] * inv, os_ref)


def _nsa_slc(qt, nb, ks, vst, B, S):
    T = B * S
    nq = S // Q_BLOCK
    n_slc = S // SLC_BLOCK
    return pl.pallas_call(
        _nsa_slc_kernel,
        grid=(NSA_KV_HEADS, B, nq),
        in_specs=[pl.BlockSpec((1, NSA_GROUP, NSA_HEAD_DIM, Q_BLOCK), lambda g, b, i: (g, 0, 0, b * nq + i)),
                  pl.BlockSpec((1, 1, n_slc, Q_BLOCK), lambda g, b, i: (g, b * nq + i, 0, 0)),
                  pl.BlockSpec((1, S, NSA_HEAD_DIM), lambda g, b, i: (g, b, 0)),
                  pl.BlockSpec((1, NSA_HEAD_DIM, S), lambda g, b, i: (g, 0, b))],
        out_specs=pl.BlockSpec((Q_BLOCK, NSA_GROUP * NSA_HEAD_DIM), lambda g, b, i: (b * nq + i, g)),
        out_shape=jax.ShapeDtypeStruct((T, NSA_WIDTH), BF16),
        scratch_shapes=[pltpu.VMEM((1, NSA_COLS), F32), pltpu.VMEM((1, NSA_COLS), F32),
                        pltpu.VMEM((NSA_HEAD_DIM, NSA_COLS), F32)],
        compiler_params=_cparams(("parallel", "parallel", "arbitrary")),
        name="nsa_selected",
    )(qt, nb, ks, vst)


N_WIN_BLOCKS = WINDOW // Q_BLOCK + 1


def _nsa_win_kernel(q_ref, *refs):
    k_refs = refs[:N_WIN_BLOCKS]
    v_refs = refs[N_WIN_BLOCKS:2 * N_WIN_BLOCKS]
    ow_ref = refs[2 * N_WIN_BLOCKS]
    i = pl.program_id(2)
    qt = _load_qt(q_ref)
    k = jnp.concatenate([r[0] for r in k_refs], 0)
    vt = jnp.concatenate([r[0] for r in v_refs], 1)
    nk = N_WIN_BLOCKS * Q_BLOCK
    t = i * Q_BLOCK + lax.broadcasted_iota(I32, (1, Q_BLOCK), 1)
    wpos = i * Q_BLOCK - WINDOW + lax.broadcasted_iota(I32, (nk, 1), 0)
    ok = (wpos <= t) & (wpos > t - WINDOW) & (wpos >= 0)
    s = _dot(k, qt) + _tile_heads(jnp.where(ok, 0.0, NEG))
    m = jnp.max(s, 0, keepdims=True)
    p = jnp.exp(s - m)
    o_t = _dot(vt, p.astype(BF16)) * (1.0 / jnp.sum(p, 0, keepdims=True))
    _store_heads(o_t, ow_ref)


def _nsa_win(qt, kw, vwt, B, S):
    T = B * S
    nq = S // Q_BLOCK

    def k_spec(c):
        back = N_WIN_BLOCKS - 1 - c
        return pl.BlockSpec((1, Q_BLOCK, NSA_HEAD_DIM),
                            lambda g, b, i: (g, b * nq + jnp.maximum(i - back, 0), 0))

    def v_spec(c):
        back = N_WIN_BLOCKS - 1 - c
        return pl.BlockSpec((1, NSA_HEAD_DIM, Q_BLOCK),
                            lambda g, b, i: (g, 0, b * nq + jnp.maximum(i - back, 0)))

    return pl.pallas_call(
        _nsa_win_kernel,
        grid=(NSA_KV_HEADS, B, nq),
        in_specs=[pl.BlockSpec((1, NSA_GROUP, NSA_HEAD_DIM, Q_BLOCK), lambda g, b, i: (g, 0, 0, b * nq + i))]
        + [k_spec(c) for c in range(N_WIN_BLOCKS)] + [v_spec(c) for c in range(N_WIN_BLOCKS)],
        out_specs=pl.BlockSpec((Q_BLOCK, NSA_GROUP * NSA_HEAD_DIM), lambda g, b, i: (b * nq + i, g)),
        out_shape=jax.ShapeDtypeStruct((T, NSA_WIDTH), BF16),
        compiler_params=_cparams(("parallel", "parallel", "arbitrary")),
        name="nsa_window",
    )(qt, *([kw] * N_WIN_BLOCKS), *([vwt] * N_WIN_BLOCKS))


def _outproj_kernel(hm_ref, oc_ref, os_ref, ow_ref, gt_ref, gb_ref, ge_ref, h_ref, w_ref, g_ref, b_ref,
                    out_ref):
    gates = jax.nn.sigmoid(gt_ref[...] + gb_ref[...]).astype(BF16)
    gx = _dot(gates, ge_ref[...])
    hn = (gx[:, 0:NSA_WIDTH] * oc_ref[...].astype(F32)
          + gx[:, NSA_WIDTH:2 * NSA_WIDTH] * os_ref[...].astype(F32)
          + gx[:, 2 * NSA_WIDTH:3 * NSA_WIDTH] * ow_ref[...].astype(F32))
    mix = (_dot(hm_ref[...], w_ref[0:MLSTM_WIDTH, :])
           + _dot(hn.astype(BF16), w_ref[MLSTM_WIDTH:MLSTM_WIDTH + NSA_WIDTH, :]))
    out_ref[...] = _layer_norm(DN_ALPHA * h_ref[...] + mix, g_ref[...], b_ref[...])


def _outproj(hm, oc, os_, ow, u, gate_b, gate_expand, h, w_out, g, b):
    T = h.shape[0]
    tm = 256
    row = lambda i: (i, 0)
    fixed = lambda i: (0, 0)
    return pl.pallas_call(
        _outproj_kernel,
        grid=(T // tm,),
        in_specs=[pl.BlockSpec((tm, MLSTM_WIDTH), row), pl.BlockSpec((tm, NSA_WIDTH), row),
                  pl.BlockSpec((tm, NSA_WIDTH), row), pl.BlockSpec((tm, NSA_WIDTH), row),
                  pl.BlockSpec((tm, LANE), lambda i: (i, U_GATE // LANE)),
                  pl.BlockSpec((1, LANE), fixed),
                  pl.BlockSpec((LANE, 3 * NSA_WIDTH), fixed),
                  pl.BlockSpec((tm, D_MODEL), row),
                  pl.BlockSpec((D_MODEL, D_MODEL), fixed),
                  pl.BlockSpec((1, D_MODEL), fixed), pl.BlockSpec((1, D_MODEL), fixed)],
        out_specs=pl.BlockSpec((tm, D_MODEL), row),
        out_shape=jax.ShapeDtypeStruct((T, D_MODEL), F32),
        compiler_params=_cparams(("parallel",)),
        name="mixer_outproj_ln",
    )(hm, oc, os_, ow, u, gate_b, gate_expand, h, w_out, g, b)


def _matmul_kernel(x_ref, w_ref, o_ref):
    o_ref[...] = _dot(x_ref[...].astype(BF16), w_ref[...]).astype(o_ref.dtype)


def _mem_kv(mem2, wkv):
    M = mem2.shape[0]
    N = wkv.shape[1]
    tn = 512
    return pl.pallas_call(
        _matmul_kernel,
        grid=(N // tn,),
        in_specs=[pl.BlockSpec((M, D_MODEL), lambda j: (0, 0)),
                  pl.BlockSpec((D_MODEL, tn), lambda j: (0, j))],
        out_specs=pl.BlockSpec((M, tn), lambda j: (0, j)),
        out_shape=jax.ShapeDtypeStruct((M, N), BF16),
        compiler_params=_cparams(("parallel",)),
        name="mem_kv_proj",
    )(mem2, wkv)


def _xattn_kernel(h_ref, wq_ref, kv_ref, o_ref):
    hb = h_ref[...].astype(BF16)
    for hd in range(XA_HEADS):
        c0 = hd * XA_HEAD_DIM
        q = (_dot(hb, wq_ref[:, c0:c0 + XA_HEAD_DIM]) * (XA_HEAD_DIM ** -0.5)).astype(BF16)
        s = _dot_nt(q, kv_ref[:, c0:c0 + XA_HEAD_DIM])
        m = jnp.max(s, -1, keepdims=True)
        p = jnp.exp(s - m)
        o = _dot(p.astype(BF16), kv_ref[:, D_MODEL + c0:D_MODEL + c0 + XA_HEAD_DIM])
        o_ref[:, c0:c0 + XA_HEAD_DIM] = (o / jnp.sum(p, -1, keepdims=True)).astype(o_ref.dtype)


def _xattn(h1, wq, kv, B, S):
    T = B * S
    tm = 256
    n_mem = kv.shape[0] // B
    per_b = S // tm
    return pl.pallas_call(
        _xattn_kernel,
        grid=(T // tm,),
        in_specs=[pl.BlockSpec((tm, D_MODEL), lambda i: (i, 0)),
                  pl.BlockSpec((D_MODEL, D_MODEL), lambda i: (0, 0)),
                  pl.BlockSpec((n_mem, 2 * D_MODEL), lambda i: (i // per_b, 0))],
        out_specs=pl.BlockSpec((tm, D_MODEL), lambda i: (i, 0)),
        out_shape=jax.ShapeDtypeStruct((T, D_MODEL), BF16),
        compiler_params=_cparams(("parallel",)),
        name="mem_xattn",
    )(h1, wq, kv)


def _xa_out_router_kernel(o_ref, wo_ref, h_ref, g_ref, b_ref, rw_ref, rb_ref,
                          h2_ref, idx_ref, wgt_ref):
    xa = _dot(o_ref[...], wo_ref[...])
    h2 = _layer_norm(DN_ALPHA * h_ref[...] + xa, g_ref[...], b_ref[...])
    h2_ref[...] = h2
    hi, lo = _split2(h2)
    logits = _dot(hi, rw_ref[0]) + _dot(hi, rw_ref[1]) + _dot(lo, rw_ref[0])
    scores = jax.nn.sigmoid(logits)
    lane = lax.broadcasted_iota(I32, scores.shape, 1)
    lane_f = lane.astype(F32)
    biased = jnp.where(lane < N_EXPERTS, scores + rb_ref[...], -jnp.inf)
    idx_mat = jnp.zeros(scores.shape, F32)
    w_mat = jnp.zeros(scores.shape, F32)
    for kk in range(TOP_K):
        mx = jnp.max(biased, -1, keepdims=True)
        first = jnp.min(jnp.where(biased == mx, lane_f, float(LANE)), -1, keepdims=True)
        hit = lane_f == first
        top_s = jnp.sum(jnp.where(hit, scores, 0.0), -1, keepdims=True)
        idx_mat = jnp.where(lane == kk, first, idx_mat)
        w_mat = jnp.where(lane == kk, top_s, w_mat)
        biased = jnp.where(hit, -jnp.inf, biased)
    idx_ref[...] = idx_mat.astype(I32)
    wgt_ref[...] = w_mat / jnp.sum(w_mat, -1, keepdims=True) * ROUTED_SCALE


def _xa_out_router(o, wo, h1, g, b, rw2, rb):
    T = h1.shape[0]
    tm = 256
    row = lambda i: (i, 0)
    fixed = lambda i: (0, 0)
    return pl.pallas_call(
        _xa_out_router_kernel,
        grid=(T // tm,),
        in_specs=[pl.BlockSpec((tm, D_MODEL), row),
                  pl.BlockSpec((D_MODEL, D_MODEL), fixed),
                  pl.BlockSpec((tm, D_MODEL), row),
                  pl.BlockSpec((1, D_MODEL), fixed), pl.BlockSpec((1, D_MODEL), fixed),
                  pl.BlockSpec((2, D_MODEL, LANE), lambda i: (0, 0, 0)),
                  pl.BlockSpec((1, LANE), fixed)],
        out_specs=[pl.BlockSpec((tm, D_MODEL), row), pl.BlockSpec((tm, LANE), row),
                   pl.BlockSpec((tm, LANE), row)],
        out_shape=[jax.ShapeDtypeStruct((T, D_MODEL), F32),
                   jax.ShapeDtypeStruct((T, LANE), I32),
                   jax.ShapeDtypeStruct((T, LANE), F32)],
        compiler_params=_cparams(("parallel",)),
        name="xattn_out_ln_router",
    )(o, wo, h1, g, b, rw2, rb)


ROUTE_TM = 512
ZERO_ROWS = 128


def _route_rank_kernel(idx_ref, rank_ref, cnt_ref):
    tm = idx_ref.shape[0]

    @pl.when(pl.program_id(0) == 0)
    def _():
        cnt_ref[...] = jnp.zeros_like(cnt_ref)

    idx = idx_ref[...]
    lane = lax.broadcasted_iota(I32, (tm, LANE), 1)
    hits = [lane == idx[:, kk:kk + 1] for kk in range(TOP_K)]
    onehot = sum(jnp.where(hit, 1.0, 0.0) for hit in hits)
    r_i = lax.broadcasted_iota(I32, (tm, tm), 0)
    c_i = lax.broadcasted_iota(I32, (tm, tm), 1)
    before = jnp.where(r_i > c_i, 1.0, 0.0).astype(BF16)
    rank = _dot(before, onehot.astype(BF16)) + cnt_ref[0:1, :]
    out = jnp.zeros((tm, LANE), F32)
    for kk in range(TOP_K):
        out = jnp.where(lane == kk, jnp.sum(jnp.where(hits[kk], rank, 0.0), -1, keepdims=True), out)
    rank_ref[...] = out.astype(I32)
    cnt_ref[0:1, :] = cnt_ref[0:1, :] + jnp.sum(onehot, 0, keepdims=True)


def _route_rank(top_idx):
    T = top_idx.shape[0]
    tm = ROUTE_TM
    return pl.pallas_call(
        _route_rank_kernel,
        grid=(T // tm,),
        in_specs=[pl.BlockSpec((tm, LANE), lambda i: (i, 0))],
        out_specs=[pl.BlockSpec((tm, LANE), lambda i: (i, 0)),
                   pl.BlockSpec((8, LANE), lambda i: (0, 0))],
        out_shape=[jax.ShapeDtypeStruct((T, LANE), I32), jax.ShapeDtypeStruct((8, LANE), F32)],
        compiler_params=_cparams(("arbitrary",)),
        name="moe_route_rank",
    )(top_idx)


def _load_route(i, idx_hbm, rank_hbm, idx_smem, rank_smem, isem):
    c0 = pltpu.make_async_copy(idx_hbm.at[i], idx_smem, isem.at[0])
    c1 = pltpu.make_async_copy(rank_hbm.at[i], rank_smem, isem.at[1])
    c0.start()
    c1.start()
    c0.wait()
    c1.wait()


_FILL_SIZES = tuple(s for s in (ZERO_ROWS >> n for n in range(ZERO_ROWS.bit_length())) if s >= SUBLANE)


def _dispatch_kernel(ps_ref, fs_ref, fn_ref, tail_ref, idx_hbm, rank_hbm, h_ref, xs_ref,
                     idx_smem, rank_smem, zero_ref, isem, sem, zsem):
    i = pl.program_id(0)
    _load_route(i, idx_hbm, rank_hbm, idx_smem, rank_smem, isem)

    def issue(r, carry):
        for kk in range(TOP_K):
            j = r * TOP_K + kk
            d = ps_ref[idx_smem[j]] + rank_smem[j]
            pltpu.make_async_copy(h_ref.at[pl.ds(r, 1)], xs_ref.at[pl.ds(d, 1)], sem).start()
        return carry

    lax.fori_loop(0, ROUTE_TM, issue, 0)

    @pl.when(i == 0)
    def _():
        zero_ref[...] = jnp.zeros_like(zero_ref)

        def fill_copies(e, go):
            n = fn_ref[e]
            first = fs_ref[e]
            lead = jnp.minimum((-first) & (SUBLANE - 1), n)
            for r in range(SUBLANE - 1):
                @pl.when(r < lead)
                def _():
                    go(pltpu.make_async_copy(zero_ref.at[pl.ds(0, 1)], xs_ref.at[pl.ds(first + r, 1)], zsem))

            rest = n - lead
            off = first + lead
            for size in _FILL_SIZES:
                take = (rest & size) != 0

                @pl.when(take)
                def _():
                    dst = xs_ref.at[pl.ds(pl.multiple_of(off, SUBLANE), size)]
                    go(pltpu.make_async_copy(zero_ref.at[pl.ds(0, size)], dst, zsem))
                off = off + jnp.where(take, size, 0)

        def tail_copies(j, go):
            off = pl.multiple_of(tail_ref[0] + j * ZERO_ROWS, ZERO_ROWS)
            go(pltpu.make_async_copy(zero_ref, xs_ref.at[pl.ds(off, ZERO_ROWS)], zsem))

        for go in (lambda cp: cp.start(), lambda cp: cp.wait()):
            lax.fori_loop(0, N_EXPERTS, lambda e, c, go=go: (fill_copies(e, go), c)[1], 0)
            lax.fori_loop(0, tail_ref[1], lambda j, c, go=go: (tail_copies(j, go), c)[1], 0)

    for _ in range(TOP_K):
        pltpu.make_async_copy(h_ref, xs_ref.at[pl.ds(0, ROUTE_TM)], sem).wait()


def _dispatch(pad_start, fill_start, fill_n, tail, idx2, rank2, h2, P):
    T = h2.shape[0]
    tm = ROUTE_TM
    return pl.pallas_call(
        _dispatch_kernel,
        grid_spec=pltpu.PrefetchScalarGridSpec(
            num_scalar_prefetch=4,
            grid=(T // tm,),
            in_specs=[pl.BlockSpec(memory_space=pl.ANY), pl.BlockSpec(memory_space=pl.ANY),
                      pl.BlockSpec((tm, D_MODEL), lambda i, *_: (i, 0))],
            out_specs=pl.BlockSpec(memory_space=pl.ANY),
            scratch_shapes=[pltpu.SMEM((tm * TOP_K,), I32), pltpu.SMEM((tm * TOP_K,), I32),
                            pltpu.VMEM((ZERO_ROWS, D_MODEL), F32),
                            pltpu.SemaphoreType.DMA((2,)), pltpu.SemaphoreType.DMA, pltpu.SemaphoreType.DMA]),
        out_shape=jax.ShapeDtypeStruct((P, D_MODEL), F32),
        compiler_params=_cparams(("arbitrary",)),
        name="moe_dispatch",
    )(pad_start, fill_start, fill_n, tail, idx2, rank2, h2)


def _experts_kernel(be_ref, nu_ref, x_ref, wg_ref, wu_ref, wd_ref, y_ref):
    used = pl.program_id(0) < nu_ref[0]

    @pl.when(used)
    def _():
        xb = x_ref[...].astype(BF16)
        a = _dot(xb, wg_ref[0])
        act = (a * jax.nn.sigmoid(a)) * _dot(xb, wu_ref[0])
        y_ref[...] = _dot(act.astype(BF16), wd_ref[0])

    @pl.when(jnp.logical_not(used))
    def _():
        y_ref[...] = jnp.zeros_like(y_ref)


def _experts(block_e, n_used, xs, wg, wu, wd):
    P = xs.shape[0]
    bm = EXPERT_BM
    rowmap = lambda j, be, nu: (jnp.minimum(j, nu[0] - 1), 0)
    wmap = lambda j, be, nu: (be[j], 0, 0)
    return pl.pallas_call(
        _experts_kernel,
        grid_spec=pltpu.PrefetchScalarGridSpec(
            num_scalar_prefetch=2,
            grid=(P // bm,),
            in_specs=[pl.BlockSpec((bm, D_MODEL), rowmap),
                      pl.BlockSpec((1, D_MODEL, D_EXPERT), wmap),
                      pl.BlockSpec((1, D_MODEL, D_EXPERT), wmap),
                      pl.BlockSpec((1, D_EXPERT, D_MODEL), wmap)],
            out_specs=pl.BlockSpec((bm, D_MODEL), lambda j, be, nu: (j, 0))),
        out_shape=jax.ShapeDtypeStruct((P, D_MODEL), F32),
        compiler_params=_cparams(("arbitrary",)),
        name="moe_experts",
    )(block_e, n_used, xs, wg, wu, wd)


def _shared_ffn_kernel(x_ref, wg_ref, wu_ref, wd_ref, o_ref, xb_ref):
    f = pl.program_id(1)

    @pl.when(f == 0)
    def _():
        xb_ref[...] = x_ref[...].astype(BF16)
        o_ref[...] = jnp.zeros_like(o_ref)

    xb = xb_ref[...]
    a = _dot(xb, wg_ref[...])
    act = (a * jax.nn.sigmoid(a)) * _dot(xb, wu_ref[...])
    o_ref[...] += _dot(act.astype(BF16), wd_ref[...])


def _shared_ffn(h2, wg, wu, wd):
    T = h2.shape[0]
    tm, tf = 512, 256
    return pl.pallas_call(
        _shared_ffn_kernel,
        grid=(T // tm, D_SHARED // tf),
        in_specs=[pl.BlockSpec((tm, D_MODEL), lambda i, f: (i, 0)),
                  pl.BlockSpec((D_MODEL, tf), lambda i, f: (0, f)),
                  pl.BlockSpec((D_MODEL, tf), lambda i, f: (0, f)),
                  pl.BlockSpec((tf, D_MODEL), lambda i, f: (f, 0))],
        out_specs=pl.BlockSpec((tm, D_MODEL), lambda i, f: (i, 0)),
        out_shape=jax.ShapeDtypeStruct((T, D_MODEL), F32),
        scratch_shapes=[pltpu.VMEM((tm, D_MODEL), BF16)],
        compiler_params=_cparams(("parallel", "arbitrary")),
        name="shared_ffn",
    )(h2, wg, wu, wd)


COMBINE_SUB = 128
COMBINE_NSUB = ROUTE_TM // COMBINE_SUB


def _combine_kernel(ps_ref, idx_hbm, rank_hbm, ys_hbm, w_ref, sh_ref, h_ref, g_ref, b_ref, out_ref,
                    idx_smem, rank_smem, gbuf, isem, sem):
    i = pl.program_id(0)
    _load_route(i, idx_hbm, rank_hbm, idx_smem, rank_smem, isem)

    def gather(sub):
        slot = sub % 2

        def issue(r, carry):
            for kk in range(TOP_K):
                j = (sub * COMBINE_SUB + r) * TOP_K + kk
                d = ps_ref[idx_smem[j]] + rank_smem[j]
                pltpu.make_async_copy(ys_hbm.at[pl.ds(d, 1)], gbuf.at[slot, kk, pl.ds(r, 1)],
                                      sem.at[slot]).start()
            return carry

        lax.fori_loop(0, COMBINE_SUB, issue, 0)

    gather(0)
    for sub in range(COMBINE_NSUB):
        slot = sub % 2
        if sub + 1 < COMBINE_NSUB:
            gather(sub + 1)
        for kk in range(TOP_K):
            pltpu.make_async_copy(ys_hbm.at[pl.ds(0, COMBINE_SUB)], gbuf.at[slot, kk], sem.at[slot]).wait()
        rs = slice(sub * COMBINE_SUB, (sub + 1) * COMBINE_SUB)
        w = w_ref[rs, :]
        routed = w[:, 0:1] * gbuf[slot, 0]
        for kk in range(1, TOP_K):
            routed = routed + w[:, kk:kk + 1] * gbuf[slot, kk]
        z = DN_ALPHA * h_ref[rs, :] + (routed + sh_ref[rs, :])
        out_ref[rs, :] = _layer_norm(z, g_ref[...], b_ref[...])


def _combine(pad_start, idx2, rank2, ys, top_w, sh, h2, g, b):
    T = h2.shape[0]
    tm = ROUTE_TM
    row = lambda i, *_: (i, 0)
    fixed = lambda i, *_: (0, 0)
    return pl.pallas_call(
        _combine_kernel,
        grid_spec=pltpu.PrefetchScalarGridSpec(
            num_scalar_prefetch=1,
            grid=(T // tm,),
            in_specs=[pl.BlockSpec(memory_space=pl.ANY), pl.BlockSpec(memory_space=pl.ANY),
                      pl.BlockSpec(memory_space=pl.ANY),
                      pl.BlockSpec((tm, LANE), row), pl.BlockSpec((tm, D_MODEL), row),
                      pl.BlockSpec((tm, D_MODEL), row),
                      pl.BlockSpec((1, D_MODEL), fixed), pl.BlockSpec((1, D_MODEL), fixed)],
            out_specs=pl.BlockSpec((tm, D_MODEL), row),
            scratch_shapes=[pltpu.SMEM((tm * TOP_K,), I32), pltpu.SMEM((tm * TOP_K,), I32),
                            pltpu.VMEM((2, TOP_K, COMBINE_SUB, D_MODEL), F32),
                            pltpu.SemaphoreType.DMA((2,)), pltpu.SemaphoreType.DMA((2,))]),
        out_shape=jax.ShapeDtypeStruct((T, D_MODEL), F32),
        compiler_params=_cparams(("arbitrary",)),
        name="moe_combine_ln",
    )(pad_start, idx2, rank2, ys, top_w, sh, h2, g, b)


def _route_plan(counts, T):
    bm = EXPERT_BM
    padded = (counts + bm - 1) // bm * bm
    pad_end = jnp.cumsum(padded)
    pad_start = pad_end - padded
    n_blocks = T * TOP_K // bm + N_EXPERTS
    block_e = jnp.minimum(jnp.searchsorted(pad_end, jnp.arange(n_blocks, dtype=I32) * bm, side='right'),
                          N_EXPERTS - 1).astype(I32)
    n_used = (pad_end[-1:] // bm).astype(I32)
    tail = jnp.concatenate([pad_end[-1:], (n_blocks - n_used) * (bm // ZERO_ROWS)]).astype(I32)
    return (pad_start.astype(I32), (pad_start + counts).astype(I32), (padded - counts).astype(I32), tail,
            block_e, n_used, n_blocks * bm)


def _overlap_matrix(S):
    n_cmp_rows = S // CMP_STRIDE
    n_slc = S // SLC_BLOCK
    c_lo = np.arange(n_cmp_rows)[:, None] * CMP_STRIDE
    j_lo = np.arange(n_slc)[None, :] * SLC_BLOCK
    ov = (c_lo <= j_lo + SLC_BLOCK - 1) & (c_lo + CMP_BLOCK - 1 >= j_lo)
    return jnp.asarray(ov.T.astype(np.float32), dtype=BF16)


def _gate_expand_matrix():
    ge = np.zeros((LANE, 3 * NSA_WIDTH), np.float32)
    for hh in range(NSA_HEADS):
        for br in range(3):
            ge[8 + hh * 3 + br, br * NSA_WIDTH + hh * NSA_HEAD_DIM:br * NSA_WIDTH + (hh + 1) * NSA_HEAD_DIM] = 1.0
    return jnp.asarray(ge, dtype=BF16)


def kernel(x, mem, positions, ln0_g, ln0_b, w_in, conv_w, conv_b, igate_b, fgate_b, mlstm_norm_g, cmp_pos, cmp_w1k, cmp_w2k, cmp_w1v, cmp_w2v, nsa_gate_b, w_out, ln1_g, ln1_b, xa_wq, xa_wk, xa_wv, xa_wo, ln2_g, ln2_b, router_w, router_bias, moe_w_gate, moe_w_up, moe_w_down, sh_w_gate, sh_w_up, sh_w_down, ln3_g, ln3_b):
    B, S, D = x.shape
    T = B * S
    assert D == D_MODEL and w_in.shape[0] == DEPTH == 1
    assert S % MLSTM_L == 0 and S % 512 == 0 and T % ROUTE_TM == 0 and (T * TOP_K) % EXPERT_BM == 0
    row = lambda a: a.reshape(1, -1)

    w = w_in[0]
    w_r = jnp.concatenate([w[:, :3072], w[:, 3080:5640], w[:, 3072:3080], w[:, 5640:5664],
                           jnp.zeros((D, U_COLS - 5664), F32)], 1).astype(BF16)
    gate_b = jnp.concatenate([igate_b[0], fgate_b[0], nsa_gate_b[0],
                              jnp.zeros((LANE - 2 * MLSTM_HEADS - 3 * NSA_HEADS,), F32)]).reshape(1, LANE)
    half = NSA_HEAD_DIM // 2
    inv_freq = ROPE_THETA ** (-jnp.arange(half, dtype=F32) / half)
    invf2 = jnp.concatenate([inv_freq, inv_freq]).reshape(1, LANE)
    sgn = jnp.concatenate([-jnp.ones((half,), F32), jnp.ones((half,), F32)]).reshape(1, LANE)
    pos8 = jnp.zeros((2, 8, CMP_STRIDE * NSA_HEAD_DIM), F32).at[:, 0, :].set(
        cmp_pos[0].reshape(2, CMP_STRIDE * NSA_HEAD_DIM)).astype(BF16)
    w1k = cmp_w1k[0].reshape(2, CMP_STRIDE * NSA_HEAD_DIM, CMP_HIDDEN).astype(BF16)
    w1v = cmp_w1v[0].reshape(2, CMP_STRIDE * NSA_HEAD_DIM, CMP_HIDDEN).astype(BF16)
    rw = jnp.pad(router_w[0], ((0, 0), (0, LANE - N_EXPERTS)))
    rw_hi = rw.astype(BF16)
    rw2 = jnp.stack([rw_hi, (rw - rw_hi.astype(F32)).astype(BF16)])
    rb = jnp.pad(router_bias[0], (0, LANE - N_EXPERTS)).reshape(1, LANE)

    h, u = _ln_inproj(x.reshape(T, D), row(ln0_g), row(ln0_b), w_r)
    hm = _mlstm(u, conv_w[0], row(conv_b[0]), gate_b, row(mlstm_norm_g[0]), B, S)
    qt, kc, vc, ks, vst, kw, vwt = _nsa_prep(u, positions.reshape(T, 1), invf2, sgn, T)
    kcc = _compress(kc, w1k, pos8, cmp_w2k[0].astype(BF16), B, S, False)
    vcct = _compress(vc, w1v, pos8, cmp_w2v[0].astype(BF16), B, S, True)
    oc, nb = _nsa_cmp(qt, kcc, vcct, _overlap_matrix(S), B, S)
    os_ = _nsa_slc(qt, nb, ks, vst, B, S)
    ow = _nsa_win(qt, kw, vwt, B, S)
    h1 = _outproj(hm, oc, os_, ow, u, gate_b, _gate_expand_matrix(), h, w_out[0].astype(BF16),
                  row(ln1_g[0]), row(ln1_b[0]))

    wkv = jnp.concatenate([xa_wk[0], xa_wv[0]], 1).astype(BF16)
    kv = _mem_kv(mem.reshape(-1, D), wkv)
    xo = _xattn(h1, xa_wq[0].astype(BF16), kv, B, S)
    h2, top_idx, top_w = _xa_out_router(xo, xa_wo[0].astype(BF16), h1, row(ln2_g[0]), row(ln2_b[0]), rw2, rb)

    rank, cnt = _route_rank(top_idx)
    counts = cnt[0, :N_EXPERTS].astype(I32)
    pad_start, fill_start, fill_n, tail, block_e, n_used, P = _route_plan(counts, T)
    idx2 = top_idx[:, :TOP_K].reshape(T // ROUTE_TM, ROUTE_TM * TOP_K)
    rank2 = rank[:, :TOP_K].reshape(T // ROUTE_TM, ROUTE_TM * TOP_K)
    xs = _dispatch(pad_start, fill_start, fill_n, tail, idx2, rank2, h2, P)
    ys = _experts(block_e, n_used, xs, moe_w_gate[0].astype(BF16), moe_w_up[0].astype(BF16),
                  moe_w_down[0].astype(BF16))
    sh = _shared_ffn(h2, sh_w_gate[0].astype(BF16), sh_w_up[0].astype(BF16), sh_w_down[0].astype(BF16))
    out = _combine(pad_start, idx2, rank2, ys, top_w, sh, h2, row(ln3_g[0]), row(ln3_b[0]))
    return out.reshape(B, S, D)
```

```python
import functools

import numpy as np
import jax
import jax.numpy as jnp
from jax import lax
from jax.experimental import pallas as pl
from jax.experimental.pallas import tpu as pltpu

F32 = jnp.float32
BF16 = jnp.bfloat16
I32 = jnp.int32

D_MODEL = 2048
MLSTM_HEADS = 4
MLSTM_DV = 256
MLSTM_DQK = 128
MLSTM_QK = MLSTM_HEADS * MLSTM_DQK
MLSTM_WIDTH = MLSTM_HEADS * MLSTM_DV
CONV_WIDTH = 4
NSA_HEAD_DIM = 128
NSA_HEADS = 8
NSA_KV_HEADS = 2
NSA_GROUP = 4
NSA_WIDTH = NSA_HEADS * NSA_HEAD_DIM
CMP_BLOCK = 32
CMP_STRIDE = 16
CMP_HIDDEN = 256
SLC_BLOCK = 64
SLC_TOPN = 16
WINDOW = 512
Q_BLOCK = 128
XA_HEADS = 4
XA_HEAD_DIM = 512
N_EXPERTS = 64
TOP_K = 6
D_EXPERT = 1408
D_SHARED = 2816
ROUTED_SCALE = 2.446
ROPE_THETA = 10000.0
LN_EPS = 1e-5
DEPTH = 1
DN_ALPHA = (2.0 * DEPTH) ** 0.25

U_MLSTM = 0
U_NSA = 3072
U_COLS = 5632
GATE_COLS = 128

LANE = 128
SUBLANE = 8
NEG = -1e30
MLSTM_L = 256
VMEM_LIMIT = 56 * 1024 * 1024
EXPERT_BM = 256


def _cparams(sem):
    return pltpu.CompilerParams(dimension_semantics=sem, vmem_limit_bytes=VMEM_LIMIT)


def _dot(a, b):
    return jnp.dot(a, b, preferred_element_type=F32)


def _dot_nt(a, b):
    return lax.dot_general(a, b, (((1,), (1,)), ((), ())), preferred_element_type=F32)


def _layer_norm(z, g, b):
    mu = jnp.mean(z, -1, keepdims=True)
    zc = z - mu
    var = jnp.mean(zc * zc, -1, keepdims=True)
    return zc * lax.rsqrt(var + LN_EPS) * g + b


def _split3(x):
    hi = x.astype(BF16)
    r = x - hi.astype(F32)
    mid = r.astype(BF16)
    lo = (r - mid.astype(F32)).astype(BF16)
    return hi, mid, lo


def _split2(x):
    hi = x.astype(BF16)
    lo = (x - hi.astype(F32)).astype(BF16)
    return hi, lo


def _ln_inproj_kernel(x_ref, g_ref, b_ref, w_ref, wg_ref, h_ref, u_ref, ug_ref, hb_ref):
    @pl.when(pl.program_id(1) == 0)
    def _():
        hn = _layer_norm(x_ref[...], g_ref[...], b_ref[...])
        h_ref[...] = hn
        hb_ref[...] = hn.astype(BF16)
        ug_ref[...] = _dot(hb_ref[...], wg_ref[...])

    u_ref[...] = _dot(hb_ref[...], w_ref[...])


def _ln_inproj(x2, g, b, w, w_gate):
    T = x2.shape[0]
    tm, tn = 512, 512
    return pl.pallas_call(
        _ln_inproj_kernel,
        grid=(T // tm, U_COLS // tn),
        in_specs=[pl.BlockSpec((tm, D_MODEL), lambda i, j: (i, 0)),
                  pl.BlockSpec((1, D_MODEL), lambda i, j: (0, 0)),
                  pl.BlockSpec((1, D_MODEL), lambda i, j: (0, 0)),
                  pl.BlockSpec((D_MODEL, tn), lambda i, j: (0, j)),
                  pl.BlockSpec((D_MODEL, GATE_COLS), lambda i, j: (0, 0))],
        out_specs=[pl.BlockSpec((tm, D_MODEL), lambda i, j: (i, 0)),
                   pl.BlockSpec((tm, tn), lambda i, j: (i, j)),
                   pl.BlockSpec((tm, GATE_COLS), lambda i, j: (i, 0))],
        out_shape=[jax.ShapeDtypeStruct((T, D_MODEL), F32),
                   jax.ShapeDtypeStruct((T, U_COLS), F32),
                   jax.ShapeDtypeStruct((T, GATE_COLS), F32)],
        scratch_shapes=[pltpu.VMEM((tm, D_MODEL), BF16)],
        compiler_params=_cparams(("parallel", "arbitrary")),
        name="ln_inproj",
    )(x2, g, b, w, w_gate)


def _log_sigmoid(x):
    return jnp.minimum(x, 0.0) - jnp.log1p(jnp.exp(-jnp.abs(x)))


def _mlstm_kernel(qk_ref, v_ref, o_ref, gt_ref, cw_ref, cb_ref, gb_ref, ng_ref, out_ref,
                  prev_ref, c_ref, n_ref, m_ref):
    L = MLSTM_L

    @pl.when(pl.program_id(1) == 0)
    def _():
        prev_ref[...] = jnp.zeros_like(prev_ref)
        c_ref[...] = jnp.zeros_like(c_ref)
        n_ref[...] = jnp.zeros_like(n_ref)
        m_ref[...] = jnp.zeros_like(m_ref)

    x = qk_ref[...]
    prev = prev_ref[...]
    row = lax.broadcasted_iota(I32, (L, 1), 0)
    cw = cw_ref[...]
    y = cb_ref[...] + cw[CONV_WIDTH - 1:CONV_WIDTH, :] * x
    for j in range(1, CONV_WIDTH):
        shifted = jnp.where(row < j, pltpu.roll(prev, j, 0), pltpu.roll(x, j, 0))
        y = y + cw[CONV_WIDTH - 1 - j:CONV_WIDTH - j, :] * shifted
    prev_ref[...] = x
    qk = y * jax.nn.sigmoid(y)

    gpre = gt_ref[...] + gb_ref[...]
    gpre_t = gpre.T
    r_i = lax.broadcasted_iota(I32, (L, L), 0)
    c_i = lax.broadcasted_iota(I32, (L, L), 1)
    causal = r_i >= c_i
    tril = jnp.where(causal, 1.0, 0.0).astype(BF16)
    triu = jnp.where(r_i <= c_i, 1.0, 0.0).astype(BF16)
    lf = _log_sigmoid(gpre)
    lf_t = _log_sigmoid(gpre_t)
    b_cols = sum(_dot(tril, part) for part in _split3(lf))
    b_rows = sum(_dot(part, triu) for part in _split3(lf_t))

    for h in range(MLSTM_HEADS):
        i_col = gpre[:, h:h + 1]
        i_row = gpre_t[h:h + 1, :]
        b_col = b_cols[:, MLSTM_HEADS + h:MLSTM_HEADS + h + 1]
        b_row = b_rows[MLSTM_HEADS + h:MLSTM_HEADS + h + 1, :]
        m_prev = m_ref[h:h + 1, 0:1]
        n_prev = n_ref[h:h + 1, :]
        c_prev = c_ref[h]

        q = qk[:, h * MLSTM_DQK:(h + 1) * MLSTM_DQK]
        k = qk[:, MLSTM_QK + h * MLSTM_DQK:MLSTM_QK + (h + 1) * MLSTM_DQK] * (MLSTM_DQK ** -0.5)
        v = v_ref[:, h * MLSTM_DV:(h + 1) * MLSTM_DV].astype(BF16)
        qb = q.astype(BF16)

        dmat = jnp.where(causal, b_col - b_row + i_row, -jnp.inf)
        inter = b_col + m_prev
        m_t = jnp.maximum(inter, jnp.max(dmat, -1, keepdims=True))
        s = _dot_nt(qb, k.astype(BF16)) * jnp.exp(dmat - m_t)
        a_inter = jnp.exp(inter - m_t)
        num = _dot(s.astype(BF16), v) + a_inter * _dot(qb, c_prev.astype(BF16))
        den = jnp.sum(s, -1, keepdims=True) + a_inter * jnp.sum(q * n_prev, -1, keepdims=True)
        hc = num / jnp.maximum(jnp.abs(den), jnp.exp(-m_t))

        mu = jnp.mean(hc, -1, keepdims=True)
        hcc = hc - mu
        var = jnp.mean(hcc * hcc, -1, keepdims=True)
        hn = hcc * lax.rsqrt(var + LN_EPS) * ng_ref[:, h * MLSTM_DV:(h + 1) * MLSTM_DV]
        og = jax.nn.sigmoid(o_ref[:, h * MLSTM_DV:(h + 1) * MLSTM_DV])
        out_ref[:, h * MLSTM_DV:(h + 1) * MLSTM_DV] = (og * hn).astype(out_ref.dtype)

        b_last = b_col[L - 1:L, :]
        g_col = b_last - b_col + i_col
        m_new = jnp.maximum(b_last + m_prev, jnp.max(g_col, 0, keepdims=True))
        kw = k * jnp.exp(g_col - m_new)
        decay = jnp.exp(b_last + m_prev - m_new)
        c_ref[h] = decay * c_prev + _dot(kw.T.astype(BF16), v)
        n_ref[h:h + 1, :] = decay * n_prev + jnp.sum(kw, 0, keepdims=True)
        m_ref[h:h + 1, :] = jnp.broadcast_to(m_new, (1, LANE))


def _mlstm(u, ug, conv_w, conv_b, gate_b, norm_g, B, S):
    T = B * S
    L = MLSTM_L
    nc = S // L
    row = lambda b, c: b * nc + c
    return pl.pallas_call(
        _mlstm_kernel,
        grid=(B, nc),
        in_specs=[pl.BlockSpec((L, 2 * MLSTM_QK), lambda b, c: (row(b, c), 0)),
                  pl.BlockSpec((L, MLSTM_WIDTH), lambda b, c: (row(b, c), 1)),
                  pl.BlockSpec((L, MLSTM_WIDTH), lambda b, c: (row(b, c), 2)),
                  pl.BlockSpec((L, GATE_COLS), lambda b, c: (row(b, c), 0)),
                  pl.BlockSpec((CONV_WIDTH, 2 * MLSTM_QK), lambda b, c: (0, 0)),
                  pl.BlockSpec((1, 2 * MLSTM_QK), lambda b, c: (0, 0)),
                  pl.BlockSpec((1, LANE), lambda b, c: (0, 0)),
                  pl.BlockSpec((1, MLSTM_WIDTH), lambda b, c: (0, 0))],
        out_specs=pl.BlockSpec((L, MLSTM_WIDTH), lambda b, c: (row(b, c), 0)),
        out_shape=jax.ShapeDtypeStruct((T, MLSTM_WIDTH), BF16),
        scratch_shapes=[pltpu.VMEM((L, 2 * MLSTM_QK), F32),
                        pltpu.VMEM((MLSTM_HEADS, MLSTM_DQK, MLSTM_DV), F32),
                        pltpu.VMEM((8, MLSTM_DQK), F32),
                        pltpu.VMEM((8, LANE), F32)],
        compiler_params=_cparams(("parallel", "arbitrary")),
        name="mlstm",
    )(u, u, u, ug, conv_w, conv_b, gate_b, norm_g)


LOG2E = 1.4426950408889634
SLC_CHUNK = 1024
SLC_CHUNK_BLOCKS = SLC_CHUNK // SLC_BLOCK
VT_EXTRA = 16


def _nsa_prep_kernel(q_ref, kv0_ref, kv1_ref, kv2_ref, pos_ref, invf_ref, sgn_ref,
                     qt_ref, kc_ref, vc_ref, ks_ref, vst_ref, kw_ref, vwt_ref):
    ang = pos_ref[...].astype(F32) * invf_ref[...]
    cos = jnp.cos(ang)
    sin = jnp.sin(ang) * sgn_ref[...]

    def rope(x):
        return x * cos + pltpu.roll(x, NSA_HEAD_DIM // 2, 1) * sin

    scale = NSA_HEAD_DIM ** -0.5 * LOG2E
    ts = q_ref.shape[0]
    tok = pl.program_id(0) * ts + lax.broadcasted_iota(I32, (ts, NSA_HEAD_DIM), 0)
    blk_lane = (tok // SLC_BLOCK) % SLC_CHUNK_BLOCKS
    blk_onehot = jnp.where(lax.broadcasted_iota(I32, (ts, NSA_HEAD_DIM), 1) == blk_lane, 1.0, 0.0).astype(BF16)
    ones_rows = jnp.where(lax.broadcasted_iota(I32, (VT_EXTRA, ts), 0) == 0, 1.0, 0.0).astype(BF16)
    for g in range(NSA_KV_HEADS):
        for h in range(NSA_GROUP):
            c0 = (g * NSA_GROUP + h) * NSA_HEAD_DIM
            qt_ref[g, h] = (rope(q_ref[:, c0:c0 + NSA_HEAD_DIM]) * scale).T.astype(BF16)
        c0 = g * NSA_HEAD_DIM
        c1 = NSA_KV_HEADS * NSA_HEAD_DIM + g * NSA_HEAD_DIM
        kc_ref[g] = rope(kv0_ref[:, c0:c0 + NSA_HEAD_DIM]).astype(BF16)
        vc_ref[g] = kv0_ref[:, c1:c1 + NSA_HEAD_DIM].astype(BF16)
        ks_ref[g, :, 0:NSA_HEAD_DIM] = rope(kv1_ref[:, c0:c0 + NSA_HEAD_DIM]).astype(BF16)
        ks_ref[g, :, NSA_HEAD_DIM:2 * NSA_HEAD_DIM] = blk_onehot
        vst_ref[g, 0:NSA_HEAD_DIM, :] = kv1_ref[:, c1:c1 + NSA_HEAD_DIM].T.astype(BF16)
        vst_ref[g, NSA_HEAD_DIM:NSA_HEAD_DIM + VT_EXTRA, :] = ones_rows
        kw_ref[g] = rope(kv2_ref[:, c0:c0 + NSA_HEAD_DIM]).astype(BF16)
        vwt_ref[g] = kv2_ref[:, c1:c1 + NSA_HEAD_DIM].T.astype(BF16)


def _nsa_prep(u, pos_col, invf2, sgn, T):
    ts = 512
    kvw = 2 * NSA_KV_HEADS * NSA_HEAD_DIM
    kv_spec = lambda n: pl.BlockSpec((ts, kvw), lambda i: (i, (U_NSA + NSA_WIDTH) // kvw + n))
    row_out = pl.BlockSpec((NSA_KV_HEADS, ts, NSA_HEAD_DIM), lambda i: (0, i, 0))
    row_shape = jax.ShapeDtypeStruct((NSA_KV_HEADS, T, NSA_HEAD_DIM), BF16)
    col_out = pl.BlockSpec((NSA_KV_HEADS, NSA_HEAD_DIM, ts), lambda i: (0, 0, i))
    col_shape = jax.ShapeDtypeStruct((NSA_KV_HEADS, NSA_HEAD_DIM, T), BF16)
    return pl.pallas_call(
        _nsa_prep_kernel,
        grid=(T // ts,),
        in_specs=[pl.BlockSpec((ts, NSA_WIDTH), lambda i: (i, U_NSA // NSA_WIDTH)),
                  kv_spec(0), kv_spec(1), kv_spec(2),
                  pl.BlockSpec((ts, 1), lambda i: (i, 0)),
                  pl.BlockSpec((1, LANE), lambda i: (0, 0)),
                  pl.BlockSpec((1, LANE), lambda i: (0, 0))],
        out_specs=[pl.BlockSpec((NSA_KV_HEADS, NSA_GROUP, NSA_HEAD_DIM, ts), lambda i: (0, 0, 0, i)),
                   row_out, row_out,
                   pl.BlockSpec((NSA_KV_HEADS, ts, 2 * NSA_HEAD_DIM), lambda i: (0, i, 0)),
                   pl.BlockSpec((NSA_KV_HEADS, NSA_HEAD_DIM + VT_EXTRA, ts), lambda i: (0, 0, i)),
                   row_out, col_out],
        out_shape=[jax.ShapeDtypeStruct((NSA_KV_HEADS, NSA_GROUP, NSA_HEAD_DIM, T), BF16),
                   row_shape, row_shape,
                   jax.ShapeDtypeStruct((NSA_KV_HEADS, T, 2 * NSA_HEAD_DIM), BF16),
                   jax.ShapeDtypeStruct((NSA_KV_HEADS, NSA_HEAD_DIM + VT_EXTRA, T), BF16),
                   row_shape, col_shape],
        compiler_params=_cparams(("parallel",)),
        name="nsa_prep",
    )(u, u, u, u, pos_col, invf2, sgn)


def _gelu_tanh(x):
    return 0.5 * x * (1.0 + jnp.tanh(0.7978845608028654 * (x + 0.044715 * x * x * x)))


def _compress_kernel(r_ref, w1_ref, pos_ref, w2_ref, out_ref, *, transpose_out):
    r = r_ref[0]
    nr = r.shape[0]
    a = _dot(r, w1_ref[0])
    b = _dot(r, w1_ref[1])
    c0 = _dot(pos_ref[0], w1_ref[0]) + _dot(pos_ref[1], w1_ref[1])
    pre = a + pltpu.roll(b, nr - 1, 0) + c0[0:1, :]
    out = _dot(_gelu_tanh(pre).astype(BF16), w2_ref[...])
    row = lax.broadcasted_iota(I32, (nr, 1), 0)
    out = jnp.where(row < nr - 1, out, 0.0)
    out_ref[0] = (out.T if transpose_out else out).astype(out_ref.dtype)


def _compress(kv, w1, pos8, w2, B, S, transpose_out):
    nr = S // CMP_STRIDE
    half = CMP_STRIDE * NSA_HEAD_DIM
    r = kv.reshape(NSA_KV_HEADS, B * nr, half)
    if transpose_out:
        out_spec = pl.BlockSpec((1, NSA_HEAD_DIM, nr), lambda g, b: (g, 0, b))
        out_shape = jax.ShapeDtypeStruct((NSA_KV_HEADS, NSA_HEAD_DIM, B * nr), BF16)
    else:
        out_spec = pl.BlockSpec((1, nr, NSA_HEAD_DIM), lambda g, b: (g, b, 0))
        out_shape = jax.ShapeDtypeStruct((NSA_KV_HEADS, B * nr, NSA_HEAD_DIM), BF16)
    return pl.pallas_call(
        functools.partial(_compress_kernel, transpose_out=transpose_out),
        grid=(NSA_KV_HEADS, B),
        in_specs=[pl.BlockSpec((1, nr, half), lambda g, b: (g, b, 0)),
                  pl.BlockSpec((2, half, CMP_HIDDEN), lambda g, b: (0, 0, 0)),
                  pl.BlockSpec((2, 8, half), lambda g, b: (0, 0, 0)),
                  pl.BlockSpec((CMP_HIDDEN, NSA_HEAD_DIM), lambda g, b: (0, 0))],
        out_specs=out_spec,
        out_shape=out_shape,
        compiler_params=_cparams(("parallel", "parallel")),
        name="nsa_compress",
    )(r, w1, pos8, w2)


NSA_COLS = NSA_GROUP * Q_BLOCK


def _load_qt(q_ref):
    return jnp.concatenate([q_ref[0, h] for h in range(NSA_GROUP)], 1)


def _store_heads(o_t, out_ref):
    for h in range(NSA_GROUP):
        out_ref[:, h * NSA_HEAD_DIM:(h + 1) * NSA_HEAD_DIM] = (
            o_t[:, h * Q_BLOCK:(h + 1) * Q_BLOCK].T.astype(out_ref.dtype))


def _tile_heads(x):
    return jnp.concatenate([x] * NSA_GROUP, 1)


def _nsa_cmp_kernel(q_ref, kc_ref, vct_ref, ovt_ref, oc_ref, nb_ref, *, n_slc, topn):
    i = pl.program_id(2)
    nc = kc_ref.shape[1]
    qt = _load_qt(q_ref)
    t = i * Q_BLOCK + lax.broadcasted_iota(I32, (1, Q_BLOCK), 1)
    c_idx = lax.broadcasted_iota(I32, (nc, 1), 0)
    valid = (c_idx * CMP_STRIDE + CMP_BLOCK - 1 <= t) & (c_idx < nc - 1)
    s = _dot(kc_ref[0], qt) + _tile_heads(jnp.where(valid, 0.0, NEG))
    m = jnp.max(s, 0, keepdims=True)
    p = jnp.where(m > 0.5 * NEG, jnp.exp2(s - m), 0.0)
    p = p * (1.0 / jnp.maximum(jnp.sum(p, 0, keepdims=True), 1e-30))
    _store_heads(_dot(vct_ref[0], p.astype(BF16)), oc_ref)
    psum = p[:, 0:Q_BLOCK]
    for h in range(1, NSA_GROUP):
        psum = psum + p[:, h * Q_BLOCK:(h + 1) * Q_BLOCK]
    imp = sum(_dot(ovt_ref[...], part) for part in _split2(psum))

    blk = lax.broadcasted_iota(I32, (n_slc, Q_BLOCK), 0)
    blk_f = blk.astype(F32)
    cur = t // SLC_BLOCK
    causal_blk = blk <= cur
    forced = (blk == 0) | (blk == cur) | (blk == cur - 1)
    score = jnp.where(causal_blk, jnp.where(forced, jnp.inf, imp), -jnp.inf)
    sel = jnp.zeros((n_slc, Q_BLOCK), F32)
    for _ in range(topn):
        mx = jnp.max(score, 0, keepdims=True)
        first = jnp.min(jnp.where(score == mx, blk_f, float(n_slc)), 0, keepdims=True)
        hit = blk_f == first
        sel = jnp.where(hit, 1.0, sel)
        score = jnp.where(hit, -jnp.inf, score)
    nb_ref[0, 0] = jnp.where(causal_blk & (sel > 0.5), 0.0, NEG)


def _nsa_cmp(qt, kcc, vcct, ovt, B, S):
    T = B * S
    nq = S // Q_BLOCK
    nc = S // CMP_STRIDE
    n_slc = S // SLC_BLOCK
    kern = functools.partial(_nsa_cmp_kernel, n_slc=n_slc, topn=min(SLC_TOPN, n_slc))
    return pl.pallas_call(
        kern,
        grid=(NSA_KV_HEADS, B, nq),
        in_specs=[pl.BlockSpec((1, NSA_GROUP, NSA_HEAD_DIM, Q_BLOCK), lambda g, b, i: (g, 0, 0, b * nq + i)),
                  pl.BlockSpec((1, nc, NSA_HEAD_DIM), lambda g, b, i: (g, b, 0)),
                  pl.BlockSpec((1, NSA_HEAD_DIM, nc), lambda g, b, i: (g, 0, b)),
                  pl.BlockSpec((n_slc, nc), lambda g, b, i: (0, 0))],
        out_specs=[pl.BlockSpec((Q_BLOCK, NSA_GROUP * NSA_HEAD_DIM), lambda g, b, i: (b * nq + i, g)),
                   pl.BlockSpec((1, 1, n_slc, Q_BLOCK), lambda g, b, i: (g, b * nq + i, 0, 0))],
        out_shape=[jax.ShapeDtypeStruct((T, NSA_WIDTH), BF16),
                   jax.ShapeDtypeStruct((NSA_KV_HEADS, B * nq, n_slc, Q_BLOCK), F32)],
        compiler_params=_cparams(("parallel", "parallel", "arbitrary")),
        name="nsa_cmp_select",
    )(qt, kcc, vcct, ovt)


def _nsa_slc_kernel(q_ref, nb_ref, ks_ref, vst_ref, os_ref, m_ref, acc_ref, s_ref):
    i = pl.program_id(2)
    qt = _load_qt(q_ref)
    t = i * Q_BLOCK + lax.broadcasted_iota(I32, (1, Q_BLOCK), 1)
    m_ref[...] = jnp.full_like(m_ref, NEG)
    acc_ref[...] = jnp.zeros_like(acc_ref)
    pad = jnp.zeros((NSA_HEAD_DIM - SLC_CHUNK_BLOCKS, Q_BLOCK), F32)

    def scores(c, slot):
        start = pl.multiple_of(c * SLC_CHUNK, SLC_CHUNK)
        k = ks_ref[0, pl.ds(start, SLC_CHUNK), :]
        nb = nb_ref[0, 0, pl.ds(pl.multiple_of(c * SLC_CHUNK_BLOCKS, SLC_CHUNK_BLOCKS), SLC_CHUNK_BLOCKS), :]
        mask_rows = jnp.concatenate([nb, pad], 0).astype(BF16)
        rhs = jnp.concatenate([qt, _tile_heads(mask_rows)], 0)
        s_ref[slot] = _dot(k, rhs)

    def absorb(c, slot, diagonal):
        start = pl.multiple_of(c * SLC_CHUNK, SLC_CHUNK)
        vt = vst_ref[0, :, pl.ds(start, SLC_CHUNK)]
        s = s_ref[slot]
        if diagonal:
            key = start + lax.broadcasted_iota(I32, (SLC_CHUNK, 1), 0)
            s = s + _tile_heads(jnp.where(key <= t, 0.0, NEG))
        m_old = m_ref[...]
        m_new = jnp.maximum(m_old, jnp.max(s, 0, keepdims=True))
        p = jnp.exp2(s - m_new).astype(BF16)
        acc_ref[...] = jnp.exp2(m_old - m_new) * acc_ref[...] + _dot(vt, p)
        m_ref[...] = m_new

    n_full = (i * Q_BLOCK) // SLC_CHUNK
    scores(0, 0)

    def body(j, carry):
        scores(2 * j + 1, 1)
        absorb(2 * j, 0, False)
        scores(2 * j + 2, 0)
        absorb(2 * j + 1, 1, False)
        return carry

    lax.fori_loop(0, n_full // 2, body, 0)

    @pl.when(n_full % 2 == 1)
    def _():
        scores(n_full, 1)
        absorb(n_full - 1, 0, False)
        absorb(n_full, 1, True)

    @pl.when(n_full % 2 == 0)
    def _():
        absorb(n_full, 0, True)

    ok = m_ref[...] > 0.5 * NEG
    l = acc_ref[NSA_HEAD_DIM:NSA_HEAD_DIM + 1, :]
    inv = jnp.where(ok, 1.0 / jnp.where(ok, l, 1.0), 0.0)
    _store_heads(acc_ref[0:NSA_HEAD_DIM, :] * inv, os_ref)


def _nsa_slc(qt, nb, ks, vst, B, S):
    T = B * S
    nq = S // Q_BLOCK
    n_slc = S // SLC_BLOCK
    return pl.pallas_call(
        _nsa_slc_kernel,
        grid=(NSA_KV_HEADS, B, nq),
        in_specs=[pl.BlockSpec((1, NSA_GROUP, NSA_HEAD_DIM, Q_BLOCK), lambda g, b, i: (g, 0, 0, b * nq + i)),
                  pl.BlockSpec((1, 1, n_slc, Q_BLOCK), lambda g, b, i: (g, b * nq + i, 0, 0)),
                  pl.BlockSpec((1, S, 2 * NSA_HEAD_DIM), lambda g, b, i: (g, b, 0)),
                  pl.BlockSpec((1, NSA_HEAD_DIM + VT_EXTRA, S), lambda g, b, i: (g, 0, b))],
        out_specs=pl.BlockSpec((Q_BLOCK, NSA_GROUP * NSA_HEAD_DIM), lambda g, b, i: (b * nq + i, g)),
        out_shape=jax.ShapeDtypeStruct((T, NSA_WIDTH), BF16),
        scratch_shapes=[pltpu.VMEM((1, NSA_COLS), F32),
                        pltpu.VMEM((NSA_HEAD_DIM + VT_EXTRA, NSA_COLS), F32),
                        pltpu.VMEM((2, SLC_CHUNK, NSA_COLS), F32)],
        compiler_params=_cparams(("parallel", "parallel", "arbitrary")),
        name="nsa_selected",
    )(qt, nb, ks, vst)


N_WIN_BLOCKS = WINDOW // Q_BLOCK + 1


def _nsa_win_kernel(q_ref, *refs):
    k_refs = refs[:N_WIN_BLOCKS]
    v_refs = refs[N_WIN_BLOCKS:2 * N_WIN_BLOCKS]
    ow_ref = refs[2 * N_WIN_BLOCKS]
    i = pl.program_id(2)
    qt = _load_qt(q_ref)
    k = jnp.concatenate([r[0] for r in k_refs], 0)
    vt = jnp.concatenate([r[0] for r in v_refs], 1)
    nk = N_WIN_BLOCKS * Q_BLOCK
    t = i * Q_BLOCK + lax.broadcasted_iota(I32, (1, Q_BLOCK), 1)
    wpos = i * Q_BLOCK - WINDOW + lax.broadcasted_iota(I32, (nk, 1), 0)
    ok = (wpos <= t) & (wpos > t - WINDOW) & (wpos >= 0)
    s = _dot(k, qt) + _tile_heads(jnp.where(ok, 0.0, NEG))
    m = jnp.max(s, 0, keepdims=True)
    p = jnp.exp2(s - m)
    o_t = _dot(vt, p.astype(BF16)) * (1.0 / jnp.sum(p, 0, keepdims=True))
    _store_heads(o_t, ow_ref)


def _nsa_win(qt, kw, vwt, B, S):
    T = B * S
    nq = S // Q_BLOCK

    def k_spec(c):
        back = N_WIN_BLOCKS - 1 - c
        return pl.BlockSpec((1, Q_BLOCK, NSA_HEAD_DIM),
                            lambda g, b, i: (g, b * nq + jnp.maximum(i - back, 0), 0))

    def v_spec(c):
        back = N_WIN_BLOCKS - 1 - c
        return pl.BlockSpec((1, NSA_HEAD_DIM, Q_BLOCK),
                            lambda g, b, i: (g, 0, b * nq + jnp.maximum(i - back, 0)))

    return pl.pallas_call(
        _nsa_win_kernel,
        grid=(NSA_KV_HEADS, B, nq),
        in_specs=[pl.BlockSpec((1, NSA_GROUP, NSA_HEAD_DIM, Q_BLOCK), lambda g, b, i: (g, 0, 0, b * nq + i))]
        + [k_spec(c) for c in range(N_WIN_BLOCKS)] + [v_spec(c) for c in range(N_WIN_BLOCKS)],
        out_specs=pl.BlockSpec((Q_BLOCK, NSA_GROUP * NSA_HEAD_DIM), lambda g, b, i: (b * nq + i, g)),
        out_shape=jax.ShapeDtypeStruct((T, NSA_WIDTH), BF16),
        compiler_params=_cparams(("parallel", "parallel", "arbitrary")),
        name="nsa_window",
    )(qt, *([kw] * N_WIN_BLOCKS), *([vwt] * N_WIN_BLOCKS))


def _outproj_kernel(hm_ref, oc_ref, os_ref, ow_ref, gt_ref, gb_ref, ge_ref, h_ref, w_ref, g_ref, b_ref,
                    out_ref):
    gates = jax.nn.sigmoid(gt_ref[...] + gb_ref[...]).astype(BF16)
    gx = _dot(gates, ge_ref[...])
    hn = (gx[:, 0:NSA_WIDTH] * oc_ref[...].astype(F32)
          + gx[:, NSA_WIDTH:2 * NSA_WIDTH] * os_ref[...].astype(F32)
          + gx[:, 2 * NSA_WIDTH:3 * NSA_WIDTH] * ow_ref[...].astype(F32))
    mix = (_dot(hm_ref[...], w_ref[0:MLSTM_WIDTH, :])
           + _dot(hn.astype(BF16), w_ref[MLSTM_WIDTH:MLSTM_WIDTH + NSA_WIDTH, :]))
    out_ref[...] = _layer_norm(DN_ALPHA * h_ref[...] + mix, g_ref[...], b_ref[...])


def _outproj(hm, oc, os_, ow, ug, gate_b, gate_expand, h, w_out, g, b):
    T = h.shape[0]
    tm = 256
    row = lambda i: (i, 0)
    fixed = lambda i: (0, 0)
    return pl.pallas_call(
        _outproj_kernel,
        grid=(T // tm,),
        in_specs=[pl.BlockSpec((tm, MLSTM_WIDTH), row), pl.BlockSpec((tm, NSA_WIDTH), row),
                  pl.BlockSpec((tm, NSA_WIDTH), row), pl.BlockSpec((tm, NSA_WIDTH), row),
                  pl.BlockSpec((tm, GATE_COLS), row),
                  pl.BlockSpec((1, LANE), fixed),
                  pl.BlockSpec((LANE, 3 * NSA_WIDTH), fixed),
                  pl.BlockSpec((tm, D_MODEL), row),
                  pl.BlockSpec((D_MODEL, D_MODEL), fixed),
                  pl.BlockSpec((1, D_MODEL), fixed), pl.BlockSpec((1, D_MODEL), fixed)],
        out_specs=pl.BlockSpec((tm, D_MODEL), row),
        out_shape=jax.ShapeDtypeStruct((T, D_MODEL), F32),
        compiler_params=_cparams(("parallel",)),
        name="mixer_outproj_ln",
    )(hm, oc, os_, ow, ug, gate_b, gate_expand, h, w_out, g, b)


def _matmul_kernel(x_ref, w_ref, o_ref):
    o_ref[...] = _dot(x_ref[...].astype(BF16), w_ref[...]).astype(o_ref.dtype)


def _mem_kv(mem2, wkv):
    M = mem2.shape[0]
    N = wkv.shape[1]
    tn = 512
    return pl.pallas_call(
        _matmul_kernel,
        grid=(N // tn,),
        in_specs=[pl.BlockSpec((M, D_MODEL), lambda j: (0, 0)),
                  pl.BlockSpec((D_MODEL, tn), lambda j: (0, j))],
        out_specs=pl.BlockSpec((M, tn), lambda j: (0, j)),
        out_shape=jax.ShapeDtypeStruct((M, N), BF16),
        compiler_params=_cparams(("parallel",)),
        name="mem_kv_proj",
    )(mem2, wkv)


def _xattn_kernel(h_ref, wq_ref, kv_ref, o_ref):
    hb = h_ref[...].astype(BF16)
    for hd in range(XA_HEADS):
        c0 = hd * XA_HEAD_DIM
        q = (_dot(hb, wq_ref[:, c0:c0 + XA_HEAD_DIM]) * (XA_HEAD_DIM ** -0.5)).astype(BF16)
        s = _dot_nt(q, kv_ref[:, c0:c0 + XA_HEAD_DIM])
        m = jnp.max(s, -1, keepdims=True)
        p = jnp.exp(s - m)
        o = _dot(p.astype(BF16), kv_ref[:, D_MODEL + c0:D_MODEL + c0 + XA_HEAD_DIM])
        o_ref[:, c0:c0 + XA_HEAD_DIM] = (o / jnp.sum(p, -1, keepdims=True)).astype(o_ref.dtype)


def _xattn(h1, wq, kv, B, S):
    T = B * S
    tm = 256
    n_mem = kv.shape[0] // B
    per_b = S // tm
    return pl.pallas_call(
        _xattn_kernel,
        grid=(T // tm,),
        in_specs=[pl.BlockSpec((tm, D_MODEL), lambda i: (i, 0)),
                  pl.BlockSpec((D_MODEL, D_MODEL), lambda i: (0, 0)),
                  pl.BlockSpec((n_mem, 2 * D_MODEL), lambda i: (i // per_b, 0))],
        out_specs=pl.BlockSpec((tm, D_MODEL), lambda i: (i, 0)),
        out_shape=jax.ShapeDtypeStruct((T, D_MODEL), BF16),
        compiler_params=_cparams(("parallel",)),
        name="mem_xattn",
    )(h1, wq, kv)


def _xa_out_router_kernel(o_ref, wo_ref, h_ref, g_ref, b_ref, rw_ref, rb_ref,
                          h2_ref, idx_ref, wgt_ref):
    xa = _dot(o_ref[...], wo_ref[...])
    h2 = _layer_norm(DN_ALPHA * h_ref[...] + xa, g_ref[...], b_ref[...])
    h2_ref[...] = h2
    hi, lo = _split2(h2)
    logits = _dot(hi, rw_ref[0]) + _dot(hi, rw_ref[1]) + _dot(lo, rw_ref[0])
    scores = jax.nn.sigmoid(logits)
    lane = lax.broadcasted_iota(I32, scores.shape, 1)
    lane_f = lane.astype(F32)
    biased = jnp.where(lane < N_EXPERTS, scores + rb_ref[...], -jnp.inf)
    idx_mat = jnp.zeros(scores.shape, F32)
    w_mat = jnp.zeros(scores.shape, F32)
    for kk in range(TOP_K):
        mx = jnp.max(biased, -1, keepdims=True)
        first = jnp.min(jnp.where(biased == mx, lane_f, float(LANE)), -1, keepdims=True)
        hit = lane_f == first
        top_s = jnp.sum(jnp.where(hit, scores, 0.0), -1, keepdims=True)
        idx_mat = jnp.where(lane == kk, first, idx_mat)
        w_mat = jnp.where(lane == kk, top_s, w_mat)
        biased = jnp.where(hit, -jnp.inf, biased)
    idx_ref[...] = idx_mat.astype(I32)
    wgt_ref[...] = w_mat / jnp.sum(w_mat, -1, keepdims=True) * ROUTED_SCALE


def _xa_out_router(o, wo, h1, g, b, rw2, rb):
    T = h1.shape[0]
    tm = 256
    row = lambda i: (i, 0)
    fixed = lambda i: (0, 0)
    return pl.pallas_call(
        _xa_out_router_kernel,
        grid=(T // tm,),
        in_specs=[pl.BlockSpec((tm, D_MODEL), row),
                  pl.BlockSpec((D_MODEL, D_MODEL), fixed),
                  pl.BlockSpec((tm, D_MODEL), row),
                  pl.BlockSpec((1, D_MODEL), fixed), pl.BlockSpec((1, D_MODEL), fixed),
                  pl.BlockSpec((2, D_MODEL, LANE), lambda i: (0, 0, 0)),
                  pl.BlockSpec((1, LANE), fixed)],
        out_specs=[pl.BlockSpec((tm, D_MODEL), row), pl.BlockSpec((tm, LANE), row),
                   pl.BlockSpec((tm, LANE), row)],
        out_shape=[jax.ShapeDtypeStruct((T, D_MODEL), F32),
                   jax.ShapeDtypeStruct((T, LANE), I32),
                   jax.ShapeDtypeStruct((T, LANE), F32)],
        compiler_params=_cparams(("parallel",)),
        name="xattn_out_ln_router",
    )(o, wo, h1, g, b, rw2, rb)


ROUTE_TM = 512
ZERO_ROWS = 128


def _route_rank_kernel(idx_ref, rank_ref, cnt_ref):
    tm = idx_ref.shape[0]

    @pl.when(pl.program_id(0) == 0)
    def _():
        cnt_ref[...] = jnp.zeros_like(cnt_ref)

    idx = idx_ref[...]
    lane = lax.broadcasted_iota(I32, (tm, LANE), 1)
    hits = [lane == idx[:, kk:kk + 1] for kk in range(TOP_K)]
    onehot = sum(jnp.where(hit, 1.0, 0.0) for hit in hits)
    r_i = lax.broadcasted_iota(I32, (tm, tm), 0)
    c_i = lax.broadcasted_iota(I32, (tm, tm), 1)
    before = jnp.where(r_i > c_i, 1.0, 0.0).astype(BF16)
    rank = _dot(before, onehot.astype(BF16)) + cnt_ref[0:1, :]
    out = jnp.zeros((tm, LANE), F32)
    for kk in range(TOP_K):
        out = jnp.where(lane == kk, jnp.sum(jnp.where(hits[kk], rank, 0.0), -1, keepdims=True), out)
    rank_ref[...] = out.astype(I32)
    cnt_ref[0:1, :] = cnt_ref[0:1, :] + jnp.sum(onehot, 0, keepdims=True)


def _route_rank(top_idx):
    T = top_idx.shape[0]
    tm = ROUTE_TM
    return pl.pallas_call(
        _route_rank_kernel,
        grid=(T // tm,),
        in_specs=[pl.BlockSpec((tm, LANE), lambda i: (i, 0))],
        out_specs=[pl.BlockSpec((tm, LANE), lambda i: (i, 0)),
                   pl.BlockSpec((8, LANE), lambda i: (0, 0))],
        out_shape=[jax.ShapeDtypeStruct((T, LANE), I32), jax.ShapeDtypeStruct((8, LANE), F32)],
        compiler_params=_cparams(("arbitrary",)),
        name="moe_route_rank",
    )(top_idx)


def _load_route(i, idx_hbm, rank_hbm, idx_smem, rank_smem, isem):
    c0 = pltpu.make_async_copy(idx_hbm.at[i], idx_smem, isem.at[0])
    c1 = pltpu.make_async_copy(rank_hbm.at[i], rank_smem, isem.at[1])
    c0.start()
    c1.start()
    c0.wait()
    c1.wait()


_FILL_SIZES = tuple(s for s in (ZERO_ROWS >> n for n in range(ZERO_ROWS.bit_length())) if s >= SUBLANE)


def _dispatch_kernel(ps_ref, fs_ref, fn_ref, tail_ref, idx_hbm, rank_hbm, h_ref, xs_ref,
                     idx_smem, rank_smem, zero_ref, isem, sem, zsem):
    i = pl.program_id(0)
    _load_route(i, idx_hbm, rank_hbm, idx_smem, rank_smem, isem)

    def issue(r, carry):
        for kk in range(TOP_K):
            j = r * TOP_K + kk
            d = ps_ref[idx_smem[j]] + rank_smem[j]
            pltpu.make_async_copy(h_ref.at[pl.ds(r, 1)], xs_ref.at[pl.ds(d, 1)], sem).start()
        return carry

    lax.fori_loop(0, ROUTE_TM, issue, 0)

    @pl.when(i == 0)
    def _():
        zero_ref[...] = jnp.zeros_like(zero_ref)

        def fill_copies(e, go):
            n = fn_ref[e]
            first = fs_ref[e]
            lead = jnp.minimum((-first) & (SUBLANE - 1), n)
            for r in range(SUBLANE - 1):
                @pl.when(r < lead)
                def _():
                    go(pltpu.make_async_copy(zero_ref.at[pl.ds(0, 1)], xs_ref.at[pl.ds(first + r, 1)], zsem))

            rest = n - lead
            off = first + lead
            for size in _FILL_SIZES:
                take = (rest & size) != 0

                @pl.when(take)
                def _():
                    dst = xs_ref.at[pl.ds(pl.multiple_of(off, SUBLANE), size)]
                    go(pltpu.make_async_copy(zero_ref.at[pl.ds(0, size)], dst, zsem))
                off = off + jnp.where(take, size, 0)

        def tail_copies(j, go):
            off = pl.multiple_of(tail_ref[0] + j * ZERO_ROWS, ZERO_ROWS)
            go(pltpu.make_async_copy(zero_ref, xs_ref.at[pl.ds(off, ZERO_ROWS)], zsem))

        for go in (lambda cp: cp.start(), lambda cp: cp.wait()):
            lax.fori_loop(0, N_EXPERTS, lambda e, c, go=go: (fill_copies(e, go), c)[1], 0)
            lax.fori_loop(0, tail_ref[1], lambda j, c, go=go: (tail_copies(j, go), c)[1], 0)

    for _ in range(TOP_K):
        pltpu.make_async_copy(h_ref, xs_ref.at[pl.ds(0, ROUTE_TM)], sem).wait()


def _dispatch(pad_start, fill_start, fill_n, tail, idx2, rank2, h2, P):
    T = h2.shape[0]
    tm = ROUTE_TM
    return pl.pallas_call(
        _dispatch_kernel,
        grid_spec=pltpu.PrefetchScalarGridSpec(
            num_scalar_prefetch=4,
            grid=(T // tm,),
            in_specs=[pl.BlockSpec(memory_space=pl.ANY), pl.BlockSpec(memory_space=pl.ANY),
                      pl.BlockSpec((tm, D_MODEL), lambda i, *_: (i, 0))],
            out_specs=pl.BlockSpec(memory_space=pl.ANY),
            scratch_shapes=[pltpu.SMEM((tm * TOP_K,), I32), pltpu.SMEM((tm * TOP_K,), I32),
                            pltpu.VMEM((ZERO_ROWS, D_MODEL), F32),
                            pltpu.SemaphoreType.DMA((2,)), pltpu.SemaphoreType.DMA, pltpu.SemaphoreType.DMA]),
        out_shape=jax.ShapeDtypeStruct((P, D_MODEL), F32),
        compiler_params=_cparams(("arbitrary",)),
        name="moe_dispatch",
    )(pad_start, fill_start, fill_n, tail, idx2, rank2, h2)


def _experts_kernel(be_ref, nu_ref, x_ref, wg_ref, wu_ref, wd_ref, y_ref):
    used = pl.program_id(0) < nu_ref[0]

    @pl.when(used)
    def _():
        xb = x_ref[...].astype(BF16)
        a = _dot(xb, wg_ref[0])
        act = (a * jax.nn.sigmoid(a)) * _dot(xb, wu_ref[0])
        y_ref[...] = _dot(act.astype(BF16), wd_ref[0])

    @pl.when(jnp.logical_not(used))
    def _():
        y_ref[...] = jnp.zeros_like(y_ref)


def _experts(block_e, n_used, xs, wg, wu, wd):
    P = xs.shape[0]
    bm = EXPERT_BM
    rowmap = lambda j, be, nu: (jnp.minimum(j, nu[0] - 1), 0)
    wmap = lambda j, be, nu: (be[j], 0, 0)
    return pl.pallas_call(
        _experts_kernel,
        grid_spec=pltpu.PrefetchScalarGridSpec(
            num_scalar_prefetch=2,
            grid=(P // bm,),
            in_specs=[pl.BlockSpec((bm, D_MODEL), rowmap),
                      pl.BlockSpec((1, D_MODEL, D_EXPERT), wmap),
                      pl.BlockSpec((1, D_MODEL, D_EXPERT), wmap),
                      pl.BlockSpec((1, D_EXPERT, D_MODEL), wmap)],
            out_specs=pl.BlockSpec((bm, D_MODEL), lambda j, be, nu: (j, 0))),
        out_shape=jax.ShapeDtypeStruct((P, D_MODEL), F32),
        compiler_params=_cparams(("arbitrary",)),
        name="moe_experts",
    )(block_e, n_used, xs, wg, wu, wd)


def _shared_ffn_kernel(x_ref, wg_ref, wu_ref, wd_ref, o_ref, xb_ref):
    f = pl.program_id(1)

    @pl.when(f == 0)
    def _():
        xb_ref[...] = x_ref[...].astype(BF16)
        o_ref[...] = jnp.zeros_like(o_ref)

    xb = xb_ref[...]
    a = _dot(xb, wg_ref[...])
    act = (a * jax.nn.sigmoid(a)) * _dot(xb, wu_ref[...])
    o_ref[...] += _dot(act.astype(BF16), wd_ref[...])


def _shared_ffn(h2, wg, wu, wd):
    T = h2.shape[0]
    tm, tf = 512, 256
    return pl.pallas_call(
        _shared_ffn_kernel,
        grid=(T // tm, D_SHARED // tf),
        in_specs=[pl.BlockSpec((tm, D_MODEL), lambda i, f: (i, 0)),
                  pl.BlockSpec((D_MODEL, tf), lambda i, f: (0, f)),
                  pl.BlockSpec((D_MODEL, tf), lambda i, f: (0, f)),
                  pl.BlockSpec((tf, D_MODEL), lambda i, f: (f, 0))],
        out_specs=pl.BlockSpec((tm, D_MODEL), lambda i, f: (i, 0)),
        out_shape=jax.ShapeDtypeStruct((T, D_MODEL), F32),
        scratch_shapes=[pltpu.VMEM((tm, D_MODEL), BF16)],
        compiler_params=_cparams(("parallel", "arbitrary")),
        name="shared_ffn",
    )(h2, wg, wu, wd)


COMBINE_SUB = 128
COMBINE_NSUB = ROUTE_TM // COMBINE_SUB


def _combine_kernel(ps_ref, idx_hbm, rank_hbm, ys_hbm, w_ref, sh_ref, h_ref, g_ref, b_ref, out_ref,
                    idx_smem, rank_smem, gbuf, isem, sem):
    i = pl.program_id(0)
    _load_route(i, idx_hbm, rank_hbm, idx_smem, rank_smem, isem)

    def gather(sub):
        slot = sub % 2

        def issue(r, carry):
            for kk in range(TOP_K):
                j = (sub * COMBINE_SUB + r) * TOP_K + kk
                d = ps_ref[idx_smem[j]] + rank_smem[j]
                pltpu.make_async_copy(ys_hbm.at[pl.ds(d, 1)], gbuf.at[slot, kk, pl.ds(r, 1)],
                                      sem.at[slot]).start()
            return carry

        lax.fori_loop(0, COMBINE_SUB, issue, 0)

    gather(0)
    for sub in range(COMBINE_NSUB):
        slot = sub % 2
        if sub + 1 < COMBINE_NSUB:
            gather(sub + 1)
        for kk in range(TOP_K):
            pltpu.make_async_copy(ys_hbm.at[pl.ds(0, COMBINE_SUB)], gbuf.at[slot, kk], sem.at[slot]).wait()
        rs = slice(sub * COMBINE_SUB, (sub + 1) * COMBINE_SUB)
        w = w_ref[rs, :]
        routed = w[:, 0:1] * gbuf[slot, 0]
        for kk in range(1, TOP_K):
            routed = routed + w[:, kk:kk + 1] * gbuf[slot, kk]
        z = DN_ALPHA * h_ref[rs, :] + (routed + sh_ref[rs, :])
        out_ref[rs, :] = _layer_norm(z, g_ref[...], b_ref[...])


def _combine(pad_start, idx2, rank2, ys, top_w, sh, h2, g, b):
    T = h2.shape[0]
    tm = ROUTE_TM
    row = lambda i, *_: (i, 0)
    fixed = lambda i, *_: (0, 0)
    return pl.pallas_call(
        _combine_kernel,
        grid_spec=pltpu.PrefetchScalarGridSpec(
            num_scalar_prefetch=1,
            grid=(T // tm,),
            in_specs=[pl.BlockSpec(memory_space=pl.ANY), pl.BlockSpec(memory_space=pl.ANY),
                      pl.BlockSpec(memory_space=pl.ANY),
                      pl.BlockSpec((tm, LANE), row), pl.BlockSpec((tm, D_MODEL), row),
                      pl.BlockSpec((tm, D_MODEL), row),
                      pl.BlockSpec((1, D_MODEL), fixed), pl.BlockSpec((1, D_MODEL), fixed)],
            out_specs=pl.BlockSpec((tm, D_MODEL), row),
            scratch_shapes=[pltpu.SMEM((tm * TOP_K,), I32), pltpu.SMEM((tm * TOP_K,), I32),
                            pltpu.VMEM((2, TOP_K, COMBINE_SUB, D_MODEL), F32),
                            pltpu.SemaphoreType.DMA((2,)), pltpu.SemaphoreType.DMA((2,))]),
        out_shape=jax.ShapeDtypeStruct((T, D_MODEL), F32),
        compiler_params=_cparams(("arbitrary",)),
        name="moe_combine_ln",
    )(pad_start, idx2, rank2, ys, top_w, sh, h2, g, b)


def _route_plan(counts, T):
    bm = EXPERT_BM
    padded = (counts + bm - 1) // bm * bm
    pad_end = jnp.cumsum(padded)
    pad_start = pad_end - padded
    n_blocks = T * TOP_K // bm + N_EXPERTS
    first_row = jnp.arange(n_blocks, dtype=I32) * bm
    block_e = jnp.minimum(jnp.sum((pad_end[None, :] <= first_row[:, None]).astype(I32), 1), N_EXPERTS - 1)
    n_used = (pad_end[-1:] // bm).astype(I32)
    tail = jnp.concatenate([pad_end[-1:], (n_blocks - n_used) * (bm // ZERO_ROWS)]).astype(I32)
    return (pad_start.astype(I32), (pad_start + counts).astype(I32), (padded - counts).astype(I32), tail,
            block_e, n_used, n_blocks * bm)


def _overlap_matrix(S):
    n_cmp_rows = S // CMP_STRIDE
    n_slc = S // SLC_BLOCK
    c_lo = np.arange(n_cmp_rows)[:, None] * CMP_STRIDE
    j_lo = np.arange(n_slc)[None, :] * SLC_BLOCK
    ov = (c_lo <= j_lo + SLC_BLOCK - 1) & (c_lo + CMP_BLOCK - 1 >= j_lo)
    return jnp.asarray(ov.T.astype(np.float32), dtype=BF16)


def _gate_expand_matrix():
    ge = np.zeros((LANE, 3 * NSA_WIDTH), np.float32)
    for hh in range(NSA_HEADS):
        for br in range(3):
            ge[8 + hh * 3 + br, br * NSA_WIDTH + hh * NSA_HEAD_DIM:br * NSA_WIDTH + (hh + 1) * NSA_HEAD_DIM] = 1.0
    return jnp.asarray(ge, dtype=BF16)


def kernel(x, mem, positions, ln0_g, ln0_b, w_in, conv_w, conv_b, igate_b, fgate_b, mlstm_norm_g, cmp_pos, cmp_w1k, cmp_w2k, cmp_w1v, cmp_w2v, nsa_gate_b, w_out, ln1_g, ln1_b, xa_wq, xa_wk, xa_wv, xa_wo, ln2_g, ln2_b, router_w, router_bias, moe_w_gate, moe_w_up, moe_w_down, sh_w_gate, sh_w_up, sh_w_down, ln3_g, ln3_b):
    B, S, D = x.shape
    T = B * S
    assert D == D_MODEL and w_in.shape[0] == DEPTH == 1
    assert S % MLSTM_L == 0 and S % SLC_CHUNK == 0 and T % ROUTE_TM == 0 and (T * TOP_K) % EXPERT_BM == 0
    row = lambda a: a.reshape(1, -1)

    w = w_in[0]
    w_r = jnp.concatenate([w[:, :3072], w[:, 3080:5640]], 1).astype(BF16)
    w_g = jnp.concatenate([w[:, 3072:3080], w[:, 5640:5664],
                           jnp.zeros((D, GATE_COLS - 2 * MLSTM_HEADS - 3 * NSA_HEADS), F32)], 1).astype(BF16)
    gate_b = jnp.concatenate([igate_b[0], fgate_b[0], nsa_gate_b[0],
                              jnp.zeros((LANE - 2 * MLSTM_HEADS - 3 * NSA_HEADS,), F32)]).reshape(1, LANE)
    half = NSA_HEAD_DIM // 2
    inv_freq = ROPE_THETA ** (-jnp.arange(half, dtype=F32) / half)
    invf2 = jnp.concatenate([inv_freq, inv_freq]).reshape(1, LANE)
    sgn = jnp.concatenate([-jnp.ones((half,), F32), jnp.ones((half,), F32)]).reshape(1, LANE)
    pos8 = jnp.zeros((2, 8, CMP_STRIDE * NSA_HEAD_DIM), F32).at[:, 0, :].set(
        cmp_pos[0].reshape(2, CMP_STRIDE * NSA_HEAD_DIM)).astype(BF16)
    w1k = cmp_w1k[0].reshape(2, CMP_STRIDE * NSA_HEAD_DIM, CMP_HIDDEN).astype(BF16)
    w1v = cmp_w1v[0].reshape(2, CMP_STRIDE * NSA_HEAD_DIM, CMP_HIDDEN).astype(BF16)
    rw = jnp.pad(router_w[0], ((0, 0), (0, LANE - N_EXPERTS)))
    rw_hi = rw.astype(BF16)
    rw2 = jnp.stack([rw_hi, (rw - rw_hi.astype(F32)).astype(BF16)])
    rb = jnp.pad(router_bias[0], (0, LANE - N_EXPERTS)).reshape(1, LANE)

    h, u, ug = _ln_inproj(x.reshape(T, D), row(ln0_g), row(ln0_b), w_r, w_g)
    hm = _mlstm(u, ug, conv_w[0], row(conv_b[0]), gate_b, row(mlstm_norm_g[0]), B, S)
    qt, kc, vc, ks, vst, kw, vwt = _nsa_prep(u, positions.reshape(T, 1), invf2, sgn, T)
    kcc = _compress(kc, w1k, pos8, cmp_w2k[0].astype(BF16), B, S, False)
    vcct = _compress(vc, w1v, pos8, cmp_w2v[0].astype(BF16), B, S, True)
    oc, nb = _nsa_cmp(qt, kcc, vcct, _overlap_matrix(S), B, S)
    os_ = _nsa_slc(qt, nb, ks, vst, B, S)
    ow = _nsa_win(qt, kw, vwt, B, S)
    h1 = _outproj(hm, oc, os_, ow, ug, gate_b, _gate_expand_matrix(), h, w_out[0].astype(BF16),
                  row(ln1_g[0]), row(ln1_b[0]))

    wkv = jnp.concatenate([xa_wk[0], xa_wv[0]], 1).astype(BF16)
    kv = _mem_kv(mem.reshape(-1, D), wkv)
    xo = _xattn(h1, xa_wq[0].astype(BF16), kv, B, S)
    h2, top_idx, top_w = _xa_out_router(xo, xa_wo[0].astype(BF16), h1, row(ln2_g[0]), row(ln2_b[0]), rw2, rb)

    rank, cnt = _route_rank(top_idx)
    counts = cnt[0, :N_EXPERTS].astype(I32)
    pad_start, fill_start, fill_n, tail, block_e, n_used, P = _route_plan(counts, T)
    idx2 = top_idx[:, :TOP_K].reshape(T // ROUTE_TM, ROUTE_TM * TOP_K)
    rank2 = rank[:, :TOP_K].reshape(T // ROUTE_TM, ROUTE_TM * TOP_K)
    xs = _dispatch(pad_start, fill_start, fill_n, tail, idx2, rank2, h2, P)
    ys = _experts(block_e, n_used, xs, moe_w_gate[0].astype(BF16), moe_w_up[0].astype(BF16),
                  moe_w_down[0].astype(BF16))
    sh = _shared_ffn(h2, sh_w_gate[0].astype(BF16), sh_w_up[0].astype(BF16), sh_w_down[0].astype(BF16))
    out = _combine(pad_start, idx2, rank2, ys, top_w, sh, h2, row(ln3_g[0]), row(ln3_b[0]))
    return out.reshape(B, S, D)
```

```python
import functools

import numpy as np
import jax
import jax.numpy as jnp
from jax import lax
from jax.experimental import pallas as pl
from jax.experimental.pallas import tpu as pltpu

F32 = jnp.float32
BF16 = jnp.bfloat16
I32 = jnp.int32

D_MODEL = 2048
MLSTM_HEADS = 4
MLSTM_DV = 256
MLSTM_DQK = 128
MLSTM_QK = MLSTM_HEADS * MLSTM_DQK
MLSTM_WIDTH = MLSTM_HEADS * MLSTM_DV
CONV_WIDTH = 4
NSA_HEAD_DIM = 128
NSA_HEADS = 8
NSA_KV_HEADS = 2
NSA_GROUP = 4
NSA_WIDTH = NSA_HEADS * NSA_HEAD_DIM
CMP_BLOCK = 32
CMP_STRIDE = 16
CMP_HIDDEN = 256
SLC_BLOCK = 64
SLC_TOPN = 16
WINDOW = 512
Q_BLOCK = 128
XA_HEADS = 4
XA_HEAD_DIM = 512
N_EXPERTS = 64
TOP_K = 6
D_EXPERT = 1408
D_SHARED = 2816
ROUTED_SCALE = 2.446
ROPE_THETA = 10000.0
LN_EPS = 1e-5
DEPTH = 1
DN_ALPHA = (2.0 * DEPTH) ** 0.25

U_MLSTM = 0
U_NSA = 3072
U_COLS = 5632
GATE_COLS = 128

LANE = 128
SUBLANE = 8
NEG = -1e30
MLSTM_L = 256
VMEM_LIMIT = 56 * 1024 * 1024
EXPERT_BM = 256


def _cparams(sem):
    return pltpu.CompilerParams(dimension_semantics=sem, vmem_limit_bytes=VMEM_LIMIT)


def _dot(a, b):
    return jnp.dot(a, b, preferred_element_type=F32)


def _dot_nt(a, b):
    return lax.dot_general(a, b, (((1,), (1,)), ((), ())), preferred_element_type=F32)


def _layer_norm(z, g, b):
    mu = jnp.mean(z, -1, keepdims=True)
    zc = z - mu
    var = jnp.mean(zc * zc, -1, keepdims=True)
    return zc * lax.rsqrt(var + LN_EPS) * g + b


def _split3(x):
    hi = x.astype(BF16)
    r = x - hi.astype(F32)
    mid = r.astype(BF16)
    lo = (r - mid.astype(F32)).astype(BF16)
    return hi, mid, lo


def _split2(x):
    hi = x.astype(BF16)
    lo = (x - hi.astype(F32)).astype(BF16)
    return hi, lo


def _ln_inproj_kernel(x_ref, g_ref, b_ref, w_ref, wg_ref, h_ref, u_ref, ug_ref, hb_ref):
    @pl.when(pl.program_id(1) == 0)
    def _():
        hn = _layer_norm(x_ref[...], g_ref[...], b_ref[...])
        h_ref[...] = hn
        hb_ref[...] = hn.astype(BF16)
        ug_ref[...] = _dot(hb_ref[...], wg_ref[...])

    u_ref[...] = _dot(hb_ref[...], w_ref[...]).astype(u_ref.dtype)


def _ln_inproj(x2, g, b, w, w_gate):
    T = x2.shape[0]
    tm, tn = 1024, 512
    return pl.pallas_call(
        _ln_inproj_kernel,
        grid=(T // tm, U_COLS // tn),
        in_specs=[pl.BlockSpec((tm, D_MODEL), lambda i, j: (i, 0)),
                  pl.BlockSpec((1, D_MODEL), lambda i, j: (0, 0)),
                  pl.BlockSpec((1, D_MODEL), lambda i, j: (0, 0)),
                  pl.BlockSpec((D_MODEL, tn), lambda i, j: (0, j)),
                  pl.BlockSpec((D_MODEL, GATE_COLS), lambda i, j: (0, 0))],
        out_specs=[pl.BlockSpec((tm, D_MODEL), lambda i, j: (i, 0)),
                   pl.BlockSpec((tm, tn), lambda i, j: (i, j)),
                   pl.BlockSpec((tm, GATE_COLS), lambda i, j: (i, 0))],
        out_shape=[jax.ShapeDtypeStruct((T, D_MODEL), F32),
                   jax.ShapeDtypeStruct((T, U_COLS), BF16),
                   jax.ShapeDtypeStruct((T, GATE_COLS), F32)],
        scratch_shapes=[pltpu.VMEM((tm, D_MODEL), BF16)],
        compiler_params=_cparams(("parallel", "arbitrary")),
        name="ln_inproj",
    )(x2, g, b, w, w_gate)


def _log_sigmoid(x):
    return jnp.minimum(x, 0.0) - jnp.log1p(jnp.exp(-jnp.abs(x)))


def _mlstm_kernel(qk_ref, v_ref, o_ref, gt_ref, cw_ref, cb_ref, gb_ref, ng_ref, out_ref,
                  prev_ref, c_ref, n_ref, m_ref):
    L = MLSTM_L

    @pl.when(pl.program_id(1) == 0)
    def _():
        prev_ref[...] = jnp.zeros_like(prev_ref)
        c_ref[...] = jnp.zeros_like(c_ref)
        n_ref[...] = jnp.zeros_like(n_ref)
        m_ref[...] = jnp.zeros_like(m_ref)

    x = qk_ref[...].astype(F32)
    prev = prev_ref[...]
    row = lax.broadcasted_iota(I32, (L, 1), 0)
    cw = cw_ref[...]
    y = cb_ref[...] + cw[CONV_WIDTH - 1:CONV_WIDTH, :] * x
    for j in range(1, CONV_WIDTH):
        shifted = jnp.where(row < j, pltpu.roll(prev, j, 0), pltpu.roll(x, j, 0))
        y = y + cw[CONV_WIDTH - 1 - j:CONV_WIDTH - j, :] * shifted
    prev_ref[...] = x
    qk = y * jax.nn.sigmoid(y)

    gpre = gt_ref[...] + gb_ref[...]
    gpre_t = gpre.T
    r_i = lax.broadcasted_iota(I32, (L, L), 0)
    c_i = lax.broadcasted_iota(I32, (L, L), 1)
    causal = r_i >= c_i
    tril = jnp.where(causal, 1.0, 0.0).astype(BF16)
    triu = jnp.where(r_i <= c_i, 1.0, 0.0).astype(BF16)
    lf = _log_sigmoid(gpre)
    lf_t = _log_sigmoid(gpre_t)
    b_cols = sum(_dot(tril, part) for part in _split3(lf))
    b_rows = sum(_dot(part, triu) for part in _split3(lf_t))

    for h in range(MLSTM_HEADS):
        i_col = gpre[:, h:h + 1]
        i_row = gpre_t[h:h + 1, :]
        b_col = b_cols[:, MLSTM_HEADS + h:MLSTM_HEADS + h + 1]
        b_row = b_rows[MLSTM_HEADS + h:MLSTM_HEADS + h + 1, :]
        m_prev = m_ref[h:h + 1, 0:1]
        n_prev = n_ref[h:h + 1, :]
        c_prev = c_ref[h]

        q = qk[:, h * MLSTM_DQK:(h + 1) * MLSTM_DQK]
        k = qk[:, MLSTM_QK + h * MLSTM_DQK:MLSTM_QK + (h + 1) * MLSTM_DQK] * (MLSTM_DQK ** -0.5)
        v = v_ref[:, h * MLSTM_DV:(h + 1) * MLSTM_DV].astype(BF16)
        qb = q.astype(BF16)

        dmat = jnp.where(causal, b_col - b_row + i_row, -jnp.inf)
        inter = b_col + m_prev
        m_t = jnp.maximum(inter, jnp.max(dmat, -1, keepdims=True))
        s = _dot_nt(qb, k.astype(BF16)) * jnp.exp(dmat - m_t)
        a_inter = jnp.exp(inter - m_t)
        num = _dot(s.astype(BF16), v) + a_inter * _dot(qb, c_prev.astype(BF16))
        den = jnp.sum(s, -1, keepdims=True) + a_inter * jnp.sum(q * n_prev, -1, keepdims=True)
        hc = num / jnp.maximum(jnp.abs(den), jnp.exp(-m_t))

        mu = jnp.mean(hc, -1, keepdims=True)
        hcc = hc - mu
        var = jnp.mean(hcc * hcc, -1, keepdims=True)
        hn = hcc * lax.rsqrt(var + LN_EPS) * ng_ref[:, h * MLSTM_DV:(h + 1) * MLSTM_DV]
        og = jax.nn.sigmoid(o_ref[:, h * MLSTM_DV:(h + 1) * MLSTM_DV].astype(F32))
        out_ref[:, h * MLSTM_DV:(h + 1) * MLSTM_DV] = (og * hn).astype(out_ref.dtype)

        b_last = b_col[L - 1:L, :]
        g_col = b_last - b_col + i_col
        m_new = jnp.maximum(b_last + m_prev, jnp.max(g_col, 0, keepdims=True))
        kw = k * jnp.exp(g_col - m_new)
        decay = jnp.exp(b_last + m_prev - m_new)
        c_ref[h] = decay * c_prev + _dot(kw.T.astype(BF16), v)
        n_ref[h:h + 1, :] = decay * n_prev + jnp.sum(kw, 0, keepdims=True)
        m_ref[h:h + 1, :] = jnp.broadcast_to(m_new, (1, LANE))


def _mlstm(u, ug, conv_w, conv_b, gate_b, norm_g, B, S):
    T = B * S
    L = MLSTM_L
    nc = S // L
    row = lambda b, c: b * nc + c
    return pl.pallas_call(
        _mlstm_kernel,
        grid=(B, nc),
        in_specs=[pl.BlockSpec((L, 2 * MLSTM_QK), lambda b, c: (row(b, c), 0)),
                  pl.BlockSpec((L, MLSTM_WIDTH), lambda b, c: (row(b, c), 1)),
                  pl.BlockSpec((L, MLSTM_WIDTH), lambda b, c: (row(b, c), 2)),
                  pl.BlockSpec((L, GATE_COLS), lambda b, c: (row(b, c), 0)),
                  pl.BlockSpec((CONV_WIDTH, 2 * MLSTM_QK), lambda b, c: (0, 0)),
                  pl.BlockSpec((1, 2 * MLSTM_QK), lambda b, c: (0, 0)),
                  pl.BlockSpec((1, LANE), lambda b, c: (0, 0)),
                  pl.BlockSpec((1, MLSTM_WIDTH), lambda b, c: (0, 0))],
        out_specs=pl.BlockSpec((L, MLSTM_WIDTH), lambda b, c: (row(b, c), 0)),
        out_shape=jax.ShapeDtypeStruct((T, MLSTM_WIDTH), BF16),
        scratch_shapes=[pltpu.VMEM((L, 2 * MLSTM_QK), F32),
                        pltpu.VMEM((MLSTM_HEADS, MLSTM_DQK, MLSTM_DV), F32),
                        pltpu.VMEM((8, MLSTM_DQK), F32),
                        pltpu.VMEM((8, LANE), F32)],
        compiler_params=_cparams(("parallel", "arbitrary")),
        name="mlstm",
    )(u, u, u, ug, conv_w, conv_b, gate_b, norm_g)


LOG2E = 1.4426950408889634
SLC_CHUNK = 1024
SLC_CHUNK_BLOCKS = SLC_CHUNK // SLC_BLOCK
VT_EXTRA = 16


def _nsa_prep_kernel(q_ref, kv0_ref, kv1_ref, kv2_ref, pos_ref, invf_ref, sgn_ref,
                     qt_ref, kc_ref, vc_ref, ks_ref, vst_ref, kw_ref, vwt_ref):
    ang = pos_ref[...].astype(F32) * invf_ref[...]
    cos = jnp.cos(ang)
    sin = jnp.sin(ang) * sgn_ref[...]

    def rope(x):
        return x * cos + pltpu.roll(x, NSA_HEAD_DIM // 2, 1) * sin

    scale = NSA_HEAD_DIM ** -0.5 * LOG2E
    ts = q_ref.shape[0]
    tok = pl.program_id(0) * ts + lax.broadcasted_iota(I32, (ts, NSA_HEAD_DIM), 0)
    blk_lane = (tok // SLC_BLOCK) % SLC_CHUNK_BLOCKS
    blk_onehot = jnp.where(lax.broadcasted_iota(I32, (ts, NSA_HEAD_DIM), 1) == blk_lane, 1.0, 0.0).astype(BF16)
    ones_rows = jnp.where(lax.broadcasted_iota(I32, (VT_EXTRA, ts), 0) == 0, 1.0, 0.0).astype(BF16)
    head = lambda ref, c: ref[:, c:c + NSA_HEAD_DIM].astype(F32)
    for g in range(NSA_KV_HEADS):
        for h in range(NSA_GROUP):
            qt_ref[g, h] = (rope(head(q_ref, (g * NSA_GROUP + h) * NSA_HEAD_DIM)) * scale).T.astype(BF16)
        c0 = g * NSA_HEAD_DIM
        c1 = NSA_KV_HEADS * NSA_HEAD_DIM + g * NSA_HEAD_DIM
        kc_ref[g] = rope(head(kv0_ref, c0)).astype(BF16)
        vc_ref[g] = kv0_ref[:, c1:c1 + NSA_HEAD_DIM].astype(BF16)
        ks_ref[g, :, 0:NSA_HEAD_DIM] = rope(head(kv1_ref, c0)).astype(BF16)
        ks_ref[g, :, NSA_HEAD_DIM:2 * NSA_HEAD_DIM] = blk_onehot
        vst_ref[g, 0:NSA_HEAD_DIM, :] = head(kv1_ref, c1).T.astype(BF16)
        vst_ref[g, NSA_HEAD_DIM:NSA_HEAD_DIM + VT_EXTRA, :] = ones_rows
        kw_ref[g] = rope(head(kv2_ref, c0)).astype(BF16)
        vwt_ref[g] = head(kv2_ref, c1).T.astype(BF16)


def _nsa_prep(u, pos_col, invf2, sgn, T):
    ts = 512
    kvw = 2 * NSA_KV_HEADS * NSA_HEAD_DIM
    kv_spec = lambda n: pl.BlockSpec((ts, kvw), lambda i: (i, (U_NSA + NSA_WIDTH) // kvw + n))
    row_out = pl.BlockSpec((NSA_KV_HEADS, ts, NSA_HEAD_DIM), lambda i: (0, i, 0))
    row_shape = jax.ShapeDtypeStruct((NSA_KV_HEADS, T, NSA_HEAD_DIM), BF16)
    col_out = pl.BlockSpec((NSA_KV_HEADS, NSA_HEAD_DIM, ts), lambda i: (0, 0, i))
    col_shape = jax.ShapeDtypeStruct((NSA_KV_HEADS, NSA_HEAD_DIM, T), BF16)
    return pl.pallas_call(
        _nsa_prep_kernel,
        grid=(T // ts,),
        in_specs=[pl.BlockSpec((ts, NSA_WIDTH), lambda i: (i, U_NSA // NSA_WIDTH)),
                  kv_spec(0), kv_spec(1), kv_spec(2),
                  pl.BlockSpec((ts, 1), lambda i: (i, 0)),
                  pl.BlockSpec((1, LANE), lambda i: (0, 0)),
                  pl.BlockSpec((1, LANE), lambda i: (0, 0))],
        out_specs=[pl.BlockSpec((NSA_KV_HEADS, NSA_GROUP, NSA_HEAD_DIM, ts), lambda i: (0, 0, 0, i)),
                   row_out, row_out,
                   pl.BlockSpec((NSA_KV_HEADS, ts, 2 * NSA_HEAD_DIM), lambda i: (0, i, 0)),
                   pl.BlockSpec((NSA_KV_HEADS, NSA_HEAD_DIM + VT_EXTRA, ts), lambda i: (0, 0, i)),
                   row_out, col_out],
        out_shape=[jax.ShapeDtypeStruct((NSA_KV_HEADS, NSA_GROUP, NSA_HEAD_DIM, T), BF16),
                   row_shape, row_shape,
                   jax.ShapeDtypeStruct((NSA_KV_HEADS, T, 2 * NSA_HEAD_DIM), BF16),
                   jax.ShapeDtypeStruct((NSA_KV_HEADS, NSA_HEAD_DIM + VT_EXTRA, T), BF16),
                   row_shape, col_shape],
        compiler_params=_cparams(("parallel",)),
        name="nsa_prep",
    )(u, u, u, u, pos_col, invf2, sgn)


def _gelu_tanh(x):
    return 0.5 * x * (1.0 + jnp.tanh(0.7978845608028654 * (x + 0.044715 * x * x * x)))


def _compress_kernel(r_ref, w1_ref, pos_ref, w2_ref, out_ref, *, transpose_out):
    r = r_ref[0]
    nr = r.shape[0]
    a = _dot(r, w1_ref[0])
    b = _dot(r, w1_ref[1])
    c0 = _dot(pos_ref[0], w1_ref[0]) + _dot(pos_ref[1], w1_ref[1])
    pre = a + pltpu.roll(b, nr - 1, 0) + c0[0:1, :]
    out = _dot(_gelu_tanh(pre).astype(BF16), w2_ref[...])
    row = lax.broadcasted_iota(I32, (nr, 1), 0)
    out = jnp.where(row < nr - 1, out, 0.0)
    out_ref[0] = (out.T if transpose_out else out).astype(out_ref.dtype)


def _compress(kv, w1, pos8, w2, B, S, transpose_out):
    nr = S // CMP_STRIDE
    half = CMP_STRIDE * NSA_HEAD_DIM
    r = kv.reshape(NSA_KV_HEADS, B * nr, half)
    if transpose_out:
        out_spec = pl.BlockSpec((1, NSA_HEAD_DIM, nr), lambda g, b: (g, 0, b))
        out_shape = jax.ShapeDtypeStruct((NSA_KV_HEADS, NSA_HEAD_DIM, B * nr), BF16)
    else:
        out_spec = pl.BlockSpec((1, nr, NSA_HEAD_DIM), lambda g, b: (g, b, 0))
        out_shape = jax.ShapeDtypeStruct((NSA_KV_HEADS, B * nr, NSA_HEAD_DIM), BF16)
    return pl.pallas_call(
        functools.partial(_compress_kernel, transpose_out=transpose_out),
        grid=(NSA_KV_HEADS, B),
        in_specs=[pl.BlockSpec((1, nr, half), lambda g, b: (g, b, 0)),
                  pl.BlockSpec((2, half, CMP_HIDDEN), lambda g, b: (0, 0, 0)),
                  pl.BlockSpec((2, 8, half), lambda g, b: (0, 0, 0)),
                  pl.BlockSpec((CMP_HIDDEN, NSA_HEAD_DIM), lambda g, b: (0, 0))],
        out_specs=out_spec,
        out_shape=out_shape,
        compiler_params=_cparams(("parallel", "parallel")),
        name="nsa_compress",
    )(r, w1, pos8, w2)


NSA_COLS = NSA_GROUP * Q_BLOCK


def _load_qt(q_ref):
    return jnp.concatenate([q_ref[0, h] for h in range(NSA_GROUP)], 1)


def _store_heads(o_t, out_ref):
    for h in range(NSA_GROUP):
        out_ref[:, h * NSA_HEAD_DIM:(h + 1) * NSA_HEAD_DIM] = (
            o_t[:, h * Q_BLOCK:(h + 1) * Q_BLOCK].T.astype(out_ref.dtype))


def _tile_heads(x):
    return jnp.concatenate([x] * NSA_GROUP, 1)


def _nsa_cmp_kernel(q_ref, kc_ref, vct_ref, ovt_ref, oc_ref, nb_ref, *, n_slc, topn):
    i = pl.program_id(2)
    nc = kc_ref.shape[1]
    qt = _load_qt(q_ref)
    t = i * Q_BLOCK + lax.broadcasted_iota(I32, (1, Q_BLOCK), 1)
    c_idx = lax.broadcasted_iota(I32, (nc, 1), 0)
    valid = (c_idx * CMP_STRIDE + CMP_BLOCK - 1 <= t) & (c_idx < nc - 1)
    s = _dot(kc_ref[0], qt) + _tile_heads(jnp.where(valid, 0.0, NEG))
    m = jnp.max(s, 0, keepdims=True)
    p = jnp.where(m > 0.5 * NEG, jnp.exp2(s - m), 0.0)
    p = p * (1.0 / jnp.maximum(jnp.sum(p, 0, keepdims=True), 1e-30))
    _store_heads(_dot(vct_ref[0], p.astype(BF16)), oc_ref)
    psum = p[:, 0:Q_BLOCK]
    for h in range(1, NSA_GROUP):
        psum = psum + p[:, h * Q_BLOCK:(h + 1) * Q_BLOCK]
    imp = sum(_dot(ovt_ref[...], part) for part in _split2(psum))

    blk = lax.broadcasted_iota(I32, (n_slc, Q_BLOCK), 0)
    blk_f = blk.astype(F32)
    cur = t // SLC_BLOCK
    causal_blk = blk <= cur
    forced = (blk == 0) | (blk == cur) | (blk == cur - 1)
    score = jnp.where(causal_blk, jnp.where(forced, jnp.inf, imp), -jnp.inf)
    sel = jnp.zeros((n_slc, Q_BLOCK), F32)
    for _ in range(topn):
        mx = jnp.max(score, 0, keepdims=True)
        first = jnp.min(jnp.where(score == mx, blk_f, float(n_slc)), 0, keepdims=True)
        hit = blk_f == first
        sel = jnp.where(hit, 1.0, sel)
        score = jnp.where(hit, -jnp.inf, score)
    nb_ref[0, 0] = jnp.where(causal_blk & (sel > 0.5), 0.0, NEG)


def _nsa_cmp(qt, kcc, vcct, ovt, B, S):
    T = B * S
    nq = S // Q_BLOCK
    nc = S // CMP_STRIDE
    n_slc = S // SLC_BLOCK
    kern = functools.partial(_nsa_cmp_kernel, n_slc=n_slc, topn=min(SLC_TOPN, n_slc))
    return pl.pallas_call(
        kern,
        grid=(NSA_KV_HEADS, B, nq),
        in_specs=[pl.BlockSpec((1, NSA_GROUP, NSA_HEAD_DIM, Q_BLOCK), lambda g, b, i: (g, 0, 0, b * nq + i)),
                  pl.BlockSpec((1, nc, NSA_HEAD_DIM), lambda g, b, i: (g, b, 0)),
                  pl.BlockSpec((1, NSA_HEAD_DIM, nc), lambda g, b, i: (g, 0, b)),
                  pl.BlockSpec((n_slc, nc), lambda g, b, i: (0, 0))],
        out_specs=[pl.BlockSpec((Q_BLOCK, NSA_GROUP * NSA_HEAD_DIM), lambda g, b, i: (b * nq + i, g)),
                   pl.BlockSpec((1, 1, n_slc, Q_BLOCK), lambda g, b, i: (g, b * nq + i, 0, 0))],
        out_shape=[jax.ShapeDtypeStruct((T, NSA_WIDTH), BF16),
                   jax.ShapeDtypeStruct((NSA_KV_HEADS, B * nq, n_slc, Q_BLOCK), F32)],
        compiler_params=_cparams(("parallel", "parallel", "arbitrary")),
        name="nsa_cmp_select",
    )(qt, kcc, vcct, ovt)


def _nsa_slc_kernel(q_ref, nb_ref, ks_ref, vst_ref, os_ref, m_ref, acc_ref, s_ref):
    i = pl.program_id(2)
    qt = _load_qt(q_ref)
    t = i * Q_BLOCK + lax.broadcasted_iota(I32, (1, Q_BLOCK), 1)
    m_ref[...] = jnp.full_like(m_ref, NEG)
    acc_ref[...] = jnp.zeros_like(acc_ref)
    pad = jnp.zeros((NSA_HEAD_DIM - SLC_CHUNK_BLOCKS, Q_BLOCK), F32)

    def scores(c, slot):
        start = pl.multiple_of(c * SLC_CHUNK, SLC_CHUNK)
        k = ks_ref[0, pl.ds(start, SLC_CHUNK), :]
        nb = nb_ref[0, 0, pl.ds(pl.multiple_of(c * SLC_CHUNK_BLOCKS, SLC_CHUNK_BLOCKS), SLC_CHUNK_BLOCKS), :]
        mask_rows = jnp.concatenate([nb, pad], 0).astype(BF16)
        rhs = jnp.concatenate([qt, _tile_heads(mask_rows)], 0)
        s_ref[slot] = _dot(k, rhs)

    def absorb(c, slot, diagonal):
        start = pl.multiple_of(c * SLC_CHUNK, SLC_CHUNK)
        vt = vst_ref[0, :, pl.ds(start, SLC_CHUNK)]
        s = s_ref[slot]
        if diagonal:
            key = start + lax.broadcasted_iota(I32, (SLC_CHUNK, 1), 0)
            s = s + _tile_heads(jnp.where(key <= t, 0.0, NEG))
        m_old = m_ref[...]
        m_new = jnp.maximum(m_old, jnp.max(s, 0, keepdims=True))
        p = jnp.exp2(s - m_new).astype(BF16)
        acc_ref[...] = jnp.exp2(m_old - m_new) * acc_ref[...] + _dot(vt, p)
        m_ref[...] = m_new

    n_full = (i * Q_BLOCK) // SLC_CHUNK
    scores(0, 0)

    def body(j, carry):
        scores(2 * j + 1, 1)
        absorb(2 * j, 0, False)
        scores(2 * j + 2, 0)
        absorb(2 * j + 1, 1, False)
        return carry

    lax.fori_loop(0, n_full // 2, body, 0)

    @pl.when(n_full % 2 == 1)
    def _():
        scores(n_full, 1)
        absorb(n_full - 1, 0, False)
        absorb(n_full, 1, True)

    @pl.when(n_full % 2 == 0)
    def _():
        absorb(n_full, 0, True)

    ok = m_ref[...] > 0.5 * NEG
    l = acc_ref[NSA_HEAD_DIM:NSA_HEAD_DIM + 1, :]
    inv = jnp.where(ok, 1.0 / jnp.where(ok, l, 1.0), 0.0)
    _store_heads(acc_ref[0:NSA_HEAD_DIM, :] * inv, os_ref)


def _nsa_slc(qt, nb, ks, vst, B, S):
    T = B * S
    nq = S // Q_BLOCK
    n_slc = S // SLC_BLOCK
    return pl.pallas_call(
        _nsa_slc_kernel,
        grid=(NSA_KV_HEADS, B, nq),
        in_specs=[pl.BlockSpec((1, NSA_GROUP, NSA_HEAD_DIM, Q_BLOCK), lambda g, b, i: (g, 0, 0, b * nq + i)),
                  pl.BlockSpec((1, 1, n_slc, Q_BLOCK), lambda g, b, i: (g, b * nq + i, 0, 0)),
                  pl.BlockSpec((1, S, 2 * NSA_HEAD_DIM), lambda g, b, i: (g, b, 0)),
                  pl.BlockSpec((1, NSA_HEAD_DIM + VT_EXTRA, S), lambda g, b, i: (g, 0, b))],
        out_specs=pl.BlockSpec((Q_BLOCK, NSA_GROUP * NSA_HEAD_DIM), lambda g, b, i: (b * nq + i, g)),
        out_shape=jax.ShapeDtypeStruct((T, NSA_WIDTH), BF16),
        scratch_shapes=[pltpu.VMEM((1, NSA_COLS), F32),
                        pltpu.VMEM((NSA_HEAD_DIM + VT_EXTRA, NSA_COLS), F32),
                        pltpu.VMEM((2, SLC_CHUNK, NSA_COLS), F32)],
        compiler_params=_cparams(("parallel", "parallel", "arbitrary")),
        name="nsa_selected",
    )(qt, nb, ks, vst)


N_WIN_BLOCKS = WINDOW // Q_BLOCK + 1


def _nsa_win_kernel(q_ref, *refs):
    k_refs = refs[:N_WIN_BLOCKS]
    v_refs = refs[N_WIN_BLOCKS:2 * N_WIN_BLOCKS]
    ow_ref = refs[2 * N_WIN_BLOCKS]
    i = pl.program_id(2)
    qt = _load_qt(q_ref)
    k = jnp.concatenate([r[0] for r in k_refs], 0)
    vt = jnp.concatenate([r[0] for r in v_refs], 1)
    nk = N_WIN_BLOCKS * Q_BLOCK
    t = i * Q_BLOCK + lax.broadcasted_iota(I32, (1, Q_BLOCK), 1)
    wpos = i * Q_BLOCK - WINDOW + lax.broadcasted_iota(I32, (nk, 1), 0)
    ok = (wpos <= t) & (wpos > t - WINDOW) & (wpos >= 0)
    s = _dot(k, qt) + _tile_heads(jnp.where(ok, 0.0, NEG))
    m = jnp.max(s, 0, keepdims=True)
    p = jnp.exp2(s - m)
    o_t = _dot(vt, p.astype(BF16)) * (1.0 / jnp.sum(p, 0, keepdims=True))
    _store_heads(o_t, ow_ref)


def _nsa_win(qt, kw, vwt, B, S):
    T = B * S
    nq = S // Q_BLOCK

    def k_spec(c):
        back = N_WIN_BLOCKS - 1 - c
        return pl.BlockSpec((1, Q_BLOCK, NSA_HEAD_DIM),
                            lambda g, b, i: (g, b * nq + jnp.maximum(i - back, 0), 0))

    def v_spec(c):
        back = N_WIN_BLOCKS - 1 - c
        return pl.BlockSpec((1, NSA_HEAD_DIM, Q_BLOCK),
                            lambda g, b, i: (g, 0, b * nq + jnp.maximum(i - back, 0)))

    return pl.pallas_call(
        _nsa_win_kernel,
        grid=(NSA_KV_HEADS, B, nq),
        in_specs=[pl.BlockSpec((1, NSA_GROUP, NSA_HEAD_DIM, Q_BLOCK), lambda g, b, i: (g, 0, 0, b * nq + i))]
        + [k_spec(c) for c in range(N_WIN_BLOCKS)] + [v_spec(c) for c in range(N_WIN_BLOCKS)],
        out_specs=pl.BlockSpec((Q_BLOCK, NSA_GROUP * NSA_HEAD_DIM), lambda g, b, i: (b * nq + i, g)),
        out_shape=jax.ShapeDtypeStruct((T, NSA_WIDTH), BF16),
        compiler_params=_cparams(("parallel", "parallel", "arbitrary")),
        name="nsa_window",
    )(qt, *([kw] * N_WIN_BLOCKS), *([vwt] * N_WIN_BLOCKS))


def _outproj_kernel(hm_ref, oc_ref, os_ref, ow_ref, gt_ref, gb_ref, ge_ref, h_ref, w_ref, g_ref, b_ref,
                    out_ref):
    gates = jax.nn.sigmoid(gt_ref[...] + gb_ref[...]).astype(BF16)
    gx = _dot(gates, ge_ref[...])
    hn = (gx[:, 0:NSA_WIDTH] * oc_ref[...].astype(F32)
          + gx[:, NSA_WIDTH:2 * NSA_WIDTH] * os_ref[...].astype(F32)
          + gx[:, 2 * NSA_WIDTH:3 * NSA_WIDTH] * ow_ref[...].astype(F32))
    mix = (_dot(hm_ref[...], w_ref[0:MLSTM_WIDTH, :])
           + _dot(hn.astype(BF16), w_ref[MLSTM_WIDTH:MLSTM_WIDTH + NSA_WIDTH, :]))
    out_ref[...] = _layer_norm(DN_ALPHA * h_ref[...] + mix, g_ref[...], b_ref[...])


def _outproj(hm, oc, os_, ow, ug, gate_b, gate_expand, h, w_out, g, b):
    T = h.shape[0]
    tm = 256
    row = lambda i: (i, 0)
    fixed = lambda i: (0, 0)
    return pl.pallas_call(
        _outproj_kernel,
        grid=(T // tm,),
        in_specs=[pl.BlockSpec((tm, MLSTM_WIDTH), row), pl.BlockSpec((tm, NSA_WIDTH), row),
                  pl.BlockSpec((tm, NSA_WIDTH), row), pl.BlockSpec((tm, NSA_WIDTH), row),
                  pl.BlockSpec((tm, GATE_COLS), row),
                  pl.BlockSpec((1, LANE), fixed),
                  pl.BlockSpec((LANE, 3 * NSA_WIDTH), fixed),
                  pl.BlockSpec((tm, D_MODEL), row),
                  pl.BlockSpec((D_MODEL, D_MODEL), fixed),
                  pl.BlockSpec((1, D_MODEL), fixed), pl.BlockSpec((1, D_MODEL), fixed)],
        out_specs=pl.BlockSpec((tm, D_MODEL), row),
        out_shape=jax.ShapeDtypeStruct((T, D_MODEL), F32),
        compiler_params=_cparams(("parallel",)),
        name="mixer_outproj_ln",
    )(hm, oc, os_, ow, ug, gate_b, gate_expand, h, w_out, g, b)


def _matmul_kernel(x_ref, w_ref, o_ref):
    o_ref[...] = _dot(x_ref[...].astype(BF16), w_ref[...]).astype(o_ref.dtype)


def _mem_kv(mem2, wkv):
    M = mem2.shape[0]
    N = wkv.shape[1]
    tn = 512
    return pl.pallas_call(
        _matmul_kernel,
        grid=(N // tn,),
        in_specs=[pl.BlockSpec((M, D_MODEL), lambda j: (0, 0)),
                  pl.BlockSpec((D_MODEL, tn), lambda j: (0, j))],
        out_specs=pl.BlockSpec((M, tn), lambda j: (0, j)),
        out_shape=jax.ShapeDtypeStruct((M, N), BF16),
        compiler_params=_cparams(("parallel",)),
        name="mem_kv_proj",
    )(mem2, wkv)


def _xattn_kernel(h_ref, wq_ref, kv_ref, o_ref):
    hb = h_ref[...].astype(BF16)
    for hd in range(XA_HEADS):
        c0 = hd * XA_HEAD_DIM
        q = (_dot(hb, wq_ref[:, c0:c0 + XA_HEAD_DIM]) * (XA_HEAD_DIM ** -0.5)).astype(BF16)
        s = _dot_nt(q, kv_ref[:, c0:c0 + XA_HEAD_DIM])
        m = jnp.max(s, -1, keepdims=True)
        p = jnp.exp(s - m)
        o = _dot(p.astype(BF16), kv_ref[:, D_MODEL + c0:D_MODEL + c0 + XA_HEAD_DIM])
        o_ref[:, c0:c0 + XA_HEAD_DIM] = (o / jnp.sum(p, -1, keepdims=True)).astype(o_ref.dtype)


def _xattn(h1, wq, kv, B, S):
    T = B * S
    tm = 512
    n_mem = kv.shape[0] // B
    per_b = S // tm
    return pl.pallas_call(
        _xattn_kernel,
        grid=(T // tm,),
        in_specs=[pl.BlockSpec((tm, D_MODEL), lambda i: (i, 0)),
                  pl.BlockSpec((D_MODEL, D_MODEL), lambda i: (0, 0)),
                  pl.BlockSpec((n_mem, 2 * D_MODEL), lambda i: (i // per_b, 0))],
        out_specs=pl.BlockSpec((tm, D_MODEL), lambda i: (i, 0)),
        out_shape=jax.ShapeDtypeStruct((T, D_MODEL), BF16),
        compiler_params=_cparams(("parallel",)),
        name="mem_xattn",
    )(h1, wq, kv)


def _xa_out_router_kernel(o_ref, wo_ref, h_ref, g_ref, b_ref, rw_ref, rb_ref,
                          h2_ref, idx_ref, wgt_ref):
    xa = _dot(o_ref[...], wo_ref[...])
    h2 = _layer_norm(DN_ALPHA * h_ref[...] + xa, g_ref[...], b_ref[...])
    h2_ref[...] = h2
    hi, lo = _split2(h2)
    logits = _dot(hi, rw_ref[0]) + _dot(hi, rw_ref[1]) + _dot(lo, rw_ref[0])
    scores = jax.nn.sigmoid(logits)
    lane = lax.broadcasted_iota(I32, scores.shape, 1)
    lane_f = lane.astype(F32)
    biased = jnp.where(lane < N_EXPERTS, scores + rb_ref[...], -jnp.inf)
    idx_mat = jnp.zeros(scores.shape, F32)
    w_mat = jnp.zeros(scores.shape, F32)
    for kk in range(TOP_K):
        mx = jnp.max(biased, -1, keepdims=True)
        first = jnp.min(jnp.where(biased == mx, lane_f, float(LANE)), -1, keepdims=True)
        hit = lane_f == first
        top_s = jnp.sum(jnp.where(hit, scores, 0.0), -1, keepdims=True)
        idx_mat = jnp.where(lane == kk, first, idx_mat)
        w_mat = jnp.where(lane == kk, top_s, w_mat)
        biased = jnp.where(hit, -jnp.inf, biased)
    idx_ref[...] = idx_mat.astype(I32)
    wgt_ref[...] = w_mat / jnp.sum(w_mat, -1, keepdims=True) * ROUTED_SCALE


def _xa_out_router(o, wo, h1, g, b, rw2, rb):
    T = h1.shape[0]
    tm = 512
    row = lambda i: (i, 0)
    fixed = lambda i: (0, 0)
    return pl.pallas_call(
        _xa_out_router_kernel,
        grid=(T // tm,),
        in_specs=[pl.BlockSpec((tm, D_MODEL), row),
                  pl.BlockSpec((D_MODEL, D_MODEL), fixed),
                  pl.BlockSpec((tm, D_MODEL), row),
                  pl.BlockSpec((1, D_MODEL), fixed), pl.BlockSpec((1, D_MODEL), fixed),
                  pl.BlockSpec((2, D_MODEL, LANE), lambda i: (0, 0, 0)),
                  pl.BlockSpec((1, LANE), fixed)],
        out_specs=[pl.BlockSpec((tm, D_MODEL), row), pl.BlockSpec((tm, LANE), row),
                   pl.BlockSpec((tm, LANE), row)],
        out_shape=[jax.ShapeDtypeStruct((T, D_MODEL), F32),
                   jax.ShapeDtypeStruct((T, LANE), I32),
                   jax.ShapeDtypeStruct((T, LANE), F32)],
        compiler_params=_cparams(("parallel",)),
        name="xattn_out_ln_router",
    )(o, wo, h1, g, b, rw2, rb)


ROUTE_TM = 512
ZERO_ROWS = 128


def _route_rank_kernel(idx_ref, rank_ref, cnt_ref):
    tm = idx_ref.shape[0]

    @pl.when(pl.program_id(0) == 0)
    def _():
        cnt_ref[...] = jnp.zeros_like(cnt_ref)

    idx = idx_ref[...]
    lane = lax.broadcasted_iota(I32, (tm, LANE), 1)
    hits = [lane == idx[:, kk:kk + 1] for kk in range(TOP_K)]
    onehot = sum(jnp.where(hit, 1.0, 0.0) for hit in hits)
    r_i = lax.broadcasted_iota(I32, (tm, tm), 0)
    c_i = lax.broadcasted_iota(I32, (tm, tm), 1)
    before = jnp.where(r_i > c_i, 1.0, 0.0).astype(BF16)
    rank = _dot(before, onehot.astype(BF16)) + cnt_ref[0:1, :]
    out = jnp.zeros((tm, LANE), F32)
    for kk in range(TOP_K):
        out = jnp.where(lane == kk, jnp.sum(jnp.where(hits[kk], rank, 0.0), -1, keepdims=True), out)
    rank_ref[...] = out.astype(I32)
    cnt_ref[0:1, :] = cnt_ref[0:1, :] + jnp.sum(onehot, 0, keepdims=True)


def _route_rank(top_idx):
    T = top_idx.shape[0]
    tm = ROUTE_TM
    return pl.pallas_call(
        _route_rank_kernel,
        grid=(T // tm,),
        in_specs=[pl.BlockSpec((tm, LANE), lambda i: (i, 0))],
        out_specs=[pl.BlockSpec((tm, LANE), lambda i: (i, 0)),
                   pl.BlockSpec((8, LANE), lambda i: (0, 0))],
        out_shape=[jax.ShapeDtypeStruct((T, LANE), I32), jax.ShapeDtypeStruct((8, LANE), F32)],
        compiler_params=_cparams(("arbitrary",)),
        name="moe_route_rank",
    )(top_idx)


def _load_route(i, idx_hbm, rank_hbm, idx_smem, rank_smem, isem):
    c0 = pltpu.make_async_copy(idx_hbm.at[i], idx_smem, isem.at[0])
    c1 = pltpu.make_async_copy(rank_hbm.at[i], rank_smem, isem.at[1])
    c0.start()
    c1.start()
    c0.wait()
    c1.wait()


_FILL_SIZES = tuple(s for s in (ZERO_ROWS >> n for n in range(ZERO_ROWS.bit_length())) if s >= SUBLANE)


def _dispatch_kernel(ps_ref, fs_ref, fn_ref, tail_ref, idx_hbm, rank_hbm, h_ref, xs_ref,
                     idx_smem, rank_smem, zero_ref, isem, sem, zsem):
    i = pl.program_id(0)
    _load_route(i, idx_hbm, rank_hbm, idx_smem, rank_smem, isem)

    def issue(r, carry):
        for kk in range(TOP_K):
            j = r * TOP_K + kk
            d = ps_ref[idx_smem[j]] + rank_smem[j]
            pltpu.make_async_copy(h_ref.at[pl.ds(r, 1)], xs_ref.at[pl.ds(d, 1)], sem).start()
        return carry

    lax.fori_loop(0, ROUTE_TM, issue, 0)

    @pl.when(i == 0)
    def _():
        zero_ref[...] = jnp.zeros_like(zero_ref)

        def fill_copies(e, go):
            n = fn_ref[e]
            first = fs_ref[e]
            lead = jnp.minimum((-first) & (SUBLANE - 1), n)
            for r in range(SUBLANE - 1):
                @pl.when(r < lead)
                def _():
                    go(pltpu.make_async_copy(zero_ref.at[pl.ds(0, 1)], xs_ref.at[pl.ds(first + r, 1)], zsem))

            rest = n - lead
            off = first + lead
            for size in _FILL_SIZES:
                take = (rest & size) != 0

                @pl.when(take)
                def _():
                    dst = xs_ref.at[pl.ds(pl.multiple_of(off, SUBLANE), size)]
                    go(pltpu.make_async_copy(zero_ref.at[pl.ds(0, size)], dst, zsem))
                off = off + jnp.where(take, size, 0)

        def tail_copies(j, go):
            off = pl.multiple_of(tail_ref[0] + j * ZERO_ROWS, ZERO_ROWS)
            go(pltpu.make_async_copy(zero_ref, xs_ref.at[pl.ds(off, ZERO_ROWS)], zsem))

        for go in (lambda cp: cp.start(), lambda cp: cp.wait()):
            lax.fori_loop(0, N_EXPERTS, lambda e, c, go=go: (fill_copies(e, go), c)[1], 0)
            lax.fori_loop(0, tail_ref[1], lambda j, c, go=go: (tail_copies(j, go), c)[1], 0)

    for _ in range(TOP_K):
        pltpu.make_async_copy(h_ref, xs_ref.at[pl.ds(0, ROUTE_TM)], sem).wait()


def _dispatch(pad_start, fill_start, fill_n, tail, idx2, rank2, h2, P):
    T = h2.shape[0]
    tm = ROUTE_TM
    return pl.pallas_call(
        _dispatch_kernel,
        grid_spec=pltpu.PrefetchScalarGridSpec(
            num_scalar_prefetch=4,
            grid=(T // tm,),
            in_specs=[pl.BlockSpec(memory_space=pl.ANY), pl.BlockSpec(memory_space=pl.ANY),
                      pl.BlockSpec((tm, D_MODEL), lambda i, *_: (i, 0))],
            out_specs=pl.BlockSpec(memory_space=pl.ANY),
            scratch_shapes=[pltpu.SMEM((tm * TOP_K,), I32), pltpu.SMEM((tm * TOP_K,), I32),
                            pltpu.VMEM((ZERO_ROWS, D_MODEL), F32),
                            pltpu.SemaphoreType.DMA((2,)), pltpu.SemaphoreType.DMA, pltpu.SemaphoreType.DMA]),
        out_shape=jax.ShapeDtypeStruct((P, D_MODEL), F32),
        compiler_params=_cparams(("arbitrary",)),
        name="moe_dispatch",
    )(pad_start, fill_start, fill_n, tail, idx2, rank2, h2)


def _experts_kernel(be_ref, nu_ref, x_ref, wgu_ref, wd_ref, y_ref):
    used = pl.program_id(0) < nu_ref[0]

    @pl.when(used)
    def _():
        gu = _dot(x_ref[...].astype(BF16), wgu_ref[0])
        a = gu[:, 0:D_EXPERT]
        act = (a * jax.nn.sigmoid(a)) * gu[:, D_EXPERT:2 * D_EXPERT]
        y_ref[...] = _dot(act.astype(BF16), wd_ref[0])

    @pl.when(jnp.logical_not(used))
    def _():
        y_ref[...] = jnp.zeros_like(y_ref)


def _experts(block_e, n_used, xs, wgu, wd):
    P = xs.shape[0]
    bm = EXPERT_BM
    rowmap = lambda j, be, nu: (jnp.minimum(j, nu[0] - 1), 0)
    wmap = lambda j, be, nu: (be[j], 0, 0)
    return pl.pallas_call(
        _experts_kernel,
        grid_spec=pltpu.PrefetchScalarGridSpec(
            num_scalar_prefetch=2,
            grid=(P // bm,),
            in_specs=[pl.BlockSpec((bm, D_MODEL), rowmap),
                      pl.BlockSpec((1, D_MODEL, 2 * D_EXPERT), wmap),
                      pl.BlockSpec((1, D_EXPERT, D_MODEL), wmap)],
            out_specs=pl.BlockSpec((bm, D_MODEL), lambda j, be, nu: (j, 0))),
        out_shape=jax.ShapeDtypeStruct((P, D_MODEL), F32),
        compiler_params=_cparams(("arbitrary",)),
        name="moe_experts",
    )(block_e, n_used, xs, wgu, wd)


def _shared_ffn_kernel(x_ref, wg_ref, wu_ref, wd_ref, o_ref, xb_ref):
    f = pl.program_id(1)

    @pl.when(f == 0)
    def _():
        xb_ref[...] = x_ref[...].astype(BF16)
        o_ref[...] = jnp.zeros_like(o_ref)

    xb = xb_ref[...]
    a = _dot(xb, wg_ref[...])
    act = (a * jax.nn.sigmoid(a)) * _dot(xb, wu_ref[...])
    o_ref[...] += _dot(act.astype(BF16), wd_ref[...])


def _shared_ffn(h2, wg, wu, wd):
    T = h2.shape[0]
    tm, tf = 512, 256
    return pl.pallas_call(
        _shared_ffn_kernel,
        grid=(T // tm, D_SHARED // tf),
        in_specs=[pl.BlockSpec((tm, D_MODEL), lambda i, f: (i, 0)),
                  pl.BlockSpec((D_MODEL, tf), lambda i, f: (0, f)),
                  pl.BlockSpec((D_MODEL, tf), lambda i, f: (0, f)),
                  pl.BlockSpec((tf, D_MODEL), lambda i, f: (f, 0))],
        out_specs=pl.BlockSpec((tm, D_MODEL), lambda i, f: (i, 0)),
        out_shape=jax.ShapeDtypeStruct((T, D_MODEL), F32),
        scratch_shapes=[pltpu.VMEM((tm, D_MODEL), BF16)],
        compiler_params=_cparams(("parallel", "arbitrary")),
        name="shared_ffn",
    )(h2, wg, wu, wd)


COMBINE_SUB = 128
COMBINE_NSUB = ROUTE_TM // COMBINE_SUB


def _combine_kernel(ps_ref, idx_hbm, rank_hbm, ys_hbm, w_ref, sh_ref, h_ref, g_ref, b_ref, out_ref,
                    idx_smem, rank_smem, gbuf, isem, sem):
    i = pl.program_id(0)
    _load_route(i, idx_hbm, rank_hbm, idx_smem, rank_smem, isem)

    def gather(sub):
        slot = sub % 2

        def issue(r, carry):
            for kk in range(TOP_K):
                j = (sub * COMBINE_SUB + r) * TOP_K + kk
                d = ps_ref[idx_smem[j]] + rank_smem[j]
                pltpu.make_async_copy(ys_hbm.at[pl.ds(d, 1)], gbuf.at[slot, kk, pl.ds(r, 1)],
                                      sem.at[slot]).start()
            return carry

        lax.fori_loop(0, COMBINE_SUB, issue, 0)

    gather(0)
    for sub in range(COMBINE_NSUB):
        slot = sub % 2
        if sub + 1 < COMBINE_NSUB:
            gather(sub + 1)
        for kk in range(TOP_K):
            pltpu.make_async_copy(ys_hbm.at[pl.ds(0, COMBINE_SUB)], gbuf.at[slot, kk], sem.at[slot]).wait()
        rs = slice(sub * COMBINE_SUB, (sub + 1) * COMBINE_SUB)
        w = w_ref[rs, :]
        routed = w[:, 0:1] * gbuf[slot, 0]
        for kk in range(1, TOP_K):
            routed = routed + w[:, kk:kk + 1] * gbuf[slot, kk]
        z = DN_ALPHA * h_ref[rs, :] + (routed + sh_ref[rs, :])
        out_ref[rs, :] = _layer_norm(z, g_ref[...], b_ref[...])


def _combine(pad_start, idx2, rank2, ys, top_w, sh, h2, g, b):
    T = h2.shape[0]
    tm = ROUTE_TM
    row = lambda i, *_: (i, 0)
    fixed = lambda i, *_: (0, 0)
    return pl.pallas_call(
        _combine_kernel,
        grid_spec=pltpu.PrefetchScalarGridSpec(
            num_scalar_prefetch=1,
            grid=(T // tm,),
            in_specs=[pl.BlockSpec(memory_space=pl.ANY), pl.BlockSpec(memory_space=pl.ANY),
                      pl.BlockSpec(memory_space=pl.ANY),
                      pl.BlockSpec((tm, LANE), row), pl.BlockSpec((tm, D_MODEL), row),
                      pl.BlockSpec((tm, D_MODEL), row),
                      pl.BlockSpec((1, D_MODEL), fixed), pl.BlockSpec((1, D_MODEL), fixed)],
            out_specs=pl.BlockSpec((tm, D_MODEL), row),
            scratch_shapes=[pltpu.SMEM((tm * TOP_K,), I32), pltpu.SMEM((tm * TOP_K,), I32),
                            pltpu.VMEM((2, TOP_K, COMBINE_SUB, D_MODEL), F32),
                            pltpu.SemaphoreType.DMA((2,)), pltpu.SemaphoreType.DMA((2,))]),
        out_shape=jax.ShapeDtypeStruct((T, D_MODEL), F32),
        compiler_params=_cparams(("arbitrary",)),
        name="moe_combine_ln",
    )(pad_start, idx2, rank2, ys, top_w, sh, h2, g, b)


def _route_plan(counts, T):
    bm = EXPERT_BM
    padded = (counts + bm - 1) // bm * bm
    pad_end = jnp.cumsum(padded)
    pad_start = pad_end - padded
    n_blocks = T * TOP_K // bm + N_EXPERTS
    first_row = jnp.arange(n_blocks, dtype=I32) * bm
    block_e = jnp.minimum(jnp.sum((pad_end[None, :] <= first_row[:, None]).astype(I32), 1), N_EXPERTS - 1)
    n_used = (pad_end[-1:] // bm).astype(I32)
    tail = jnp.concatenate([pad_end[-1:], (n_blocks - n_used) * (bm // ZERO_ROWS)]).astype(I32)
    return (pad_start.astype(I32), (pad_start + counts).astype(I32), (padded - counts).astype(I32), tail,
            block_e, n_used, n_blocks * bm)


def _overlap_matrix(S):
    n_cmp_rows = S // CMP_STRIDE
    n_slc = S // SLC_BLOCK
    c_lo = np.arange(n_cmp_rows)[:, None] * CMP_STRIDE
    j_lo = np.arange(n_slc)[None, :] * SLC_BLOCK
    ov = (c_lo <= j_lo + SLC_BLOCK - 1) & (c_lo + CMP_BLOCK - 1 >= j_lo)
    return jnp.asarray(ov.T.astype(np.float32), dtype=BF16)


def _gate_expand_matrix():
    ge = np.zeros((LANE, 3 * NSA_WIDTH), np.float32)
    for hh in range(NSA_HEADS):
        for br in range(3):
            ge[8 + hh * 3 + br, br * NSA_WIDTH + hh * NSA_HEAD_DIM:br * NSA_WIDTH + (hh + 1) * NSA_HEAD_DIM] = 1.0
    return jnp.asarray(ge, dtype=BF16)


def kernel(x, mem, positions, ln0_g, ln0_b, w_in, conv_w, conv_b, igate_b, fgate_b, mlstm_norm_g, cmp_pos, cmp_w1k, cmp_w2k, cmp_w1v, cmp_w2v, nsa_gate_b, w_out, ln1_g, ln1_b, xa_wq, xa_wk, xa_wv, xa_wo, ln2_g, ln2_b, router_w, router_bias, moe_w_gate, moe_w_up, moe_w_down, sh_w_gate, sh_w_up, sh_w_down, ln3_g, ln3_b):
    B, S, D = x.shape
    T = B * S
    assert D == D_MODEL and w_in.shape[0] == DEPTH == 1
    assert S % MLSTM_L == 0 and S % SLC_CHUNK == 0 and T % ROUTE_TM == 0 and (T * TOP_K) % EXPERT_BM == 0
    row = lambda a: a.reshape(1, -1)

    w = w_in[0]
    w_r = jnp.concatenate([w[:, :3072], w[:, 3080:5640]], 1).astype(BF16)
    w_g = jnp.concatenate([w[:, 3072:3080], w[:, 5640:5664],
                           jnp.zeros((D, GATE_COLS - 2 * MLSTM_HEADS - 3 * NSA_HEADS), F32)], 1).astype(BF16)
    gate_b = jnp.concatenate([igate_b[0], fgate_b[0], nsa_gate_b[0],
                              jnp.zeros((LANE - 2 * MLSTM_HEADS - 3 * NSA_HEADS,), F32)]).reshape(1, LANE)
    half = NSA_HEAD_DIM // 2
    inv_freq = ROPE_THETA ** (-jnp.arange(half, dtype=F32) / half)
    invf2 = jnp.concatenate([inv_freq, inv_freq]).reshape(1, LANE)
    sgn = jnp.concatenate([-jnp.ones((half,), F32), jnp.ones((half,), F32)]).reshape(1, LANE)
    pos8 = jnp.zeros((2, 8, CMP_STRIDE * NSA_HEAD_DIM), F32).at[:, 0, :].set(
        cmp_pos[0].reshape(2, CMP_STRIDE * NSA_HEAD_DIM)).astype(BF16)
    w1k = cmp_w1k[0].reshape(2, CMP_STRIDE * NSA_HEAD_DIM, CMP_HIDDEN).astype(BF16)
    w1v = cmp_w1v[0].reshape(2, CMP_STRIDE * NSA_HEAD_DIM, CMP_HIDDEN).astype(BF16)
    rw = jnp.pad(router_w[0], ((0, 0), (0, LANE - N_EXPERTS)))
    rw_hi = rw.astype(BF16)
    rw2 = jnp.stack([rw_hi, (rw - rw_hi.astype(F32)).astype(BF16)])
    rb = jnp.pad(router_bias[0], (0, LANE - N_EXPERTS)).reshape(1, LANE)

    h, u, ug = _ln_inproj(x.reshape(T, D), row(ln0_g), row(ln0_b), w_r, w_g)
    hm = _mlstm(u, ug, conv_w[0], row(conv_b[0]), gate_b, row(mlstm_norm_g[0]), B, S)
    qt, kc, vc, ks, vst, kw, vwt = _nsa_prep(u, positions.reshape(T, 1), invf2, sgn, T)
    kcc = _compress(kc, w1k, pos8, cmp_w2k[0].astype(BF16), B, S, False)
    vcct = _compress(vc, w1v, pos8, cmp_w2v[0].astype(BF16), B, S, True)
    oc, nb = _nsa_cmp(qt, kcc, vcct, _overlap_matrix(S), B, S)
    os_ = _nsa_slc(qt, nb, ks, vst, B, S)
    ow = _nsa_win(qt, kw, vwt, B, S)
    h1 = _outproj(hm, oc, os_, ow, ug, gate_b, _gate_expand_matrix(), h, w_out[0].astype(BF16),
                  row(ln1_g[0]), row(ln1_b[0]))

    wkv = jnp.concatenate([xa_wk[0], xa_wv[0]], 1).astype(BF16)
    kv = _mem_kv(mem.reshape(-1, D), wkv)
    xo = _xattn(h1, xa_wq[0].astype(BF16), kv, B, S)
    h2, top_idx, top_w = _xa_out_router(xo, xa_wo[0].astype(BF16), h1, row(ln2_g[0]), row(ln2_b[0]), rw2, rb)

    rank, cnt = _route_rank(top_idx)
    counts = cnt[0, :N_EXPERTS].astype(I32)
    pad_start, fill_start, fill_n, tail, block_e, n_used, P = _route_plan(counts, T)
    idx2 = top_idx[:, :TOP_K].reshape(T // ROUTE_TM, ROUTE_TM * TOP_K)
    rank2 = rank[:, :TOP_K].reshape(T // ROUTE_TM, ROUTE_TM * TOP_K)
    xs = _dispatch(pad_start, fill_start, fill_n, tail, idx2, rank2, h2, P)
    w_gu = jnp.concatenate([moe_w_gate[0].astype(BF16), moe_w_up[0].astype(BF16)], -1)
    ys = _experts(block_e, n_used, xs, w_gu, moe_w_down[0].astype(BF16))
    sh = _shared_ffn(h2, sh_w_gate[0].astype(BF16), sh_w_up[0].astype(BF16), sh_w_down[0].astype(BF16))
    out = _combine(pad_start, idx2, rank2, ys, top_w, sh, h2, row(ln3_g[0]), row(ln3_b[0]))
    return out.reshape(B, S, D)
```

```python
import functools

import numpy as np
import jax
import jax.numpy as jnp
from jax import lax
from jax.experimental import pallas as pl
from jax.experimental.pallas import tpu as pltpu

F32 = jnp.float32
BF16 = jnp.bfloat16
I32 = jnp.int32

D_MODEL = 2048
MLSTM_HEADS = 4
MLSTM_DV = 256
MLSTM_DQK = 128
MLSTM_QK = MLSTM_HEADS * MLSTM_DQK
MLSTM_WIDTH = MLSTM_HEADS * MLSTM_DV
CONV_WIDTH = 4
NSA_HEAD_DIM = 128
NSA_HEADS = 8
NSA_KV_HEADS = 2
NSA_GROUP = 4
NSA_WIDTH = NSA_HEADS * NSA_HEAD_DIM
CMP_BLOCK = 32
CMP_STRIDE = 16
CMP_HIDDEN = 256
SLC_BLOCK = 64
SLC_TOPN = 16
WINDOW = 512
Q_BLOCK = 128
XA_HEADS = 4
XA_HEAD_DIM = 512
N_EXPERTS = 64
TOP_K = 6
D_EXPERT = 1408
D_SHARED = 2816
ROUTED_SCALE = 2.446
ROPE_THETA = 10000.0
LN_EPS = 1e-5
DEPTH = 1
DN_ALPHA = (2.0 * DEPTH) ** 0.25

U_MLSTM = 0
U_NSA = 3072
U_COLS = 5632
GATE_COLS = 128

LANE = 128
SUBLANE = 8
NEG = -1e30
MLSTM_L = 256
VMEM_LIMIT = 56 * 1024 * 1024
EXPERT_BM = 256


def _cparams(sem):
    return pltpu.CompilerParams(dimension_semantics=sem, vmem_limit_bytes=VMEM_LIMIT)


def _dot(a, b):
    return jnp.dot(a, b, preferred_element_type=F32)


def _dot_nt(a, b):
    return lax.dot_general(a, b, (((1,), (1,)), ((), ())), preferred_element_type=F32)


def _layer_norm(z, g, b):
    mu = jnp.mean(z, -1, keepdims=True)
    zc = z - mu
    var = jnp.mean(zc * zc, -1, keepdims=True)
    return zc * lax.rsqrt(var + LN_EPS) * g + b


def _split3(x):
    hi = x.astype(BF16)
    r = x - hi.astype(F32)
    mid = r.astype(BF16)
    lo = (r - mid.astype(F32)).astype(BF16)
    return hi, mid, lo


def _split2(x):
    hi = x.astype(BF16)
    lo = (x - hi.astype(F32)).astype(BF16)
    return hi, lo


def _ln_inproj_kernel(x_ref, g_ref, b_ref, w_ref, wg_ref, h_ref, u_ref, ug_ref, hb_ref):
    @pl.when(pl.program_id(1) == 0)
    def _():
        hn = _layer_norm(x_ref[...], g_ref[...], b_ref[...])
        h_ref[...] = hn
        hb_ref[...] = hn.astype(BF16)
        ug_ref[...] = _dot(hb_ref[...], wg_ref[...])

    u_ref[...] = _dot(hb_ref[...], w_ref[...]).astype(u_ref.dtype)


def _ln_inproj(x2, g, b, w, w_gate):
    T = x2.shape[0]
    tm, tn = 1024, 512
    return pl.pallas_call(
        _ln_inproj_kernel,
        grid=(T // tm, U_COLS // tn),
        in_specs=[pl.BlockSpec((tm, D_MODEL), lambda i, j: (i, 0)),
                  pl.BlockSpec((1, D_MODEL), lambda i, j: (0, 0)),
                  pl.BlockSpec((1, D_MODEL), lambda i, j: (0, 0)),
                  pl.BlockSpec((D_MODEL, tn), lambda i, j: (0, j)),
                  pl.BlockSpec((D_MODEL, GATE_COLS), lambda i, j: (0, 0))],
        out_specs=[pl.BlockSpec((tm, D_MODEL), lambda i, j: (i, 0)),
                   pl.BlockSpec((tm, tn), lambda i, j: (i, j)),
                   pl.BlockSpec((tm, GATE_COLS), lambda i, j: (i, 0))],
        out_shape=[jax.ShapeDtypeStruct((T, D_MODEL), F32),
                   jax.ShapeDtypeStruct((T, U_COLS), BF16),
                   jax.ShapeDtypeStruct((T, GATE_COLS), F32)],
        scratch_shapes=[pltpu.VMEM((tm, D_MODEL), BF16)],
        compiler_params=_cparams(("parallel", "arbitrary")),
        name="ln_inproj",
    )(x2, g, b, w, w_gate)


def _log_sigmoid(x):
    return jnp.minimum(x, 0.0) - jnp.log1p(jnp.exp(-jnp.abs(x)))


def _mlstm_kernel(qk_ref, v_ref, o_ref, gt_ref, cw_ref, cb_ref, gb_ref, ng_ref, out_ref,
                  prev_ref, c_ref, n_ref, m_ref):
    L = MLSTM_L

    @pl.when(pl.program_id(1) == 0)
    def _():
        prev_ref[...] = jnp.zeros_like(prev_ref)
        c_ref[...] = jnp.zeros_like(c_ref)
        n_ref[...] = jnp.zeros_like(n_ref)
        m_ref[...] = jnp.zeros_like(m_ref)

    x = qk_ref[...].astype(F32)
    prev = prev_ref[...]
    row = lax.broadcasted_iota(I32, (L, 1), 0)
    cw = cw_ref[...]
    y = cb_ref[...] + cw[CONV_WIDTH - 1:CONV_WIDTH, :] * x
    for j in range(1, CONV_WIDTH):
        shifted = jnp.where(row < j, pltpu.roll(prev, j, 0), pltpu.roll(x, j, 0))
        y = y + cw[CONV_WIDTH - 1 - j:CONV_WIDTH - j, :] * shifted
    prev_ref[...] = x
    qk = y * jax.nn.sigmoid(y)

    gpre = gt_ref[...] + gb_ref[...]
    gpre_t = gpre.T
    r_i = lax.broadcasted_iota(I32, (L, L), 0)
    c_i = lax.broadcasted_iota(I32, (L, L), 1)
    causal = r_i >= c_i
    tril = jnp.where(causal, 1.0, 0.0).astype(BF16)
    triu = jnp.where(r_i <= c_i, 1.0, 0.0).astype(BF16)
    lf = _log_sigmoid(gpre)
    lf_t = _log_sigmoid(gpre_t)
    b_cols = sum(_dot(tril, part) for part in _split3(lf))
    b_rows = sum(_dot(part, triu) for part in _split3(lf_t))

    for h in range(MLSTM_HEADS):
        i_col = gpre[:, h:h + 1]
        i_row = gpre_t[h:h + 1, :]
        b_col = b_cols[:, MLSTM_HEADS + h:MLSTM_HEADS + h + 1]
        b_row = b_rows[MLSTM_HEADS + h:MLSTM_HEADS + h + 1, :]
        m_prev = m_ref[h:h + 1, 0:1]
        n_prev = n_ref[h:h + 1, :]
        c_prev = c_ref[h]

        q = qk[:, h * MLSTM_DQK:(h + 1) * MLSTM_DQK]
        k = qk[:, MLSTM_QK + h * MLSTM_DQK:MLSTM_QK + (h + 1) * MLSTM_DQK] * (MLSTM_DQK ** -0.5)
        v = v_ref[:, h * MLSTM_DV:(h + 1) * MLSTM_DV].astype(BF16)
        qb = q.astype(BF16)

        dmat = jnp.where(causal, b_col - b_row + i_row, -jnp.inf)
        inter = b_col + m_prev
        m_t = jnp.maximum(inter, jnp.max(dmat, -1, keepdims=True))
        s = _dot_nt(qb, k.astype(BF16)) * jnp.exp(dmat - m_t)
        a_inter = jnp.exp(inter - m_t)
        num = _dot(s.astype(BF16), v) + a_inter * _dot(qb, c_prev.astype(BF16))
        den = jnp.sum(s, -1, keepdims=True) + a_inter * jnp.sum(q * n_prev, -1, keepdims=True)
        hc = num / jnp.maximum(jnp.abs(den), jnp.exp(-m_t))

        mu = jnp.mean(hc, -1, keepdims=True)
        hcc = hc - mu
        var = jnp.mean(hcc * hcc, -1, keepdims=True)
        hn = hcc * lax.rsqrt(var + LN_EPS) * ng_ref[:, h * MLSTM_DV:(h + 1) * MLSTM_DV]
        og = jax.nn.sigmoid(o_ref[:, h * MLSTM_DV:(h + 1) * MLSTM_DV].astype(F32))
        out_ref[:, h * MLSTM_DV:(h + 1) * MLSTM_DV] = (og * hn).astype(out_ref.dtype)

        b_last = b_col[L - 1:L, :]
        g_col = b_last - b_col + i_col
        m_new = jnp.maximum(b_last + m_prev, jnp.max(g_col, 0, keepdims=True))
        kw = k * jnp.exp(g_col - m_new)
        decay = jnp.exp(b_last + m_prev - m_new)
        c_ref[h] = decay * c_prev + _dot(kw.T.astype(BF16), v)
        n_ref[h:h + 1, :] = decay * n_prev + jnp.sum(kw, 0, keepdims=True)
        m_ref[h:h + 1, :] = jnp.broadcast_to(m_new, (1, LANE))


def _mlstm(u, ug, conv_w, conv_b, gate_b, norm_g, B, S):
    T = B * S
    L = MLSTM_L
    nc = S // L
    row = lambda b, c: b * nc + c
    return pl.pallas_call(
        _mlstm_kernel,
        grid=(B, nc),
        in_specs=[pl.BlockSpec((L, 2 * MLSTM_QK), lambda b, c: (row(b, c), 0)),
                  pl.BlockSpec((L, MLSTM_WIDTH), lambda b, c: (row(b, c), 1)),
                  pl.BlockSpec((L, MLSTM_WIDTH), lambda b, c: (row(b, c), 2)),
                  pl.BlockSpec((L, GATE_COLS), lambda b, c: (row(b, c), 0)),
                  pl.BlockSpec((CONV_WIDTH, 2 * MLSTM_QK), lambda b, c: (0, 0)),
                  pl.BlockSpec((1, 2 * MLSTM_QK), lambda b, c: (0, 0)),
                  pl.BlockSpec((1, LANE), lambda b, c: (0, 0)),
                  pl.BlockSpec((1, MLSTM_WIDTH), lambda b, c: (0, 0))],
        out_specs=pl.BlockSpec((L, MLSTM_WIDTH), lambda b, c: (row(b, c), 0)),
        out_shape=jax.ShapeDtypeStruct((T, MLSTM_WIDTH), BF16),
        scratch_shapes=[pltpu.VMEM((L, 2 * MLSTM_QK), F32),
                        pltpu.VMEM((MLSTM_HEADS, MLSTM_DQK, MLSTM_DV), F32),
                        pltpu.VMEM((8, MLSTM_DQK), F32),
                        pltpu.VMEM((8, LANE), F32)],
        compiler_params=_cparams(("parallel", "arbitrary")),
        name="mlstm",
    )(u, u, u, ug, conv_w, conv_b, gate_b, norm_g)


LOG2E = 1.4426950408889634
SLC_CHUNK = 1024
SLC_CHUNK_BLOCKS = SLC_CHUNK // SLC_BLOCK
VT_EXTRA = 16


def _nsa_prep_kernel(q_ref, kv0_ref, kv1_ref, kv2_ref, pos_ref, invf_ref, sgn_ref,
                     qt_ref, kc_ref, vc_ref, ks_ref, vst_ref, kw_ref, vwt_ref):
    ang = pos_ref[...].astype(F32) * invf_ref[...]
    cos = jnp.cos(ang)
    sin = jnp.sin(ang) * sgn_ref[...]

    def rope(x):
        return x * cos + pltpu.roll(x, NSA_HEAD_DIM // 2, 1) * sin

    scale = NSA_HEAD_DIM ** -0.5 * LOG2E
    ts = q_ref.shape[0]
    tok = pl.program_id(0) * ts + lax.broadcasted_iota(I32, (ts, NSA_HEAD_DIM), 0)
    blk_lane = (tok // SLC_BLOCK) % SLC_CHUNK_BLOCKS
    blk_onehot = jnp.where(lax.broadcasted_iota(I32, (ts, NSA_HEAD_DIM), 1) == blk_lane, 1.0, 0.0).astype(BF16)
    ones_rows = jnp.where(lax.broadcasted_iota(I32, (VT_EXTRA, ts), 0) == 0, 1.0, 0.0).astype(BF16)
    head = lambda ref, c: ref[:, c:c + NSA_HEAD_DIM].astype(F32)
    for g in range(NSA_KV_HEADS):
        for h in range(NSA_GROUP):
            qt_ref[g, h] = (rope(head(q_ref, (g * NSA_GROUP + h) * NSA_HEAD_DIM)) * scale).T.astype(BF16)
        c0 = g * NSA_HEAD_DIM
        c1 = NSA_KV_HEADS * NSA_HEAD_DIM + g * NSA_HEAD_DIM
        kc_ref[g] = rope(head(kv0_ref, c0)).astype(BF16)
        vc_ref[g] = kv0_ref[:, c1:c1 + NSA_HEAD_DIM].astype(BF16)
        ks_ref[g, :, 0:NSA_HEAD_DIM] = rope(head(kv1_ref, c0)).astype(BF16)
        ks_ref[g, :, NSA_HEAD_DIM:2 * NSA_HEAD_DIM] = blk_onehot
        vst_ref[g, 0:NSA_HEAD_DIM, :] = head(kv1_ref, c1).T.astype(BF16)
        vst_ref[g, NSA_HEAD_DIM:NSA_HEAD_DIM + VT_EXTRA, :] = ones_rows
        kw_ref[g] = rope(head(kv2_ref, c0)).astype(BF16)
        vwt_ref[g] = head(kv2_ref, c1).T.astype(BF16)


def _nsa_prep(u, pos_col, invf2, sgn, T):
    ts = 512
    kvw = 2 * NSA_KV_HEADS * NSA_HEAD_DIM
    kv_spec = lambda n: pl.BlockSpec((ts, kvw), lambda i: (i, (U_NSA + NSA_WIDTH) // kvw + n))
    row_out = pl.BlockSpec((NSA_KV_HEADS, ts, NSA_HEAD_DIM), lambda i: (0, i, 0))
    row_shape = jax.ShapeDtypeStruct((NSA_KV_HEADS, T, NSA_HEAD_DIM), BF16)
    col_out = pl.BlockSpec((NSA_KV_HEADS, NSA_HEAD_DIM, ts), lambda i: (0, 0, i))
    col_shape = jax.ShapeDtypeStruct((NSA_KV_HEADS, NSA_HEAD_DIM, T), BF16)
    return pl.pallas_call(
        _nsa_prep_kernel,
        grid=(T // ts,),
        in_specs=[pl.BlockSpec((ts, NSA_WIDTH), lambda i: (i, U_NSA // NSA_WIDTH)),
                  kv_spec(0), kv_spec(1), kv_spec(2),
                  pl.BlockSpec((ts, 1), lambda i: (i, 0)),
                  pl.BlockSpec((1, LANE), lambda i: (0, 0)),
                  pl.BlockSpec((1, LANE), lambda i: (0, 0))],
        out_specs=[pl.BlockSpec((NSA_KV_HEADS, NSA_GROUP, NSA_HEAD_DIM, ts), lambda i: (0, 0, 0, i)),
                   row_out, row_out,
                   pl.BlockSpec((NSA_KV_HEADS, ts, 2 * NSA_HEAD_DIM), lambda i: (0, i, 0)),
                   pl.BlockSpec((NSA_KV_HEADS, NSA_HEAD_DIM + VT_EXTRA, ts), lambda i: (0, 0, i)),
                   row_out, col_out],
        out_shape=[jax.ShapeDtypeStruct((NSA_KV_HEADS, NSA_GROUP, NSA_HEAD_DIM, T), BF16),
                   row_shape, row_shape,
                   jax.ShapeDtypeStruct((NSA_KV_HEADS, T, 2 * NSA_HEAD_DIM), BF16),
                   jax.ShapeDtypeStruct((NSA_KV_HEADS, NSA_HEAD_DIM + VT_EXTRA, T), BF16),
                   row_shape, col_shape],
        compiler_params=_cparams(("parallel",)),
        name="nsa_prep",
    )(u, u, u, u, pos_col, invf2, sgn)


def _gelu_tanh(x):
    return 0.5 * x * (1.0 + jnp.tanh(0.7978845608028654 * (x + 0.044715 * x * x * x)))


def _compress_kernel(r_ref, w1_ref, pos_ref, w2_ref, out_ref, *, transpose_out):
    r = r_ref[0]
    nr = r.shape[0]
    a = _dot(r, w1_ref[0])
    b = _dot(r, w1_ref[1])
    c0 = _dot(pos_ref[0], w1_ref[0]) + _dot(pos_ref[1], w1_ref[1])
    pre = a + pltpu.roll(b, nr - 1, 0) + c0[0:1, :]
    out = _dot(_gelu_tanh(pre).astype(BF16), w2_ref[...])
    row = lax.broadcasted_iota(I32, (nr, 1), 0)
    out = jnp.where(row < nr - 1, out, 0.0)
    out_ref[0] = (out.T if transpose_out else out).astype(out_ref.dtype)


def _compress(kv, w1, pos8, w2, B, S, transpose_out):
    nr = S // CMP_STRIDE
    half = CMP_STRIDE * NSA_HEAD_DIM
    r = kv.reshape(NSA_KV_HEADS, B * nr, half)
    if transpose_out:
        out_spec = pl.BlockSpec((1, NSA_HEAD_DIM, nr), lambda g, b: (g, 0, b))
        out_shape = jax.ShapeDtypeStruct((NSA_KV_HEADS, NSA_HEAD_DIM, B * nr), BF16)
    else:
        out_spec = pl.BlockSpec((1, nr, NSA_HEAD_DIM), lambda g, b: (g, b, 0))
        out_shape = jax.ShapeDtypeStruct((NSA_KV_HEADS, B * nr, NSA_HEAD_DIM), BF16)
    return pl.pallas_call(
        functools.partial(_compress_kernel, transpose_out=transpose_out),
        grid=(NSA_KV_HEADS, B),
        in_specs=[pl.BlockSpec((1, nr, half), lambda g, b: (g, b, 0)),
                  pl.BlockSpec((2, half, CMP_HIDDEN), lambda g, b: (0, 0, 0)),
                  pl.BlockSpec((2, 8, half), lambda g, b: (0, 0, 0)),
                  pl.BlockSpec((CMP_HIDDEN, NSA_HEAD_DIM), lambda g, b: (0, 0))],
        out_specs=out_spec,
        out_shape=out_shape,
        compiler_params=_cparams(("parallel", "parallel")),
        name="nsa_compress",
    )(r, w1, pos8, w2)


NSA_COLS = NSA_GROUP * Q_BLOCK


def _load_qt(q_ref):
    return jnp.concatenate([q_ref[0, h] for h in range(NSA_GROUP)], 1)


def _store_heads(o_t, out_ref):
    for h in range(NSA_GROUP):
        out_ref[:, h * NSA_HEAD_DIM:(h + 1) * NSA_HEAD_DIM] = (
            o_t[:, h * Q_BLOCK:(h + 1) * Q_BLOCK].T.astype(out_ref.dtype))


def _tile_heads(x):
    return jnp.concatenate([x] * NSA_GROUP, 1)


def _nsa_cmp_kernel(q_ref, kc_ref, vct_ref, ovt_ref, oc_ref, nb_ref, *, n_slc, topn):
    i = pl.program_id(2)
    nc = kc_ref.shape[1]
    qt = _load_qt(q_ref)
    t = i * Q_BLOCK + lax.broadcasted_iota(I32, (1, Q_BLOCK), 1)
    c_idx = lax.broadcasted_iota(I32, (nc, 1), 0)
    valid = (c_idx * CMP_STRIDE + CMP_BLOCK - 1 <= t) & (c_idx < nc - 1)
    s = _dot(kc_ref[0], qt) + _tile_heads(jnp.where(valid, 0.0, NEG))
    m = jnp.max(s, 0, keepdims=True)
    p = jnp.where(m > 0.5 * NEG, jnp.exp2(s - m), 0.0)
    p = p * (1.0 / jnp.maximum(jnp.sum(p, 0, keepdims=True), 1e-30))
    _store_heads(_dot(vct_ref[0], p.astype(BF16)), oc_ref)
    psum = p[:, 0:Q_BLOCK]
    for h in range(1, NSA_GROUP):
        psum = psum + p[:, h * Q_BLOCK:(h + 1) * Q_BLOCK]
    imp = sum(_dot(ovt_ref[...], part) for part in _split2(psum))

    blk = lax.broadcasted_iota(I32, (n_slc, Q_BLOCK), 0)
    blk_f = blk.astype(F32)
    cur = t // SLC_BLOCK
    causal_blk = blk <= cur
    forced = (blk == 0) | (blk == cur) | (blk == cur - 1)
    score = jnp.where(causal_blk, jnp.where(forced, jnp.inf, imp), -jnp.inf)
    sel = jnp.zeros((n_slc, Q_BLOCK), F32)
    for _ in range(topn):
        mx = jnp.max(score, 0, keepdims=True)
        first = jnp.min(jnp.where(score == mx, blk_f, float(n_slc)), 0, keepdims=True)
        hit = blk_f == first
        sel = jnp.where(hit, 1.0, sel)
        score = jnp.where(hit, -jnp.inf, score)
    nb_ref[0, 0] = jnp.where(causal_blk & (sel > 0.5), 0.0, NEG)


def _nsa_cmp(qt, kcc, vcct, ovt, B, S):
    T = B * S
    nq = S // Q_BLOCK
    nc = S // CMP_STRIDE
    n_slc = S // SLC_BLOCK
    kern = functools.partial(_nsa_cmp_kernel, n_slc=n_slc, topn=min(SLC_TOPN, n_slc))
    return pl.pallas_call(
        kern,
        grid=(NSA_KV_HEADS, B, nq),
        in_specs=[pl.BlockSpec((1, NSA_GROUP, NSA_HEAD_DIM, Q_BLOCK), lambda g, b, i: (g, 0, 0, b * nq + i)),
                  pl.BlockSpec((1, nc, NSA_HEAD_DIM), lambda g, b, i: (g, b, 0)),
                  pl.BlockSpec((1, NSA_HEAD_DIM, nc), lambda g, b, i: (g, 0, b)),
                  pl.BlockSpec((n_slc, nc), lambda g, b, i: (0, 0))],
        out_specs=[pl.BlockSpec((Q_BLOCK, NSA_GROUP * NSA_HEAD_DIM), lambda g, b, i: (b * nq + i, g)),
                   pl.BlockSpec((1, 1, n_slc, Q_BLOCK), lambda g, b, i: (g, b * nq + i, 0, 0))],
        out_shape=[jax.ShapeDtypeStruct((T, NSA_WIDTH), BF16),
                   jax.ShapeDtypeStruct((NSA_KV_HEADS, B * nq, n_slc, Q_BLOCK), F32)],
        compiler_params=_cparams(("parallel", "parallel", "arbitrary")),
        name="nsa_cmp_select",
    )(qt, kcc, vcct, ovt)


def _nsa_slc_kernel(q_ref, nb_ref, ks_ref, vst_ref, os_ref, m_ref, acc_ref, s_ref):
    i = pl.program_id(2)
    qt = _load_qt(q_ref)
    t = i * Q_BLOCK + lax.broadcasted_iota(I32, (1, Q_BLOCK), 1)
    m_ref[...] = jnp.full_like(m_ref, NEG)
    acc_ref[...] = jnp.zeros_like(acc_ref)
    pad = jnp.zeros((NSA_HEAD_DIM - SLC_CHUNK_BLOCKS, Q_BLOCK), F32)

    def scores(c, slot):
        start = pl.multiple_of(c * SLC_CHUNK, SLC_CHUNK)
        k = ks_ref[0, pl.ds(start, SLC_CHUNK), :]
        nb = nb_ref[0, 0, pl.ds(pl.multiple_of(c * SLC_CHUNK_BLOCKS, SLC_CHUNK_BLOCKS), SLC_CHUNK_BLOCKS), :]
        mask_rows = jnp.concatenate([nb, pad], 0).astype(BF16)
        rhs = jnp.concatenate([qt, _tile_heads(mask_rows)], 0)
        s_ref[slot] = _dot(k, rhs)

    def absorb(c, slot, diagonal):
        start = pl.multiple_of(c * SLC_CHUNK, SLC_CHUNK)
        vt = vst_ref[0, :, pl.ds(start, SLC_CHUNK)]
        s = s_ref[slot]
        if diagonal:
            key = start + lax.broadcasted_iota(I32, (SLC_CHUNK, 1), 0)
            s = s + _tile_heads(jnp.where(key <= t, 0.0, NEG))
        m_old = m_ref[...]
        m_new = jnp.maximum(m_old, jnp.max(s, 0, keepdims=True))
        p = jnp.exp2(s - m_new).astype(BF16)
        acc_ref[...] = jnp.exp2(m_old - m_new) * acc_ref[...] + _dot(vt, p)
        m_ref[...] = m_new

    n_full = (i * Q_BLOCK) // SLC_CHUNK
    scores(0, 0)

    def body(j, carry):
        scores(2 * j + 1, 1)
        absorb(2 * j, 0, False)
        scores(2 * j + 2, 0)
        absorb(2 * j + 1, 1, False)
        return carry

    lax.fori_loop(0, n_full // 2, body, 0)

    @pl.when(n_full % 2 == 1)
    def _():
        scores(n_full, 1)
        absorb(n_full - 1, 0, False)
        absorb(n_full, 1, True)

    @pl.when(n_full % 2 == 0)
    def _():
        absorb(n_full, 0, True)

    ok = m_ref[...] > 0.5 * NEG
    l = acc_ref[NSA_HEAD_DIM:NSA_HEAD_DIM + 1, :]
    inv = jnp.where(ok, 1.0 / jnp.where(ok, l, 1.0), 0.0)
    _store_heads(acc_ref[0:NSA_HEAD_DIM, :] * inv, os_ref)


def _nsa_slc(qt, nb, ks, vst, B, S):
    T = B * S
    nq = S // Q_BLOCK
    n_slc = S // SLC_BLOCK
    return pl.pallas_call(
        _nsa_slc_kernel,
        grid=(NSA_KV_HEADS, B, nq),
        in_specs=[pl.BlockSpec((1, NSA_GROUP, NSA_HEAD_DIM, Q_BLOCK), lambda g, b, i: (g, 0, 0, b * nq + i)),
                  pl.BlockSpec((1, 1, n_slc, Q_BLOCK), lambda g, b, i: (g, b * nq + i, 0, 0)),
                  pl.BlockSpec((1, S, 2 * NSA_HEAD_DIM), lambda g, b, i: (g, b, 0)),
                  pl.BlockSpec((1, NSA_HEAD_DIM + VT_EXTRA, S), lambda g, b, i: (g, 0, b))],
        out_specs=pl.BlockSpec((Q_BLOCK, NSA_GROUP * NSA_HEAD_DIM), lambda g, b, i: (b * nq + i, g)),
        out_shape=jax.ShapeDtypeStruct((T, NSA_WIDTH), BF16),
        scratch_shapes=[pltpu.VMEM((1, NSA_COLS), F32),
                        pltpu.VMEM((NSA_HEAD_DIM + VT_EXTRA, NSA_COLS), F32),
                        pltpu.VMEM((2, SLC_CHUNK, NSA_COLS), F32)],
        compiler_params=_cparams(("parallel", "parallel", "arbitrary")),
        name="nsa_selected",
    )(qt, nb, ks, vst)


N_WIN_BLOCKS = WINDOW // Q_BLOCK + 1


def _nsa_win_kernel(q_ref, *refs):
    k_refs = refs[:N_WIN_BLOCKS]
    v_refs = refs[N_WIN_BLOCKS:2 * N_WIN_BLOCKS]
    ow_ref = refs[2 * N_WIN_BLOCKS]
    i = pl.program_id(2)
    qt = _load_qt(q_ref)
    k = jnp.concatenate([r[0] for r in k_refs], 0)
    vt = jnp.concatenate([r[0] for r in v_refs], 1)
    nk = N_WIN_BLOCKS * Q_BLOCK
    t = i * Q_BLOCK + lax.broadcasted_iota(I32, (1, Q_BLOCK), 1)
    wpos = i * Q_BLOCK - WINDOW + lax.broadcasted_iota(I32, (nk, 1), 0)
    ok = (wpos <= t) & (wpos > t - WINDOW) & (wpos >= 0)
    s = _dot(k, qt) + _tile_heads(jnp.where(ok, 0.0, NEG))
    m = jnp.max(s, 0, keepdims=True)
    p = jnp.exp2(s - m)
    o_t = _dot(vt, p.astype(BF16)) * (1.0 / jnp.sum(p, 0, keepdims=True))
    _store_heads(o_t, ow_ref)


def _nsa_win(qt, kw, vwt, B, S):
    T = B * S
    nq = S // Q_BLOCK

    def k_spec(c):
        back = N_WIN_BLOCKS - 1 - c
        return pl.BlockSpec((1, Q_BLOCK, NSA_HEAD_DIM),
                            lambda g, b, i: (g, b * nq + jnp.maximum(i - back, 0), 0))

    def v_spec(c):
        back = N_WIN_BLOCKS - 1 - c
        return pl.BlockSpec((1, NSA_HEAD_DIM, Q_BLOCK),
                            lambda g, b, i: (g, 0, b * nq + jnp.maximum(i - back, 0)))

    return pl.pallas_call(
        _nsa_win_kernel,
        grid=(NSA_KV_HEADS, B, nq),
        in_specs=[pl.BlockSpec((1, NSA_GROUP, NSA_HEAD_DIM, Q_BLOCK), lambda g, b, i: (g, 0, 0, b * nq + i))]
        + [k_spec(c) for c in range(N_WIN_BLOCKS)] + [v_spec(c) for c in range(N_WIN_BLOCKS)],
        out_specs=pl.BlockSpec((Q_BLOCK, NSA_GROUP * NSA_HEAD_DIM), lambda g, b, i: (b * nq + i, g)),
        out_shape=jax.ShapeDtypeStruct((T, NSA_WIDTH), BF16),
        compiler_params=_cparams(("parallel", "parallel", "arbitrary")),
        name="nsa_window",
    )(qt, *([kw] * N_WIN_BLOCKS), *([vwt] * N_WIN_BLOCKS))


def _outproj_kernel(hm_ref, oc_ref, os_ref, ow_ref, gt_ref, gb_ref, ge_ref, h_ref, w_ref, g_ref, b_ref,
                    out_ref):
    gates = jax.nn.sigmoid(gt_ref[...] + gb_ref[...]).astype(BF16)
    gx = _dot(gates, ge_ref[...])
    hn = (gx[:, 0:NSA_WIDTH] * oc_ref[...].astype(F32)
          + gx[:, NSA_WIDTH:2 * NSA_WIDTH] * os_ref[...].astype(F32)
          + gx[:, 2 * NSA_WIDTH:3 * NSA_WIDTH] * ow_ref[...].astype(F32))
    mix = (_dot(hm_ref[...], w_ref[0:MLSTM_WIDTH, :])
           + _dot(hn.astype(BF16), w_ref[MLSTM_WIDTH:MLSTM_WIDTH + NSA_WIDTH, :]))
    out_ref[...] = _layer_norm(DN_ALPHA * h_ref[...] + mix, g_ref[...], b_ref[...])


def _outproj(hm, oc, os_, ow, ug, gate_b, gate_expand, h, w_out, g, b):
    T = h.shape[0]
    tm = 256
    row = lambda i: (i, 0)
    fixed = lambda i: (0, 0)
    return pl.pallas_call(
        _outproj_kernel,
        grid=(T // tm,),
        in_specs=[pl.BlockSpec((tm, MLSTM_WIDTH), row), pl.BlockSpec((tm, NSA_WIDTH), row),
                  pl.BlockSpec((tm, NSA_WIDTH), row), pl.BlockSpec((tm, NSA_WIDTH), row),
                  pl.BlockSpec((tm, GATE_COLS), row),
                  pl.BlockSpec((1, LANE), fixed),
                  pl.BlockSpec((LANE, 3 * NSA_WIDTH), fixed),
                  pl.BlockSpec((tm, D_MODEL), row),
                  pl.BlockSpec((D_MODEL, D_MODEL), fixed),
                  pl.BlockSpec((1, D_MODEL), fixed), pl.BlockSpec((1, D_MODEL), fixed)],
        out_specs=pl.BlockSpec((tm, D_MODEL), row),
        out_shape=jax.ShapeDtypeStruct((T, D_MODEL), F32),
        compiler_params=_cparams(("parallel",)),
        name="mixer_outproj_ln",
    )(hm, oc, os_, ow, ug, gate_b, gate_expand, h, w_out, g, b)


def _matmul_kernel(x_ref, w_ref, o_ref):
    o_ref[...] = _dot(x_ref[...].astype(BF16), w_ref[...]).astype(o_ref.dtype)


def _mem_kv(mem2, wkv):
    M = mem2.shape[0]
    N = wkv.shape[1]
    tn = 512
    return pl.pallas_call(
        _matmul_kernel,
        grid=(N // tn,),
        in_specs=[pl.BlockSpec((M, D_MODEL), lambda j: (0, 0)),
                  pl.BlockSpec((D_MODEL, tn), lambda j: (0, j))],
        out_specs=pl.BlockSpec((M, tn), lambda j: (0, j)),
        out_shape=jax.ShapeDtypeStruct((M, N), BF16),
        compiler_params=_cparams(("parallel",)),
        name="mem_kv_proj",
    )(mem2, wkv)


def _xattn_kernel(h_ref, wq_ref, kv_ref, o_ref):
    hb = h_ref[...].astype(BF16)
    for hd in range(XA_HEADS):
        c0 = hd * XA_HEAD_DIM
        q = (_dot(hb, wq_ref[:, c0:c0 + XA_HEAD_DIM]) * (XA_HEAD_DIM ** -0.5)).astype(BF16)
        s = _dot_nt(q, kv_ref[:, c0:c0 + XA_HEAD_DIM])
        m = jnp.max(s, -1, keepdims=True)
        p = jnp.exp(s - m)
        o = _dot(p.astype(BF16), kv_ref[:, D_MODEL + c0:D_MODEL + c0 + XA_HEAD_DIM])
        o_ref[:, c0:c0 + XA_HEAD_DIM] = (o / jnp.sum(p, -1, keepdims=True)).astype(o_ref.dtype)


def _xattn(h1, wq, kv, B, S):
    T = B * S
    tm = 512
    n_mem = kv.shape[0] // B
    per_b = S // tm
    return pl.pallas_call(
        _xattn_kernel,
        grid=(T // tm,),
        in_specs=[pl.BlockSpec((tm, D_MODEL), lambda i: (i, 0)),
                  pl.BlockSpec((D_MODEL, D_MODEL), lambda i: (0, 0)),
                  pl.BlockSpec((n_mem, 2 * D_MODEL), lambda i: (i // per_b, 0))],
        out_specs=pl.BlockSpec((tm, D_MODEL), lambda i: (i, 0)),
        out_shape=jax.ShapeDtypeStruct((T, D_MODEL), BF16),
        compiler_params=_cparams(("parallel",)),
        name="mem_xattn",
    )(h1, wq, kv)


def _xa_out_router_kernel(o_ref, wo_ref, h_ref, g_ref, b_ref, rw_ref, rb_ref,
                          h2_ref, idx_ref, wgt_ref):
    xa = _dot(o_ref[...], wo_ref[...])
    h2 = _layer_norm(DN_ALPHA * h_ref[...] + xa, g_ref[...], b_ref[...])
    h2_ref[...] = h2
    hi, lo = _split2(h2)
    logits = _dot(hi, rw_ref[0]) + _dot(hi, rw_ref[1]) + _dot(lo, rw_ref[0])
    scores = jax.nn.sigmoid(logits)
    lane = lax.broadcasted_iota(I32, scores.shape, 1)
    lane_f = lane.astype(F32)
    biased = jnp.where(lane < N_EXPERTS, scores + rb_ref[...], -jnp.inf)
    idx_mat = jnp.zeros(scores.shape, F32)
    w_mat = jnp.zeros(scores.shape, F32)
    for kk in range(TOP_K):
        mx = jnp.max(biased, -1, keepdims=True)
        first = jnp.min(jnp.where(biased == mx, lane_f, float(LANE)), -1, keepdims=True)
        hit = lane_f == first
        top_s = jnp.sum(jnp.where(hit, scores, 0.0), -1, keepdims=True)
        idx_mat = jnp.where(lane == kk, first, idx_mat)
        w_mat = jnp.where(lane == kk, top_s, w_mat)
        biased = jnp.where(hit, -jnp.inf, biased)
    idx_ref[...] = idx_mat.astype(I32)
    wgt_ref[...] = w_mat / jnp.sum(w_mat, -1, keepdims=True) * ROUTED_SCALE


def _xa_out_router(o, wo, h1, g, b, rw2, rb):
    T = h1.shape[0]
    tm = 256
    row = lambda i: (i, 0)
    fixed = lambda i: (0, 0)
    return pl.pallas_call(
        _xa_out_router_kernel,
        grid=(T // tm,),
        in_specs=[pl.BlockSpec((tm, D_MODEL), row),
                  pl.BlockSpec((D_MODEL, D_MODEL), fixed),
                  pl.BlockSpec((tm, D_MODEL), row),
                  pl.BlockSpec((1, D_MODEL), fixed), pl.BlockSpec((1, D_MODEL), fixed),
                  pl.BlockSpec((2, D_MODEL, LANE), lambda i: (0, 0, 0)),
                  pl.BlockSpec((1, LANE), fixed)],
        out_specs=[pl.BlockSpec((tm, D_MODEL), row), pl.BlockSpec((tm, LANE), row),
                   pl.BlockSpec((tm, LANE), row)],
        out_shape=[jax.ShapeDtypeStruct((T, D_MODEL), F32),
                   jax.ShapeDtypeStruct((T, LANE), I32),
                   jax.ShapeDtypeStruct((T, LANE), F32)],
        compiler_params=_cparams(("parallel",)),
        name="xattn_out_ln_router",
    )(o, wo, h1, g, b, rw2, rb)


ROUTE_TM = 512
ZERO_ROWS = 128


def _route_rank_kernel(idx_ref, rank_ref, cnt_ref):
    tm = idx_ref.shape[0]

    @pl.when(pl.program_id(0) == 0)
    def _():
        cnt_ref[...] = jnp.zeros_like(cnt_ref)

    idx = idx_ref[...]
    lane = lax.broadcasted_iota(I32, (tm, LANE), 1)
    hits = [lane == idx[:, kk:kk + 1] for kk in range(TOP_K)]
    onehot = sum(jnp.where(hit, 1.0, 0.0) for hit in hits)
    r_i = lax.broadcasted_iota(I32, (tm, tm), 0)
    c_i = lax.broadcasted_iota(I32, (tm, tm), 1)
    before = jnp.where(r_i > c_i, 1.0, 0.0).astype(BF16)
    rank = _dot(before, onehot.astype(BF16)) + cnt_ref[0:1, :]
    out = jnp.zeros((tm, LANE), F32)
    for kk in range(TOP_K):
        out = jnp.where(lane == kk, jnp.sum(jnp.where(hits[kk], rank, 0.0), -1, keepdims=True), out)
    rank_ref[...] = out.astype(I32)
    cnt_ref[0:1, :] = cnt_ref[0:1, :] + jnp.sum(onehot, 0, keepdims=True)


def _route_rank(top_idx):
    T = top_idx.shape[0]
    tm = ROUTE_TM
    return pl.pallas_call(
        _route_rank_kernel,
        grid=(T // tm,),
        in_specs=[pl.BlockSpec((tm, LANE), lambda i: (i, 0))],
        out_specs=[pl.BlockSpec((tm, LANE), lambda i: (i, 0)),
                   pl.BlockSpec((8, LANE), lambda i: (0, 0))],
        out_shape=[jax.ShapeDtypeStruct((T, LANE), I32), jax.ShapeDtypeStruct((8, LANE), F32)],
        compiler_params=_cparams(("arbitrary",)),
        name="moe_route_rank",
    )(top_idx)


def _load_route(i, dest_hbm, dest_smem, isem):
    cp = pltpu.make_async_copy(dest_hbm.at[i], dest_smem, isem)
    cp.start()
    cp.wait()


_FILL_SIZES = tuple(s for s in (ZERO_ROWS >> n for n in range(ZERO_ROWS.bit_length())) if s >= SUBLANE)


def _experts_kernel(be_ref, nu_ref, x_ref, wg_ref, wu_ref, wd_ref, y_ref):
    used = pl.program_id(0) < nu_ref[0]

    @pl.when(used)
    def _():
        xb = x_ref[...].astype(BF16)
        a = _dot(xb, wg_ref[0])
        act = (a * jax.nn.sigmoid(a)) * _dot(xb, wu_ref[0])
        y_ref[...] = _dot(act.astype(BF16), wd_ref[0])

    @pl.when(jnp.logical_not(used))
    def _():
        y_ref[...] = jnp.zeros_like(y_ref)


def _experts(block_e, n_used, xs, wg, wu, wd):
    bm = EXPERT_BM
    P = xs.shape[0] // bm * bm
    rowmap = lambda j, be, nu: (jnp.minimum(j, nu[0] - 1), 0)
    wmap = lambda j, be, nu: (be[j], 0, 0)
    return pl.pallas_call(
        _experts_kernel,
        grid_spec=pltpu.PrefetchScalarGridSpec(
            num_scalar_prefetch=2,
            grid=(P // bm,),
            in_specs=[pl.BlockSpec((bm, D_MODEL), rowmap),
                      pl.BlockSpec((1, D_MODEL, D_EXPERT), wmap),
                      pl.BlockSpec((1, D_MODEL, D_EXPERT), wmap),
                      pl.BlockSpec((1, D_EXPERT, D_MODEL), wmap)],
            out_specs=pl.BlockSpec((bm, D_MODEL), lambda j, be, nu: (j, 0))),
        out_shape=jax.ShapeDtypeStruct((P, D_MODEL), F32),
        compiler_params=_cparams(("arbitrary",)),
        name="moe_experts",
    )(block_e, n_used, xs, wg, wu, wd)


SHARED_TF = 256
SHARED_STEPS = D_SHARED // SHARED_TF
DISPATCH_SHARE = -(-ROUTE_TM // (SHARED_STEPS * SUBLANE)) * SUBLANE
DUMP_ROWS = (DISPATCH_SHARE * SHARED_STEPS - ROUTE_TM) * TOP_K


def _zero_fill(fs_ref, fn_ref, tail_ref, zero_ref, xs_ref, zsem):
    zero_ref[...] = jnp.zeros_like(zero_ref)

    def fill_copies(e, go):
        n = fn_ref[e]
        first = fs_ref[e]
        lead = jnp.minimum((-first) & (SUBLANE - 1), n)
        for r in range(SUBLANE - 1):
            @pl.when(r < lead)
            def _():
                go(pltpu.make_async_copy(zero_ref.at[pl.ds(0, 1)], xs_ref.at[pl.ds(first + r, 1)], zsem))

        rest = n - lead
        off = first + lead
        for size in _FILL_SIZES:
            take = (rest & size) != 0

            @pl.when(take)
            def _():
                dst = xs_ref.at[pl.ds(pl.multiple_of(off, SUBLANE), size)]
                go(pltpu.make_async_copy(zero_ref.at[pl.ds(0, size)], dst, zsem))
            off = off + jnp.where(take, size, 0)

    def tail_copies(j, go):
        off = pl.multiple_of(tail_ref[0] + j * ZERO_ROWS, ZERO_ROWS)
        go(pltpu.make_async_copy(zero_ref, xs_ref.at[pl.ds(off, ZERO_ROWS)], zsem))

    for go in (lambda cp: cp.start(), lambda cp: cp.wait()):
        lax.fori_loop(0, N_EXPERTS, lambda e, c, go=go: (fill_copies(e, go), c)[1], 0)
        lax.fori_loop(0, tail_ref[1], lambda j, c, go=go: (tail_copies(j, go), c)[1], 0)


def _shared_ffn_kernel(fs_ref, fn_ref, tail_ref, dest_hbm, x_ref, wg_ref, wu_ref, wd_ref,
                       o_ref, xs_ref, xb_ref, dest_smem, zero_ref, isem, sem, zsem, *, dump_base):
    i = pl.program_id(0)
    f = pl.program_id(1)

    @pl.when(f == 0)
    def _():
        _load_route(i, dest_hbm, dest_smem, isem)
        xb_ref[...] = x_ref[...].astype(BF16)
        o_ref[...] = jnp.zeros_like(o_ref)

    @pl.when((f == 0) & (i == 0))
    def _():
        _zero_fill(fs_ref, fn_ref, tail_ref, zero_ref, xs_ref, zsem)

    for rr in range(DISPATCH_SHARE):
        r = f * DISPATCH_SHARE + rr
        live = r < ROUTE_TM
        rc = jnp.minimum(r, ROUTE_TM - 1)
        for kk in range(TOP_K):
            d = jnp.where(live, dest_smem[rc * TOP_K + kk], dump_base + (r - ROUTE_TM) * TOP_K + kk)
            pltpu.make_async_copy(x_ref.at[pl.ds(rc, 1)], xs_ref.at[pl.ds(d, 1)], sem).start()

    xb = xb_ref[...]
    a = _dot(xb, wg_ref[...])
    act = (a * jax.nn.sigmoid(a)) * _dot(xb, wu_ref[...])
    o_ref[...] += _dot(act.astype(BF16), wd_ref[...])

    @pl.when(f == SHARED_STEPS - 1)
    def _():
        for _ in range(TOP_K):
            pltpu.make_async_copy(x_ref, xs_ref.at[pl.ds(0, ROUTE_TM)], sem).wait()
        pltpu.make_async_copy(x_ref.at[pl.ds(0, DUMP_ROWS)], xs_ref.at[pl.ds(0, DUMP_ROWS)], sem).wait()


def _shared_ffn_dispatch(fill_start, fill_n, tail, dest2, h2, wg, wu, wd, P):
    T = h2.shape[0]
    tm, tf = ROUTE_TM, SHARED_TF
    return pl.pallas_call(
        functools.partial(_shared_ffn_kernel, dump_base=P),
        grid_spec=pltpu.PrefetchScalarGridSpec(
            num_scalar_prefetch=3,
            grid=(T // tm, SHARED_STEPS),
            in_specs=[pl.BlockSpec(memory_space=pl.ANY),
                      pl.BlockSpec((tm, D_MODEL), lambda i, f, *_: (i, 0)),
                      pl.BlockSpec((D_MODEL, tf), lambda i, f, *_: (0, f)),
                      pl.BlockSpec((D_MODEL, tf), lambda i, f, *_: (0, f)),
                      pl.BlockSpec((tf, D_MODEL), lambda i, f, *_: (f, 0))],
            out_specs=[pl.BlockSpec((tm, D_MODEL), lambda i, f, *_: (i, 0)),
                       pl.BlockSpec(memory_space=pl.ANY)],
            scratch_shapes=[pltpu.VMEM((tm, D_MODEL), BF16),
                            pltpu.SMEM((tm * TOP_K,), I32),
                            pltpu.VMEM((ZERO_ROWS, D_MODEL), F32),
                            pltpu.SemaphoreType.DMA, pltpu.SemaphoreType.DMA, pltpu.SemaphoreType.DMA]),
        out_shape=[jax.ShapeDtypeStruct((T, D_MODEL), F32),
                   jax.ShapeDtypeStruct((P + DUMP_ROWS, D_MODEL), F32)],
        compiler_params=_cparams(("arbitrary", "arbitrary")),
        name="shared_ffn_dispatch",
    )(fill_start, fill_n, tail, dest2, h2, wg, wu, wd)


COMBINE_SUB = 128
COMBINE_NSUB = ROUTE_TM // COMBINE_SUB


def _combine_kernel(dest_hbm, ys_hbm, w_ref, sh_ref, h_ref, g_ref, b_ref, out_ref,
                    dest_smem, gbuf, isem, sem):
    i = pl.program_id(0)
    _load_route(i, dest_hbm, dest_smem, isem)

    def gather(sub):
        slot = sub % 2

        def issue(r, carry):
            for kk in range(TOP_K):
                d = dest_smem[(sub * COMBINE_SUB + r) * TOP_K + kk]
                pltpu.make_async_copy(ys_hbm.at[pl.ds(d, 1)], gbuf.at[slot, kk, pl.ds(r, 1)],
                                      sem.at[slot]).start()
            return carry

        lax.fori_loop(0, COMBINE_SUB, issue, 0)

    gather(0)
    for sub in range(COMBINE_NSUB):
        slot = sub % 2
        if sub + 1 < COMBINE_NSUB:
            gather(sub + 1)
        for kk in range(TOP_K):
            pltpu.make_async_copy(ys_hbm.at[pl.ds(0, COMBINE_SUB)], gbuf.at[slot, kk], sem.at[slot]).wait()
        rs = slice(sub * COMBINE_SUB, (sub + 1) * COMBINE_SUB)
        w = w_ref[rs, :]
        routed = w[:, 0:1] * gbuf[slot, 0]
        for kk in range(1, TOP_K):
            routed = routed + w[:, kk:kk + 1] * gbuf[slot, kk]
        z = DN_ALPHA * h_ref[rs, :] + (routed + sh_ref[rs, :])
        out_ref[rs, :] = _layer_norm(z, g_ref[...], b_ref[...])


def _combine(dest2, ys, top_w, sh, h2, g, b):
    T = h2.shape[0]
    tm = ROUTE_TM
    row = lambda i: (i, 0)
    fixed = lambda i: (0, 0)
    return pl.pallas_call(
        _combine_kernel,
        grid=(T // tm,),
        in_specs=[pl.BlockSpec(memory_space=pl.ANY), pl.BlockSpec(memory_space=pl.ANY),
                  pl.BlockSpec((tm, LANE), row), pl.BlockSpec((tm, D_MODEL), row),
                  pl.BlockSpec((tm, D_MODEL), row),
                  pl.BlockSpec((1, D_MODEL), fixed), pl.BlockSpec((1, D_MODEL), fixed)],
        out_specs=pl.BlockSpec((tm, D_MODEL), row),
        scratch_shapes=[pltpu.SMEM((tm * TOP_K,), I32),
                        pltpu.VMEM((2, TOP_K, COMBINE_SUB, D_MODEL), F32),
                        pltpu.SemaphoreType.DMA, pltpu.SemaphoreType.DMA((2,))],
        out_shape=jax.ShapeDtypeStruct((T, D_MODEL), F32),
        compiler_params=_cparams(("arbitrary",)),
        name="moe_combine_ln",
    )(dest2, ys, top_w, sh, h2, g, b)


def _route_plan(counts, T):
    bm = EXPERT_BM
    padded = (counts + bm - 1) // bm * bm
    pad_end = jnp.cumsum(padded)
    pad_start = pad_end - padded
    n_blocks = T * TOP_K // bm + N_EXPERTS
    first_row = jnp.arange(n_blocks, dtype=I32) * bm
    block_e = jnp.minimum(jnp.sum((pad_end[None, :] <= first_row[:, None]).astype(I32), 1), N_EXPERTS - 1)
    n_used = (pad_end[-1:] // bm).astype(I32)
    tail = jnp.concatenate([pad_end[-1:], (n_blocks - n_used) * (bm // ZERO_ROWS)]).astype(I32)
    return (pad_start.astype(I32), (pad_start + counts).astype(I32), (padded - counts).astype(I32), tail,
            block_e, n_used, n_blocks * bm)


def _overlap_matrix(S):
    n_cmp_rows = S // CMP_STRIDE
    n_slc = S // SLC_BLOCK
    c_lo = np.arange(n_cmp_rows)[:, None] * CMP_STRIDE
    j_lo = np.arange(n_slc)[None, :] * SLC_BLOCK
    ov = (c_lo <= j_lo + SLC_BLOCK - 1) & (c_lo + CMP_BLOCK - 1 >= j_lo)
    return jnp.asarray(ov.T.astype(np.float32), dtype=BF16)


def _gate_expand_matrix():
    ge = np.zeros((LANE, 3 * NSA_WIDTH), np.float32)
    for hh in range(NSA_HEADS):
        for br in range(3):
            ge[8 + hh * 3 + br, br * NSA_WIDTH + hh * NSA_HEAD_DIM:br * NSA_WIDTH + (hh + 1) * NSA_HEAD_DIM] = 1.0
    return jnp.asarray(ge, dtype=BF16)


def kernel(x, mem, positions, ln0_g, ln0_b, w_in, conv_w, conv_b, igate_b, fgate_b, mlstm_norm_g, cmp_pos, cmp_w1k, cmp_w2k, cmp_w1v, cmp_w2v, nsa_gate_b, w_out, ln1_g, ln1_b, xa_wq, xa_wk, xa_wv, xa_wo, ln2_g, ln2_b, router_w, router_bias, moe_w_gate, moe_w_up, moe_w_down, sh_w_gate, sh_w_up, sh_w_down, ln3_g, ln3_b):
    B, S, D = x.shape
    T = B * S
    assert D == D_MODEL and w_in.shape[0] == DEPTH == 1
    assert S % MLSTM_L == 0 and S % SLC_CHUNK == 0 and T % ROUTE_TM == 0 and (T * TOP_K) % EXPERT_BM == 0
    row = lambda a: a.reshape(1, -1)

    w = w_in[0]
    w_r = jnp.concatenate([w[:, :3072], w[:, 3080:5640]], 1).astype(BF16)
    w_g = jnp.concatenate([w[:, 3072:3080], w[:, 5640:5664],
                           jnp.zeros((D, GATE_COLS - 2 * MLSTM_HEADS - 3 * NSA_HEADS), F32)], 1).astype(BF16)
    gate_b = jnp.concatenate([igate_b[0], fgate_b[0], nsa_gate_b[0],
                              jnp.zeros((LANE - 2 * MLSTM_HEADS - 3 * NSA_HEADS,), F32)]).reshape(1, LANE)
    half = NSA_HEAD_DIM // 2
    inv_freq = ROPE_THETA ** (-jnp.arange(half, dtype=F32) / half)
    invf2 = jnp.concatenate([inv_freq, inv_freq]).reshape(1, LANE)
    sgn = jnp.concatenate([-jnp.ones((half,), F32), jnp.ones((half,), F32)]).reshape(1, LANE)
    pos8 = jnp.zeros((2, 8, CMP_STRIDE * NSA_HEAD_DIM), F32).at[:, 0, :].set(
        cmp_pos[0].reshape(2, CMP_STRIDE * NSA_HEAD_DIM)).astype(BF16)
    w1k = cmp_w1k[0].reshape(2, CMP_STRIDE * NSA_HEAD_DIM, CMP_HIDDEN).astype(BF16)
    w1v = cmp_w1v[0].reshape(2, CMP_STRIDE * NSA_HEAD_DIM, CMP_HIDDEN).astype(BF16)
    rw = jnp.pad(router_w[0], ((0, 0), (0, LANE - N_EXPERTS)))
    rw_hi = rw.astype(BF16)
    rw2 = jnp.stack([rw_hi, (rw - rw_hi.astype(F32)).astype(BF16)])
    rb = jnp.pad(router_bias[0], (0, LANE - N_EXPERTS)).reshape(1, LANE)

    h, u, ug = _ln_inproj(x.reshape(T, D), row(ln0_g), row(ln0_b), w_r, w_g)
    hm = _mlstm(u, ug, conv_w[0], row(conv_b[0]), gate_b, row(mlstm_norm_g[0]), B, S)
    qt, kc, vc, ks, vst, kw, vwt = _nsa_prep(u, positions.reshape(T, 1), invf2, sgn, T)
    kcc = _compress(kc, w1k, pos8, cmp_w2k[0].astype(BF16), B, S, False)
    vcct = _compress(vc, w1v, pos8, cmp_w2v[0].astype(BF16), B, S, True)
    oc, nb = _nsa_cmp(qt, kcc, vcct, _overlap_matrix(S), B, S)
    os_ = _nsa_slc(qt, nb, ks, vst, B, S)
    ow = _nsa_win(qt, kw, vwt, B, S)
    h1 = _outproj(hm, oc, os_, ow, ug, gate_b, _gate_expand_matrix(), h, w_out[0].astype(BF16),
                  row(ln1_g[0]), row(ln1_b[0]))

    wkv = jnp.concatenate([xa_wk[0], xa_wv[0]], 1).astype(BF16)
    kv = _mem_kv(mem.reshape(-1, D), wkv)
    xo = _xattn(h1, xa_wq[0].astype(BF16), kv, B, S)
    h2, top_idx, top_w = _xa_out_router(xo, xa_wo[0].astype(BF16), h1, row(ln2_g[0]), row(ln2_b[0]), rw2, rb)

    rank, cnt = _route_rank(top_idx)
    counts = cnt[0, :N_EXPERTS].astype(I32)
    pad_start, fill_start, fill_n, tail, block_e, n_used, P = _route_plan(counts, T)
    slot_e = top_idx[:, :TOP_K, None] == jnp.arange(N_EXPERTS, dtype=I32)[None, None, :]
    dest = rank[:, :TOP_K] + jnp.sum(jnp.where(slot_e, pad_start[None, None, :], 0), -1)
    dest2 = dest.reshape(T // ROUTE_TM, ROUTE_TM * TOP_K)
    sh, xs = _shared_ffn_dispatch(fill_start, fill_n, tail, dest2, h2,
                                  sh_w_gate[0].astype(BF16), sh_w_up[0].astype(BF16),
                                  sh_w_down[0].astype(BF16), P)
    ys = _experts(block_e, n_used, xs, moe_w_gate[0].astype(BF16), moe_w_up[0].astype(BF16),
                  moe_w_down[0].astype(BF16))
    out = _combine(dest2, ys, top_w, sh, h2, row(ln3_g[0]), row(ln3_b[0]))
    return out.reshape(B, S, D)
```

```python
import functools

import numpy as np
import jax
import jax.numpy as jnp
from jax import lax
from jax.experimental import pallas as pl
from jax.experimental.pallas import tpu as pltpu

F32 = jnp.float32
BF16 = jnp.bfloat16
I32 = jnp.int32

D_MODEL = 2048
MLSTM_HEADS = 4
MLSTM_DV = 256
MLSTM_DQK = 128
MLSTM_QK = MLSTM_HEADS * MLSTM_DQK
MLSTM_WIDTH = MLSTM_HEADS * MLSTM_DV
CONV_WIDTH = 4
NSA_HEAD_DIM = 128
NSA_HEADS = 8
NSA_KV_HEADS = 2
NSA_GROUP = 4
NSA_WIDTH = NSA_HEADS * NSA_HEAD_DIM
CMP_BLOCK = 32
CMP_STRIDE = 16
CMP_HIDDEN = 256
SLC_BLOCK = 64
SLC_TOPN = 16
WINDOW = 512
Q_BLOCK = 128
XA_HEADS = 4
XA_HEAD_DIM = 512
N_EXPERTS = 64
TOP_K = 6
D_EXPERT = 1408
D_SHARED = 2816
ROUTED_SCALE = 2.446
ROPE_THETA = 10000.0
LN_EPS = 1e-5
DEPTH = 1
DN_ALPHA = (2.0 * DEPTH) ** 0.25

U_MLSTM = 0
U_NSA = 3072
U_COLS = 5632
GATE_COLS = 128

LANE = 128
SUBLANE = 8
NEG = -1e30
MLSTM_L = 256
VMEM_LIMIT = 56 * 1024 * 1024
EXPERT_BM = 256


def _cparams(sem):
    return pltpu.CompilerParams(dimension_semantics=sem, vmem_limit_bytes=VMEM_LIMIT)


def _dot(a, b):
    return jnp.dot(a, b, preferred_element_type=F32)


def _dot_nt(a, b):
    return lax.dot_general(a, b, (((1,), (1,)), ((), ())), preferred_element_type=F32)


def _layer_norm(z, g, b):
    mu = jnp.mean(z, -1, keepdims=True)
    zc = z - mu
    var = jnp.mean(zc * zc, -1, keepdims=True)
    return zc * lax.rsqrt(var + LN_EPS) * g + b


def _split3(x):
    hi = x.astype(BF16)
    r = x - hi.astype(F32)
    mid = r.astype(BF16)
    lo = (r - mid.astype(F32)).astype(BF16)
    return hi, mid, lo


def _split2(x):
    hi = x.astype(BF16)
    lo = (x - hi.astype(F32)).astype(BF16)
    return hi, lo


def _ln_inproj_kernel(x_ref, g_ref, b_ref, w_ref, wg_ref, h_ref, u_ref, ug_ref, hb_ref):
    @pl.when(pl.program_id(1) == 0)
    def _():
        hn = _layer_norm(x_ref[...], g_ref[...], b_ref[...])
        h_ref[...] = hn
        hb_ref[...] = hn.astype(BF16)
        ug_ref[...] = _dot(hb_ref[...], wg_ref[...])

    u_ref[...] = _dot(hb_ref[...], w_ref[...]).astype(u_ref.dtype)


def _ln_inproj(x2, g, b, w, w_gate):
    T = x2.shape[0]
    tm, tn = 1024, 512
    return pl.pallas_call(
        _ln_inproj_kernel,
        grid=(T // tm, U_COLS // tn),
        in_specs=[pl.BlockSpec((tm, D_MODEL), lambda i, j: (i, 0)),
                  pl.BlockSpec((1, D_MODEL), lambda i, j: (0, 0)),
                  pl.BlockSpec((1, D_MODEL), lambda i, j: (0, 0)),
                  pl.BlockSpec((D_MODEL, tn), lambda i, j: (0, j)),
                  pl.BlockSpec((D_MODEL, GATE_COLS), lambda i, j: (0, 0))],
        out_specs=[pl.BlockSpec((tm, D_MODEL), lambda i, j: (i, 0)),
                   pl.BlockSpec((tm, tn), lambda i, j: (i, j)),
                   pl.BlockSpec((tm, GATE_COLS), lambda i, j: (i, 0))],
        out_shape=[jax.ShapeDtypeStruct((T, D_MODEL), F32),
                   jax.ShapeDtypeStruct((T, U_COLS), BF16),
                   jax.ShapeDtypeStruct((T, GATE_COLS), F32)],
        scratch_shapes=[pltpu.VMEM((tm, D_MODEL), BF16)],
        compiler_params=_cparams(("parallel", "arbitrary")),
        name="ln_inproj",
    )(x2, g, b, w, w_gate)


def _log_sigmoid(x):
    return jnp.minimum(x, 0.0) - jnp.log1p(jnp.exp(-jnp.abs(x)))


def _mlstm_kernel(qk_ref, v_ref, o_ref, gt_ref, cw_ref, cb_ref, gb_ref, ng_ref, out_ref,
                  prev_ref, c_ref, n_ref, m_ref):
    L = MLSTM_L

    @pl.when(pl.program_id(1) == 0)
    def _():
        prev_ref[...] = jnp.zeros_like(prev_ref)
        c_ref[...] = jnp.zeros_like(c_ref)
        n_ref[...] = jnp.zeros_like(n_ref)
        m_ref[...] = jnp.zeros_like(m_ref)

    x = qk_ref[...].astype(F32)
    prev = prev_ref[...]
    row = lax.broadcasted_iota(I32, (L, 1), 0)
    cw = cw_ref[...]
    y = cb_ref[...] + cw[CONV_WIDTH - 1:CONV_WIDTH, :] * x
    for j in range(1, CONV_WIDTH):
        shifted = jnp.where(row < j, pltpu.roll(prev, j, 0), pltpu.roll(x, j, 0))
        y = y + cw[CONV_WIDTH - 1 - j:CONV_WIDTH - j, :] * shifted
    prev_ref[...] = x
    qk = y * jax.nn.sigmoid(y)

    gpre = gt_ref[...] + gb_ref[...]
    gpre_t = gpre.T
    r_i = lax.broadcasted_iota(I32, (L, L), 0)
    c_i = lax.broadcasted_iota(I32, (L, L), 1)
    causal = r_i >= c_i
    tril = jnp.where(causal, 1.0, 0.0).astype(BF16)
    triu = jnp.where(r_i <= c_i, 1.0, 0.0).astype(BF16)
    lf = _log_sigmoid(gpre)
    lf_t = _log_sigmoid(gpre_t)
    b_cols = sum(_dot(tril, part) for part in _split3(lf))
    b_rows = sum(_dot(part, triu) for part in _split3(lf_t))

    for h in range(MLSTM_HEADS):
        i_col = gpre[:, h:h + 1]
        i_row = gpre_t[h:h + 1, :]
        b_col = b_cols[:, MLSTM_HEADS + h:MLSTM_HEADS + h + 1]
        b_row = b_rows[MLSTM_HEADS + h:MLSTM_HEADS + h + 1, :]
        m_prev = m_ref[h:h + 1, 0:1]
        n_prev = n_ref[h:h + 1, :]
        c_prev = c_ref[h]

        q = qk[:, h * MLSTM_DQK:(h + 1) * MLSTM_DQK]
        k = qk[:, MLSTM_QK + h * MLSTM_DQK:MLSTM_QK + (h + 1) * MLSTM_DQK] * (MLSTM_DQK ** -0.5)
        v = v_ref[:, h * MLSTM_DV:(h + 1) * MLSTM_DV].astype(BF16)
        qb = q.astype(BF16)

        dmat = jnp.where(causal, b_col - b_row + i_row, -jnp.inf)
        inter = b_col + m_prev
        m_t = jnp.maximum(inter, jnp.max(dmat, -1, keepdims=True))
        s = _dot_nt(qb, k.astype(BF16)) * jnp.exp(dmat - m_t)
        a_inter = jnp.exp(inter - m_t)
        num = _dot(s.astype(BF16), v) + a_inter * _dot(qb, c_prev.astype(BF16))
        den = jnp.sum(s, -1, keepdims=True) + a_inter * jnp.sum(q * n_prev, -1, keepdims=True)
        hc = num / jnp.maximum(jnp.abs(den), jnp.exp(-m_t))

        mu = jnp.mean(hc, -1, keepdims=True)
        hcc = hc - mu
        var = jnp.mean(hcc * hcc, -1, keepdims=True)
        hn = hcc * lax.rsqrt(var + LN_EPS) * ng_ref[:, h * MLSTM_DV:(h + 1) * MLSTM_DV]
        og = jax.nn.sigmoid(o_ref[:, h * MLSTM_DV:(h + 1) * MLSTM_DV].astype(F32))
        out_ref[:, h * MLSTM_DV:(h + 1) * MLSTM_DV] = (og * hn).astype(out_ref.dtype)

        b_last = b_col[L - 1:L, :]
        g_col = b_last - b_col + i_col
        m_new = jnp.maximum(b_last + m_prev, jnp.max(g_col, 0, keepdims=True))
        kw = k * jnp.exp(g_col - m_new)
        decay = jnp.exp(b_last + m_prev - m_new)
        c_ref[h] = decay * c_prev + _dot(kw.T.astype(BF16), v)
        n_ref[h:h + 1, :] = decay * n_prev + jnp.sum(kw, 0, keepdims=True)
        m_ref[h:h + 1, :] = jnp.broadcast_to(m_new, (1, LANE))


def _mlstm(u, ug, conv_w, conv_b, gate_b, norm_g, B, S):
    T = B * S
    L = MLSTM_L
    nc = S // L
    row = lambda b, c: b * nc + c
    return pl.pallas_call(
        _mlstm_kernel,
        grid=(B, nc),
        in_specs=[pl.BlockSpec((L, 2 * MLSTM_QK), lambda b, c: (row(b, c), 0)),
                  pl.BlockSpec((L, MLSTM_WIDTH), lambda b, c: (row(b, c), 1)),
                  pl.BlockSpec((L, MLSTM_WIDTH), lambda b, c: (row(b, c), 2)),
                  pl.BlockSpec((L, GATE_COLS), lambda b, c: (row(b, c), 0)),
                  pl.BlockSpec((CONV_WIDTH, 2 * MLSTM_QK), lambda b, c: (0, 0)),
                  pl.BlockSpec((1, 2 * MLSTM_QK), lambda b, c: (0, 0)),
                  pl.BlockSpec((1, LANE), lambda b, c: (0, 0)),
                  pl.BlockSpec((1, MLSTM_WIDTH), lambda b, c: (0, 0))],
        out_specs=pl.BlockSpec((L, MLSTM_WIDTH), lambda b, c: (row(b, c), 0)),
        out_shape=jax.ShapeDtypeStruct((T, MLSTM_WIDTH), BF16),
        scratch_shapes=[pltpu.VMEM((L, 2 * MLSTM_QK), F32),
                        pltpu.VMEM((MLSTM_HEADS, MLSTM_DQK, MLSTM_DV), F32),
                        pltpu.VMEM((8, MLSTM_DQK), F32),
                        pltpu.VMEM((8, LANE), F32)],
        compiler_params=_cparams(("parallel", "arbitrary")),
        name="mlstm",
    )(u, u, u, ug, conv_w, conv_b, gate_b, norm_g)


LOG2E = 1.4426950408889634
SLC_CHUNK = 1024
SLC_CHUNK_BLOCKS = SLC_CHUNK // SLC_BLOCK
VT_EXTRA = 16
CMP_ROW_STEP = 256


def _nsa_prep_kernel(q_ref, kv0_ref, kv1_ref, kv2_ref, pos_ref, invf_ref, sgn_ref,
                     qt_ref, kc_ref, vc_ref, ks_ref, vst_ref, kw_ref, vwt_ref):
    ang = pos_ref[...].astype(F32) * invf_ref[...]
    cos = jnp.cos(ang)
    sin = jnp.sin(ang) * sgn_ref[...]

    def rope(x):
        return x * cos + pltpu.roll(x, NSA_HEAD_DIM // 2, 1) * sin

    scale = NSA_HEAD_DIM ** -0.5 * LOG2E
    ts = q_ref.shape[0]
    tok = pl.program_id(0) * ts + lax.broadcasted_iota(I32, (ts, NSA_HEAD_DIM), 0)
    blk_lane = (tok // SLC_BLOCK) % SLC_CHUNK_BLOCKS
    blk_onehot = jnp.where(lax.broadcasted_iota(I32, (ts, NSA_HEAD_DIM), 1) == blk_lane, 1.0, 0.0).astype(BF16)
    ones_rows = jnp.where(lax.broadcasted_iota(I32, (VT_EXTRA, ts), 0) == 0, 1.0, 0.0).astype(BF16)
    head = lambda ref, c: ref[:, c:c + NSA_HEAD_DIM].astype(F32)
    for g in range(NSA_KV_HEADS):
        for h in range(NSA_GROUP):
            qt_ref[g, h] = (rope(head(q_ref, (g * NSA_GROUP + h) * NSA_HEAD_DIM)) * scale).T.astype(BF16)
        c0 = g * NSA_HEAD_DIM
        c1 = NSA_KV_HEADS * NSA_HEAD_DIM + g * NSA_HEAD_DIM
        kc_ref[g] = rope(head(kv0_ref, c0)).astype(BF16)
        vc_ref[g] = kv0_ref[:, c1:c1 + NSA_HEAD_DIM].astype(BF16)
        ks_ref[g, :, 0:NSA_HEAD_DIM] = rope(head(kv1_ref, c0)).astype(BF16)
        ks_ref[g, :, NSA_HEAD_DIM:2 * NSA_HEAD_DIM] = blk_onehot
        vst_ref[g, 0:NSA_HEAD_DIM, :] = head(kv1_ref, c1).T.astype(BF16)
        vst_ref[g, NSA_HEAD_DIM:NSA_HEAD_DIM + VT_EXTRA, :] = ones_rows
        kw_ref[g] = rope(head(kv2_ref, c0)).astype(BF16)
        vwt_ref[g] = head(kv2_ref, c1).T.astype(BF16)


def _nsa_prep(u, pos_col, invf2, sgn, T):
    ts = 512
    kvw = 2 * NSA_KV_HEADS * NSA_HEAD_DIM
    kv_spec = lambda n: pl.BlockSpec((ts, kvw), lambda i: (i, (U_NSA + NSA_WIDTH) // kvw + n))
    row_out = pl.BlockSpec((NSA_KV_HEADS, ts, NSA_HEAD_DIM), lambda i: (0, i, 0))
    row_shape = jax.ShapeDtypeStruct((NSA_KV_HEADS, T, NSA_HEAD_DIM), BF16)
    col_out = pl.BlockSpec((NSA_KV_HEADS, NSA_HEAD_DIM, ts), lambda i: (0, 0, i))
    col_shape = jax.ShapeDtypeStruct((NSA_KV_HEADS, NSA_HEAD_DIM, T), BF16)
    return pl.pallas_call(
        _nsa_prep_kernel,
        grid=(T // ts,),
        in_specs=[pl.BlockSpec((ts, NSA_WIDTH), lambda i: (i, U_NSA // NSA_WIDTH)),
                  kv_spec(0), kv_spec(1), kv_spec(2),
                  pl.BlockSpec((ts, 1), lambda i: (i, 0)),
                  pl.BlockSpec((1, LANE), lambda i: (0, 0)),
                  pl.BlockSpec((1, LANE), lambda i: (0, 0))],
        out_specs=[pl.BlockSpec((NSA_KV_HEADS, NSA_GROUP, NSA_HEAD_DIM, ts), lambda i: (0, 0, 0, i)),
                   row_out, row_out,
                   pl.BlockSpec((NSA_KV_HEADS, ts, 2 * NSA_HEAD_DIM), lambda i: (0, i, 0)),
                   pl.BlockSpec((NSA_KV_HEADS, NSA_HEAD_DIM + VT_EXTRA, ts), lambda i: (0, 0, i)),
                   row_out, col_out],
        out_shape=[jax.ShapeDtypeStruct((NSA_KV_HEADS, NSA_GROUP, NSA_HEAD_DIM, T), BF16),
                   row_shape, row_shape,
                   jax.ShapeDtypeStruct((NSA_KV_HEADS, T, 2 * NSA_HEAD_DIM), BF16),
                   jax.ShapeDtypeStruct((NSA_KV_HEADS, NSA_HEAD_DIM + VT_EXTRA, T), BF16),
                   row_shape, col_shape],
        compiler_params=_cparams(("parallel",)),
        name="nsa_prep",
    )(u, u, u, u, pos_col, invf2, sgn)


def _gelu_tanh(x):
    return 0.5 * x * (1.0 + jnp.tanh(0.7978845608028654 * (x + 0.044715 * x * x * x)))


def _compress_kernel(r_ref, w1_ref, pos_ref, w2_ref, out_ref, *, transpose_out):
    r = r_ref[0]
    nr = r.shape[0]
    a = _dot(r, w1_ref[0])
    b = _dot(r, w1_ref[1])
    c0 = _dot(pos_ref[0], w1_ref[0]) + _dot(pos_ref[1], w1_ref[1])
    pre = a + pltpu.roll(b, nr - 1, 0) + c0[0:1, :]
    out = _dot(_gelu_tanh(pre).astype(BF16), w2_ref[...])
    row = lax.broadcasted_iota(I32, (nr, 1), 0)
    out = jnp.where(row < nr - 1, out, 0.0)
    out_ref[0] = (out.T if transpose_out else out).astype(out_ref.dtype)


def _compress(kv, w1, pos8, w2, B, S, transpose_out):
    nr = S // CMP_STRIDE
    half = CMP_STRIDE * NSA_HEAD_DIM
    r = kv.reshape(NSA_KV_HEADS, B * nr, half)
    if transpose_out:
        out_spec = pl.BlockSpec((1, NSA_HEAD_DIM, nr), lambda g, b: (g, 0, b))
        out_shape = jax.ShapeDtypeStruct((NSA_KV_HEADS, NSA_HEAD_DIM, B * nr), BF16)
    else:
        out_spec = pl.BlockSpec((1, nr, NSA_HEAD_DIM), lambda g, b: (g, b, 0))
        out_shape = jax.ShapeDtypeStruct((NSA_KV_HEADS, B * nr, NSA_HEAD_DIM), BF16)
    return pl.pallas_call(
        functools.partial(_compress_kernel, transpose_out=transpose_out),
        grid=(NSA_KV_HEADS, B),
        in_specs=[pl.BlockSpec((1, nr, half), lambda g, b: (g, b, 0)),
                  pl.BlockSpec((2, half, CMP_HIDDEN), lambda g, b: (0, 0, 0)),
                  pl.BlockSpec((2, 8, half), lambda g, b: (0, 0, 0)),
                  pl.BlockSpec((CMP_HIDDEN, NSA_HEAD_DIM), lambda g, b: (0, 0))],
        out_specs=out_spec,
        out_shape=out_shape,
        compiler_params=_cparams(("parallel", "parallel")),
        name="nsa_compress",
    )(r, w1, pos8, w2)


NSA_COLS = NSA_GROUP * Q_BLOCK


def _load_qt(q_ref):
    return jnp.concatenate([q_ref[0, h] for h in range(NSA_GROUP)], 1)


def _store_heads(o_t, out_ref):
    for h in range(NSA_GROUP):
        out_ref[:, h * NSA_HEAD_DIM:(h + 1) * NSA_HEAD_DIM] = (
            o_t[:, h * Q_BLOCK:(h + 1) * Q_BLOCK].T.astype(out_ref.dtype))


def _tile_heads(x):
    return jnp.concatenate([x] * NSA_GROUP, 1)


def _nsa_cmp_kernel(q_ref, kc_ref, vct_ref, ovt_ref, oc_ref, nb_ref, *, n_slc, topn):
    i = pl.program_id(2)
    nc = kc_ref.shape[1]
    qt = _load_qt(q_ref)
    t = i * Q_BLOCK + lax.broadcasted_iota(I32, (1, Q_BLOCK), 1)
    cur = t // SLC_BLOCK
    step = min(CMP_ROW_STEP, nc)
    ratio = SLC_BLOCK // CMP_STRIDE

    def causal_prefix(rows):
        c_idx = lax.broadcasted_iota(I32, (rows, 1), 0)
        valid = (c_idx * CMP_STRIDE + CMP_BLOCK - 1 <= t) & (c_idx < nc - 1)
        s = _dot(kc_ref[0, 0:rows, :], qt) + _tile_heads(jnp.where(valid, 0.0, NEG))
        m = jnp.max(s, 0, keepdims=True)
        p = jnp.where(m > 0.5 * NEG, jnp.exp2(s - m), 0.0)
        p = p * (1.0 / jnp.maximum(jnp.sum(p, 0, keepdims=True), 1e-30))
        _store_heads(_dot(vct_ref[0, :, 0:rows], p.astype(BF16)), oc_ref)
        psum = p[:, 0:Q_BLOCK]
        for h in range(1, NSA_GROUP):
            psum = psum + p[:, h * Q_BLOCK:(h + 1) * Q_BLOCK]
        nb_rows = rows // ratio
        imp = sum(_dot(ovt_ref[0:nb_rows, 0:rows], part) for part in _split2(psum))

        blk = lax.broadcasted_iota(I32, (nb_rows, Q_BLOCK), 0)
        blk_f = blk.astype(F32)
        causal_blk = blk <= cur
        forced = (blk == 0) | (blk == cur) | (blk == cur - 1)
        score = jnp.where(causal_blk, jnp.where(forced, jnp.inf, imp), -jnp.inf)
        sel = jnp.zeros((nb_rows, Q_BLOCK), F32)
        for _ in range(topn):
            mx = jnp.max(score, 0, keepdims=True)
            first = jnp.min(jnp.where(score == mx, blk_f, float(n_slc)), 0, keepdims=True)
            hit = blk_f == first
            sel = jnp.where(hit, 1.0, sel)
            score = jnp.where(hit, -jnp.inf, score)
        nb_ref[0, 0, 0:nb_rows, :] = jnp.where(causal_blk & (sel > 0.5), 0.0, NEG)
        if nb_rows < n_slc:
            nb_ref[0, 0, nb_rows:n_slc, :] = jnp.full((n_slc - nb_rows, Q_BLOCK), NEG, F32)

    need = ((i + 1) * Q_BLOCK - CMP_BLOCK) // CMP_STRIDE + 1
    for v in range(nc // step):
        @pl.when((need + step - 1) // step == v + 1)
        def _():
            causal_prefix((v + 1) * step)


def _nsa_cmp(qt, kcc, vcct, ovt, B, S):
    T = B * S
    nq = S // Q_BLOCK
    nc = S // CMP_STRIDE
    n_slc = S // SLC_BLOCK
    kern = functools.partial(_nsa_cmp_kernel, n_slc=n_slc, topn=min(SLC_TOPN, n_slc))
    return pl.pallas_call(
        kern,
        grid=(NSA_KV_HEADS, B, nq),
        in_specs=[pl.BlockSpec((1, NSA_GROUP, NSA_HEAD_DIM, Q_BLOCK), lambda g, b, i: (g, 0, 0, b * nq + i)),
                  pl.BlockSpec((1, nc, NSA_HEAD_DIM), lambda g, b, i: (g, b, 0)),
                  pl.BlockSpec((1, NSA_HEAD_DIM, nc), lambda g, b, i: (g, 0, b)),
                  pl.BlockSpec((n_slc, nc), lambda g, b, i: (0, 0))],
        out_specs=[pl.BlockSpec((Q_BLOCK, NSA_GROUP * NSA_HEAD_DIM), lambda g, b, i: (b * nq + i, g)),
                   pl.BlockSpec((1, 1, n_slc, Q_BLOCK), lambda g, b, i: (g, b * nq + i, 0, 0))],
        out_shape=[jax.ShapeDtypeStruct((T, NSA_WIDTH), BF16),
                   jax.ShapeDtypeStruct((NSA_KV_HEADS, B * nq, n_slc, Q_BLOCK), F32)],
        compiler_params=_cparams(("parallel", "parallel", "arbitrary")),
        name="nsa_cmp_select",
    )(qt, kcc, vcct, ovt)


SLC_QB = 2
SLC_Q = SLC_QB * Q_BLOCK
SLC_COLS = NSA_GROUP * SLC_Q


def _nsa_slc_kernel(q_ref, nb_ref, ks_ref, vst_ref, os_ref, m_ref, acc_ref, s_ref):
    i = pl.program_id(2)
    qt = _load_qt(q_ref)
    t = i * SLC_Q + lax.broadcasted_iota(I32, (1, SLC_Q), 1)
    m_ref[...] = jnp.full_like(m_ref, NEG)
    acc_ref[...] = jnp.zeros_like(acc_ref)
    pad = jnp.zeros((NSA_HEAD_DIM - SLC_CHUNK_BLOCKS, SLC_Q), F32)

    def scores(c, slot):
        start = pl.multiple_of(c * SLC_CHUNK, SLC_CHUNK)
        k = ks_ref[0, pl.ds(start, SLC_CHUNK), :]
        blocks = pl.ds(pl.multiple_of(c * SLC_CHUNK_BLOCKS, SLC_CHUNK_BLOCKS), SLC_CHUNK_BLOCKS)
        nb = jnp.concatenate([nb_ref[0, qb, blocks, :] for qb in range(SLC_QB)], 1)
        mask_rows = jnp.concatenate([nb, pad], 0).astype(BF16)
        rhs = jnp.concatenate([qt, _tile_heads(mask_rows)], 0)
        s_ref[slot] = _dot(k, rhs)

    def absorb(c, slot, diagonal):
        start = pl.multiple_of(c * SLC_CHUNK, SLC_CHUNK)
        vt = vst_ref[0, :, pl.ds(start, SLC_CHUNK)]
        s = s_ref[slot]
        if diagonal:
            key = start + lax.broadcasted_iota(I32, (SLC_CHUNK, 1), 0)
            s = s + _tile_heads(jnp.where(key <= t, 0.0, NEG))
        m_old = m_ref[...]
        m_new = jnp.maximum(m_old, jnp.max(s, 0, keepdims=True))
        p = jnp.exp2(s - m_new).astype(BF16)
        acc_ref[...] = jnp.exp2(m_old - m_new) * acc_ref[...] + _dot(vt, p)
        m_ref[...] = m_new

    n_full = (i * SLC_Q) // SLC_CHUNK
    scores(0, 0)

    def body(j, carry):
        scores(2 * j + 1, 1)
        absorb(2 * j, 0, False)
        scores(2 * j + 2, 0)
        absorb(2 * j + 1, 1, False)
        return carry

    lax.fori_loop(0, n_full // 2, body, 0)

    @pl.when(n_full % 2 == 1)
    def _():
        scores(n_full, 1)
        absorb(n_full - 1, 0, False)
        absorb(n_full, 1, True)

    @pl.when(n_full % 2 == 0)
    def _():
        absorb(n_full, 0, True)

    ok = m_ref[...] > 0.5 * NEG
    l = acc_ref[NSA_HEAD_DIM:NSA_HEAD_DIM + 1, :]
    inv = jnp.where(ok, 1.0 / jnp.where(ok, l, 1.0), 0.0)
    o_t = acc_ref[0:NSA_HEAD_DIM, :] * inv
    for h in range(NSA_GROUP):
        os_ref[:, h * NSA_HEAD_DIM:(h + 1) * NSA_HEAD_DIM] = (
            o_t[:, h * SLC_Q:(h + 1) * SLC_Q].T.astype(os_ref.dtype))


def _nsa_slc(qt, nb, ks, vst, B, S):
    T = B * S
    nq = S // SLC_Q
    n_slc = S // SLC_BLOCK
    return pl.pallas_call(
        _nsa_slc_kernel,
        grid=(NSA_KV_HEADS, B, nq),
        in_specs=[pl.BlockSpec((1, NSA_GROUP, NSA_HEAD_DIM, SLC_Q), lambda g, b, i: (g, 0, 0, b * nq + i)),
                  pl.BlockSpec((1, SLC_QB, n_slc, Q_BLOCK), lambda g, b, i: (g, b * nq + i, 0, 0)),
                  pl.BlockSpec((1, S, 2 * NSA_HEAD_DIM), lambda g, b, i: (g, b, 0)),
                  pl.BlockSpec((1, NSA_HEAD_DIM + VT_EXTRA, S), lambda g, b, i: (g, 0, b))],
        out_specs=pl.BlockSpec((SLC_Q, NSA_GROUP * NSA_HEAD_DIM), lambda g, b, i: (b * nq + i, g)),
        out_shape=jax.ShapeDtypeStruct((T, NSA_WIDTH), BF16),
        scratch_shapes=[pltpu.VMEM((1, SLC_COLS), F32),
                        pltpu.VMEM((NSA_HEAD_DIM + VT_EXTRA, SLC_COLS), F32),
                        pltpu.VMEM((2, SLC_CHUNK, SLC_COLS), F32)],
        compiler_params=_cparams(("parallel", "parallel", "arbitrary")),
        name="nsa_selected",
    )(qt, nb, ks, vst)


N_WIN_BLOCKS = WINDOW // Q_BLOCK + 1


def _nsa_win_kernel(q_ref, *refs):
    k_refs = refs[:N_WIN_BLOCKS]
    v_refs = refs[N_WIN_BLOCKS:2 * N_WIN_BLOCKS]
    ow_ref = refs[2 * N_WIN_BLOCKS]
    i = pl.program_id(2)
    qt = _load_qt(q_ref)
    k = jnp.concatenate([r[0] for r in k_refs], 0)
    vt = jnp.concatenate([r[0] for r in v_refs], 1)
    nk = N_WIN_BLOCKS * Q_BLOCK
    t = i * Q_BLOCK + lax.broadcasted_iota(I32, (1, Q_BLOCK), 1)
    wpos = i * Q_BLOCK - WINDOW + lax.broadcasted_iota(I32, (nk, 1), 0)
    ok = (wpos <= t) & (wpos > t - WINDOW) & (wpos >= 0)
    s = _dot(k, qt) + _tile_heads(jnp.where(ok, 0.0, NEG))
    m = jnp.max(s, 0, keepdims=True)
    p = jnp.exp2(s - m)
    o_t = _dot(vt, p.astype(BF16)) * (1.0 / jnp.sum(p, 0, keepdims=True))
    _store_heads(o_t, ow_ref)


def _nsa_win(qt, kw, vwt, B, S):
    T = B * S
    nq = S // Q_BLOCK

    def k_spec(c):
        back = N_WIN_BLOCKS - 1 - c
        return pl.BlockSpec((1, Q_BLOCK, NSA_HEAD_DIM),
                            lambda g, b, i: (g, b * nq + jnp.maximum(i - back, 0), 0))

    def v_spec(c):
        back = N_WIN_BLOCKS - 1 - c
        return pl.BlockSpec((1, NSA_HEAD_DIM, Q_BLOCK),
                            lambda g, b, i: (g, 0, b * nq + jnp.maximum(i - back, 0)))

    return pl.pallas_call(
        _nsa_win_kernel,
        grid=(NSA_KV_HEADS, B, nq),
        in_specs=[pl.BlockSpec((1, NSA_GROUP, NSA_HEAD_DIM, Q_BLOCK), lambda g, b, i: (g, 0, 0, b * nq + i))]
        + [k_spec(c) for c in range(N_WIN_BLOCKS)] + [v_spec(c) for c in range(N_WIN_BLOCKS)],
        out_specs=pl.BlockSpec((Q_BLOCK, NSA_GROUP * NSA_HEAD_DIM), lambda g, b, i: (b * nq + i, g)),
        out_shape=jax.ShapeDtypeStruct((T, NSA_WIDTH), BF16),
        compiler_params=_cparams(("parallel", "parallel", "arbitrary")),
        name="nsa_window",
    )(qt, *([kw] * N_WIN_BLOCKS), *([vwt] * N_WIN_BLOCKS))


def _outproj_kernel(hm_ref, oc_ref, os_ref, ow_ref, gt_ref, gb_ref, ge_ref, h_ref, w_ref, g_ref, b_ref,
                    out_ref):
    gates = jax.nn.sigmoid(gt_ref[...] + gb_ref[...]).astype(BF16)
    gx = _dot(gates, ge_ref[...])
    hn = (gx[:, 0:NSA_WIDTH] * oc_ref[...].astype(F32)
          + gx[:, NSA_WIDTH:2 * NSA_WIDTH] * os_ref[...].astype(F32)
          + gx[:, 2 * NSA_WIDTH:3 * NSA_WIDTH] * ow_ref[...].astype(F32))
    mix = (_dot(hm_ref[...], w_ref[0:MLSTM_WIDTH, :])
           + _dot(hn.astype(BF16), w_ref[MLSTM_WIDTH:MLSTM_WIDTH + NSA_WIDTH, :]))
    out_ref[...] = _layer_norm(DN_ALPHA * h_ref[...] + mix, g_ref[...], b_ref[...])


def _outproj(hm, oc, os_, ow, ug, gate_b, gate_expand, h, w_out, g, b):
    T = h.shape[0]
    tm = 256
    row = lambda i: (i, 0)
    fixed = lambda i: (0, 0)
    return pl.pallas_call(
        _outproj_kernel,
        grid=(T // tm,),
        in_specs=[pl.BlockSpec((tm, MLSTM_WIDTH), row), pl.BlockSpec((tm, NSA_WIDTH), row),
                  pl.BlockSpec((tm, NSA_WIDTH), row), pl.BlockSpec((tm, NSA_WIDTH), row),
                  pl.BlockSpec((tm, GATE_COLS), row),
                  pl.BlockSpec((1, LANE), fixed),
                  pl.BlockSpec((LANE, 3 * NSA_WIDTH), fixed),
                  pl.BlockSpec((tm, D_MODEL), row),
                  pl.BlockSpec((D_MODEL, D_MODEL), fixed),
                  pl.BlockSpec((1, D_MODEL), fixed), pl.BlockSpec((1, D_MODEL), fixed)],
        out_specs=pl.BlockSpec((tm, D_MODEL), row),
        out_shape=jax.ShapeDtypeStruct((T, D_MODEL), F32),
        compiler_params=_cparams(("parallel",)),
        name="mixer_outproj_ln",
    )(hm, oc, os_, ow, ug, gate_b, gate_expand, h, w_out, g, b)


def _matmul_kernel(x_ref, w_ref, o_ref):
    o_ref[...] = _dot(x_ref[...].astype(BF16), w_ref[...]).astype(o_ref.dtype)


def _mem_kv(mem2, wkv):
    M = mem2.shape[0]
    N = wkv.shape[1]
    tn = 512
    return pl.pallas_call(
        _matmul_kernel,
        grid=(N // tn,),
        in_specs=[pl.BlockSpec((M, D_MODEL), lambda j: (0, 0)),
                  pl.BlockSpec((D_MODEL, tn), lambda j: (0, j))],
        out_specs=pl.BlockSpec((M, tn), lambda j: (0, j)),
        out_shape=jax.ShapeDtypeStruct((M, N), BF16),
        compiler_params=_cparams(("parallel",)),
        name="mem_kv_proj",
    )(mem2, wkv)


def _xattn_kernel(h_ref, wq_ref, kv_ref, o_ref):
    hb = h_ref[...].astype(BF16)
    for hd in range(XA_HEADS):
        c0 = hd * XA_HEAD_DIM
        q = (_dot(hb, wq_ref[:, c0:c0 + XA_HEAD_DIM]) * (XA_HEAD_DIM ** -0.5)).astype(BF16)
        s = _dot_nt(q, kv_ref[:, c0:c0 + XA_HEAD_DIM])
        m = jnp.max(s, -1, keepdims=True)
        p = jnp.exp(s - m)
        o = _dot(p.astype(BF16), kv_ref[:, D_MODEL + c0:D_MODEL + c0 + XA_HEAD_DIM])
        o_ref[:, c0:c0 + XA_HEAD_DIM] = (o / jnp.sum(p, -1, keepdims=True)).astype(o_ref.dtype)


def _xattn(h1, wq, kv, B, S):
    T = B * S
    tm = 512
    n_mem = kv.shape[0] // B
    per_b = S // tm
    return pl.pallas_call(
        _xattn_kernel,
        grid=(T // tm,),
        in_specs=[pl.BlockSpec((tm, D_MODEL), lambda i: (i, 0)),
                  pl.BlockSpec((D_MODEL, D_MODEL), lambda i: (0, 0)),
                  pl.BlockSpec((n_mem, 2 * D_MODEL), lambda i: (i // per_b, 0))],
        out_specs=pl.BlockSpec((tm, D_MODEL), lambda i: (i, 0)),
        out_shape=jax.ShapeDtypeStruct((T, D_MODEL), BF16),
        compiler_params=_cparams(("parallel",)),
        name="mem_xattn",
    )(h1, wq, kv)


def _xa_out_router_kernel(o_ref, wo_ref, h_ref, g_ref, b_ref, rw_ref, rb_ref,
                          h2_ref, idx_ref, wgt_ref):
    xa = _dot(o_ref[...], wo_ref[...])
    h2 = _layer_norm(DN_ALPHA * h_ref[...] + xa, g_ref[...], b_ref[...])
    h2_ref[...] = h2
    hi, lo = _split2(h2)
    logits = _dot(hi, rw_ref[0]) + _dot(hi, rw_ref[1]) + _dot(lo, rw_ref[0])
    scores = jax.nn.sigmoid(logits)
    lane = lax.broadcasted_iota(I32, scores.shape, 1)
    lane_f = lane.astype(F32)
    biased = jnp.where(lane < N_EXPERTS, scores + rb_ref[...], -jnp.inf)
    idx_mat = jnp.zeros(scores.shape, F32)
    w_mat = jnp.zeros(scores.shape, F32)
    for kk in range(TOP_K):
        mx = jnp.max(biased, -1, keepdims=True)
        first = jnp.min(jnp.where(biased == mx, lane_f, float(LANE)), -1, keepdims=True)
        hit = lane_f == first
        top_s = jnp.sum(jnp.where(hit, scores, 0.0), -1, keepdims=True)
        idx_mat = jnp.where(lane == kk, first, idx_mat)
        w_mat = jnp.where(lane == kk, top_s, w_mat)
        biased = jnp.where(hit, -jnp.inf, biased)
    idx_ref[...] = idx_mat.astype(I32)
    wgt_ref[...] = w_mat / jnp.sum(w_mat, -1, keepdims=True) * ROUTED_SCALE


def _xa_out_router(o, wo, h1, g, b, rw2, rb):
    T = h1.shape[0]
    tm = 256
    row = lambda i: (i, 0)
    fixed = lambda i: (0, 0)
    return pl.pallas_call(
        _xa_out_router_kernel,
        grid=(T // tm,),
        in_specs=[pl.BlockSpec((tm, D_MODEL), row),
                  pl.BlockSpec((D_MODEL, D_MODEL), fixed),
                  pl.BlockSpec((tm, D_MODEL), row),
                  pl.BlockSpec((1, D_MODEL), fixed), pl.BlockSpec((1, D_MODEL), fixed),
                  pl.BlockSpec((2, D_MODEL, LANE), lambda i: (0, 0, 0)),
                  pl.BlockSpec((1, LANE), fixed)],
        out_specs=[pl.BlockSpec((tm, D_MODEL), row), pl.BlockSpec((tm, LANE), row),
                   pl.BlockSpec((tm, LANE), row)],
        out_shape=[jax.ShapeDtypeStruct((T, D_MODEL), F32),
                   jax.ShapeDtypeStruct((T, LANE), I32),
                   jax.ShapeDtypeStruct((T, LANE), F32)],
        compiler_params=_cparams(("parallel",)),
        name="xattn_out_ln_router",
    )(o, wo, h1, g, b, rw2, rb)


ROUTE_TM = 512
ZERO_ROWS = 128


def _route_rank_kernel(idx_ref, rank_ref, cnt_ref):
    tm = idx_ref.shape[0]

    @pl.when(pl.program_id(0) == 0)
    def _():
        cnt_ref[...] = jnp.zeros_like(cnt_ref)

    idx = idx_ref[...]
    lane = lax.broadcasted_iota(I32, (tm, LANE), 1)
    hits = [lane == idx[:, kk:kk + 1] for kk in range(TOP_K)]
    onehot = sum(jnp.where(hit, 1.0, 0.0) for hit in hits)
    r_i = lax.broadcasted_iota(I32, (tm, tm), 0)
    c_i = lax.broadcasted_iota(I32, (tm, tm), 1)
    before = jnp.where(r_i > c_i, 1.0, 0.0).astype(BF16)
    rank = _dot(before, onehot.astype(BF16)) + cnt_ref[0:1, :]
    out = jnp.zeros((tm, LANE), F32)
    for kk in range(TOP_K):
        out = jnp.where(lane == kk, jnp.sum(jnp.where(hits[kk], rank, 0.0), -1, keepdims=True), out)
    rank_ref[...] = out.astype(I32)
    cnt_ref[0:1, :] = cnt_ref[0:1, :] + jnp.sum(onehot, 0, keepdims=True)


def _route_rank(top_idx):
    T = top_idx.shape[0]
    tm = ROUTE_TM
    return pl.pallas_call(
        _route_rank_kernel,
        grid=(T // tm,),
        in_specs=[pl.BlockSpec((tm, LANE), lambda i: (i, 0))],
        out_specs=[pl.BlockSpec((tm, LANE), lambda i: (i, 0)),
                   pl.BlockSpec((8, LANE), lambda i: (0, 0))],
        out_shape=[jax.ShapeDtypeStruct((T, LANE), I32), jax.ShapeDtypeStruct((8, LANE), F32)],
        compiler_params=_cparams(("arbitrary",)),
        name="moe_route_rank",
    )(top_idx)


def _load_route(i, dest_hbm, dest_smem, isem):
    cp = pltpu.make_async_copy(dest_hbm.at[i], dest_smem, isem)
    cp.start()
    cp.wait()


_FILL_SIZES = tuple(s for s in (ZERO_ROWS >> n for n in range(ZERO_ROWS.bit_length())) if s >= SUBLANE)


def _experts_kernel(be_ref, nu_ref, x_ref, wg_ref, wu_ref, wd_ref, y_ref):
    used = pl.program_id(0) < nu_ref[0]

    @pl.when(used)
    def _():
        xb = x_ref[...].astype(BF16)
        a = _dot(xb, wg_ref[0])
        act = (a * jax.nn.sigmoid(a)) * _dot(xb, wu_ref[0])
        y_ref[...] = _dot(act.astype(BF16), wd_ref[0])

    @pl.when(jnp.logical_not(used))
    def _():
        y_ref[...] = jnp.zeros_like(y_ref)


def _experts(block_e, n_used, xs, wg, wu, wd):
    bm = EXPERT_BM
    P = xs.shape[0] // bm * bm
    rowmap = lambda j, be, nu: (jnp.minimum(j, nu[0] - 1), 0)
    wmap = lambda j, be, nu: (be[j], 0, 0)
    return pl.pallas_call(
        _experts_kernel,
        grid_spec=pltpu.PrefetchScalarGridSpec(
            num_scalar_prefetch=2,
            grid=(P // bm,),
            in_specs=[pl.BlockSpec((bm, D_MODEL), rowmap),
                      pl.BlockSpec((1, D_MODEL, D_EXPERT), wmap),
                      pl.BlockSpec((1, D_MODEL, D_EXPERT), wmap),
                      pl.BlockSpec((1, D_EXPERT, D_MODEL), wmap)],
            out_specs=pl.BlockSpec((bm, D_MODEL), lambda j, be, nu: (j, 0))),
        out_shape=jax.ShapeDtypeStruct((P, D_MODEL), F32),
        compiler_params=_cparams(("arbitrary",)),
        name="moe_experts",
    )(block_e, n_used, xs, wg, wu, wd)


SHARED_TF = 256
SHARED_STEPS = D_SHARED // SHARED_TF
DISPATCH_SHARE = -(-ROUTE_TM // (SHARED_STEPS * SUBLANE)) * SUBLANE
DUMP_ROWS = (DISPATCH_SHARE * SHARED_STEPS - ROUTE_TM) * TOP_K


def _zero_fill(fs_ref, fn_ref, tail_ref, zero_ref, xs_ref, zsem):
    zero_ref[...] = jnp.zeros_like(zero_ref)

    def fill_copies(e, go):
        n = fn_ref[e]
        first = fs_ref[e]
        lead = jnp.minimum((-first) & (SUBLANE - 1), n)
        for r in range(SUBLANE - 1):
            @pl.when(r < lead)
            def _():
                go(pltpu.make_async_copy(zero_ref.at[pl.ds(0, 1)], xs_ref.at[pl.ds(first + r, 1)], zsem))

        rest = n - lead
        off = first + lead
        for size in _FILL_SIZES:
            take = (rest & size) != 0

            @pl.when(take)
            def _():
                dst = xs_ref.at[pl.ds(pl.multiple_of(off, SUBLANE), size)]
                go(pltpu.make_async_copy(zero_ref.at[pl.ds(0, size)], dst, zsem))
            off = off + jnp.where(take, size, 0)

    def tail_copies(j, go):
        off = pl.multiple_of(tail_ref[0] + j * ZERO_ROWS, ZERO_ROWS)
        go(pltpu.make_async_copy(zero_ref, xs_ref.at[pl.ds(off, ZERO_ROWS)], zsem))

    for go in (lambda cp: cp.start(), lambda cp: cp.wait()):
        lax.fori_loop(0, N_EXPERTS, lambda e, c, go=go: (fill_copies(e, go), c)[1], 0)
        lax.fori_loop(0, tail_ref[1], lambda j, c, go=go: (tail_copies(j, go), c)[1], 0)


def _shared_ffn_kernel(fs_ref, fn_ref, tail_ref, dest_hbm, x_ref, wg_ref, wu_ref, wd_ref,
                       o_ref, xs_ref, xb_ref, dest_smem, zero_ref, isem, sem, zsem, *, dump_base):
    i = pl.program_id(0)
    f = pl.program_id(1)

    @pl.when(f == 0)
    def _():
        _load_route(i, dest_hbm, dest_smem, isem)
        xb_ref[...] = x_ref[...].astype(BF16)
        o_ref[...] = jnp.zeros_like(o_ref)

    @pl.when((f == 0) & (i == 0))
    def _():
        _zero_fill(fs_ref, fn_ref, tail_ref, zero_ref, xs_ref, zsem)

    for rr in range(DISPATCH_SHARE):
        r = f * DISPATCH_SHARE + rr
        live = r < ROUTE_TM
        rc = jnp.minimum(r, ROUTE_TM - 1)
        for kk in range(TOP_K):
            d = jnp.where(live, dest_smem[rc * TOP_K + kk], dump_base + (r - ROUTE_TM) * TOP_K + kk)
            pltpu.make_async_copy(x_ref.at[pl.ds(rc, 1)], xs_ref.at[pl.ds(d, 1)], sem).start()

    xb = xb_ref[...]
    a = _dot(xb, wg_ref[...])
    act = (a * jax.nn.sigmoid(a)) * _dot(xb, wu_ref[...])
    o_ref[...] += _dot(act.astype(BF16), wd_ref[...])

    @pl.when(f == SHARED_STEPS - 1)
    def _():
        for _ in range(TOP_K):
            pltpu.make_async_copy(x_ref, xs_ref.at[pl.ds(0, ROUTE_TM)], sem).wait()
        pltpu.make_async_copy(x_ref.at[pl.ds(0, DUMP_ROWS)], xs_ref.at[pl.ds(0, DUMP_ROWS)], sem).wait()


def _shared_ffn_dispatch(fill_start, fill_n, tail, dest2, h2, wg, wu, wd, P):
    T = h2.shape[0]
    tm, tf = ROUTE_TM, SHARED_TF
    return pl.pallas_call(
        functools.partial(_shared_ffn_kernel, dump_base=P),
        grid_spec=pltpu.PrefetchScalarGridSpec(
            num_scalar_prefetch=3,
            grid=(T // tm, SHARED_STEPS),
            in_specs=[pl.BlockSpec(memory_space=pl.ANY),
                      pl.BlockSpec((tm, D_MODEL), lambda i, f, *_: (i, 0)),
                      pl.BlockSpec((D_MODEL, tf), lambda i, f, *_: (0, f)),
                      pl.BlockSpec((D_MODEL, tf), lambda i, f, *_: (0, f)),
                      pl.BlockSpec((tf, D_MODEL), lambda i, f, *_: (f, 0))],
            out_specs=[pl.BlockSpec((tm, D_MODEL), lambda i, f, *_: (i, 0)),
                       pl.BlockSpec(memory_space=pl.ANY)],
            scratch_shapes=[pltpu.VMEM((tm, D_MODEL), BF16),
                            pltpu.SMEM((tm * TOP_K,), I32),
                            pltpu.VMEM((ZERO_ROWS, D_MODEL), F32),
                            pltpu.SemaphoreType.DMA, pltpu.SemaphoreType.DMA, pltpu.SemaphoreType.DMA]),
        out_shape=[jax.ShapeDtypeStruct((T, D_MODEL), F32),
                   jax.ShapeDtypeStruct((P + DUMP_ROWS, D_MODEL), F32)],
        compiler_params=_cparams(("arbitrary", "arbitrary")),
        name="shared_ffn_dispatch",
    )(fill_start, fill_n, tail, dest2, h2, wg, wu, wd)


COMBINE_SUB = 128
COMBINE_NSUB = ROUTE_TM // COMBINE_SUB


def _combine_kernel(dest_hbm, ys_hbm, w_ref, sh_ref, h_ref, g_ref, b_ref, out_ref,
                    dest_smem, gbuf, isem, sem):
    i = pl.program_id(0)
    _load_route(i, dest_hbm, dest_smem, isem)

    def gather(sub):
        slot = sub % 2

        def issue(r, carry):
            for kk in range(TOP_K):
                d = dest_smem[(sub * COMBINE_SUB + r) * TOP_K + kk]
                pltpu.make_async_copy(ys_hbm.at[pl.ds(d, 1)], gbuf.at[slot, kk, pl.ds(r, 1)],
                                      sem.at[slot]).start()
            return carry

        lax.fori_loop(0, COMBINE_SUB, issue, 0, unroll=8)

    gather(0)
    for sub in range(COMBINE_NSUB):
        slot = sub % 2
        if sub + 1 < COMBINE_NSUB:
            gather(sub + 1)
        for kk in range(TOP_K):
            pltpu.make_async_copy(ys_hbm.at[pl.ds(0, COMBINE_SUB)], gbuf.at[slot, kk], sem.at[slot]).wait()
        rs = slice(sub * COMBINE_SUB, (sub + 1) * COMBINE_SUB)
        w = w_ref[rs, :]
        routed = w[:, 0:1] * gbuf[slot, 0]
        for kk in range(1, TOP_K):
            routed = routed + w[:, kk:kk + 1] * gbuf[slot, kk]
        z = DN_ALPHA * h_ref[rs, :] + (routed + sh_ref[rs, :])
        out_ref[rs, :] = _layer_norm(z, g_ref[...], b_ref[...])


def _combine(dest2, ys, top_w, sh, h2, g, b):
    T = h2.shape[0]
    tm = ROUTE_TM
    row = lambda i: (i, 0)
    fixed = lambda i: (0, 0)
    return pl.pallas_call(
        _combine_kernel,
        grid=(T // tm,),
        in_specs=[pl.BlockSpec(memory_space=pl.ANY), pl.BlockSpec(memory_space=pl.ANY),
                  pl.BlockSpec((tm, LANE), row), pl.BlockSpec((tm, D_MODEL), row),
                  pl.BlockSpec((tm, D_MODEL), row),
                  pl.BlockSpec((1, D_MODEL), fixed), pl.BlockSpec((1, D_MODEL), fixed)],
        out_specs=pl.BlockSpec((tm, D_MODEL), row),
        scratch_shapes=[pltpu.SMEM((tm * TOP_K,), I32),
                        pltpu.VMEM((2, TOP_K, COMBINE_SUB, D_MODEL), F32),
                        pltpu.SemaphoreType.DMA, pltpu.SemaphoreType.DMA((2,))],
        out_shape=jax.ShapeDtypeStruct((T, D_MODEL), F32),
        compiler_params=_cparams(("arbitrary",)),
        name="moe_combine_ln",
    )(dest2, ys, top_w, sh, h2, g, b)


def _route_plan(counts, T):
    bm = EXPERT_BM
    padded = (counts + bm - 1) // bm * bm
    pad_end = jnp.cumsum(padded)
    pad_start = pad_end - padded
    n_blocks = T * TOP_K // bm + N_EXPERTS
    first_row = jnp.arange(n_blocks, dtype=I32) * bm
    block_e = jnp.minimum(jnp.sum((pad_end[None, :] <= first_row[:, None]).astype(I32), 1), N_EXPERTS - 1)
    n_used = (pad_end[-1:] // bm).astype(I32)
    tail = jnp.concatenate([pad_end[-1:], (n_blocks - n_used) * (bm // ZERO_ROWS)]).astype(I32)
    return (pad_start.astype(I32), (pad_start + counts).astype(I32), (padded - counts).astype(I32), tail,
            block_e, n_used, n_blocks * bm)


def _overlap_matrix(S):
    n_cmp_rows = S // CMP_STRIDE
    n_slc = S // SLC_BLOCK
    c_lo = np.arange(n_cmp_rows)[:, None] * CMP_STRIDE
    j_lo = np.arange(n_slc)[None, :] * SLC_BLOCK
    ov = (c_lo <= j_lo + SLC_BLOCK - 1) & (c_lo + CMP_BLOCK - 1 >= j_lo)
    return jnp.asarray(ov.T.astype(np.float32), dtype=BF16)


def _gate_expand_matrix():
    ge = np.zeros((LANE, 3 * NSA_WIDTH), np.float32)
    for hh in range(NSA_HEADS):
        for br in range(3):
            ge[8 + hh * 3 + br, br * NSA_WIDTH + hh * NSA_HEAD_DIM:br * NSA_WIDTH + (hh + 1) * NSA_HEAD_DIM] = 1.0
    return jnp.asarray(ge, dtype=BF16)


def kernel(x, mem, positions, ln0_g, ln0_b, w_in, conv_w, conv_b, igate_b, fgate_b, mlstm_norm_g, cmp_pos, cmp_w1k, cmp_w2k, cmp_w1v, cmp_w2v, nsa_gate_b, w_out, ln1_g, ln1_b, xa_wq, xa_wk, xa_wv, xa_wo, ln2_g, ln2_b, router_w, router_bias, moe_w_gate, moe_w_up, moe_w_down, sh_w_gate, sh_w_up, sh_w_down, ln3_g, ln3_b):
    B, S, D = x.shape
    T = B * S
    assert D == D_MODEL and w_in.shape[0] == DEPTH == 1
    assert S % MLSTM_L == 0 and S % SLC_CHUNK == 0 and T % ROUTE_TM == 0 and (T * TOP_K) % EXPERT_BM == 0
    row = lambda a: a.reshape(1, -1)

    w = w_in[0]
    w_r = jnp.concatenate([w[:, :3072], w[:, 3080:5640]], 1).astype(BF16)
    w_g = jnp.concatenate([w[:, 3072:3080], w[:, 5640:5664],
                           jnp.zeros((D, GATE_COLS - 2 * MLSTM_HEADS - 3 * NSA_HEADS), F32)], 1).astype(BF16)
    gate_b = jnp.concatenate([igate_b[0], fgate_b[0], nsa_gate_b[0],
                              jnp.zeros((LANE - 2 * MLSTM_HEADS - 3 * NSA_HEADS,), F32)]).reshape(1, LANE)
    half = NSA_HEAD_DIM // 2
    inv_freq = ROPE_THETA ** (-jnp.arange(half, dtype=F32) / half)
    invf2 = jnp.concatenate([inv_freq, inv_freq]).reshape(1, LANE)
    sgn = jnp.concatenate([-jnp.ones((half,), F32), jnp.ones((half,), F32)]).reshape(1, LANE)
    pos8 = jnp.zeros((2, 8, CMP_STRIDE * NSA_HEAD_DIM), F32).at[:, 0, :].set(
        cmp_pos[0].reshape(2, CMP_STRIDE * NSA_HEAD_DIM)).astype(BF16)
    w1k = cmp_w1k[0].reshape(2, CMP_STRIDE * NSA_HEAD_DIM, CMP_HIDDEN).astype(BF16)
    w1v = cmp_w1v[0].reshape(2, CMP_STRIDE * NSA_HEAD_DIM, CMP_HIDDEN).astype(BF16)
    rw = jnp.pad(router_w[0], ((0, 0), (0, LANE - N_EXPERTS)))
    rw_hi = rw.astype(BF16)
    rw2 = jnp.stack([rw_hi, (rw - rw_hi.astype(F32)).astype(BF16)])
    rb = jnp.pad(router_bias[0], (0, LANE - N_EXPERTS)).reshape(1, LANE)

    h, u, ug = _ln_inproj(x.reshape(T, D), row(ln0_g), row(ln0_b), w_r, w_g)
    hm = _mlstm(u, ug, conv_w[0], row(conv_b[0]), gate_b, row(mlstm_norm_g[0]), B, S)
    qt, kc, vc, ks, vst, kw, vwt = _nsa_prep(u, positions.reshape(T, 1), invf2, sgn, T)
    kcc = _compress(kc, w1k, pos8, cmp_w2k[0].astype(BF16), B, S, False)
    vcct = _compress(vc, w1v, pos8, cmp_w2v[0].astype(BF16), B, S, True)
    oc, nb = _nsa_cmp(qt, kcc, vcct, _overlap_matrix(S), B, S)
    os_ = _nsa_slc(qt, nb, ks, vst, B, S)
    ow = _nsa_win(qt, kw, vwt, B, S)
    h1 = _outproj(hm, oc, os_, ow, ug, gate_b, _gate_expand_matrix(), h, w_out[0].astype(BF16),
                  row(ln1_g[0]), row(ln1_b[0]))

    wkv = jnp.concatenate([xa_wk[0], xa_wv[0]], 1).astype(BF16)
    kv = _mem_kv(mem.reshape(-1, D), wkv)
    xo = _xattn(h1, xa_wq[0].astype(BF16), kv, B, S)
    h2, top_idx, top_w = _xa_out_router(xo, xa_wo[0].astype(BF16), h1, row(ln2_g[0]), row(ln2_b[0]), rw2, rb)

    rank, cnt = _route_rank(top_idx)
    counts = cnt[0, :N_EXPERTS].astype(I32)
    pad_start, fill_start, fill_n, tail, block_e, n_used, P = _route_plan(counts, T)
    slot_e = top_idx[:, :TOP_K, None] == jnp.arange(N_EXPERTS, dtype=I32)[None, None, :]
    dest = rank[:, :TOP_K] + jnp.sum(jnp.where(slot_e, pad_start[None, None, :], 0), -1)
    dest2 = dest.reshape(T // ROUTE_TM, ROUTE_TM * TOP_K)
    sh, xs = _shared_ffn_dispatch(fill_start, fill_n, tail, dest2, h2,
                                  sh_w_gate[0].astype(BF16), sh_w_up[0].astype(BF16),
                                  sh_w_down[0].astype(BF16), P)
    ys = _experts(block_e, n_used, xs, moe_w_gate[0].astype(BF16), moe_w_up[0].astype(BF16),
                  moe_w_down[0].astype(BF16))
    out = _combine(dest2, ys, top_w, sh, h2, row(ln3_g[0]), row(ln3_b[0]))
    return out.reshape(B, S, D)
```

```python
import functools

import numpy as np
import jax
import jax.numpy as jnp
from jax import lax
from jax.experimental import pallas as pl
from jax.experimental.pallas import tpu as pltpu

F32 = jnp.float32
BF16 = jnp.bfloat16
I32 = jnp.int32

D_MODEL = 2048
MLSTM_HEADS = 4
MLSTM_DV = 256
MLSTM_DQK = 128
MLSTM_QK = MLSTM_HEADS * MLSTM_DQK
MLSTM_WIDTH = MLSTM_HEADS * MLSTM_DV
CONV_WIDTH = 4
NSA_HEAD_DIM = 128
NSA_HEADS = 8
NSA_KV_HEADS = 2
NSA_GROUP = 4
NSA_WIDTH = NSA_HEADS * NSA_HEAD_DIM
CMP_BLOCK = 32
CMP_STRIDE = 16
CMP_HIDDEN = 256
SLC_BLOCK = 64
SLC_TOPN = 16
WINDOW = 512
Q_BLOCK = 128
XA_HEADS = 4
XA_HEAD_DIM = 512
N_EXPERTS = 64
TOP_K = 6
D_EXPERT = 1408
D_SHARED = 2816
ROUTED_SCALE = 2.446
ROPE_THETA = 10000.0
LN_EPS = 1e-5
DEPTH = 1
DN_ALPHA = (2.0 * DEPTH) ** 0.25

U_MLSTM = 0
U_NSA = 3072
U_COLS = 5632
GATE_COLS = 128

LANE = 128
SUBLANE = 8
NEG = -1e30
MLSTM_L = 256
VMEM_LIMIT = 56 * 1024 * 1024
EXPERT_BM = 256


def _cparams(sem):
    return pltpu.CompilerParams(dimension_semantics=sem, vmem_limit_bytes=VMEM_LIMIT)


def _dot(a, b):
    return jnp.dot(a, b, preferred_element_type=F32)


def _dot_nt(a, b):
    return lax.dot_general(a, b, (((1,), (1,)), ((), ())), preferred_element_type=F32)


def _layer_norm(z, g, b):
    mu = jnp.mean(z, -1, keepdims=True)
    zc = z - mu
    var = jnp.mean(zc * zc, -1, keepdims=True)
    return zc * lax.rsqrt(var + LN_EPS) * g + b


def _split3(x):
    hi = x.astype(BF16)
    r = x - hi.astype(F32)
    mid = r.astype(BF16)
    lo = (r - mid.astype(F32)).astype(BF16)
    return hi, mid, lo


def _split2(x):
    hi = x.astype(BF16)
    lo = (x - hi.astype(F32)).astype(BF16)
    return hi, lo


def _ln_inproj_kernel(x_ref, g_ref, b_ref, w_ref, wg_ref, h_ref, u_ref, ug_ref, hb_ref):
    @pl.when(pl.program_id(1) == 0)
    def _():
        hn = _layer_norm(x_ref[...], g_ref[...], b_ref[...])
        h_ref[...] = hn
        hb_ref[...] = hn.astype(BF16)
        ug_ref[...] = _dot(hb_ref[...], wg_ref[...])

    u_ref[...] = _dot(hb_ref[...], w_ref[...]).astype(u_ref.dtype)


def _ln_inproj(x2, g, b, w, w_gate):
    T = x2.shape[0]
    tm, tn = 1024, 512
    return pl.pallas_call(
        _ln_inproj_kernel,
        grid=(T // tm, U_COLS // tn),
        in_specs=[pl.BlockSpec((tm, D_MODEL), lambda i, j: (i, 0)),
                  pl.BlockSpec((1, D_MODEL), lambda i, j: (0, 0)),
                  pl.BlockSpec((1, D_MODEL), lambda i, j: (0, 0)),
                  pl.BlockSpec((D_MODEL, tn), lambda i, j: (0, j)),
                  pl.BlockSpec((D_MODEL, GATE_COLS), lambda i, j: (0, 0))],
        out_specs=[pl.BlockSpec((tm, D_MODEL), lambda i, j: (i, 0)),
                   pl.BlockSpec((tm, tn), lambda i, j: (i, j)),
                   pl.BlockSpec((tm, GATE_COLS), lambda i, j: (i, 0))],
        out_shape=[jax.ShapeDtypeStruct((T, D_MODEL), F32),
                   jax.ShapeDtypeStruct((T, U_COLS), BF16),
                   jax.ShapeDtypeStruct((T, GATE_COLS), F32)],
        scratch_shapes=[pltpu.VMEM((tm, D_MODEL), BF16)],
        compiler_params=_cparams(("parallel", "arbitrary")),
        name="ln_inproj",
    )(x2, g, b, w, w_gate)


def _log_sigmoid(x):
    return jnp.minimum(x, 0.0) - jnp.log1p(jnp.exp(-jnp.abs(x)))


def _mlstm_kernel(qk_ref, v_ref, o_ref, gt_ref, cw_ref, cb_ref, gb_ref, ng_ref, out_ref,
                  prev_ref, c_ref, n_ref, m_ref):
    L = MLSTM_L

    @pl.when(pl.program_id(1) == 0)
    def _():
        prev_ref[...] = jnp.zeros_like(prev_ref)
        c_ref[...] = jnp.zeros_like(c_ref)
        n_ref[...] = jnp.zeros_like(n_ref)
        m_ref[...] = jnp.zeros_like(m_ref)

    x = qk_ref[...].astype(F32)
    prev = prev_ref[...]
    row = lax.broadcasted_iota(I32, (L, 1), 0)
    cw = cw_ref[...]
    y = cb_ref[...] + cw[CONV_WIDTH - 1:CONV_WIDTH, :] * x
    for j in range(1, CONV_WIDTH):
        shifted = jnp.where(row < j, pltpu.roll(prev, j, 0), pltpu.roll(x, j, 0))
        y = y + cw[CONV_WIDTH - 1 - j:CONV_WIDTH - j, :] * shifted
    prev_ref[...] = x
    qk = y * jax.nn.sigmoid(y)

    gpre = gt_ref[...] + gb_ref[...]
    gpre_t = gpre.T
    r_i = lax.broadcasted_iota(I32, (L, L), 0)
    c_i = lax.broadcasted_iota(I32, (L, L), 1)
    causal = r_i >= c_i
    tril = jnp.where(causal, 1.0, 0.0).astype(BF16)
    triu = jnp.where(r_i <= c_i, 1.0, 0.0).astype(BF16)
    lf = _log_sigmoid(gpre)
    lf_t = _log_sigmoid(gpre_t)
    b_cols = sum(_dot(tril, part) for part in _split3(lf))
    b_rows = sum(_dot(part, triu) for part in _split3(lf_t))

    for h in range(MLSTM_HEADS):
        i_col = gpre[:, h:h + 1]
        i_row = gpre_t[h:h + 1, :]
        b_col = b_cols[:, MLSTM_HEADS + h:MLSTM_HEADS + h + 1]
        b_row = b_rows[MLSTM_HEADS + h:MLSTM_HEADS + h + 1, :]
        m_prev = m_ref[h:h + 1, 0:1]
        n_prev = n_ref[h:h + 1, :]
        c_prev = c_ref[h]

        q = qk[:, h * MLSTM_DQK:(h + 1) * MLSTM_DQK]
        k = qk[:, MLSTM_QK + h * MLSTM_DQK:MLSTM_QK + (h + 1) * MLSTM_DQK] * (MLSTM_DQK ** -0.5)
        v = v_ref[:, h * MLSTM_DV:(h + 1) * MLSTM_DV].astype(BF16)
        qb = q.astype(BF16)

        dmat = jnp.where(causal, b_col - b_row + i_row, -jnp.inf)
        inter = b_col + m_prev
        m_t = jnp.maximum(inter, jnp.max(dmat, -1, keepdims=True))
        s = _dot_nt(qb, k.astype(BF16)) * jnp.exp(dmat - m_t)
        a_inter = jnp.exp(inter - m_t)
        num = _dot(s.astype(BF16), v) + a_inter * _dot(qb, c_prev.astype(BF16))
        den = jnp.sum(s, -1, keepdims=True) + a_inter * jnp.sum(q * n_prev, -1, keepdims=True)
        hc = num / jnp.maximum(jnp.abs(den), jnp.exp(-m_t))

        mu = jnp.mean(hc, -1, keepdims=True)
        hcc = hc - mu
        var = jnp.mean(hcc * hcc, -1, keepdims=True)
        hn = hcc * lax.rsqrt(var + LN_EPS) * ng_ref[:, h * MLSTM_DV:(h + 1) * MLSTM_DV]
        og = jax.nn.sigmoid(o_ref[:, h * MLSTM_DV:(h + 1) * MLSTM_DV].astype(F32))
        out_ref[:, h * MLSTM_DV:(h + 1) * MLSTM_DV] = (og * hn).astype(out_ref.dtype)

        b_last = b_col[L - 1:L, :]
        g_col = b_last - b_col + i_col
        m_new = jnp.maximum(b_last + m_prev, jnp.max(g_col, 0, keepdims=True))
        kw = k * jnp.exp(g_col - m_new)
        decay = jnp.exp(b_last + m_prev - m_new)
        c_ref[h] = decay * c_prev + _dot(kw.T.astype(BF16), v)
        n_ref[h:h + 1, :] = decay * n_prev + jnp.sum(kw, 0, keepdims=True)
        m_ref[h:h + 1, :] = jnp.broadcast_to(m_new, (1, LANE))


def _mlstm(u, ug, conv_w, conv_b, gate_b, norm_g, B, S):
    T = B * S
    L = MLSTM_L
    nc = S // L
    row = lambda b, c: b * nc + c
    return pl.pallas_call(
        _mlstm_kernel,
        grid=(B, nc),
        in_specs=[pl.BlockSpec((L, 2 * MLSTM_QK), lambda b, c: (row(b, c), 0)),
                  pl.BlockSpec((L, MLSTM_WIDTH), lambda b, c: (row(b, c), 1)),
                  pl.BlockSpec((L, MLSTM_WIDTH), lambda b, c: (row(b, c), 2)),
                  pl.BlockSpec((L, GATE_COLS), lambda b, c: (row(b, c), 0)),
                  pl.BlockSpec((CONV_WIDTH, 2 * MLSTM_QK), lambda b, c: (0, 0)),
                  pl.BlockSpec((1, 2 * MLSTM_QK), lambda b, c: (0, 0)),
                  pl.BlockSpec((1, LANE), lambda b, c: (0, 0)),
                  pl.BlockSpec((1, MLSTM_WIDTH), lambda b, c: (0, 0))],
        out_specs=pl.BlockSpec((L, MLSTM_WIDTH), lambda b, c: (row(b, c), 0)),
        out_shape=jax.ShapeDtypeStruct((T, MLSTM_WIDTH), BF16),
        scratch_shapes=[pltpu.VMEM((L, 2 * MLSTM_QK), F32),
                        pltpu.VMEM((MLSTM_HEADS, MLSTM_DQK, MLSTM_DV), F32),
                        pltpu.VMEM((8, MLSTM_DQK), F32),
                        pltpu.VMEM((8, LANE), F32)],
        compiler_params=_cparams(("parallel", "arbitrary")),
        name="mlstm",
    )(u, u, u, ug, conv_w, conv_b, gate_b, norm_g)


LOG2E = 1.4426950408889634
SLC_CHUNK = 1024
SLC_CHUNK_BLOCKS = SLC_CHUNK // SLC_BLOCK
VT_EXTRA = 16
N_FORCED = 3
CMP_ROW_STEP = 256


def _nsa_prep_kernel(q_ref, kv0_ref, kv1_ref, kv2_ref, pos_ref, invf_ref, sgn_ref,
                     qt_ref, kc_ref, vc_ref, ks_ref, vst_ref, kw_ref, vwt_ref):
    ang = pos_ref[...].astype(F32) * invf_ref[...]
    cos = jnp.cos(ang)
    sin = jnp.sin(ang) * sgn_ref[...]

    def rope(x):
        return x * cos + pltpu.roll(x, NSA_HEAD_DIM // 2, 1) * sin

    scale = NSA_HEAD_DIM ** -0.5 * LOG2E
    ts = q_ref.shape[0]
    tok = pl.program_id(0) * ts + lax.broadcasted_iota(I32, (ts, NSA_HEAD_DIM), 0)
    blk_lane = (tok // SLC_BLOCK) % SLC_CHUNK_BLOCKS
    blk_onehot = jnp.where(lax.broadcasted_iota(I32, (ts, NSA_HEAD_DIM), 1) == blk_lane, 1.0, 0.0).astype(BF16)
    ones_rows = jnp.where(lax.broadcasted_iota(I32, (VT_EXTRA, ts), 0) == 0, 1.0, 0.0).astype(BF16)
    head = lambda ref, c: ref[:, c:c + NSA_HEAD_DIM].astype(F32)
    for g in range(NSA_KV_HEADS):
        for h in range(NSA_GROUP):
            qt_ref[g, h] = (rope(head(q_ref, (g * NSA_GROUP + h) * NSA_HEAD_DIM)) * scale).T.astype(BF16)
        c0 = g * NSA_HEAD_DIM
        c1 = NSA_KV_HEADS * NSA_HEAD_DIM + g * NSA_HEAD_DIM
        kc_ref[g] = rope(head(kv0_ref, c0)).astype(BF16)
        vc_ref[g] = kv0_ref[:, c1:c1 + NSA_HEAD_DIM].astype(BF16)
        ks_ref[g, :, 0:NSA_HEAD_DIM] = rope(head(kv1_ref, c0)).astype(BF16)
        ks_ref[g, :, NSA_HEAD_DIM:2 * NSA_HEAD_DIM] = blk_onehot
        vst_ref[g, 0:NSA_HEAD_DIM, :] = head(kv1_ref, c1).T.astype(BF16)
        vst_ref[g, NSA_HEAD_DIM:NSA_HEAD_DIM + VT_EXTRA, :] = ones_rows
        kw_ref[g] = rope(head(kv2_ref, c0)).astype(BF16)
        vwt_ref[g] = head(kv2_ref, c1).T.astype(BF16)


def _nsa_prep(u, pos_col, invf2, sgn, T):
    ts = 512
    kvw = 2 * NSA_KV_HEADS * NSA_HEAD_DIM
    kv_spec = lambda n: pl.BlockSpec((ts, kvw), lambda i: (i, (U_NSA + NSA_WIDTH) // kvw + n))
    row_out = pl.BlockSpec((NSA_KV_HEADS, ts, NSA_HEAD_DIM), lambda i: (0, i, 0))
    row_shape = jax.ShapeDtypeStruct((NSA_KV_HEADS, T, NSA_HEAD_DIM), BF16)
    col_out = pl.BlockSpec((NSA_KV_HEADS, NSA_HEAD_DIM, ts), lambda i: (0, 0, i))
    col_shape = jax.ShapeDtypeStruct((NSA_KV_HEADS, NSA_HEAD_DIM, T), BF16)
    return pl.pallas_call(
        _nsa_prep_kernel,
        grid=(T // ts,),
        in_specs=[pl.BlockSpec((ts, NSA_WIDTH), lambda i: (i, U_NSA // NSA_WIDTH)),
                  kv_spec(0), kv_spec(1), kv_spec(2),
                  pl.BlockSpec((ts, 1), lambda i: (i, 0)),
                  pl.BlockSpec((1, LANE), lambda i: (0, 0)),
                  pl.BlockSpec((1, LANE), lambda i: (0, 0))],
        out_specs=[pl.BlockSpec((NSA_KV_HEADS, NSA_GROUP, NSA_HEAD_DIM, ts), lambda i: (0, 0, 0, i)),
                   row_out, row_out,
                   pl.BlockSpec((NSA_KV_HEADS, ts, 2 * NSA_HEAD_DIM), lambda i: (0, i, 0)),
                   pl.BlockSpec((NSA_KV_HEADS, NSA_HEAD_DIM + VT_EXTRA, ts), lambda i: (0, 0, i)),
                   row_out, col_out],
        out_shape=[jax.ShapeDtypeStruct((NSA_KV_HEADS, NSA_GROUP, NSA_HEAD_DIM, T), BF16),
                   row_shape, row_shape,
                   jax.ShapeDtypeStruct((NSA_KV_HEADS, T, 2 * NSA_HEAD_DIM), BF16),
                   jax.ShapeDtypeStruct((NSA_KV_HEADS, NSA_HEAD_DIM + VT_EXTRA, T), BF16),
                   row_shape, col_shape],
        compiler_params=_cparams(("parallel",)),
        name="nsa_prep",
    )(u, u, u, u, pos_col, invf2, sgn)


def _gelu_tanh(x):
    return 0.5 * x * (1.0 + jnp.tanh(0.7978845608028654 * (x + 0.044715 * x * x * x)))


def _compress_kernel(r_ref, w1_ref, pos_ref, w2_ref, out_ref, *, transpose_out):
    r = r_ref[0]
    nr = r.shape[0]
    a = _dot(r, w1_ref[0])
    b = _dot(r, w1_ref[1])
    c0 = _dot(pos_ref[0], w1_ref[0]) + _dot(pos_ref[1], w1_ref[1])
    pre = a + pltpu.roll(b, nr - 1, 0) + c0[0:1, :]
    out = _dot(_gelu_tanh(pre).astype(BF16), w2_ref[...])
    row = lax.broadcasted_iota(I32, (nr, 1), 0)
    out = jnp.where(row < nr - 1, out, 0.0)
    out_ref[0] = (out.T if transpose_out else out).astype(out_ref.dtype)


def _compress(kv, w1, pos8, w2, B, S, transpose_out):
    nr = S // CMP_STRIDE
    half = CMP_STRIDE * NSA_HEAD_DIM
    r = kv.reshape(NSA_KV_HEADS, B * nr, half)
    if transpose_out:
        out_spec = pl.BlockSpec((1, NSA_HEAD_DIM, nr), lambda g, b: (g, 0, b))
        out_shape = jax.ShapeDtypeStruct((NSA_KV_HEADS, NSA_HEAD_DIM, B * nr), BF16)
    else:
        out_spec = pl.BlockSpec((1, nr, NSA_HEAD_DIM), lambda g, b: (g, b, 0))
        out_shape = jax.ShapeDtypeStruct((NSA_KV_HEADS, B * nr, NSA_HEAD_DIM), BF16)
    return pl.pallas_call(
        functools.partial(_compress_kernel, transpose_out=transpose_out),
        grid=(NSA_KV_HEADS, B),
        in_specs=[pl.BlockSpec((1, nr, half), lambda g, b: (g, b, 0)),
                  pl.BlockSpec((2, half, CMP_HIDDEN), lambda g, b: (0, 0, 0)),
                  pl.BlockSpec((2, 8, half), lambda g, b: (0, 0, 0)),
                  pl.BlockSpec((CMP_HIDDEN, NSA_HEAD_DIM), lambda g, b: (0, 0))],
        out_specs=out_spec,
        out_shape=out_shape,
        compiler_params=_cparams(("parallel", "parallel")),
        name="nsa_compress",
    )(r, w1, pos8, w2)


NSA_COLS = NSA_GROUP * Q_BLOCK


def _load_qt(q_ref):
    return jnp.concatenate([q_ref[0, h] for h in range(NSA_GROUP)], 1)


def _store_heads(o_t, out_ref):
    for h in range(NSA_GROUP):
        out_ref[:, h * NSA_HEAD_DIM:(h + 1) * NSA_HEAD_DIM] = (
            o_t[:, h * Q_BLOCK:(h + 1) * Q_BLOCK].T.astype(out_ref.dtype))


def _tile_heads(x):
    return jnp.concatenate([x] * NSA_GROUP, 1)


def _nsa_cmp_kernel(q_ref, kc_ref, vct_ref, ovt_ref, wd_ref, oc_ref, nb_ref, wdb_ref, *, n_slc, topn):
    wdb_ref[...] = wd_ref[...].astype(BF16)
    i = pl.program_id(2)
    nc = kc_ref.shape[1]
    qt = _load_qt(q_ref)
    t = i * Q_BLOCK + lax.broadcasted_iota(I32, (1, Q_BLOCK), 1)
    cur = t // SLC_BLOCK
    step = min(CMP_ROW_STEP, nc)
    ratio = SLC_BLOCK // CMP_STRIDE

    def causal_prefix(rows):
        c_idx = lax.broadcasted_iota(I32, (rows, 1), 0)
        valid = (c_idx * CMP_STRIDE + CMP_BLOCK - 1 <= t) & (c_idx < nc - 1)
        s = _dot(kc_ref[0, 0:rows, :], qt) + _tile_heads(jnp.where(valid, 0.0, NEG))
        m = jnp.max(s, 0, keepdims=True)
        p = jnp.where(m > 0.5 * NEG, jnp.exp2(s - m), 0.0)
        p = p * (1.0 / jnp.maximum(jnp.sum(p, 0, keepdims=True), 1e-30))
        _store_heads(_dot(vct_ref[0, :, 0:rows], p.astype(BF16)), oc_ref)
        psum = p[:, 0:Q_BLOCK]
        for h in range(1, NSA_GROUP):
            psum = psum + p[:, h * Q_BLOCK:(h + 1) * Q_BLOCK]
        nb_rows = rows // ratio
        imp = sum(_dot(ovt_ref[0:nb_rows, 0:rows], part) for part in _split2(psum))

        blk = lax.broadcasted_iota(I32, (nb_rows, Q_BLOCK), 0)
        blk_f = blk.astype(F32)
        causal_blk = blk <= cur
        forced = (blk == 0) | (blk == cur) | (blk == cur - 1)
        score = jnp.where(causal_blk & jnp.logical_not(forced), imp, -jnp.inf)
        sel = jnp.where(causal_blk & forced, 1.0, 0.0)
        for _ in range(topn - N_FORCED):
            mx = jnp.max(score, 0, keepdims=True)
            first = jnp.min(jnp.where(score == mx, blk_f, float(n_slc)), 0, keepdims=True)
            hit = blk_f == first
            sel = jnp.where(hit, 1.0, sel)
            score = jnp.where(hit, -jnp.inf, score)
        nb_ref[0, 0, 0:nb_rows, :] = jnp.where(causal_blk & (sel > 0.5), 0.0, NEG)
        if nb_rows < n_slc:
            nb_ref[0, 0, nb_rows:n_slc, :] = jnp.full((n_slc - nb_rows, Q_BLOCK), NEG, F32)

    need = ((i + 1) * Q_BLOCK - CMP_BLOCK) // CMP_STRIDE + 1
    for v in range(nc // step):
        @pl.when((need + step - 1) // step == v + 1)
        def _():
            causal_prefix((v + 1) * step)


def _nsa_cmp(qt, kcc, vcct, ovt, w_down, B, S):
    T = B * S
    nq = S // Q_BLOCK
    nc = S // CMP_STRIDE
    n_slc = S // SLC_BLOCK
    wd2 = w_down.reshape(-1, w_down.shape[-1])
    slab = wd2.shape[0] // (NSA_KV_HEADS * B * nq)
    assert slab * NSA_KV_HEADS * B * nq == wd2.shape[0] and slab % (2 * SUBLANE) == 0
    step = lambda g, b, i: ((g * B + b) * nq + i, 0)
    kern = functools.partial(_nsa_cmp_kernel, n_slc=n_slc, topn=min(SLC_TOPN, n_slc))
    return pl.pallas_call(
        kern,
        grid=(NSA_KV_HEADS, B, nq),
        in_specs=[pl.BlockSpec((1, NSA_GROUP, NSA_HEAD_DIM, Q_BLOCK), lambda g, b, i: (g, 0, 0, b * nq + i)),
                  pl.BlockSpec((1, nc, NSA_HEAD_DIM), lambda g, b, i: (g, b, 0)),
                  pl.BlockSpec((1, NSA_HEAD_DIM, nc), lambda g, b, i: (g, 0, b)),
                  pl.BlockSpec((n_slc, nc), lambda g, b, i: (0, 0)),
                  pl.BlockSpec((slab, wd2.shape[1]), step)],
        out_specs=[pl.BlockSpec((Q_BLOCK, NSA_GROUP * NSA_HEAD_DIM), lambda g, b, i: (b * nq + i, g)),
                   pl.BlockSpec((1, 1, n_slc, Q_BLOCK), lambda g, b, i: (g, b * nq + i, 0, 0)),
                   pl.BlockSpec((slab, wd2.shape[1]), step)],
        out_shape=[jax.ShapeDtypeStruct((T, NSA_WIDTH), BF16),
                   jax.ShapeDtypeStruct((NSA_KV_HEADS, B * nq, n_slc, Q_BLOCK), F32),
                   jax.ShapeDtypeStruct(wd2.shape, BF16)],
        compiler_params=_cparams(("parallel", "parallel", "arbitrary")),
        name="nsa_cmp_select",
    )(qt, kcc, vcct, ovt, wd2)


SLC_QB = 2
SLC_Q = SLC_QB * Q_BLOCK
SLC_COLS = NSA_GROUP * SLC_Q


def _nsa_slc_kernel(q_ref, nb_ref, ks_ref, vst_ref, wg_ref, wu_ref, os_ref, wgb_ref, wub_ref,
                    m_ref, acc_ref, s_ref):
    wgb_ref[...] = wg_ref[...].astype(BF16)
    wub_ref[...] = wu_ref[...].astype(BF16)
    i = pl.program_id(2)
    qt = _load_qt(q_ref)
    t = i * SLC_Q + lax.broadcasted_iota(I32, (1, SLC_Q), 1)
    m_ref[...] = jnp.full_like(m_ref, NEG)
    acc_ref[...] = jnp.zeros_like(acc_ref)
    pad = jnp.zeros((NSA_HEAD_DIM - SLC_CHUNK_BLOCKS, SLC_Q), F32)

    def scores(c, slot):
        start = pl.multiple_of(c * SLC_CHUNK, SLC_CHUNK)
        k = ks_ref[0, pl.ds(start, SLC_CHUNK), :]
        blocks = pl.ds(pl.multiple_of(c * SLC_CHUNK_BLOCKS, SLC_CHUNK_BLOCKS), SLC_CHUNK_BLOCKS)
        nb = jnp.concatenate([nb_ref[0, qb, blocks, :] for qb in range(SLC_QB)], 1)
        mask_rows = jnp.concatenate([nb, pad], 0).astype(BF16)
        rhs = jnp.concatenate([qt, _tile_heads(mask_rows)], 0)
        s_ref[slot] = _dot(k, rhs)

    def absorb(c, slot, diagonal):
        start = pl.multiple_of(c * SLC_CHUNK, SLC_CHUNK)
        vt = vst_ref[0, :, pl.ds(start, SLC_CHUNK)]
        s = s_ref[slot]
        if diagonal:
            key = start + lax.broadcasted_iota(I32, (SLC_CHUNK, 1), 0)
            s = s + _tile_heads(jnp.where(key <= t, 0.0, NEG))
        m_old = m_ref[...]
        m_new = jnp.maximum(m_old, jnp.max(s, 0, keepdims=True))
        p = jnp.exp2(s - m_new).astype(BF16)
        acc_ref[...] = jnp.exp2(m_old - m_new) * acc_ref[...] + _dot(vt, p)
        m_ref[...] = m_new

    n_full = (i * SLC_Q) // SLC_CHUNK
    scores(0, 0)

    def body(j, carry):
        scores(2 * j + 1, 1)
        absorb(2 * j, 0, False)
        scores(2 * j + 2, 0)
        absorb(2 * j + 1, 1, False)
        return carry

    lax.fori_loop(0, n_full // 2, body, 0)

    @pl.when(n_full % 2 == 1)
    def _():
        scores(n_full, 1)
        absorb(n_full - 1, 0, False)
        absorb(n_full, 1, True)

    @pl.when(n_full % 2 == 0)
    def _():
        absorb(n_full, 0, True)

    ok = m_ref[...] > 0.5 * NEG
    l = acc_ref[NSA_HEAD_DIM:NSA_HEAD_DIM + 1, :]
    inv = jnp.where(ok, 1.0 / jnp.where(ok, l, 1.0), 0.0)
    o_t = acc_ref[0:NSA_HEAD_DIM, :] * inv
    for h in range(NSA_GROUP):
        os_ref[:, h * NSA_HEAD_DIM:(h + 1) * NSA_HEAD_DIM] = (
            o_t[:, h * SLC_Q:(h + 1) * SLC_Q].T.astype(os_ref.dtype))


def _nsa_slc(qt, nb, ks, vst, w_gate, w_up, B, S):
    T = B * S
    nq = S // SLC_Q
    n_slc = S // SLC_BLOCK
    wg2 = w_gate.reshape(-1, w_gate.shape[-1])
    wu2 = w_up.reshape(-1, w_up.shape[-1])
    slab = wg2.shape[0] // (NSA_KV_HEADS * B * nq)
    assert slab * NSA_KV_HEADS * B * nq == wg2.shape[0] and slab % (2 * SUBLANE) == 0 and wu2.shape == wg2.shape
    step = lambda g, b, i: ((g * B + b) * nq + i, 0)
    resident = dict(pipeline_mode=pl.Buffered(1))
    return pl.pallas_call(
        _nsa_slc_kernel,
        grid=(NSA_KV_HEADS, B, nq),
        in_specs=[pl.BlockSpec((1, NSA_GROUP, NSA_HEAD_DIM, SLC_Q), lambda g, b, i: (g, 0, 0, b * nq + i)),
                  pl.BlockSpec((1, SLC_QB, n_slc, Q_BLOCK), lambda g, b, i: (g, b * nq + i, 0, 0)),
                  pl.BlockSpec((1, S, 2 * NSA_HEAD_DIM), lambda g, b, i: (g, b, 0), **resident),
                  pl.BlockSpec((1, NSA_HEAD_DIM + VT_EXTRA, S), lambda g, b, i: (g, 0, b), **resident),
                  pl.BlockSpec((slab, wg2.shape[1]), step), pl.BlockSpec((slab, wg2.shape[1]), step)],
        out_specs=[pl.BlockSpec((SLC_Q, NSA_GROUP * NSA_HEAD_DIM), lambda g, b, i: (b * nq + i, g)),
                   pl.BlockSpec((slab, wg2.shape[1]), step), pl.BlockSpec((slab, wg2.shape[1]), step)],
        out_shape=[jax.ShapeDtypeStruct((T, NSA_WIDTH), BF16),
                   jax.ShapeDtypeStruct(wg2.shape, BF16), jax.ShapeDtypeStruct(wg2.shape, BF16)],
        scratch_shapes=[pltpu.VMEM((1, SLC_COLS), F32),
                        pltpu.VMEM((NSA_HEAD_DIM + VT_EXTRA, SLC_COLS), F32),
                        pltpu.VMEM((2, SLC_CHUNK, SLC_COLS), F32)],
        compiler_params=_cparams(("parallel", "parallel", "arbitrary")),
        name="nsa_selected",
    )(qt, nb, ks, vst, wg2, wu2)


N_WIN_BLOCKS = WINDOW // Q_BLOCK + 1


def _nsa_win_kernel(q_ref, *refs):
    k_refs = refs[:N_WIN_BLOCKS]
    v_refs = refs[N_WIN_BLOCKS:2 * N_WIN_BLOCKS]
    ow_ref = refs[2 * N_WIN_BLOCKS]
    i = pl.program_id(2)
    qt = _load_qt(q_ref)
    k = jnp.concatenate([r[0] for r in k_refs], 0)
    vt = jnp.concatenate([r[0] for r in v_refs], 1)
    nk = N_WIN_BLOCKS * Q_BLOCK
    t = i * Q_BLOCK + lax.broadcasted_iota(I32, (1, Q_BLOCK), 1)
    wpos = i * Q_BLOCK - WINDOW + lax.broadcasted_iota(I32, (nk, 1), 0)
    ok = (wpos <= t) & (wpos > t - WINDOW) & (wpos >= 0)
    s = _dot(k, qt) + _tile_heads(jnp.where(ok, 0.0, NEG))
    m = jnp.max(s, 0, keepdims=True)
    p = jnp.exp2(s - m)
    o_t = _dot(vt, p.astype(BF16)) * (1.0 / jnp.sum(p, 0, keepdims=True))
    _store_heads(o_t, ow_ref)


def _nsa_win(qt, kw, vwt, B, S):
    T = B * S
    nq = S // Q_BLOCK

    def k_spec(c):
        back = N_WIN_BLOCKS - 1 - c
        return pl.BlockSpec((1, Q_BLOCK, NSA_HEAD_DIM),
                            lambda g, b, i: (g, b * nq + jnp.maximum(i - back, 0), 0))

    def v_spec(c):
        back = N_WIN_BLOCKS - 1 - c
        return pl.BlockSpec((1, NSA_HEAD_DIM, Q_BLOCK),
                            lambda g, b, i: (g, 0, b * nq + jnp.maximum(i - back, 0)))

    return pl.pallas_call(
        _nsa_win_kernel,
        grid=(NSA_KV_HEADS, B, nq),
        in_specs=[pl.BlockSpec((1, NSA_GROUP, NSA_HEAD_DIM, Q_BLOCK), lambda g, b, i: (g, 0, 0, b * nq + i))]
        + [k_spec(c) for c in range(N_WIN_BLOCKS)] + [v_spec(c) for c in range(N_WIN_BLOCKS)],
        out_specs=pl.BlockSpec((Q_BLOCK, NSA_GROUP * NSA_HEAD_DIM), lambda g, b, i: (b * nq + i, g)),
        out_shape=jax.ShapeDtypeStruct((T, NSA_WIDTH), BF16),
        compiler_params=_cparams(("parallel", "parallel", "arbitrary")),
        name="nsa_window",
    )(qt, *([kw] * N_WIN_BLOCKS), *([vwt] * N_WIN_BLOCKS))


def _outproj_kernel(hm_ref, oc_ref, os_ref, ow_ref, gt_ref, gb_ref, ge_ref, h_ref, w_ref, g_ref, b_ref,
                    out_ref):
    gates = jax.nn.sigmoid(gt_ref[...] + gb_ref[...]).astype(BF16)
    gx = _dot(gates, ge_ref[...])
    hn = (gx[:, 0:NSA_WIDTH] * oc_ref[...].astype(F32)
          + gx[:, NSA_WIDTH:2 * NSA_WIDTH] * os_ref[...].astype(F32)
          + gx[:, 2 * NSA_WIDTH:3 * NSA_WIDTH] * ow_ref[...].astype(F32))
    mix = (_dot(hm_ref[...], w_ref[0:MLSTM_WIDTH, :])
           + _dot(hn.astype(BF16), w_ref[MLSTM_WIDTH:MLSTM_WIDTH + NSA_WIDTH, :]))
    out_ref[...] = _layer_norm(DN_ALPHA * h_ref[...] + mix, g_ref[...], b_ref[...])


def _outproj(hm, oc, os_, ow, ug, gate_b, gate_expand, h, w_out, g, b):
    T = h.shape[0]
    tm = 256
    row = lambda i: (i, 0)
    fixed = lambda i: (0, 0)
    return pl.pallas_call(
        _outproj_kernel,
        grid=(T // tm,),
        in_specs=[pl.BlockSpec((tm, MLSTM_WIDTH), row), pl.BlockSpec((tm, NSA_WIDTH), row),
                  pl.BlockSpec((tm, NSA_WIDTH), row), pl.BlockSpec((tm, NSA_WIDTH), row),
                  pl.BlockSpec((tm, GATE_COLS), row),
                  pl.BlockSpec((1, LANE), fixed),
                  pl.BlockSpec((LANE, 3 * NSA_WIDTH), fixed),
                  pl.BlockSpec((tm, D_MODEL), row),
                  pl.BlockSpec((D_MODEL, D_MODEL), fixed),
                  pl.BlockSpec((1, D_MODEL), fixed), pl.BlockSpec((1, D_MODEL), fixed)],
        out_specs=pl.BlockSpec((tm, D_MODEL), row),
        out_shape=jax.ShapeDtypeStruct((T, D_MODEL), F32),
        compiler_params=_cparams(("parallel",)),
        name="mixer_outproj_ln",
    )(hm, oc, os_, ow, ug, gate_b, gate_expand, h, w_out, g, b)


def _matmul_kernel(x_ref, w_ref, o_ref):
    o_ref[...] = _dot(x_ref[...].astype(BF16), w_ref[...]).astype(o_ref.dtype)


def _mem_kv(mem2, wkv):
    M = mem2.shape[0]
    N = wkv.shape[1]
    tn = 512
    return pl.pallas_call(
        _matmul_kernel,
        grid=(N // tn,),
        in_specs=[pl.BlockSpec((M, D_MODEL), lambda j: (0, 0)),
                  pl.BlockSpec((D_MODEL, tn), lambda j: (0, j))],
        out_specs=pl.BlockSpec((M, tn), lambda j: (0, j)),
        out_shape=jax.ShapeDtypeStruct((M, N), BF16),
        compiler_params=_cparams(("parallel",)),
        name="mem_kv_proj",
    )(mem2, wkv)


def _xattn_kernel(h_ref, wq_ref, kv_ref, o_ref):
    hb = h_ref[...].astype(BF16)
    for hd in range(XA_HEADS):
        c0 = hd * XA_HEAD_DIM
        q = (_dot(hb, wq_ref[:, c0:c0 + XA_HEAD_DIM]) * (XA_HEAD_DIM ** -0.5)).astype(BF16)
        s = _dot_nt(q, kv_ref[:, c0:c0 + XA_HEAD_DIM])
        m = jnp.max(s, -1, keepdims=True)
        p = jnp.exp(s - m)
        o = _dot(p.astype(BF16), kv_ref[:, D_MODEL + c0:D_MODEL + c0 + XA_HEAD_DIM])
        o_ref[:, c0:c0 + XA_HEAD_DIM] = (o / jnp.sum(p, -1, keepdims=True)).astype(o_ref.dtype)


def _xattn(h1, wq, kv, B, S):
    T = B * S
    tm = 512
    n_mem = kv.shape[0] // B
    per_b = S // tm
    return pl.pallas_call(
        _xattn_kernel,
        grid=(T // tm,),
        in_specs=[pl.BlockSpec((tm, D_MODEL), lambda i: (i, 0)),
                  pl.BlockSpec((D_MODEL, D_MODEL), lambda i: (0, 0)),
                  pl.BlockSpec((n_mem, 2 * D_MODEL), lambda i: (i // per_b, 0))],
        out_specs=pl.BlockSpec((tm, D_MODEL), lambda i: (i, 0)),
        out_shape=jax.ShapeDtypeStruct((T, D_MODEL), BF16),
        compiler_params=_cparams(("parallel",)),
        name="mem_xattn",
    )(h1, wq, kv)


def _xa_out_router_kernel(o_ref, wo_ref, h_ref, g_ref, b_ref, rw_ref, rb_ref,
                          h2_ref, idx_ref, wgt_ref):
    xa = _dot(o_ref[...], wo_ref[...])
    h2 = _layer_norm(DN_ALPHA * h_ref[...] + xa, g_ref[...], b_ref[...])
    h2_ref[...] = h2
    hi, lo = _split2(h2)
    logits = _dot(hi, rw_ref[0]) + _dot(hi, rw_ref[1]) + _dot(lo, rw_ref[0])
    scores = jax.nn.sigmoid(logits)
    lane = lax.broadcasted_iota(I32, scores.shape, 1)
    lane_f = lane.astype(F32)
    biased = jnp.where(lane < N_EXPERTS, scores + rb_ref[...], -jnp.inf)
    idx_mat = jnp.zeros(scores.shape, F32)
    w_mat = jnp.zeros(scores.shape, F32)
    for kk in range(TOP_K):
        mx = jnp.max(biased, -1, keepdims=True)
        first = jnp.min(jnp.where(biased == mx, lane_f, float(LANE)), -1, keepdims=True)
        hit = lane_f == first
        top_s = jnp.sum(jnp.where(hit, scores, 0.0), -1, keepdims=True)
        idx_mat = jnp.where(lane == kk, first, idx_mat)
        w_mat = jnp.where(lane == kk, top_s, w_mat)
        biased = jnp.where(hit, -jnp.inf, biased)
    idx_ref[...] = idx_mat.astype(I32)
    wgt_ref[...] = w_mat / jnp.sum(w_mat, -1, keepdims=True) * ROUTED_SCALE


def _xa_out_router(o, wo, h1, g, b, rw2, rb):
    T = h1.shape[0]
    tm = 256
    row = lambda i: (i, 0)
    fixed = lambda i: (0, 0)
    return pl.pallas_call(
        _xa_out_router_kernel,
        grid=(T // tm,),
        in_specs=[pl.BlockSpec((tm, D_MODEL), row),
                  pl.BlockSpec((D_MODEL, D_MODEL), fixed),
                  pl.BlockSpec((tm, D_MODEL), row),
                  pl.BlockSpec((1, D_MODEL), fixed), pl.BlockSpec((1, D_MODEL), fixed),
                  pl.BlockSpec((2, D_MODEL, LANE), lambda i: (0, 0, 0)),
                  pl.BlockSpec((1, LANE), fixed)],
        out_specs=[pl.BlockSpec((tm, D_MODEL), row), pl.BlockSpec((tm, LANE), row),
                   pl.BlockSpec((tm, LANE), row)],
        out_shape=[jax.ShapeDtypeStruct((T, D_MODEL), F32),
                   jax.ShapeDtypeStruct((T, LANE), I32),
                   jax.ShapeDtypeStruct((T, LANE), F32)],
        compiler_params=_cparams(("parallel",)),
        name="xattn_out_ln_router",
    )(o, wo, h1, g, b, rw2, rb)


ROUTE_TM = 512
ZERO_ROWS = 128


def _route_rank_kernel(idx_ref, rank_ref, cnt_ref):
    tm = idx_ref.shape[0]

    @pl.when(pl.program_id(0) == 0)
    def _():
        cnt_ref[...] = jnp.zeros_like(cnt_ref)

    idx = idx_ref[...]
    lane = lax.broadcasted_iota(I32, (tm, LANE), 1)
    hits = [lane == idx[:, kk:kk + 1] for kk in range(TOP_K)]
    onehot = sum(jnp.where(hit, 1.0, 0.0) for hit in hits)
    r_i = lax.broadcasted_iota(I32, (tm, tm), 0)
    c_i = lax.broadcasted_iota(I32, (tm, tm), 1)
    before = jnp.where(r_i > c_i, 1.0, 0.0).astype(BF16)
    rank = _dot(before, onehot.astype(BF16)) + cnt_ref[0:1, :]
    out = jnp.zeros((tm, LANE), F32)
    for kk in range(TOP_K):
        out = jnp.where(lane == kk, jnp.sum(jnp.where(hits[kk], rank, 0.0), -1, keepdims=True), out)
    rank_ref[...] = out.astype(I32)
    cnt_ref[0:1, :] = cnt_ref[0:1, :] + jnp.sum(onehot, 0, keepdims=True)


def _route_rank(top_idx):
    T = top_idx.shape[0]
    tm = ROUTE_TM
    return pl.pallas_call(
        _route_rank_kernel,
        grid=(T // tm,),
        in_specs=[pl.BlockSpec((tm, LANE), lambda i: (i, 0))],
        out_specs=[pl.BlockSpec((tm, LANE), lambda i: (i, 0)),
                   pl.BlockSpec((8, LANE), lambda i: (0, 0))],
        out_shape=[jax.ShapeDtypeStruct((T, LANE), I32), jax.ShapeDtypeStruct((8, LANE), F32)],
        compiler_params=_cparams(("arbitrary",)),
        name="moe_route_rank",
    )(top_idx)


def _load_route(i, dest_hbm, dest_smem, isem):
    cp = pltpu.make_async_copy(dest_hbm.at[i], dest_smem, isem)
    cp.start()
    cp.wait()


_FILL_SIZES = tuple(s for s in (ZERO_ROWS >> n for n in range(ZERO_ROWS.bit_length())) if s >= SUBLANE)


def _experts_kernel(be_ref, nu_ref, x_ref, wg_ref, wu_ref, wd_ref, y_ref):
    used = pl.program_id(0) < nu_ref[0]

    @pl.when(used)
    def _():
        xb = x_ref[...].astype(BF16)
        a = _dot(xb, wg_ref[0])
        act = (a * jax.nn.sigmoid(a)) * _dot(xb, wu_ref[0])
        y_ref[...] = _dot(act.astype(BF16), wd_ref[0])

    @pl.when(jnp.logical_not(used))
    def _():
        y_ref[...] = jnp.zeros_like(y_ref)


def _experts(block_e, n_used, xs, wg, wu, wd):
    bm = EXPERT_BM
    P = xs.shape[0] // bm * bm
    rowmap = lambda j, be, nu: (jnp.minimum(j, nu[0] - 1), 0)
    wmap = lambda j, be, nu: (be[j], 0, 0)
    return pl.pallas_call(
        _experts_kernel,
        grid_spec=pltpu.PrefetchScalarGridSpec(
            num_scalar_prefetch=2,
            grid=(P // bm,),
            in_specs=[pl.BlockSpec((bm, D_MODEL), rowmap),
                      pl.BlockSpec((1, D_MODEL, D_EXPERT), wmap),
                      pl.BlockSpec((1, D_MODEL, D_EXPERT), wmap),
                      pl.BlockSpec((1, D_EXPERT, D_MODEL), wmap)],
            out_specs=pl.BlockSpec((bm, D_MODEL), lambda j, be, nu: (j, 0))),
        out_shape=jax.ShapeDtypeStruct((P, D_MODEL), F32),
        compiler_params=_cparams(("arbitrary",)),
        name="moe_experts",
    )(block_e, n_used, xs, wg, wu, wd)


SHARED_TF = 256
SHARED_STEPS = D_SHARED // SHARED_TF
DISPATCH_SHARE = -(-ROUTE_TM // (SHARED_STEPS * SUBLANE)) * SUBLANE
DUMP_ROWS = (DISPATCH_SHARE * SHARED_STEPS - ROUTE_TM) * TOP_K


def _zero_fill(fs_ref, fn_ref, tail_ref, zero_ref, xs_ref, zsem):
    zero_ref[...] = jnp.zeros_like(zero_ref)

    def fill_copies(e, go):
        n = fn_ref[e]
        first = fs_ref[e]
        lead = jnp.minimum((-first) & (SUBLANE - 1), n)
        for r in range(SUBLANE - 1):
            @pl.when(r < lead)
            def _():
                go(pltpu.make_async_copy(zero_ref.at[pl.ds(0, 1)], xs_ref.at[pl.ds(first + r, 1)], zsem))

        rest = n - lead
        off = first + lead
        for size in _FILL_SIZES:
            take = (rest & size) != 0

            @pl.when(take)
            def _():
                dst = xs_ref.at[pl.ds(pl.multiple_of(off, SUBLANE), size)]
                go(pltpu.make_async_copy(zero_ref.at[pl.ds(0, size)], dst, zsem))
            off = off + jnp.where(take, size, 0)

    def tail_copies(j, go):
        off = pl.multiple_of(tail_ref[0] + j * ZERO_ROWS, ZERO_ROWS)
        go(pltpu.make_async_copy(zero_ref, xs_ref.at[pl.ds(off, ZERO_ROWS)], zsem))

    for go in (lambda cp: cp.start(), lambda cp: cp.wait()):
        lax.fori_loop(0, N_EXPERTS, lambda e, c, go=go: (fill_copies(e, go), c)[1], 0)
        lax.fori_loop(0, tail_ref[1], lambda j, c, go=go: (tail_copies(j, go), c)[1], 0)


def _shared_ffn_kernel(fs_ref, fn_ref, tail_ref, dest_hbm, x_ref, wg_ref, wu_ref, wd_ref,
                       o_ref, xs_ref, xb_ref, dest_smem, zero_ref, isem, sem, zsem, *, dump_base):
    i = pl.program_id(0)
    f = pl.program_id(1)

    @pl.when(f == 0)
    def _():
        _load_route(i, dest_hbm, dest_smem, isem)
        xb_ref[...] = x_ref[...].astype(BF16)
        o_ref[...] = jnp.zeros_like(o_ref)

    @pl.when((f == 0) & (i == 0))
    def _():
        _zero_fill(fs_ref, fn_ref, tail_ref, zero_ref, xs_ref, zsem)

    for rr in range(DISPATCH_SHARE):
        r = f * DISPATCH_SHARE + rr
        live = r < ROUTE_TM
        rc = jnp.minimum(r, ROUTE_TM - 1)
        for kk in range(TOP_K):
            d = jnp.where(live, dest_smem[rc * TOP_K + kk], dump_base + (r - ROUTE_TM) * TOP_K + kk)
            pltpu.make_async_copy(x_ref.at[pl.ds(rc, 1)], xs_ref.at[pl.ds(d, 1)], sem).start()

    xb = xb_ref[...]
    a = _dot(xb, wg_ref[...])
    act = (a * jax.nn.sigmoid(a)) * _dot(xb, wu_ref[...])
    o_ref[...] += _dot(act.astype(BF16), wd_ref[...])

    @pl.when(f == SHARED_STEPS - 1)
    def _():
        for _ in range(TOP_K):
            pltpu.make_async_copy(x_ref, xs_ref.at[pl.ds(0, ROUTE_TM)], sem).wait()
        pltpu.make_async_copy(x_ref.at[pl.ds(0, DUMP_ROWS)], xs_ref.at[pl.ds(0, DUMP_ROWS)], sem).wait()


def _shared_ffn_dispatch(fill_start, fill_n, tail, dest2, h2, wg, wu, wd, P):
    T = h2.shape[0]
    tm, tf = ROUTE_TM, SHARED_TF
    return pl.pallas_call(
        functools.partial(_shared_ffn_kernel, dump_base=P),
        grid_spec=pltpu.PrefetchScalarGridSpec(
            num_scalar_prefetch=3,
            grid=(T // tm, SHARED_STEPS),
            in_specs=[pl.BlockSpec(memory_space=pl.ANY),
                      pl.BlockSpec((tm, D_MODEL), lambda i, f, *_: (i, 0)),
                      pl.BlockSpec((D_MODEL, tf), lambda i, f, *_: (0, f)),
                      pl.BlockSpec((D_MODEL, tf), lambda i, f, *_: (0, f)),
                      pl.BlockSpec((tf, D_MODEL), lambda i, f, *_: (f, 0))],
            out_specs=[pl.BlockSpec((tm, D_MODEL), lambda i, f, *_: (i, 0)),
                       pl.BlockSpec(memory_space=pl.ANY)],
            scratch_shapes=[pltpu.VMEM((tm, D_MODEL), BF16),
                            pltpu.SMEM((tm * TOP_K,), I32),
                            pltpu.VMEM((ZERO_ROWS, D_MODEL), F32),
                            pltpu.SemaphoreType.DMA, pltpu.SemaphoreType.DMA, pltpu.SemaphoreType.DMA]),
        out_shape=[jax.ShapeDtypeStruct((T, D_MODEL), F32),
                   jax.ShapeDtypeStruct((P + DUMP_ROWS, D_MODEL), F32)],
        compiler_params=_cparams(("arbitrary", "arbitrary")),
        name="shared_ffn_dispatch",
    )(fill_start, fill_n, tail, dest2, h2, wg, wu, wd)


COMBINE_SUB = 128
COMBINE_NSUB = ROUTE_TM // COMBINE_SUB


def _combine_kernel(dest_hbm, ys_hbm, w_ref, sh_ref, h_ref, g_ref, b_ref, out_ref,
                    dest_smem, gbuf, isem, sem):
    i = pl.program_id(0)
    _load_route(i, dest_hbm, dest_smem, isem)

    def gather(sub):
        slot = sub % 2

        def issue(r, carry):
            for kk in range(TOP_K):
                d = dest_smem[(sub * COMBINE_SUB + r) * TOP_K + kk]
                pltpu.make_async_copy(ys_hbm.at[pl.ds(d, 1)], gbuf.at[slot, kk, pl.ds(r, 1)],
                                      sem.at[slot]).start()
            return carry

        lax.fori_loop(0, COMBINE_SUB, issue, 0, unroll=8)

    gather(0)
    for sub in range(COMBINE_NSUB):
        slot = sub % 2
        if sub + 1 < COMBINE_NSUB:
            gather(sub + 1)
        for kk in range(TOP_K):
            pltpu.make_async_copy(ys_hbm.at[pl.ds(0, COMBINE_SUB)], gbuf.at[slot, kk], sem.at[slot]).wait()
        rs = slice(sub * COMBINE_SUB, (sub + 1) * COMBINE_SUB)
        w = w_ref[rs, :]
        routed = w[:, 0:1] * gbuf[slot, 0]
        for kk in range(1, TOP_K):
            routed = routed + w[:, kk:kk + 1] * gbuf[slot, kk]
        z = DN_ALPHA * h_ref[rs, :] + (routed + sh_ref[rs, :])
        out_ref[rs, :] = _layer_norm(z, g_ref[...], b_ref[...])


def _combine(dest2, ys, top_w, sh, h2, g, b):
    T = h2.shape[0]
    tm = ROUTE_TM
    row = lambda i: (i, 0)
    fixed = lambda i: (0, 0)
    return pl.pallas_call(
        _combine_kernel,
        grid=(T // tm,),
        in_specs=[pl.BlockSpec(memory_space=pl.ANY), pl.BlockSpec(memory_space=pl.ANY),
                  pl.BlockSpec((tm, LANE), row), pl.BlockSpec((tm, D_MODEL), row),
                  pl.BlockSpec((tm, D_MODEL), row),
                  pl.BlockSpec((1, D_MODEL), fixed), pl.BlockSpec((1, D_MODEL), fixed)],
        out_specs=pl.BlockSpec((tm, D_MODEL), row),
        scratch_shapes=[pltpu.SMEM((tm * TOP_K,), I32),
                        pltpu.VMEM((2, TOP_K, COMBINE_SUB, D_MODEL), F32),
                        pltpu.SemaphoreType.DMA, pltpu.SemaphoreType.DMA((2,))],
        out_shape=jax.ShapeDtypeStruct((T, D_MODEL), F32),
        compiler_params=_cparams(("arbitrary",)),
        name="moe_combine_ln",
    )(dest2, ys, top_w, sh, h2, g, b)


def _route_plan(counts, T):
    bm = EXPERT_BM
    padded = (counts + bm - 1) // bm * bm
    pad_end = jnp.cumsum(padded)
    pad_start = pad_end - padded
    n_blocks = T * TOP_K // bm + N_EXPERTS
    first_row = jnp.arange(n_blocks, dtype=I32) * bm
    block_e = jnp.minimum(jnp.sum((pad_end[None, :] <= first_row[:, None]).astype(I32), 1), N_EXPERTS - 1)
    n_used = (pad_end[-1:] // bm).astype(I32)
    tail = jnp.concatenate([pad_end[-1:], (n_blocks - n_used) * (bm // ZERO_ROWS)]).astype(I32)
    return (pad_start.astype(I32), (pad_start + counts).astype(I32), (padded - counts).astype(I32), tail,
            block_e, n_used, n_blocks * bm)


def _overlap_matrix(S):
    n_cmp_rows = S // CMP_STRIDE
    n_slc = S // SLC_BLOCK
    c_lo = np.arange(n_cmp_rows)[:, None] * CMP_STRIDE
    j_lo = np.arange(n_slc)[None, :] * SLC_BLOCK
    ov = (c_lo <= j_lo + SLC_BLOCK - 1) & (c_lo + CMP_BLOCK - 1 >= j_lo)
    return jnp.asarray(ov.T.astype(np.float32), dtype=BF16)


def _gate_expand_matrix():
    ge = np.zeros((LANE, 3 * NSA_WIDTH), np.float32)
    for hh in range(NSA_HEADS):
        for br in range(3):
            ge[8 + hh * 3 + br, br * NSA_WIDTH + hh * NSA_HEAD_DIM:br * NSA_WIDTH + (hh + 1) * NSA_HEAD_DIM] = 1.0
    return jnp.asarray(ge, dtype=BF16)


def kernel(x, mem, positions, ln0_g, ln0_b, w_in, conv_w, conv_b, igate_b, fgate_b, mlstm_norm_g, cmp_pos, cmp_w1k, cmp_w2k, cmp_w1v, cmp_w2v, nsa_gate_b, w_out, ln1_g, ln1_b, xa_wq, xa_wk, xa_wv, xa_wo, ln2_g, ln2_b, router_w, router_bias, moe_w_gate, moe_w_up, moe_w_down, sh_w_gate, sh_w_up, sh_w_down, ln3_g, ln3_b):
    B, S, D = x.shape
    T = B * S
    assert D == D_MODEL and w_in.shape[0] == DEPTH == 1
    assert S % MLSTM_L == 0 and S % SLC_CHUNK == 0 and T % ROUTE_TM == 0 and (T * TOP_K) % EXPERT_BM == 0
    row = lambda a: a.reshape(1, -1)

    w = w_in[0]
    w_r = jnp.concatenate([w[:, :3072], w[:, 3080:5640]], 1).astype(BF16)
    w_g = jnp.concatenate([w[:, 3072:3080], w[:, 5640:5664],
                           jnp.zeros((D, GATE_COLS - 2 * MLSTM_HEADS - 3 * NSA_HEADS), F32)], 1).astype(BF16)
    gate_b = jnp.concatenate([igate_b[0], fgate_b[0], nsa_gate_b[0],
                              jnp.zeros((LANE - 2 * MLSTM_HEADS - 3 * NSA_HEADS,), F32)]).reshape(1, LANE)
    half = NSA_HEAD_DIM // 2
    inv_freq = ROPE_THETA ** (-jnp.arange(half, dtype=F32) / half)
    invf2 = jnp.concatenate([inv_freq, inv_freq]).reshape(1, LANE)
    sgn = jnp.concatenate([-jnp.ones((half,), F32), jnp.ones((half,), F32)]).reshape(1, LANE)
    pos8 = jnp.zeros((2, 8, CMP_STRIDE * NSA_HEAD_DIM), F32).at[:, 0, :].set(
        cmp_pos[0].reshape(2, CMP_STRIDE * NSA_HEAD_DIM)).astype(BF16)
    w1k = cmp_w1k[0].reshape(2, CMP_STRIDE * NSA_HEAD_DIM, CMP_HIDDEN).astype(BF16)
    w1v = cmp_w1v[0].reshape(2, CMP_STRIDE * NSA_HEAD_DIM, CMP_HIDDEN).astype(BF16)
    rw = jnp.pad(router_w[0], ((0, 0), (0, LANE - N_EXPERTS)))
    rw_hi = rw.astype(BF16)
    rw2 = jnp.stack([rw_hi, (rw - rw_hi.astype(F32)).astype(BF16)])
    rb = jnp.pad(router_bias[0], (0, LANE - N_EXPERTS)).reshape(1, LANE)

    h, u, ug = _ln_inproj(x.reshape(T, D), row(ln0_g), row(ln0_b), w_r, w_g)
    hm = _mlstm(u, ug, conv_w[0], row(conv_b[0]), gate_b, row(mlstm_norm_g[0]), B, S)
    qt, kc, vc, ks, vst, kw, vwt = _nsa_prep(u, positions.reshape(T, 1), invf2, sgn, T)
    kcc = _compress(kc, w1k, pos8, cmp_w2k[0].astype(BF16), B, S, False)
    vcct = _compress(vc, w1v, pos8, cmp_w2v[0].astype(BF16), B, S, True)
    oc, nb, wd_b = _nsa_cmp(qt, kcc, vcct, _overlap_matrix(S), moe_w_down[0], B, S)
    os_, wg_b, wu_b = _nsa_slc(qt, nb, ks, vst, moe_w_gate[0], moe_w_up[0], B, S)
    ow = _nsa_win(qt, kw, vwt, B, S)
    h1 = _outproj(hm, oc, os_, ow, ug, gate_b, _gate_expand_matrix(), h, w_out[0].astype(BF16),
                  row(ln1_g[0]), row(ln1_b[0]))

    wkv = jnp.concatenate([xa_wk[0], xa_wv[0]], 1).astype(BF16)
    kv = _mem_kv(mem.reshape(-1, D), wkv)
    xo = _xattn(h1, xa_wq[0].astype(BF16), kv, B, S)
    h2, top_idx, top_w = _xa_out_router(xo, xa_wo[0].astype(BF16), h1, row(ln2_g[0]), row(ln2_b[0]), rw2, rb)

    rank, cnt = _route_rank(top_idx)
    counts = cnt[0, :N_EXPERTS].astype(I32)
    pad_start, fill_start, fill_n, tail, block_e, n_used, P = _route_plan(counts, T)
    slot_e = top_idx[:, :TOP_K, None] == jnp.arange(N_EXPERTS, dtype=I32)[None, None, :]
    dest = rank[:, :TOP_K] + jnp.sum(jnp.where(slot_e, pad_start[None, None, :], 0), -1)
    dest2 = dest.reshape(T // ROUTE_TM, ROUTE_TM * TOP_K)
    sh, xs = _shared_ffn_dispatch(fill_start, fill_n, tail, dest2, h2,
                                  sh_w_gate[0].astype(BF16), sh_w_up[0].astype(BF16),
                                  sh_w_down[0].astype(BF16), P)
    ys = _experts(block_e, n_used, xs, wg_b.reshape(moe_w_gate.shape[1:]), wu_b.reshape(moe_w_up.shape[1:]),
                  wd_b.reshape(moe_w_down.shape[1:]))
    out = _combine(dest2, ys, top_w, sh, h2, row(ln3_g[0]), row(ln3_b[0]))
    return out.reshape(B, S, D)
```

```python
import functools

import numpy as np
import jax
import jax.numpy as jnp
from jax import lax
from jax.experimental import pallas as pl
from jax.experimental.pallas import tpu as pltpu

F32 = jnp.float32
BF16 = jnp.bfloat16
I32 = jnp.int32

D_MODEL = 2048
MLSTM_HEADS = 4
MLSTM_DV = 256
MLSTM_DQK = 128
MLSTM_QK = MLSTM_HEADS * MLSTM_DQK
MLSTM_WIDTH = MLSTM_HEADS * MLSTM_DV
CONV_WIDTH = 4
NSA_HEAD_DIM = 128
NSA_HEADS = 8
NSA_KV_HEADS = 2
NSA_GROUP = 4
NSA_WIDTH = NSA_HEADS * NSA_HEAD_DIM
CMP_BLOCK = 32
CMP_STRIDE = 16
CMP_HIDDEN = 256
SLC_BLOCK = 64
SLC_TOPN = 16
WINDOW = 512
Q_BLOCK = 128
XA_HEADS = 4
XA_HEAD_DIM = 512
N_EXPERTS = 64
TOP_K = 6
D_EXPERT = 1408
D_SHARED = 2816
ROUTED_SCALE = 2.446
ROPE_THETA = 10000.0
LN_EPS = 1e-5
DEPTH = 1
DN_ALPHA = (2.0 * DEPTH) ** 0.25

U_MLSTM = 0
U_NSA = 3072
U_COLS = 5632
GATE_COLS = 128

LANE = 128
SUBLANE = 8
NEG = -1e30
MLSTM_L = 256
VMEM_LIMIT = 56 * 1024 * 1024
EXPERT_BM = 256


def _cparams(sem):
    return pltpu.CompilerParams(dimension_semantics=sem, vmem_limit_bytes=VMEM_LIMIT)


def _dot(a, b):
    return jnp.dot(a, b, preferred_element_type=F32)


def _dot_nt(a, b):
    return lax.dot_general(a, b, (((1,), (1,)), ((), ())), preferred_element_type=F32)


def _layer_norm(z, g, b):
    mu = jnp.mean(z, -1, keepdims=True)
    zc = z - mu
    var = jnp.mean(zc * zc, -1, keepdims=True)
    return zc * lax.rsqrt(var + LN_EPS) * g + b


def _split3(x):
    hi = x.astype(BF16)
    r = x - hi.astype(F32)
    mid = r.astype(BF16)
    lo = (r - mid.astype(F32)).astype(BF16)
    return hi, mid, lo


def _split2(x):
    hi = x.astype(BF16)
    lo = (x - hi.astype(F32)).astype(BF16)
    return hi, lo


def _ln_inproj_kernel(x_ref, g_ref, b_ref, w_ref, wg_ref, h_ref, u_ref, ug_ref, hb_ref):
    @pl.when(pl.program_id(1) == 0)
    def _():
        hn = _layer_norm(x_ref[...], g_ref[...], b_ref[...])
        h_ref[...] = hn
        hb_ref[...] = hn.astype(BF16)
        ug_ref[...] = _dot(hb_ref[...], wg_ref[...])

    u_ref[...] = _dot(hb_ref[...], w_ref[...]).astype(u_ref.dtype)


def _ln_inproj(x2, g, b, w, w_gate):
    T = x2.shape[0]
    tm, tn = 1024, 512
    return pl.pallas_call(
        _ln_inproj_kernel,
        grid=(T // tm, U_COLS // tn),
        in_specs=[pl.BlockSpec((tm, D_MODEL), lambda i, j: (i, 0)),
                  pl.BlockSpec((1, D_MODEL), lambda i, j: (0, 0)),
                  pl.BlockSpec((1, D_MODEL), lambda i, j: (0, 0)),
                  pl.BlockSpec((D_MODEL, tn), lambda i, j: (0, j)),
                  pl.BlockSpec((D_MODEL, GATE_COLS), lambda i, j: (0, 0))],
        out_specs=[pl.BlockSpec((tm, D_MODEL), lambda i, j: (i, 0)),
                   pl.BlockSpec((tm, tn), lambda i, j: (i, j)),
                   pl.BlockSpec((tm, GATE_COLS), lambda i, j: (i, 0))],
        out_shape=[jax.ShapeDtypeStruct((T, D_MODEL), F32),
                   jax.ShapeDtypeStruct((T, U_COLS), BF16),
                   jax.ShapeDtypeStruct((T, GATE_COLS), F32)],
        scratch_shapes=[pltpu.VMEM((tm, D_MODEL), BF16)],
        compiler_params=_cparams(("parallel", "arbitrary")),
        name="ln_inproj",
    )(x2, g, b, w, w_gate)


def _log_sigmoid(x):
    return jnp.minimum(x, 0.0) - jnp.log1p(jnp.exp(-jnp.abs(x)))


def _mlstm_kernel(qk_ref, v_ref, o_ref, gt_ref, cw_ref, cb_ref, gb_ref, ng_ref, out_ref,
                  prev_ref, c_ref, n_ref, m_ref):
    L = MLSTM_L

    @pl.when(pl.program_id(1) == 0)
    def _():
        prev_ref[...] = jnp.zeros_like(prev_ref)
        c_ref[...] = jnp.zeros_like(c_ref)
        n_ref[...] = jnp.zeros_like(n_ref)
        m_ref[...] = jnp.zeros_like(m_ref)

    x = qk_ref[...].astype(F32)
    prev = prev_ref[...]
    row = lax.broadcasted_iota(I32, (L, 1), 0)
    cw = cw_ref[...]
    y = cb_ref[...] + cw[CONV_WIDTH - 1:CONV_WIDTH, :] * x
    for j in range(1, CONV_WIDTH):
        shifted = jnp.where(row < j, pltpu.roll(prev, j, 0), pltpu.roll(x, j, 0))
        y = y + cw[CONV_WIDTH - 1 - j:CONV_WIDTH - j, :] * shifted
    prev_ref[...] = x
    qk = y * jax.nn.sigmoid(y)

    gpre = gt_ref[...] + gb_ref[...]
    gpre_t = gpre.T
    r_i = lax.broadcasted_iota(I32, (L, L), 0)
    c_i = lax.broadcasted_iota(I32, (L, L), 1)
    causal = r_i >= c_i
    tril = jnp.where(causal, 1.0, 0.0).astype(BF16)
    triu = jnp.where(r_i <= c_i, 1.0, 0.0).astype(BF16)
    lf = _log_sigmoid(gpre)
    lf_t = _log_sigmoid(gpre_t)
    b_cols = sum(_dot(tril, part) for part in _split3(lf))
    b_rows = sum(_dot(part, triu) for part in _split3(lf_t))

    for h in range(MLSTM_HEADS):
        i_col = gpre[:, h:h + 1]
        i_row = gpre_t[h:h + 1, :]
        b_col = b_cols[:, MLSTM_HEADS + h:MLSTM_HEADS + h + 1]
        b_row = b_rows[MLSTM_HEADS + h:MLSTM_HEADS + h + 1, :]
        m_prev = m_ref[h:h + 1, 0:1]
        n_prev = n_ref[h:h + 1, :]
        c_prev = c_ref[h]

        q = qk[:, h * MLSTM_DQK:(h + 1) * MLSTM_DQK]
        k = qk[:, MLSTM_QK + h * MLSTM_DQK:MLSTM_QK + (h + 1) * MLSTM_DQK] * (MLSTM_DQK ** -0.5)
        v = v_ref[:, h * MLSTM_DV:(h + 1) * MLSTM_DV].astype(BF16)
        qb = q.astype(BF16)

        dmat = jnp.where(causal, b_col - b_row + i_row, -jnp.inf)
        inter = b_col + m_prev
        m_t = jnp.maximum(inter, jnp.max(dmat, -1, keepdims=True))
        s = _dot_nt(qb, k.astype(BF16)) * jnp.exp(dmat - m_t)
        a_inter = jnp.exp(inter - m_t)
        num = _dot(s.astype(BF16), v) + a_inter * _dot(qb, c_prev.astype(BF16))
        den = jnp.sum(s, -1, keepdims=True) + a_inter * jnp.sum(q * n_prev, -1, keepdims=True)
        hc = num / jnp.maximum(jnp.abs(den), jnp.exp(-m_t))

        mu = jnp.mean(hc, -1, keepdims=True)
        hcc = hc - mu
        var = jnp.mean(hcc * hcc, -1, keepdims=True)
        hn = hcc * lax.rsqrt(var + LN_EPS) * ng_ref[:, h * MLSTM_DV:(h + 1) * MLSTM_DV]
        og = jax.nn.sigmoid(o_ref[:, h * MLSTM_DV:(h + 1) * MLSTM_DV].astype(F32))
        out_ref[:, h * MLSTM_DV:(h + 1) * MLSTM_DV] = (og * hn).astype(out_ref.dtype)

        b_last = b_col[L - 1:L, :]
        g_col = b_last - b_col + i_col
        m_new = jnp.maximum(b_last + m_prev, jnp.max(g_col, 0, keepdims=True))
        kw = k * jnp.exp(g_col - m_new)
        decay = jnp.exp(b_last + m_prev - m_new)
        c_ref[h] = decay * c_prev + _dot(kw.T.astype(BF16), v)
        n_ref[h:h + 1, :] = decay * n_prev + jnp.sum(kw, 0, keepdims=True)
        m_ref[h:h + 1, :] = jnp.broadcast_to(m_new, (1, LANE))


def _mlstm(u, ug, conv_w, conv_b, gate_b, norm_g, B, S):
    T = B * S
    L = MLSTM_L
    nc = S // L
    row = lambda b, c: b * nc + c
    return pl.pallas_call(
        _mlstm_kernel,
        grid=(B, nc),
        in_specs=[pl.BlockSpec((L, 2 * MLSTM_QK), lambda b, c: (row(b, c), 0)),
                  pl.BlockSpec((L, MLSTM_WIDTH), lambda b, c: (row(b, c), 1)),
                  pl.BlockSpec((L, MLSTM_WIDTH), lambda b, c: (row(b, c), 2)),
                  pl.BlockSpec((L, GATE_COLS), lambda b, c: (row(b, c), 0)),
                  pl.BlockSpec((CONV_WIDTH, 2 * MLSTM_QK), lambda b, c: (0, 0)),
                  pl.BlockSpec((1, 2 * MLSTM_QK), lambda b, c: (0, 0)),
                  pl.BlockSpec((1, LANE), lambda b, c: (0, 0)),
                  pl.BlockSpec((1, MLSTM_WIDTH), lambda b, c: (0, 0))],
        out_specs=pl.BlockSpec((L, MLSTM_WIDTH), lambda b, c: (row(b, c), 0)),
        out_shape=jax.ShapeDtypeStruct((T, MLSTM_WIDTH), BF16),
        scratch_shapes=[pltpu.VMEM((L, 2 * MLSTM_QK), F32),
                        pltpu.VMEM((MLSTM_HEADS, MLSTM_DQK, MLSTM_DV), F32),
                        pltpu.VMEM((8, MLSTM_DQK), F32),
                        pltpu.VMEM((8, LANE), F32)],
        compiler_params=_cparams(("parallel", "arbitrary")),
        name="mlstm",
    )(u, u, u, ug, conv_w, conv_b, gate_b, norm_g)


LOG2E = 1.4426950408889634
SLC_CHUNK = 1024
SLC_CHUNK_BLOCKS = SLC_CHUNK // SLC_BLOCK
VT_EXTRA = 16
N_FORCED = 3
CMP_ROW_STEP = 256


def _nsa_prep_kernel(q_ref, kv0_ref, kv1_ref, kv2_ref, pos_ref, invf_ref, sgn_ref,
                     qt_ref, kc_ref, vc_ref, ks_ref, vst_ref, kw_ref, vwt_ref):
    ang = pos_ref[...].astype(F32) * invf_ref[...]
    cos = jnp.cos(ang)
    sin = jnp.sin(ang) * sgn_ref[...]

    def rope(x):
        return x * cos + pltpu.roll(x, NSA_HEAD_DIM // 2, 1) * sin

    scale = NSA_HEAD_DIM ** -0.5 * LOG2E
    ts = q_ref.shape[0]
    tok = pl.program_id(0) * ts + lax.broadcasted_iota(I32, (ts, NSA_HEAD_DIM), 0)
    blk_lane = (tok // SLC_BLOCK) % SLC_CHUNK_BLOCKS
    blk_onehot = jnp.where(lax.broadcasted_iota(I32, (ts, NSA_HEAD_DIM), 1) == blk_lane, 1.0, 0.0).astype(BF16)
    ones_rows = jnp.where(lax.broadcasted_iota(I32, (VT_EXTRA, ts), 0) == 0, 1.0, 0.0).astype(BF16)
    head = lambda ref, c: ref[:, c:c + NSA_HEAD_DIM].astype(F32)
    for g in range(NSA_KV_HEADS):
        for h in range(NSA_GROUP):
            qt_ref[g, h] = (rope(head(q_ref, (g * NSA_GROUP + h) * NSA_HEAD_DIM)) * scale).T.astype(BF16)
        c0 = g * NSA_HEAD_DIM
        c1 = NSA_KV_HEADS * NSA_HEAD_DIM + g * NSA_HEAD_DIM
        kc_ref[g] = rope(head(kv0_ref, c0)).astype(BF16)
        vc_ref[g] = kv0_ref[:, c1:c1 + NSA_HEAD_DIM].astype(BF16)
        ks_ref[g, :, 0:NSA_HEAD_DIM] = rope(head(kv1_ref, c0)).astype(BF16)
        ks_ref[g, :, NSA_HEAD_DIM:2 * NSA_HEAD_DIM] = blk_onehot
        vst_ref[g, 0:NSA_HEAD_DIM, :] = head(kv1_ref, c1).T.astype(BF16)
        vst_ref[g, NSA_HEAD_DIM:NSA_HEAD_DIM + VT_EXTRA, :] = ones_rows
        kw_ref[g] = rope(head(kv2_ref, c0)).astype(BF16)
        vwt_ref[g] = head(kv2_ref, c1).T.astype(BF16)


def _nsa_prep(u, pos_col, invf2, sgn, T):
    ts = 512
    kvw = 2 * NSA_KV_HEADS * NSA_HEAD_DIM
    kv_spec = lambda n: pl.BlockSpec((ts, kvw), lambda i: (i, (U_NSA + NSA_WIDTH) // kvw + n))
    row_out = pl.BlockSpec((NSA_KV_HEADS, ts, NSA_HEAD_DIM), lambda i: (0, i, 0))
    row_shape = jax.ShapeDtypeStruct((NSA_KV_HEADS, T, NSA_HEAD_DIM), BF16)
    col_out = pl.BlockSpec((NSA_KV_HEADS, NSA_HEAD_DIM, ts), lambda i: (0, 0, i))
    col_shape = jax.ShapeDtypeStruct((NSA_KV_HEADS, NSA_HEAD_DIM, T), BF16)
    return pl.pallas_call(
        _nsa_prep_kernel,
        grid=(T // ts,),
        in_specs=[pl.BlockSpec((ts, NSA_WIDTH), lambda i: (i, U_NSA // NSA_WIDTH)),
                  kv_spec(0), kv_spec(1), kv_spec(2),
                  pl.BlockSpec((ts, 1), lambda i: (i, 0)),
                  pl.BlockSpec((1, LANE), lambda i: (0, 0)),
                  pl.BlockSpec((1, LANE), lambda i: (0, 0))],
        out_specs=[pl.BlockSpec((NSA_KV_HEADS, NSA_GROUP, NSA_HEAD_DIM, ts), lambda i: (0, 0, 0, i)),
                   row_out, row_out,
                   pl.BlockSpec((NSA_KV_HEADS, ts, 2 * NSA_HEAD_DIM), lambda i: (0, i, 0)),
                   pl.BlockSpec((NSA_KV_HEADS, NSA_HEAD_DIM + VT_EXTRA, ts), lambda i: (0, 0, i)),
                   row_out, col_out],
        out_shape=[jax.ShapeDtypeStruct((NSA_KV_HEADS, NSA_GROUP, NSA_HEAD_DIM, T), BF16),
                   row_shape, row_shape,
                   jax.ShapeDtypeStruct((NSA_KV_HEADS, T, 2 * NSA_HEAD_DIM), BF16),
                   jax.ShapeDtypeStruct((NSA_KV_HEADS, NSA_HEAD_DIM + VT_EXTRA, T), BF16),
                   row_shape, col_shape],
        compiler_params=_cparams(("parallel",)),
        name="nsa_prep",
    )(u, u, u, u, pos_col, invf2, sgn)


def _gelu_tanh(x):
    return 0.5 * x * (1.0 + jnp.tanh(0.7978845608028654 * (x + 0.044715 * x * x * x)))


def _compress_kernel(r_ref, w1_ref, pos_ref, w2_ref, out_ref, *, transpose_out):
    r = r_ref[0]
    nr = r.shape[0]
    a = _dot(r, w1_ref[0])
    b = _dot(r, w1_ref[1])
    c0 = _dot(pos_ref[0], w1_ref[0]) + _dot(pos_ref[1], w1_ref[1])
    pre = a + pltpu.roll(b, nr - 1, 0) + c0[0:1, :]
    out = _dot(_gelu_tanh(pre).astype(BF16), w2_ref[...])
    row = lax.broadcasted_iota(I32, (nr, 1), 0)
    out = jnp.where(row < nr - 1, out, 0.0)
    out_ref[0] = (out.T if transpose_out else out).astype(out_ref.dtype)


def _compress(kv, w1, pos8, w2, B, S, transpose_out):
    nr = S // CMP_STRIDE
    half = CMP_STRIDE * NSA_HEAD_DIM
    r = kv.reshape(NSA_KV_HEADS, B * nr, half)
    if transpose_out:
        out_spec = pl.BlockSpec((1, NSA_HEAD_DIM, nr), lambda g, b: (g, 0, b))
        out_shape = jax.ShapeDtypeStruct((NSA_KV_HEADS, NSA_HEAD_DIM, B * nr), BF16)
    else:
        out_spec = pl.BlockSpec((1, nr, NSA_HEAD_DIM), lambda g, b: (g, b, 0))
        out_shape = jax.ShapeDtypeStruct((NSA_KV_HEADS, B * nr, NSA_HEAD_DIM), BF16)
    return pl.pallas_call(
        functools.partial(_compress_kernel, transpose_out=transpose_out),
        grid=(NSA_KV_HEADS, B),
        in_specs=[pl.BlockSpec((1, nr, half), lambda g, b: (g, b, 0)),
                  pl.BlockSpec((2, half, CMP_HIDDEN), lambda g, b: (0, 0, 0)),
                  pl.BlockSpec((2, 8, half), lambda g, b: (0, 0, 0)),
                  pl.BlockSpec((CMP_HIDDEN, NSA_HEAD_DIM), lambda g, b: (0, 0))],
        out_specs=out_spec,
        out_shape=out_shape,
        compiler_params=_cparams(("parallel", "parallel")),
        name="nsa_compress",
    )(r, w1, pos8, w2)


NSA_COLS = NSA_GROUP * Q_BLOCK


def _load_qt(q_ref):
    return jnp.concatenate([q_ref[0, h] for h in range(NSA_GROUP)], 1)


def _store_heads(o_t, out_ref):
    for h in range(NSA_GROUP):
        out_ref[:, h * NSA_HEAD_DIM:(h + 1) * NSA_HEAD_DIM] = (
            o_t[:, h * Q_BLOCK:(h + 1) * Q_BLOCK].T.astype(out_ref.dtype))


def _tile_heads(x):
    return jnp.concatenate([x] * NSA_GROUP, 1)


def _nsa_cmp_kernel(q_ref, kc_ref, vct_ref, ovt_ref, wd_ref, oc_ref, nb_ref, wdb_ref, *, n_slc, topn):
    wdb_ref[...] = wd_ref[...].astype(BF16)
    i = pl.program_id(2)
    nc = kc_ref.shape[1]
    qt = _load_qt(q_ref)
    t = i * Q_BLOCK + lax.broadcasted_iota(I32, (1, Q_BLOCK), 1)
    cur = t // SLC_BLOCK
    step = min(CMP_ROW_STEP, nc)
    ratio = SLC_BLOCK // CMP_STRIDE

    def causal_prefix(rows):
        c_idx = lax.broadcasted_iota(I32, (rows, 1), 0)
        valid = (c_idx * CMP_STRIDE + CMP_BLOCK - 1 <= t) & (c_idx < nc - 1)
        s = _dot(kc_ref[0, 0:rows, :], qt) + _tile_heads(jnp.where(valid, 0.0, NEG))
        m = jnp.max(s, 0, keepdims=True)
        p = jnp.where(m > 0.5 * NEG, jnp.exp2(s - m), 0.0)
        p = p * (1.0 / jnp.maximum(jnp.sum(p, 0, keepdims=True), 1e-30))
        _store_heads(_dot(vct_ref[0, :, 0:rows], p.astype(BF16)), oc_ref)
        psum = p[:, 0:Q_BLOCK]
        for h in range(1, NSA_GROUP):
            psum = psum + p[:, h * Q_BLOCK:(h + 1) * Q_BLOCK]
        nb_rows = rows // ratio
        imp = sum(_dot(ovt_ref[0:nb_rows, 0:rows], part) for part in _split2(psum))

        blk = lax.broadcasted_iota(I32, (nb_rows, Q_BLOCK), 0)
        blk_f = blk.astype(F32)
        causal_blk = blk <= cur
        forced = (blk == 0) | (blk == cur) | (blk == cur - 1)
        score = jnp.where(causal_blk & jnp.logical_not(forced), imp, -jnp.inf)
        sel = jnp.where(causal_blk & forced, 1.0, 0.0)
        for _ in range(topn - N_FORCED):
            mx = jnp.max(score, 0, keepdims=True)
            first = jnp.min(jnp.where(score == mx, blk_f, float(n_slc)), 0, keepdims=True)
            hit = blk_f == first
            sel = jnp.where(hit, 1.0, sel)
            score = jnp.where(hit, -jnp.inf, score)
        nb_ref[0, 0, 0:nb_rows, :] = jnp.where(causal_blk & (sel > 0.5), 0.0, NEG)
        if nb_rows < n_slc:
            nb_ref[0, 0, nb_rows:n_slc, :] = jnp.full((n_slc - nb_rows, Q_BLOCK), NEG, F32)

    need = ((i + 1) * Q_BLOCK - CMP_BLOCK) // CMP_STRIDE + 1
    for v in range(nc // step):
        @pl.when((need + step - 1) // step == v + 1)
        def _():
            causal_prefix((v + 1) * step)


def _nsa_cmp(qt, kcc, vcct, ovt, w_down, B, S):
    T = B * S
    nq = S // Q_BLOCK
    nc = S // CMP_STRIDE
    n_slc = S // SLC_BLOCK
    wd2 = w_down.reshape(-1, w_down.shape[-1])
    slab = wd2.shape[0] // (NSA_KV_HEADS * B * nq)
    assert slab * NSA_KV_HEADS * B * nq == wd2.shape[0] and slab % (2 * SUBLANE) == 0
    step = lambda g, b, i: ((g * B + b) * nq + i, 0)
    kern = functools.partial(_nsa_cmp_kernel, n_slc=n_slc, topn=min(SLC_TOPN, n_slc))
    return pl.pallas_call(
        kern,
        grid=(NSA_KV_HEADS, B, nq),
        in_specs=[pl.BlockSpec((1, NSA_GROUP, NSA_HEAD_DIM, Q_BLOCK), lambda g, b, i: (g, 0, 0, b * nq + i)),
                  pl.BlockSpec((1, nc, NSA_HEAD_DIM), lambda g, b, i: (g, b, 0)),
                  pl.BlockSpec((1, NSA_HEAD_DIM, nc), lambda g, b, i: (g, 0, b)),
                  pl.BlockSpec((n_slc, nc), lambda g, b, i: (0, 0)),
                  pl.BlockSpec((slab, wd2.shape[1]), step)],
        out_specs=[pl.BlockSpec((Q_BLOCK, NSA_GROUP * NSA_HEAD_DIM), lambda g, b, i: (b * nq + i, g)),
                   pl.BlockSpec((1, 1, n_slc, Q_BLOCK), lambda g, b, i: (g, b * nq + i, 0, 0)),
                   pl.BlockSpec((slab, wd2.shape[1]), step)],
        out_shape=[jax.ShapeDtypeStruct((T, NSA_WIDTH), BF16),
                   jax.ShapeDtypeStruct((NSA_KV_HEADS, B * nq, n_slc, Q_BLOCK), F32),
                   jax.ShapeDtypeStruct(wd2.shape, BF16)],
        compiler_params=_cparams(("parallel", "parallel", "arbitrary")),
        name="nsa_cmp_select",
    )(qt, kcc, vcct, ovt, wd2)


SLC_QB = 2
SLC_Q = SLC_QB * Q_BLOCK
SLC_COLS = NSA_GROUP * SLC_Q
SLC_WIN_BLOCKS = WINDOW // Q_BLOCK + SLC_QB


def _store_slc_heads(o_t, out_ref):
    for h in range(NSA_GROUP):
        out_ref[:, h * NSA_HEAD_DIM:(h + 1) * NSA_HEAD_DIM] = (
            o_t[:, h * SLC_Q:(h + 1) * SLC_Q].T.astype(out_ref.dtype))


def _nsa_slc_kernel(q_ref, nb_ref, ks_ref, vst_ref, wg_ref, wu_ref, *refs):
    kw_refs = refs[:SLC_WIN_BLOCKS]
    vw_refs = refs[SLC_WIN_BLOCKS:2 * SLC_WIN_BLOCKS]
    os_ref, ow_ref, wgb_ref, wub_ref, m_ref, acc_ref, s_ref = refs[2 * SLC_WIN_BLOCKS:]
    wgb_ref[...] = wg_ref[...].astype(BF16)
    wub_ref[...] = wu_ref[...].astype(BF16)
    i = pl.program_id(2)
    qt = _load_qt(q_ref)
    t = i * SLC_Q + lax.broadcasted_iota(I32, (1, SLC_Q), 1)
    m_ref[...] = jnp.full_like(m_ref, NEG)
    acc_ref[...] = jnp.zeros_like(acc_ref)
    pad = jnp.zeros((NSA_HEAD_DIM - SLC_CHUNK_BLOCKS, SLC_Q), F32)

    def scores(c, slot):
        start = pl.multiple_of(c * SLC_CHUNK, SLC_CHUNK)
        k = ks_ref[0, pl.ds(start, SLC_CHUNK), :]
        blocks = pl.ds(pl.multiple_of(c * SLC_CHUNK_BLOCKS, SLC_CHUNK_BLOCKS), SLC_CHUNK_BLOCKS)
        nb = jnp.concatenate([nb_ref[0, qb, blocks, :] for qb in range(SLC_QB)], 1)
        mask_rows = jnp.concatenate([nb, pad], 0).astype(BF16)
        rhs = jnp.concatenate([qt, _tile_heads(mask_rows)], 0)
        s_ref[slot] = _dot(k, rhs)

    def absorb(c, slot, diagonal):
        start = pl.multiple_of(c * SLC_CHUNK, SLC_CHUNK)
        vt = vst_ref[0, :, pl.ds(start, SLC_CHUNK)]
        s = s_ref[slot]
        if diagonal:
            key = start + lax.broadcasted_iota(I32, (SLC_CHUNK, 1), 0)
            s = s + _tile_heads(jnp.where(key <= t, 0.0, NEG))
        m_old = m_ref[...]
        m_new = jnp.maximum(m_old, jnp.max(s, 0, keepdims=True))
        p = jnp.exp2(s - m_new).astype(BF16)
        acc_ref[...] = jnp.exp2(m_old - m_new) * acc_ref[...] + _dot(vt, p)
        m_ref[...] = m_new

    n_full = (i * SLC_Q) // SLC_CHUNK
    scores(0, 0)

    kw = jnp.concatenate([r[0] for r in kw_refs], 0)
    ones_rows = jnp.where(lax.broadcasted_iota(I32, (VT_EXTRA, kw.shape[0]), 0) == 0, 1.0, 0.0).astype(BF16)
    vwt = jnp.concatenate([jnp.concatenate([r[0] for r in vw_refs], 1), ones_rows], 0)
    wpos = (i * SLC_QB - WINDOW // Q_BLOCK) * Q_BLOCK + lax.broadcasted_iota(I32, (kw.shape[0], 1), 0)
    in_window = (wpos <= t) & (wpos > t - WINDOW) & (wpos >= 0)
    s_w = _dot(kw, qt) + _tile_heads(jnp.where(in_window, 0.0, NEG))
    p_w = jnp.exp2(s_w - jnp.max(s_w, 0, keepdims=True)).astype(BF16)
    o_w = _dot(vwt, p_w)
    _store_slc_heads(o_w[0:NSA_HEAD_DIM, :] * (1.0 / o_w[NSA_HEAD_DIM:NSA_HEAD_DIM + 1, :]), ow_ref)

    def body(j, carry):
        scores(2 * j + 1, 1)
        absorb(2 * j, 0, False)
        scores(2 * j + 2, 0)
        absorb(2 * j + 1, 1, False)
        return carry

    lax.fori_loop(0, n_full // 2, body, 0)

    @pl.when(n_full % 2 == 1)
    def _():
        scores(n_full, 1)
        absorb(n_full - 1, 0, False)
        absorb(n_full, 1, True)

    @pl.when(n_full % 2 == 0)
    def _():
        absorb(n_full, 0, True)

    ok = m_ref[...] > 0.5 * NEG
    l = acc_ref[NSA_HEAD_DIM:NSA_HEAD_DIM + 1, :]
    inv = jnp.where(ok, 1.0 / jnp.where(ok, l, 1.0), 0.0)
    _store_slc_heads(acc_ref[0:NSA_HEAD_DIM, :] * inv, os_ref)


def _nsa_slc(qt, nb, ks, vst, kw, vwt, w_gate, w_up, B, S):
    T = B * S
    nq = S // SLC_Q
    n_slc = S // SLC_BLOCK
    wg2 = w_gate.reshape(-1, w_gate.shape[-1])
    wu2 = w_up.reshape(-1, w_up.shape[-1])
    slab = wg2.shape[0] // (NSA_KV_HEADS * B * nq)
    assert slab * NSA_KV_HEADS * B * nq == wg2.shape[0] and slab % (2 * SUBLANE) == 0 and wu2.shape == wg2.shape
    step = lambda g, b, i: ((g * B + b) * nq + i, 0)
    resident = dict(pipeline_mode=pl.Buffered(1))
    nblk = S // Q_BLOCK

    def win_block(c):
        return lambda g, b, i: jnp.maximum(i * SLC_QB - WINDOW // Q_BLOCK + c, 0) + b * nblk

    kw_specs = [pl.BlockSpec((1, Q_BLOCK, NSA_HEAD_DIM), lambda g, b, i, f=win_block(c): (g, f(g, b, i), 0))
                for c in range(SLC_WIN_BLOCKS)]
    vw_specs = [pl.BlockSpec((1, NSA_HEAD_DIM, Q_BLOCK), lambda g, b, i, f=win_block(c): (g, 0, f(g, b, i)))
                for c in range(SLC_WIN_BLOCKS)]
    tok_out = pl.BlockSpec((SLC_Q, NSA_GROUP * NSA_HEAD_DIM), lambda g, b, i: (b * nq + i, g))
    return pl.pallas_call(
        _nsa_slc_kernel,
        grid=(NSA_KV_HEADS, B, nq),
        in_specs=[pl.BlockSpec((1, NSA_GROUP, NSA_HEAD_DIM, SLC_Q), lambda g, b, i: (g, 0, 0, b * nq + i)),
                  pl.BlockSpec((1, SLC_QB, n_slc, Q_BLOCK), lambda g, b, i: (g, b * nq + i, 0, 0)),
                  pl.BlockSpec((1, S, 2 * NSA_HEAD_DIM), lambda g, b, i: (g, b, 0), **resident),
                  pl.BlockSpec((1, NSA_HEAD_DIM + VT_EXTRA, S), lambda g, b, i: (g, 0, b), **resident),
                  pl.BlockSpec((slab, wg2.shape[1]), step), pl.BlockSpec((slab, wg2.shape[1]), step)]
        + kw_specs + vw_specs,
        out_specs=[tok_out, tok_out,
                   pl.BlockSpec((slab, wg2.shape[1]), step), pl.BlockSpec((slab, wg2.shape[1]), step)],
        out_shape=[jax.ShapeDtypeStruct((T, NSA_WIDTH), BF16), jax.ShapeDtypeStruct((T, NSA_WIDTH), BF16),
                   jax.ShapeDtypeStruct(wg2.shape, BF16), jax.ShapeDtypeStruct(wg2.shape, BF16)],
        scratch_shapes=[pltpu.VMEM((1, SLC_COLS), F32),
                        pltpu.VMEM((NSA_HEAD_DIM + VT_EXTRA, SLC_COLS), F32),
                        pltpu.VMEM((2, SLC_CHUNK, SLC_COLS), F32)],
        compiler_params=_cparams(("parallel", "parallel", "arbitrary")),
        name="nsa_selected",
    )(qt, nb, ks, vst, wg2, wu2, *([kw] * SLC_WIN_BLOCKS), *([vwt] * SLC_WIN_BLOCKS))


def _outproj_kernel(hm_ref, oc_ref, os_ref, ow_ref, gt_ref, gb_ref, ge_ref, h_ref, w_ref, g_ref, b_ref,
                    out_ref):
    gates = jax.nn.sigmoid(gt_ref[...] + gb_ref[...]).astype(BF16)
    gx = _dot(gates, ge_ref[...])
    hn = (gx[:, 0:NSA_WIDTH] * oc_ref[...].astype(F32)
          + gx[:, NSA_WIDTH:2 * NSA_WIDTH] * os_ref[...].astype(F32)
          + gx[:, 2 * NSA_WIDTH:3 * NSA_WIDTH] * ow_ref[...].astype(F32))
    mix = (_dot(hm_ref[...], w_ref[0:MLSTM_WIDTH, :])
           + _dot(hn.astype(BF16), w_ref[MLSTM_WIDTH:MLSTM_WIDTH + NSA_WIDTH, :]))
    out_ref[...] = _layer_norm(DN_ALPHA * h_ref[...] + mix, g_ref[...], b_ref[...])


def _outproj(hm, oc, os_, ow, ug, gate_b, gate_expand, h, w_out, g, b):
    T = h.shape[0]
    tm = 256
    row = lambda i: (i, 0)
    fixed = lambda i: (0, 0)
    return pl.pallas_call(
        _outproj_kernel,
        grid=(T // tm,),
        in_specs=[pl.BlockSpec((tm, MLSTM_WIDTH), row), pl.BlockSpec((tm, NSA_WIDTH), row),
                  pl.BlockSpec((tm, NSA_WIDTH), row), pl.BlockSpec((tm, NSA_WIDTH), row),
                  pl.BlockSpec((tm, GATE_COLS), row),
                  pl.BlockSpec((1, LANE), fixed),
                  pl.BlockSpec((LANE, 3 * NSA_WIDTH), fixed),
                  pl.BlockSpec((tm, D_MODEL), row),
                  pl.BlockSpec((D_MODEL, D_MODEL), fixed),
                  pl.BlockSpec((1, D_MODEL), fixed), pl.BlockSpec((1, D_MODEL), fixed)],
        out_specs=pl.BlockSpec((tm, D_MODEL), row),
        out_shape=jax.ShapeDtypeStruct((T, D_MODEL), F32),
        compiler_params=_cparams(("parallel",)),
        name="mixer_outproj_ln",
    )(hm, oc, os_, ow, ug, gate_b, gate_expand, h, w_out, g, b)


def _matmul_kernel(x_ref, w_ref, o_ref):
    o_ref[...] = _dot(x_ref[...].astype(BF16), w_ref[...]).astype(o_ref.dtype)


def _mem_kv(mem2, wkv):
    M = mem2.shape[0]
    N = wkv.shape[1]
    tn = 512
    return pl.pallas_call(
        _matmul_kernel,
        grid=(N // tn,),
        in_specs=[pl.BlockSpec((M, D_MODEL), lambda j: (0, 0)),
                  pl.BlockSpec((D_MODEL, tn), lambda j: (0, j))],
        out_specs=pl.BlockSpec((M, tn), lambda j: (0, j)),
        out_shape=jax.ShapeDtypeStruct((M, N), BF16),
        compiler_params=_cparams(("parallel",)),
        name="mem_kv_proj",
    )(mem2, wkv)


def _xattn_kernel(h_ref, wq_ref, kv_ref, o_ref):
    hb = h_ref[...].astype(BF16)
    for hd in range(XA_HEADS):
        c0 = hd * XA_HEAD_DIM
        q = (_dot(hb, wq_ref[:, c0:c0 + XA_HEAD_DIM]) * (XA_HEAD_DIM ** -0.5)).astype(BF16)
        s = _dot_nt(q, kv_ref[:, c0:c0 + XA_HEAD_DIM])
        m = jnp.max(s, -1, keepdims=True)
        p = jnp.exp(s - m)
        o = _dot(p.astype(BF16), kv_ref[:, D_MODEL + c0:D_MODEL + c0 + XA_HEAD_DIM])
        o_ref[:, c0:c0 + XA_HEAD_DIM] = (o / jnp.sum(p, -1, keepdims=True)).astype(o_ref.dtype)


def _xattn(h1, wq, kv, B, S):
    T = B * S
    tm = 512
    n_mem = kv.shape[0] // B
    per_b = S // tm
    return pl.pallas_call(
        _xattn_kernel,
        grid=(T // tm,),
        in_specs=[pl.BlockSpec((tm, D_MODEL), lambda i: (i, 0)),
                  pl.BlockSpec((D_MODEL, D_MODEL), lambda i: (0, 0)),
                  pl.BlockSpec((n_mem, 2 * D_MODEL), lambda i: (i // per_b, 0))],
        out_specs=pl.BlockSpec((tm, D_MODEL), lambda i: (i, 0)),
        out_shape=jax.ShapeDtypeStruct((T, D_MODEL), BF16),
        compiler_params=_cparams(("parallel",)),
        name="mem_xattn",
    )(h1, wq, kv)


def _xa_out_router_kernel(o_ref, wo_ref, h_ref, g_ref, b_ref, rw_ref, rb_ref,
                          h2_ref, idx_ref, wgt_ref):
    xa = _dot(o_ref[...], wo_ref[...])
    h2 = _layer_norm(DN_ALPHA * h_ref[...] + xa, g_ref[...], b_ref[...])
    h2_ref[...] = h2
    hi, lo = _split2(h2)
    logits = _dot(hi, rw_ref[0]) + _dot(hi, rw_ref[1]) + _dot(lo, rw_ref[0])
    scores = jax.nn.sigmoid(logits)
    lane = lax.broadcasted_iota(I32, scores.shape, 1)
    lane_f = lane.astype(F32)
    biased = jnp.where(lane < N_EXPERTS, scores + rb_ref[...], -jnp.inf)
    idx_mat = jnp.zeros(scores.shape, F32)
    w_mat = jnp.zeros(scores.shape, F32)
    for kk in range(TOP_K):
        mx = jnp.max(biased, -1, keepdims=True)
        first = jnp.min(jnp.where(biased == mx, lane_f, float(LANE)), -1, keepdims=True)
        hit = lane_f == first
        top_s = jnp.sum(jnp.where(hit, scores, 0.0), -1, keepdims=True)
        idx_mat = jnp.where(lane == kk, first, idx_mat)
        w_mat = jnp.where(lane == kk, top_s, w_mat)
        biased = jnp.where(hit, -jnp.inf, biased)
    idx_ref[...] = idx_mat.astype(I32)
    wgt_ref[...] = w_mat / jnp.sum(w_mat, -1, keepdims=True) * ROUTED_SCALE


def _xa_out_router(o, wo, h1, g, b, rw2, rb):
    T = h1.shape[0]
    tm = 256
    row = lambda i: (i, 0)
    fixed = lambda i: (0, 0)
    return pl.pallas_call(
        _xa_out_router_kernel,
        grid=(T // tm,),
        in_specs=[pl.BlockSpec((tm, D_MODEL), row),
                  pl.BlockSpec((D_MODEL, D_MODEL), fixed),
                  pl.BlockSpec((tm, D_MODEL), row),
                  pl.BlockSpec((1, D_MODEL), fixed), pl.BlockSpec((1, D_MODEL), fixed),
                  pl.BlockSpec((2, D_MODEL, LANE), lambda i: (0, 0, 0)),
                  pl.BlockSpec((1, LANE), fixed)],
        out_specs=[pl.BlockSpec((tm, D_MODEL), row), pl.BlockSpec((tm, LANE), row),
                   pl.BlockSpec((tm, LANE), row)],
        out_shape=[jax.ShapeDtypeStruct((T, D_MODEL), F32),
                   jax.ShapeDtypeStruct((T, LANE), I32),
                   jax.ShapeDtypeStruct((T, LANE), F32)],
        compiler_params=_cparams(("parallel",)),
        name="xattn_out_ln_router",
    )(o, wo, h1, g, b, rw2, rb)


ROUTE_TM = 512
ZERO_ROWS = 128


def _route_rank_kernel(idx_ref, rank_ref, cnt_ref):
    tm = idx_ref.shape[0]

    @pl.when(pl.program_id(0) == 0)
    def _():
        cnt_ref[...] = jnp.zeros_like(cnt_ref)

    idx = idx_ref[...]
    lane = lax.broadcasted_iota(I32, (tm, LANE), 1)
    hits = [lane == idx[:, kk:kk + 1] for kk in range(TOP_K)]
    onehot = sum(jnp.where(hit, 1.0, 0.0) for hit in hits)
    r_i = lax.broadcasted_iota(I32, (tm, tm), 0)
    c_i = lax.broadcasted_iota(I32, (tm, tm), 1)
    before = jnp.where(r_i > c_i, 1.0, 0.0).astype(BF16)
    rank = _dot(before, onehot.astype(BF16)) + cnt_ref[0:1, :]
    out = jnp.zeros((tm, LANE), F32)
    for kk in range(TOP_K):
        out = jnp.where(lane == kk, jnp.sum(jnp.where(hits[kk], rank, 0.0), -1, keepdims=True), out)
    rank_ref[...] = out.astype(I32)
    cnt_ref[0:1, :] = cnt_ref[0:1, :] + jnp.sum(onehot, 0, keepdims=True)


def _route_rank(top_idx):
    T = top_idx.shape[0]
    tm = ROUTE_TM
    return pl.pallas_call(
        _route_rank_kernel,
        grid=(T // tm,),
        in_specs=[pl.BlockSpec((tm, LANE), lambda i: (i, 0))],
        out_specs=[pl.BlockSpec((tm, LANE), lambda i: (i, 0)),
                   pl.BlockSpec((8, LANE), lambda i: (0, 0))],
        out_shape=[jax.ShapeDtypeStruct((T, LANE), I32), jax.ShapeDtypeStruct((8, LANE), F32)],
        compiler_params=_cparams(("arbitrary",)),
        name="moe_route_rank",
    )(top_idx)


def _load_route(i, dest_hbm, dest_smem, isem):
    cp = pltpu.make_async_copy(dest_hbm.at[i], dest_smem, isem)
    cp.start()
    cp.wait()


_FILL_SIZES = tuple(s for s in (ZERO_ROWS >> n for n in range(ZERO_ROWS.bit_length())) if s >= SUBLANE)


def _experts_kernel(be_ref, nu_ref, x_ref, wg_ref, wu_ref, wd_ref, y_ref):
    used = pl.program_id(0) < nu_ref[0]

    @pl.when(used)
    def _():
        xb = x_ref[...].astype(BF16)
        a = _dot(xb, wg_ref[0])
        act = (a * jax.nn.sigmoid(a)) * _dot(xb, wu_ref[0])
        y_ref[...] = _dot(act.astype(BF16), wd_ref[0])

    @pl.when(jnp.logical_not(used))
    def _():
        y_ref[...] = jnp.zeros_like(y_ref)


def _experts(block_e, n_used, xs, wg, wu, wd):
    bm = EXPERT_BM
    P = xs.shape[0] // bm * bm
    rowmap = lambda j, be, nu: (jnp.minimum(j, nu[0] - 1), 0)
    wmap = lambda j, be, nu: (be[j], 0, 0)
    return pl.pallas_call(
        _experts_kernel,
        grid_spec=pltpu.PrefetchScalarGridSpec(
            num_scalar_prefetch=2,
            grid=(P // bm,),
            in_specs=[pl.BlockSpec((bm, D_MODEL), rowmap),
                      pl.BlockSpec((1, D_MODEL, D_EXPERT), wmap),
                      pl.BlockSpec((1, D_MODEL, D_EXPERT), wmap),
                      pl.BlockSpec((1, D_EXPERT, D_MODEL), wmap)],
            out_specs=pl.BlockSpec((bm, D_MODEL), lambda j, be, nu: (j, 0))),
        out_shape=jax.ShapeDtypeStruct((P, D_MODEL), F32),
        compiler_params=_cparams(("arbitrary",)),
        name="moe_experts",
    )(block_e, n_used, xs, wg, wu, wd)


SHARED_TF = 256
SHARED_STEPS = D_SHARED // SHARED_TF
DISPATCH_SHARE = -(-ROUTE_TM // (SHARED_STEPS * SUBLANE)) * SUBLANE
DUMP_ROWS = (DISPATCH_SHARE * SHARED_STEPS - ROUTE_TM) * TOP_K


def _zero_fill(fs_ref, fn_ref, tail_ref, zero_ref, xs_ref, zsem):
    zero_ref[...] = jnp.zeros_like(zero_ref)

    def fill_copies(e, go):
        n = fn_ref[e]
        first = fs_ref[e]
        lead = jnp.minimum((-first) & (SUBLANE - 1), n)
        for r in range(SUBLANE - 1):
            @pl.when(r < lead)
            def _():
                go(pltpu.make_async_copy(zero_ref.at[pl.ds(0, 1)], xs_ref.at[pl.ds(first + r, 1)], zsem))

        rest = n - lead
        off = first + lead
        for size in _FILL_SIZES:
            take = (rest & size) != 0

            @pl.when(take)
            def _():
                dst = xs_ref.at[pl.ds(pl.multiple_of(off, SUBLANE), size)]
                go(pltpu.make_async_copy(zero_ref.at[pl.ds(0, size)], dst, zsem))
            off = off + jnp.where(take, size, 0)

    def tail_copies(j, go):
        off = pl.multiple_of(tail_ref[0] + j * ZERO_ROWS, ZERO_ROWS)
        go(pltpu.make_async_copy(zero_ref, xs_ref.at[pl.ds(off, ZERO_ROWS)], zsem))

    for go in (lambda cp: cp.start(), lambda cp: cp.wait()):
        lax.fori_loop(0, N_EXPERTS, lambda e, c, go=go: (fill_copies(e, go), c)[1], 0)
        lax.fori_loop(0, tail_ref[1], lambda j, c, go=go: (tail_copies(j, go), c)[1], 0)


def _shared_ffn_kernel(fs_ref, fn_ref, tail_ref, dest_hbm, x_ref, wg_ref, wu_ref, wd_ref,
                       o_ref, xs_ref, xb_ref, dest_smem, zero_ref, isem, sem, zsem, *, dump_base):
    i = pl.program_id(0)
    f = pl.program_id(1)

    @pl.when(f == 0)
    def _():
        _load_route(i, dest_hbm, dest_smem, isem)
        xb_ref[...] = x_ref[...].astype(BF16)
        o_ref[...] = jnp.zeros_like(o_ref)

    @pl.when((f == 0) & (i == 0))
    def _():
        _zero_fill(fs_ref, fn_ref, tail_ref, zero_ref, xs_ref, zsem)

    for rr in range(DISPATCH_SHARE):
        r = f * DISPATCH_SHARE + rr
        live = r < ROUTE_TM
        rc = jnp.minimum(r, ROUTE_TM - 1)
        for kk in range(TOP_K):
            d = jnp.where(live, dest_smem[rc * TOP_K + kk], dump_base + (r - ROUTE_TM) * TOP_K + kk)
            pltpu.make_async_copy(x_ref.at[pl.ds(rc, 1)], xs_ref.at[pl.ds(d, 1)], sem).start()

    xb = xb_ref[...]
    a = _dot(xb, wg_ref[...])
    act = (a * jax.nn.sigmoid(a)) * _dot(xb, wu_ref[...])
    o_ref[...] += _dot(act.astype(BF16), wd_ref[...])

    @pl.when(f == SHARED_STEPS - 1)
    def _():
        for _ in range(TOP_K):
            pltpu.make_async_copy(x_ref, xs_ref.at[pl.ds(0, ROUTE_TM)], sem).wait()
        pltpu.make_async_copy(x_ref.at[pl.ds(0, DUMP_ROWS)], xs_ref.at[pl.ds(0, DUMP_ROWS)], sem).wait()


def _shared_ffn_dispatch(fill_start, fill_n, tail, dest2, h2, wg, wu, wd, P):
    T = h2.shape[0]
    tm, tf = ROUTE_TM, SHARED_TF
    return pl.pallas_call(
        functools.partial(_shared_ffn_kernel, dump_base=P),
        grid_spec=pltpu.PrefetchScalarGridSpec(
            num_scalar_prefetch=3,
            grid=(T // tm, SHARED_STEPS),
            in_specs=[pl.BlockSpec(memory_space=pl.ANY),
                      pl.BlockSpec((tm, D_MODEL), lambda i, f, *_: (i, 0)),
                      pl.BlockSpec((D_MODEL, tf), lambda i, f, *_: (0, f)),
                      pl.BlockSpec((D_MODEL, tf), lambda i, f, *_: (0, f)),
                      pl.BlockSpec((tf, D_MODEL), lambda i, f, *_: (f, 0))],
            out_specs=[pl.BlockSpec((tm, D_MODEL), lambda i, f, *_: (i, 0)),
                       pl.BlockSpec(memory_space=pl.ANY)],
            scratch_shapes=[pltpu.VMEM((tm, D_MODEL), BF16),
                            pltpu.SMEM((tm * TOP_K,), I32),
                            pltpu.VMEM((ZERO_ROWS, D_MODEL), F32),
                            pltpu.SemaphoreType.DMA, pltpu.SemaphoreType.DMA, pltpu.SemaphoreType.DMA]),
        out_shape=[jax.ShapeDtypeStruct((T, D_MODEL), F32),
                   jax.ShapeDtypeStruct((P + DUMP_ROWS, D_MODEL), F32)],
        compiler_params=_cparams(("arbitrary", "arbitrary")),
        name="shared_ffn_dispatch",
    )(fill_start, fill_n, tail, dest2, h2, wg, wu, wd)


COMBINE_SUB = 128
COMBINE_NSUB = ROUTE_TM // COMBINE_SUB


def _combine_kernel(dest_hbm, ys_hbm, w_ref, sh_ref, h_ref, g_ref, b_ref, out_ref,
                    dest_smem, gbuf, isem, sem):
    i = pl.program_id(0)
    _load_route(i, dest_hbm, dest_smem, isem)

    def gather(sub):
        slot = sub % 2

        def issue(r, carry):
            for kk in range(TOP_K):
                d = dest_smem[(sub * COMBINE_SUB + r) * TOP_K + kk]
                pltpu.make_async_copy(ys_hbm.at[pl.ds(d, 1)], gbuf.at[slot, kk, pl.ds(r, 1)],
                                      sem.at[slot]).start()
            return carry

        lax.fori_loop(0, COMBINE_SUB, issue, 0, unroll=8)

    gather(0)
    for sub in range(COMBINE_NSUB):
        slot = sub % 2
        if sub + 1 < COMBINE_NSUB:
            gather(sub + 1)
        for kk in range(TOP_K):
            pltpu.make_async_copy(ys_hbm.at[pl.ds(0, COMBINE_SUB)], gbuf.at[slot, kk], sem.at[slot]).wait()
        rs = slice(sub * COMBINE_SUB, (sub + 1) * COMBINE_SUB)
        w = w_ref[rs, :]
        routed = w[:, 0:1] * gbuf[slot, 0]
        for kk in range(1, TOP_K):
            routed = routed + w[:, kk:kk + 1] * gbuf[slot, kk]
        z = DN_ALPHA * h_ref[rs, :] + (routed + sh_ref[rs, :])
        out_ref[rs, :] = _layer_norm(z, g_ref[...], b_ref[...])


def _combine(dest2, ys, top_w, sh, h2, g, b):
    T = h2.shape[0]
    tm = ROUTE_TM
    row = lambda i: (i, 0)
    fixed = lambda i: (0, 0)
    return pl.pallas_call(
        _combine_kernel,
        grid=(T // tm,),
        in_specs=[pl.BlockSpec(memory_space=pl.ANY), pl.BlockSpec(memory_space=pl.ANY),
                  pl.BlockSpec((tm, LANE), row), pl.BlockSpec((tm, D_MODEL), row),
                  pl.BlockSpec((tm, D_MODEL), row),
                  pl.BlockSpec((1, D_MODEL), fixed), pl.BlockSpec((1, D_MODEL), fixed)],
        out_specs=pl.BlockSpec((tm, D_MODEL), row),
        scratch_shapes=[pltpu.SMEM((tm * TOP_K,), I32),
                        pltpu.VMEM((2, TOP_K, COMBINE_SUB, D_MODEL), F32),
                        pltpu.SemaphoreType.DMA, pltpu.SemaphoreType.DMA((2,))],
        out_shape=jax.ShapeDtypeStruct((T, D_MODEL), F32),
        compiler_params=_cparams(("arbitrary",)),
        name="moe_combine_ln",
    )(dest2, ys, top_w, sh, h2, g, b)


def _route_plan(counts, T):
    bm = EXPERT_BM
    padded = (counts + bm - 1) // bm * bm
    pad_end = jnp.cumsum(padded)
    pad_start = pad_end - padded
    n_blocks = T * TOP_K // bm + N_EXPERTS
    first_row = jnp.arange(n_blocks, dtype=I32) * bm
    block_e = jnp.minimum(jnp.sum((pad_end[None, :] <= first_row[:, None]).astype(I32), 1), N_EXPERTS - 1)
    n_used = (pad_end[-1:] // bm).astype(I32)
    tail = jnp.concatenate([pad_end[-1:], (n_blocks - n_used) * (bm // ZERO_ROWS)]).astype(I32)
    return (pad_start.astype(I32), (pad_start + counts).astype(I32), (padded - counts).astype(I32), tail,
            block_e, n_used, n_blocks * bm)


def _overlap_matrix(S):
    n_cmp_rows = S // CMP_STRIDE
    n_slc = S // SLC_BLOCK
    c_lo = np.arange(n_cmp_rows)[:, None] * CMP_STRIDE
    j_lo = np.arange(n_slc)[None, :] * SLC_BLOCK
    ov = (c_lo <= j_lo + SLC_BLOCK - 1) & (c_lo + CMP_BLOCK - 1 >= j_lo)
    return jnp.asarray(ov.T.astype(np.float32), dtype=BF16)


def _gate_expand_matrix():
    ge = np.zeros((LANE, 3 * NSA_WIDTH), np.float32)
    for hh in range(NSA_HEADS):
        for br in range(3):
            ge[8 + hh * 3 + br, br * NSA_WIDTH + hh * NSA_HEAD_DIM:br * NSA_WIDTH + (hh + 1) * NSA_HEAD_DIM] = 1.0
    return jnp.asarray(ge, dtype=BF16)


def kernel(x, mem, positions, ln0_g, ln0_b, w_in, conv_w, conv_b, igate_b, fgate_b, mlstm_norm_g, cmp_pos, cmp_w1k, cmp_w2k, cmp_w1v, cmp_w2v, nsa_gate_b, w_out, ln1_g, ln1_b, xa_wq, xa_wk, xa_wv, xa_wo, ln2_g, ln2_b, router_w, router_bias, moe_w_gate, moe_w_up, moe_w_down, sh_w_gate, sh_w_up, sh_w_down, ln3_g, ln3_b):
    B, S, D = x.shape
    T = B * S
    assert D == D_MODEL and w_in.shape[0] == DEPTH == 1
    assert S % MLSTM_L == 0 and S % SLC_CHUNK == 0 and T % ROUTE_TM == 0 and (T * TOP_K) % EXPERT_BM == 0
    row = lambda a: a.reshape(1, -1)

    w = w_in[0]
    w_r = jnp.concatenate([w[:, :3072], w[:, 3080:5640]], 1).astype(BF16)
    w_g = jnp.concatenate([w[:, 3072:3080], w[:, 5640:5664],
                           jnp.zeros((D, GATE_COLS - 2 * MLSTM_HEADS - 3 * NSA_HEADS), F32)], 1).astype(BF16)
    gate_b = jnp.concatenate([igate_b[0], fgate_b[0], nsa_gate_b[0],
                              jnp.zeros((LANE - 2 * MLSTM_HEADS - 3 * NSA_HEADS,), F32)]).reshape(1, LANE)
    half = NSA_HEAD_DIM // 2
    inv_freq = ROPE_THETA ** (-jnp.arange(half, dtype=F32) / half)
    invf2 = jnp.concatenate([inv_freq, inv_freq]).reshape(1, LANE)
    sgn = jnp.concatenate([-jnp.ones((half,), F32), jnp.ones((half,), F32)]).reshape(1, LANE)
    pos8 = jnp.zeros((2, 8, CMP_STRIDE * NSA_HEAD_DIM), F32).at[:, 0, :].set(
        cmp_pos[0].reshape(2, CMP_STRIDE * NSA_HEAD_DIM)).astype(BF16)
    w1k = cmp_w1k[0].reshape(2, CMP_STRIDE * NSA_HEAD_DIM, CMP_HIDDEN).astype(BF16)
    w1v = cmp_w1v[0].reshape(2, CMP_STRIDE * NSA_HEAD_DIM, CMP_HIDDEN).astype(BF16)
    rw = jnp.pad(router_w[0], ((0, 0), (0, LANE - N_EXPERTS)))
    rw_hi = rw.astype(BF16)
    rw2 = jnp.stack([rw_hi, (rw - rw_hi.astype(F32)).astype(BF16)])
    rb = jnp.pad(router_bias[0], (0, LANE - N_EXPERTS)).reshape(1, LANE)

    h, u, ug = _ln_inproj(x.reshape(T, D), row(ln0_g), row(ln0_b), w_r, w_g)
    hm = _mlstm(u, ug, conv_w[0], row(conv_b[0]), gate_b, row(mlstm_norm_g[0]), B, S)
    qt, kc, vc, ks, vst, kw, vwt = _nsa_prep(u, positions.reshape(T, 1), invf2, sgn, T)
    kcc = _compress(kc, w1k, pos8, cmp_w2k[0].astype(BF16), B, S, False)
    vcct = _compress(vc, w1v, pos8, cmp_w2v[0].astype(BF16), B, S, True)
    oc, nb, wd_b = _nsa_cmp(qt, kcc, vcct, _overlap_matrix(S), moe_w_down[0], B, S)
    os_, ow, wg_b, wu_b = _nsa_slc(qt, nb, ks, vst, kw, vwt, moe_w_gate[0], moe_w_up[0], B, S)
    h1 = _outproj(hm, oc, os_, ow, ug, gate_b, _gate_expand_matrix(), h, w_out[0].astype(BF16),
                  row(ln1_g[0]), row(ln1_b[0]))

    wkv = jnp.concatenate([xa_wk[0], xa_wv[0]], 1).astype(BF16)
    kv = _mem_kv(mem.reshape(-1, D), wkv)
    xo = _xattn(h1, xa_wq[0].astype(BF16), kv, B, S)
    h2, top_idx, top_w = _xa_out_router(xo, xa_wo[0].astype(BF16), h1, row(ln2_g[0]), row(ln2_b[0]), rw2, rb)

    rank, cnt = _route_rank(top_idx)
    counts = cnt[0, :N_EXPERTS].astype(I32)
    pad_start, fill_start, fill_n, tail, block_e, n_used, P = _route_plan(counts, T)
    slot_e = top_idx[:, :TOP_K, None] == jnp.arange(N_EXPERTS, dtype=I32)[None, None, :]
    dest = rank[:, :TOP_K] + jnp.sum(jnp.where(slot_e, pad_start[None, None, :], 0), -1)
    dest2 = dest.reshape(T // ROUTE_TM, ROUTE_TM * TOP_K)
    sh, xs = _shared_ffn_dispatch(fill_start, fill_n, tail, dest2, h2,
                                  sh_w_gate[0].astype(BF16), sh_w_up[0].astype(BF16),
                                  sh_w_down[0].astype(BF16), P)
    ys = _experts(block_e, n_used, xs, wg_b.reshape(moe_w_gate.shape[1:]), wu_b.reshape(moe_w_up.shape[1:]),
                  wd_b.reshape(moe_w_down.shape[1:]))
    out = _combine(dest2, ys, top_w, sh, h2, row(ln3_g[0]), row(ln3_b[0]))
    return out.reshape(B, S, D)
```

```python
import functools

import numpy as np
import jax
import jax.numpy as jnp
from jax import lax
from jax.experimental import pallas as pl
from jax.experimental.pallas import tpu as pltpu

F32 = jnp.float32
BF16 = jnp.bfloat16
I32 = jnp.int32

D_MODEL = 2048
MLSTM_HEADS = 4
MLSTM_DV = 256
MLSTM_DQK = 128
MLSTM_QK = MLSTM_HEADS * MLSTM_DQK
MLSTM_WIDTH = MLSTM_HEADS * MLSTM_DV
CONV_WIDTH = 4
NSA_HEAD_DIM = 128
NSA_HEADS = 8
NSA_KV_HEADS = 2
NSA_GROUP = 4
NSA_WIDTH = NSA_HEADS * NSA_HEAD_DIM
CMP_BLOCK = 32
CMP_STRIDE = 16
CMP_HIDDEN = 256
SLC_BLOCK = 64
SLC_TOPN = 16
WINDOW = 512
Q_BLOCK = 128
XA_HEADS = 4
XA_HEAD_DIM = 512
N_EXPERTS = 64
TOP_K = 6
D_EXPERT = 1408
D_SHARED = 2816
ROUTED_SCALE = 2.446
ROPE_THETA = 10000.0
LN_EPS = 1e-5
DEPTH = 1
DN_ALPHA = (2.0 * DEPTH) ** 0.25

U_MLSTM = 0
U_NSA = 3072
U_COLS = 5632
GATE_COLS = 128

LANE = 128
SUBLANE = 8
NEG = -1e30
MLSTM_L = 256
VMEM_LIMIT = 56 * 1024 * 1024
EXPERT_BM = 256


def _cparams(sem):
    return pltpu.CompilerParams(dimension_semantics=sem, vmem_limit_bytes=VMEM_LIMIT)


def _dot(a, b):
    return jnp.dot(a, b, preferred_element_type=F32)


def _dot_nt(a, b):
    return lax.dot_general(a, b, (((1,), (1,)), ((), ())), preferred_element_type=F32)


def _layer_norm(z, g, b):
    mu = jnp.mean(z, -1, keepdims=True)
    zc = z - mu
    var = jnp.mean(zc * zc, -1, keepdims=True)
    return zc * lax.rsqrt(var + LN_EPS) * g + b


def _split3(x):
    hi = x.astype(BF16)
    r = x - hi.astype(F32)
    mid = r.astype(BF16)
    lo = (r - mid.astype(F32)).astype(BF16)
    return hi, mid, lo


def _split2(x):
    hi = x.astype(BF16)
    lo = (x - hi.astype(F32)).astype(BF16)
    return hi, lo


def _ln_inproj_kernel(x_ref, g_ref, b_ref, w_ref, wg_ref, h_ref, u_ref, ug_ref, hb_ref):
    @pl.when(pl.program_id(1) == 0)
    def _():
        hn = _layer_norm(x_ref[...], g_ref[...], b_ref[...])
        h_ref[...] = hn
        hb_ref[...] = hn.astype(BF16)
        ug_ref[...] = _dot(hb_ref[...], wg_ref[...])

    u_ref[...] = _dot(hb_ref[...], w_ref[...]).astype(u_ref.dtype)


def _ln_inproj(x2, g, b, w, w_gate):
    T = x2.shape[0]
    tm, tn = 1024, 512
    return pl.pallas_call(
        _ln_inproj_kernel,
        grid=(T // tm, U_COLS // tn),
        in_specs=[pl.BlockSpec((tm, D_MODEL), lambda i, j: (i, 0)),
                  pl.BlockSpec((1, D_MODEL), lambda i, j: (0, 0)),
                  pl.BlockSpec((1, D_MODEL), lambda i, j: (0, 0)),
                  pl.BlockSpec((D_MODEL, tn), lambda i, j: (0, j)),
                  pl.BlockSpec((D_MODEL, GATE_COLS), lambda i, j: (0, 0))],
        out_specs=[pl.BlockSpec((tm, D_MODEL), lambda i, j: (i, 0)),
                   pl.BlockSpec((tm, tn), lambda i, j: (i, j)),
                   pl.BlockSpec((tm, GATE_COLS), lambda i, j: (i, 0))],
        out_shape=[jax.ShapeDtypeStruct((T, D_MODEL), F32),
                   jax.ShapeDtypeStruct((T, U_COLS), BF16),
                   jax.ShapeDtypeStruct((T, GATE_COLS), F32)],
        scratch_shapes=[pltpu.VMEM((tm, D_MODEL), BF16)],
        compiler_params=_cparams(("parallel", "arbitrary")),
        name="ln_inproj",
    )(x2, g, b, w, w_gate)


def _log_sigmoid(x):
    return jnp.minimum(x, 0.0) - jnp.log1p(jnp.exp(-jnp.abs(x)))


def _mlstm_kernel(qk_ref, v_ref, o_ref, gt_ref, cw_ref, cb_ref, gb_ref, ng_ref, out_ref,
                  prev_ref, c_ref, n_ref, m_ref):
    L = MLSTM_L

    @pl.when(pl.program_id(1) == 0)
    def _():
        prev_ref[...] = jnp.zeros_like(prev_ref)
        c_ref[...] = jnp.zeros_like(c_ref)
        n_ref[...] = jnp.zeros_like(n_ref)
        m_ref[...] = jnp.zeros_like(m_ref)

    x = qk_ref[...].astype(F32)
    prev = prev_ref[...]
    row = lax.broadcasted_iota(I32, (L, 1), 0)
    cw = cw_ref[...]
    y = cb_ref[...] + cw[CONV_WIDTH - 1:CONV_WIDTH, :] * x
    for j in range(1, CONV_WIDTH):
        shifted = jnp.where(row < j, pltpu.roll(prev, j, 0), pltpu.roll(x, j, 0))
        y = y + cw[CONV_WIDTH - 1 - j:CONV_WIDTH - j, :] * shifted
    prev_ref[...] = x
    qk = y * jax.nn.sigmoid(y)

    gpre = gt_ref[...] + gb_ref[...]
    gpre_t = gpre.T
    r_i = lax.broadcasted_iota(I32, (L, L), 0)
    c_i = lax.broadcasted_iota(I32, (L, L), 1)
    causal = r_i >= c_i
    tril = jnp.where(causal, 1.0, 0.0).astype(BF16)
    triu = jnp.where(r_i <= c_i, 1.0, 0.0).astype(BF16)
    lf = _log_sigmoid(gpre)
    lf_t = _log_sigmoid(gpre_t)
    b_cols = sum(_dot(tril, part) for part in _split3(lf))
    b_rows = sum(_dot(part, triu) for part in _split3(lf_t))

    for h in range(MLSTM_HEADS):
        i_col = gpre[:, h:h + 1]
        i_row = gpre_t[h:h + 1, :]
        b_col = b_cols[:, MLSTM_HEADS + h:MLSTM_HEADS + h + 1]
        b_row = b_rows[MLSTM_HEADS + h:MLSTM_HEADS + h + 1, :]
        m_prev = m_ref[h:h + 1, 0:1]
        n_prev = n_ref[h:h + 1, :]
        c_prev = c_ref[h]

        q = qk[:, h * MLSTM_DQK:(h + 1) * MLSTM_DQK]
        k = qk[:, MLSTM_QK + h * MLSTM_DQK:MLSTM_QK + (h + 1) * MLSTM_DQK] * (MLSTM_DQK ** -0.5)
        v = v_ref[:, h * MLSTM_DV:(h + 1) * MLSTM_DV].astype(BF16)
        qb = q.astype(BF16)

        dmat = jnp.where(causal, b_col - b_row + i_row, -jnp.inf)
        inter = b_col + m_prev
        m_t = jnp.maximum(inter, jnp.max(dmat, -1, keepdims=True))
        s = _dot_nt(qb, k.astype(BF16)) * jnp.exp(dmat - m_t)
        a_inter = jnp.exp(inter - m_t)
        num = _dot(s.astype(BF16), v) + a_inter * _dot(qb, c_prev.astype(BF16))
        den = jnp.sum(s, -1, keepdims=True) + a_inter * jnp.sum(q * n_prev, -1, keepdims=True)
        hc = num / jnp.maximum(jnp.abs(den), jnp.exp(-m_t))

        mu = jnp.mean(hc, -1, keepdims=True)
        hcc = hc - mu
        var = jnp.mean(hcc * hcc, -1, keepdims=True)
        hn = hcc * lax.rsqrt(var + LN_EPS) * ng_ref[:, h * MLSTM_DV:(h + 1) * MLSTM_DV]
        og = jax.nn.sigmoid(o_ref[:, h * MLSTM_DV:(h + 1) * MLSTM_DV].astype(F32))
        out_ref[:, h * MLSTM_DV:(h + 1) * MLSTM_DV] = (og * hn).astype(out_ref.dtype)

        b_last = b_col[L - 1:L, :]
        g_col = b_last - b_col + i_col
        m_new = jnp.maximum(b_last + m_prev, jnp.max(g_col, 0, keepdims=True))
        kw = k * jnp.exp(g_col - m_new)
        decay = jnp.exp(b_last + m_prev - m_new)
        c_ref[h] = decay * c_prev + _dot(kw.T.astype(BF16), v)
        n_ref[h:h + 1, :] = decay * n_prev + jnp.sum(kw, 0, keepdims=True)
        m_ref[h:h + 1, :] = jnp.broadcast_to(m_new, (1, LANE))


def _mlstm(u, ug, conv_w, conv_b, gate_b, norm_g, B, S):
    T = B * S
    L = MLSTM_L
    nc = S // L
    row = lambda b, c: b * nc + c
    return pl.pallas_call(
        _mlstm_kernel,
        grid=(B, nc),
        in_specs=[pl.BlockSpec((L, 2 * MLSTM_QK), lambda b, c: (row(b, c), 0)),
                  pl.BlockSpec((L, MLSTM_WIDTH), lambda b, c: (row(b, c), 1)),
                  pl.BlockSpec((L, MLSTM_WIDTH), lambda b, c: (row(b, c), 2)),
                  pl.BlockSpec((L, GATE_COLS), lambda b, c: (row(b, c), 0)),
                  pl.BlockSpec((CONV_WIDTH, 2 * MLSTM_QK), lambda b, c: (0, 0)),
                  pl.BlockSpec((1, 2 * MLSTM_QK), lambda b, c: (0, 0)),
                  pl.BlockSpec((1, LANE), lambda b, c: (0, 0)),
                  pl.BlockSpec((1, MLSTM_WIDTH), lambda b, c: (0, 0))],
        out_specs=pl.BlockSpec((L, MLSTM_WIDTH), lambda b, c: (row(b, c), 0)),
        out_shape=jax.ShapeDtypeStruct((T, MLSTM_WIDTH), BF16),
        scratch_shapes=[pltpu.VMEM((L, 2 * MLSTM_QK), F32),
                        pltpu.VMEM((MLSTM_HEADS, MLSTM_DQK, MLSTM_DV), F32),
                        pltpu.VMEM((8, MLSTM_DQK), F32),
                        pltpu.VMEM((8, LANE), F32)],
        compiler_params=_cparams(("parallel", "arbitrary")),
        name="mlstm",
    )(u, u, u, ug, conv_w, conv_b, gate_b, norm_g)


LOG2E = 1.4426950408889634
SLC_CHUNK = 1024
SLC_CHUNK_BLOCKS = SLC_CHUNK // SLC_BLOCK
VT_EXTRA = 16
N_FORCED = 3
CMP_ROW_STEP = 256


def _nsa_prep_kernel(q_ref, kv0_ref, kv1_ref, kv2_ref, pos_ref, invf_ref, sgn_ref,
                     qt_ref, kc_ref, vc_ref, ks_ref, vst_ref, kw_ref, vwt_ref):
    ang = pos_ref[...].astype(F32) * invf_ref[...]
    cos = jnp.cos(ang)
    sin = jnp.sin(ang) * sgn_ref[...]

    def rope(x):
        return x * cos + pltpu.roll(x, NSA_HEAD_DIM // 2, 1) * sin

    scale = NSA_HEAD_DIM ** -0.5 * LOG2E
    ts = q_ref.shape[0]
    tok = pl.program_id(0) * ts + lax.broadcasted_iota(I32, (ts, NSA_HEAD_DIM), 0)
    blk_lane = (tok // SLC_BLOCK) % SLC_CHUNK_BLOCKS
    blk_onehot = jnp.where(lax.broadcasted_iota(I32, (ts, NSA_HEAD_DIM), 1) == blk_lane, 1.0, 0.0).astype(BF16)
    ones_rows = jnp.where(lax.broadcasted_iota(I32, (VT_EXTRA, ts), 0) == 0, 1.0, 0.0).astype(BF16)
    head = lambda ref, c: ref[:, c:c + NSA_HEAD_DIM].astype(F32)
    for g in range(NSA_KV_HEADS):
        for h in range(NSA_GROUP):
            qt_ref[g, h] = (rope(head(q_ref, (g * NSA_GROUP + h) * NSA_HEAD_DIM)) * scale).T.astype(BF16)
        c0 = g * NSA_HEAD_DIM
        c1 = NSA_KV_HEADS * NSA_HEAD_DIM + g * NSA_HEAD_DIM
        kc_ref[g] = rope(head(kv0_ref, c0)).astype(BF16)
        vc_ref[g] = kv0_ref[:, c1:c1 + NSA_HEAD_DIM].astype(BF16)
        ks_ref[g, :, 0:NSA_HEAD_DIM] = rope(head(kv1_ref, c0)).astype(BF16)
        ks_ref[g, :, NSA_HEAD_DIM:2 * NSA_HEAD_DIM] = blk_onehot
        vst_ref[g, 0:NSA_HEAD_DIM, :] = head(kv1_ref, c1).T.astype(BF16)
        vst_ref[g, NSA_HEAD_DIM:NSA_HEAD_DIM + VT_EXTRA, :] = ones_rows
        kw_ref[g] = rope(head(kv2_ref, c0)).astype(BF16)
        vwt_ref[g] = head(kv2_ref, c1).T.astype(BF16)


def _nsa_prep(u, pos_col, invf2, sgn, T):
    ts = 512
    kvw = 2 * NSA_KV_HEADS * NSA_HEAD_DIM
    kv_spec = lambda n: pl.BlockSpec((ts, kvw), lambda i: (i, (U_NSA + NSA_WIDTH) // kvw + n))
    row_out = pl.BlockSpec((NSA_KV_HEADS, ts, NSA_HEAD_DIM), lambda i: (0, i, 0))
    row_shape = jax.ShapeDtypeStruct((NSA_KV_HEADS, T, NSA_HEAD_DIM), BF16)
    col_out = pl.BlockSpec((NSA_KV_HEADS, NSA_HEAD_DIM, ts), lambda i: (0, 0, i))
    col_shape = jax.ShapeDtypeStruct((NSA_KV_HEADS, NSA_HEAD_DIM, T), BF16)
    return pl.pallas_call(
        _nsa_prep_kernel,
        grid=(T // ts,),
        in_specs=[pl.BlockSpec((ts, NSA_WIDTH), lambda i: (i, U_NSA // NSA_WIDTH)),
                  kv_spec(0), kv_spec(1), kv_spec(2),
                  pl.BlockSpec((ts, 1), lambda i: (i, 0)),
                  pl.BlockSpec((1, LANE), lambda i: (0, 0)),
                  pl.BlockSpec((1, LANE), lambda i: (0, 0))],
        out_specs=[pl.BlockSpec((NSA_KV_HEADS, NSA_GROUP, NSA_HEAD_DIM, ts), lambda i: (0, 0, 0, i)),
                   row_out, row_out,
                   pl.BlockSpec((NSA_KV_HEADS, ts, 2 * NSA_HEAD_DIM), lambda i: (0, i, 0)),
                   pl.BlockSpec((NSA_KV_HEADS, NSA_HEAD_DIM + VT_EXTRA, ts), lambda i: (0, 0, i)),
                   row_out, col_out],
        out_shape=[jax.ShapeDtypeStruct((NSA_KV_HEADS, NSA_GROUP, NSA_HEAD_DIM, T), BF16),
                   row_shape, row_shape,
                   jax.ShapeDtypeStruct((NSA_KV_HEADS, T, 2 * NSA_HEAD_DIM), BF16),
                   jax.ShapeDtypeStruct((NSA_KV_HEADS, NSA_HEAD_DIM + VT_EXTRA, T), BF16),
                   row_shape, col_shape],
        compiler_params=_cparams(("parallel",)),
        name="nsa_prep",
    )(u, u, u, u, pos_col, invf2, sgn)


def _gelu_tanh(x):
    return 0.5 * x * (1.0 + jnp.tanh(0.7978845608028654 * (x + 0.044715 * x * x * x)))


def _compress_kernel(r_ref, w1_ref, pos_ref, w2_ref, out_ref, *, transpose_out):
    r = r_ref[0]
    nr = r.shape[0]
    a = _dot(r, w1_ref[0])
    b = _dot(r, w1_ref[1])
    c0 = _dot(pos_ref[0], w1_ref[0]) + _dot(pos_ref[1], w1_ref[1])
    pre = a + pltpu.roll(b, nr - 1, 0) + c0[0:1, :]
    out = _dot(_gelu_tanh(pre).astype(BF16), w2_ref[...])
    row = lax.broadcasted_iota(I32, (nr, 1), 0)
    out = jnp.where(row < nr - 1, out, 0.0)
    out_ref[0] = (out.T if transpose_out else out).astype(out_ref.dtype)


def _compress(kv, w1, pos8, w2, B, S, transpose_out):
    nr = S // CMP_STRIDE
    half = CMP_STRIDE * NSA_HEAD_DIM
    r = kv.reshape(NSA_KV_HEADS, B * nr, half)
    if transpose_out:
        out_spec = pl.BlockSpec((1, NSA_HEAD_DIM, nr), lambda g, b: (g, 0, b))
        out_shape = jax.ShapeDtypeStruct((NSA_KV_HEADS, NSA_HEAD_DIM, B * nr), BF16)
    else:
        out_spec = pl.BlockSpec((1, nr, NSA_HEAD_DIM), lambda g, b: (g, b, 0))
        out_shape = jax.ShapeDtypeStruct((NSA_KV_HEADS, B * nr, NSA_HEAD_DIM), BF16)
    return pl.pallas_call(
        functools.partial(_compress_kernel, transpose_out=transpose_out),
        grid=(NSA_KV_HEADS, B),
        in_specs=[pl.BlockSpec((1, nr, half), lambda g, b: (g, b, 0)),
                  pl.BlockSpec((2, half, CMP_HIDDEN), lambda g, b: (0, 0, 0)),
                  pl.BlockSpec((2, 8, half), lambda g, b: (0, 0, 0)),
                  pl.BlockSpec((CMP_HIDDEN, NSA_HEAD_DIM), lambda g, b: (0, 0))],
        out_specs=out_spec,
        out_shape=out_shape,
        compiler_params=_cparams(("parallel", "parallel")),
        name="nsa_compress",
    )(r, w1, pos8, w2)


NSA_COLS = NSA_GROUP * Q_BLOCK


def _load_qt(q_ref):
    return jnp.concatenate([q_ref[0, h] for h in range(NSA_GROUP)], 1)


def _store_heads(o_t, out_ref):
    for h in range(NSA_GROUP):
        out_ref[:, h * NSA_HEAD_DIM:(h + 1) * NSA_HEAD_DIM] = (
            o_t[:, h * Q_BLOCK:(h + 1) * Q_BLOCK].T.astype(out_ref.dtype))


def _tile_heads(x):
    return jnp.concatenate([x] * NSA_GROUP, 1)


def _nsa_cmp_kernel(q_ref, kc_ref, vct_ref, ovt_ref, wd_ref, oc_ref, nb_ref, wdb_ref, *, n_slc, topn):
    wdb_ref[...] = wd_ref[...].astype(BF16)
    i = pl.program_id(2)
    nc = kc_ref.shape[1]
    qt = _load_qt(q_ref)
    t = i * Q_BLOCK + lax.broadcasted_iota(I32, (1, Q_BLOCK), 1)
    cur = t // SLC_BLOCK
    step = min(CMP_ROW_STEP, nc)
    ratio = SLC_BLOCK // CMP_STRIDE

    def causal_prefix(rows):
        c_idx = lax.broadcasted_iota(I32, (rows, 1), 0)
        valid = (c_idx * CMP_STRIDE + CMP_BLOCK - 1 <= t) & (c_idx < nc - 1)
        s = _dot(kc_ref[0, 0:rows, :], qt) + _tile_heads(jnp.where(valid, 0.0, NEG))
        m = jnp.max(s, 0, keepdims=True)
        p = jnp.where(m > 0.5 * NEG, jnp.exp2(s - m), 0.0)
        p = p * (1.0 / jnp.maximum(jnp.sum(p, 0, keepdims=True), 1e-30))
        _store_heads(_dot(vct_ref[0, :, 0:rows], p.astype(BF16)), oc_ref)
        psum = p[:, 0:Q_BLOCK]
        for h in range(1, NSA_GROUP):
            psum = psum + p[:, h * Q_BLOCK:(h + 1) * Q_BLOCK]
        nb_rows = rows // ratio
        imp = sum(_dot(ovt_ref[0:nb_rows, 0:rows], part) for part in _split2(psum))

        blk = lax.broadcasted_iota(I32, (nb_rows, Q_BLOCK), 0)
        blk_f = blk.astype(F32)
        causal_blk = blk <= cur
        forced = (blk == 0) | (blk == cur) | (blk == cur - 1)
        score = jnp.where(causal_blk & jnp.logical_not(forced), imp, -jnp.inf)
        sel = jnp.where(causal_blk & forced, 1.0, 0.0)
        for _ in range(topn - N_FORCED):
            mx = jnp.max(score, 0, keepdims=True)
            first = jnp.min(jnp.where(score == mx, blk_f, float(n_slc)), 0, keepdims=True)
            hit = blk_f == first
            sel = jnp.where(hit, 1.0, sel)
            score = jnp.where(hit, -jnp.inf, score)
        nb_ref[0, 0, 0:nb_rows, :] = jnp.where(causal_blk & (sel > 0.5), 0.0, NEG)
        if nb_rows < n_slc:
            nb_ref[0, 0, nb_rows:n_slc, :] = jnp.full((n_slc - nb_rows, Q_BLOCK), NEG, F32)

    need = ((i + 1) * Q_BLOCK - CMP_BLOCK) // CMP_STRIDE + 1
    for v in range(nc // step):
        @pl.when((need + step - 1) // step == v + 1)
        def _():
            causal_prefix((v + 1) * step)


def _nsa_cmp(qt, kcc, vcct, ovt, w_down, B, S):
    T = B * S
    nq = S // Q_BLOCK
    nc = S // CMP_STRIDE
    n_slc = S // SLC_BLOCK
    wd2 = w_down.reshape(-1, w_down.shape[-1])
    slab = wd2.shape[0] // (NSA_KV_HEADS * B * nq)
    assert slab * NSA_KV_HEADS * B * nq == wd2.shape[0] and slab % (2 * SUBLANE) == 0
    step = lambda g, b, i: ((g * B + b) * nq + i, 0)
    kern = functools.partial(_nsa_cmp_kernel, n_slc=n_slc, topn=min(SLC_TOPN, n_slc))
    return pl.pallas_call(
        kern,
        grid=(NSA_KV_HEADS, B, nq),
        in_specs=[pl.BlockSpec((1, NSA_GROUP, NSA_HEAD_DIM, Q_BLOCK), lambda g, b, i: (g, 0, 0, b * nq + i)),
                  pl.BlockSpec((1, nc, NSA_HEAD_DIM), lambda g, b, i: (g, b, 0)),
                  pl.BlockSpec((1, NSA_HEAD_DIM, nc), lambda g, b, i: (g, 0, b)),
                  pl.BlockSpec((n_slc, nc), lambda g, b, i: (0, 0)),
                  pl.BlockSpec((slab, wd2.shape[1]), step)],
        out_specs=[pl.BlockSpec((Q_BLOCK, NSA_GROUP * NSA_HEAD_DIM), lambda g, b, i: (b * nq + i, g)),
                   pl.BlockSpec((1, 1, n_slc, Q_BLOCK), lambda g, b, i: (g, b * nq + i, 0, 0)),
                   pl.BlockSpec((slab, wd2.shape[1]), step)],
        out_shape=[jax.ShapeDtypeStruct((T, NSA_WIDTH), BF16),
                   jax.ShapeDtypeStruct((NSA_KV_HEADS, B * nq, n_slc, Q_BLOCK), F32),
                   jax.ShapeDtypeStruct(wd2.shape, BF16)],
        compiler_params=_cparams(("parallel", "parallel", "arbitrary")),
        name="nsa_cmp_select",
    )(qt, kcc, vcct, ovt, wd2)


SLC_QB = 2
SLC_Q = SLC_QB * Q_BLOCK
SLC_COLS = NSA_GROUP * SLC_Q
SLC_WIN_BLOCKS = WINDOW // Q_BLOCK + SLC_QB


def _store_slc_heads(o_t, out_ref):
    for h in range(NSA_GROUP):
        out_ref[:, h * NSA_HEAD_DIM:(h + 1) * NSA_HEAD_DIM] = (
            o_t[:, h * SLC_Q:(h + 1) * SLC_Q].T.astype(out_ref.dtype))


def _nsa_slc_kernel(q_ref, nb_ref, ks_ref, vst_ref, wg_ref, wu_ref, *refs):
    kw_refs = refs[:SLC_WIN_BLOCKS]
    vw_refs = refs[SLC_WIN_BLOCKS:2 * SLC_WIN_BLOCKS]
    os_ref, ow_ref, wgb_ref, wub_ref, m_ref, acc_ref, s_ref = refs[2 * SLC_WIN_BLOCKS:]
    wgb_ref[...] = wg_ref[...].astype(BF16)
    wub_ref[...] = wu_ref[...].astype(BF16)
    i = pl.program_id(2)
    qt = _load_qt(q_ref)
    t = i * SLC_Q + lax.broadcasted_iota(I32, (1, SLC_Q), 1)
    m_ref[...] = jnp.full_like(m_ref, NEG)
    acc_ref[...] = jnp.zeros_like(acc_ref)
    pad = jnp.zeros((NSA_HEAD_DIM - SLC_CHUNK_BLOCKS, SLC_Q), F32)

    def scores(c, slot):
        start = pl.multiple_of(c * SLC_CHUNK, SLC_CHUNK)
        k = ks_ref[0, pl.ds(start, SLC_CHUNK), :]
        blocks = pl.ds(pl.multiple_of(c * SLC_CHUNK_BLOCKS, SLC_CHUNK_BLOCKS), SLC_CHUNK_BLOCKS)
        nb = jnp.concatenate([nb_ref[0, qb, blocks, :] for qb in range(SLC_QB)], 1)
        mask_rows = jnp.concatenate([nb, pad], 0).astype(BF16)
        rhs = jnp.concatenate([qt, _tile_heads(mask_rows)], 0)
        s_ref[slot] = _dot(k, rhs)

    def absorb(c, slot, diagonal):
        start = pl.multiple_of(c * SLC_CHUNK, SLC_CHUNK)
        vt = vst_ref[0, :, pl.ds(start, SLC_CHUNK)]
        s = s_ref[slot]
        if diagonal:
            key = start + lax.broadcasted_iota(I32, (SLC_CHUNK, 1), 0)
            s = s + _tile_heads(jnp.where(key <= t, 0.0, NEG))
        m_old = m_ref[...]
        m_new = jnp.maximum(m_old, jnp.max(s, 0, keepdims=True))
        p = jnp.exp2(s - m_new).astype(BF16)
        acc_ref[...] = jnp.exp2(m_old - m_new) * acc_ref[...] + _dot(vt, p)
        m_ref[...] = m_new

    n_full = (i * SLC_Q) // SLC_CHUNK
    scores(0, 0)

    kw = jnp.concatenate([r[0] for r in kw_refs], 0)
    ones_rows = jnp.where(lax.broadcasted_iota(I32, (VT_EXTRA, kw.shape[0]), 0) == 0, 1.0, 0.0).astype(BF16)
    vwt = jnp.concatenate([jnp.concatenate([r[0] for r in vw_refs], 1), ones_rows], 0)
    wpos = (i * SLC_QB - WINDOW // Q_BLOCK) * Q_BLOCK + lax.broadcasted_iota(I32, (kw.shape[0], 1), 0)
    in_window = (wpos <= t) & (wpos > t - WINDOW) & (wpos >= 0)
    s_w = _dot(kw, qt) + _tile_heads(jnp.where(in_window, 0.0, NEG))
    p_w = jnp.exp2(s_w - jnp.max(s_w, 0, keepdims=True)).astype(BF16)
    o_w = _dot(vwt, p_w)
    _store_slc_heads(o_w[0:NSA_HEAD_DIM, :] * (1.0 / o_w[NSA_HEAD_DIM:NSA_HEAD_DIM + 1, :]), ow_ref)

    def body(j, carry):
        scores(2 * j + 1, 1)
        absorb(2 * j, 0, False)
        scores(2 * j + 2, 0)
        absorb(2 * j + 1, 1, False)
        return carry

    lax.fori_loop(0, n_full // 2, body, 0)

    @pl.when(n_full % 2 == 1)
    def _():
        scores(n_full, 1)
        absorb(n_full - 1, 0, False)
        absorb(n_full, 1, True)

    @pl.when(n_full % 2 == 0)
    def _():
        absorb(n_full, 0, True)

    ok = m_ref[...] > 0.5 * NEG
    l = acc_ref[NSA_HEAD_DIM:NSA_HEAD_DIM + 1, :]
    inv = jnp.where(ok, 1.0 / jnp.where(ok, l, 1.0), 0.0)
    _store_slc_heads(acc_ref[0:NSA_HEAD_DIM, :] * inv, os_ref)


def _nsa_slc(qt, nb, ks, vst, kw, vwt, w_gate, w_up, B, S):
    T = B * S
    nq = S // SLC_Q
    n_slc = S // SLC_BLOCK
    wg2 = w_gate.reshape(-1, w_gate.shape[-1])
    wu2 = w_up.reshape(-1, w_up.shape[-1])
    slab = wg2.shape[0] // (NSA_KV_HEADS * B * nq)
    assert slab * NSA_KV_HEADS * B * nq == wg2.shape[0] and slab % (2 * SUBLANE) == 0 and wu2.shape == wg2.shape
    step = lambda g, b, i: ((g * B + b) * nq + i, 0)
    resident = dict(pipeline_mode=pl.Buffered(1))
    nblk = S // Q_BLOCK

    def win_block(c):
        return lambda g, b, i: jnp.maximum(i * SLC_QB - WINDOW // Q_BLOCK + c, 0) + b * nblk

    kw_specs = [pl.BlockSpec((1, Q_BLOCK, NSA_HEAD_DIM), lambda g, b, i, f=win_block(c): (g, f(g, b, i), 0))
                for c in range(SLC_WIN_BLOCKS)]
    vw_specs = [pl.BlockSpec((1, NSA_HEAD_DIM, Q_BLOCK), lambda g, b, i, f=win_block(c): (g, 0, f(g, b, i)))
                for c in range(SLC_WIN_BLOCKS)]
    tok_out = pl.BlockSpec((SLC_Q, NSA_GROUP * NSA_HEAD_DIM), lambda g, b, i: (b * nq + i, g))
    return pl.pallas_call(
        _nsa_slc_kernel,
        grid=(NSA_KV_HEADS, B, nq),
        in_specs=[pl.BlockSpec((1, NSA_GROUP, NSA_HEAD_DIM, SLC_Q), lambda g, b, i: (g, 0, 0, b * nq + i)),
                  pl.BlockSpec((1, SLC_QB, n_slc, Q_BLOCK), lambda g, b, i: (g, b * nq + i, 0, 0)),
                  pl.BlockSpec((1, S, 2 * NSA_HEAD_DIM), lambda g, b, i: (g, b, 0), **resident),
                  pl.BlockSpec((1, NSA_HEAD_DIM + VT_EXTRA, S), lambda g, b, i: (g, 0, b), **resident),
                  pl.BlockSpec((slab, wg2.shape[1]), step), pl.BlockSpec((slab, wg2.shape[1]), step)]
        + kw_specs + vw_specs,
        out_specs=[tok_out, tok_out,
                   pl.BlockSpec((slab, wg2.shape[1]), step), pl.BlockSpec((slab, wg2.shape[1]), step)],
        out_shape=[jax.ShapeDtypeStruct((T, NSA_WIDTH), BF16), jax.ShapeDtypeStruct((T, NSA_WIDTH), BF16),
                   jax.ShapeDtypeStruct(wg2.shape, BF16), jax.ShapeDtypeStruct(wg2.shape, BF16)],
        scratch_shapes=[pltpu.VMEM((1, SLC_COLS), F32),
                        pltpu.VMEM((NSA_HEAD_DIM + VT_EXTRA, SLC_COLS), F32),
                        pltpu.VMEM((2, SLC_CHUNK, SLC_COLS), F32)],
        compiler_params=_cparams(("parallel", "parallel", "arbitrary")),
        name="nsa_selected",
    )(qt, nb, ks, vst, wg2, wu2, *([kw] * SLC_WIN_BLOCKS), *([vwt] * SLC_WIN_BLOCKS))


def _outproj_kernel(hm_ref, oc_ref, os_ref, ow_ref, gt_ref, gb_ref, ge_ref, h_ref, w_ref, g_ref, b_ref,
                    out_ref):
    gates = jax.nn.sigmoid(gt_ref[...] + gb_ref[...]).astype(BF16)
    gx = _dot(gates, ge_ref[...])
    hn = (gx[:, 0:NSA_WIDTH] * oc_ref[...].astype(F32)
          + gx[:, NSA_WIDTH:2 * NSA_WIDTH] * os_ref[...].astype(F32)
          + gx[:, 2 * NSA_WIDTH:3 * NSA_WIDTH] * ow_ref[...].astype(F32))
    mix = (_dot(hm_ref[...], w_ref[0:MLSTM_WIDTH, :])
           + _dot(hn.astype(BF16), w_ref[MLSTM_WIDTH:MLSTM_WIDTH + NSA_WIDTH, :]))
    out_ref[...] = _layer_norm(DN_ALPHA * h_ref[...] + mix, g_ref[...], b_ref[...])


def _outproj(hm, oc, os_, ow, ug, gate_b, gate_expand, h, w_out, g, b):
    T = h.shape[0]
    tm = 256
    row = lambda i: (i, 0)
    fixed = lambda i: (0, 0)
    return pl.pallas_call(
        _outproj_kernel,
        grid=(T // tm,),
        in_specs=[pl.BlockSpec((tm, MLSTM_WIDTH), row), pl.BlockSpec((tm, NSA_WIDTH), row),
                  pl.BlockSpec((tm, NSA_WIDTH), row), pl.BlockSpec((tm, NSA_WIDTH), row),
                  pl.BlockSpec((tm, GATE_COLS), row),
                  pl.BlockSpec((1, LANE), fixed),
                  pl.BlockSpec((LANE, 3 * NSA_WIDTH), fixed),
                  pl.BlockSpec((tm, D_MODEL), row),
                  pl.BlockSpec((D_MODEL, D_MODEL), fixed),
                  pl.BlockSpec((1, D_MODEL), fixed), pl.BlockSpec((1, D_MODEL), fixed)],
        out_specs=pl.BlockSpec((tm, D_MODEL), row),
        out_shape=jax.ShapeDtypeStruct((T, D_MODEL), F32),
        compiler_params=_cparams(("parallel",)),
        name="mixer_outproj_ln",
    )(hm, oc, os_, ow, ug, gate_b, gate_expand, h, w_out, g, b)


def _matmul_kernel(x_ref, w_ref, o_ref):
    o_ref[...] = _dot(x_ref[...].astype(BF16), w_ref[...]).astype(o_ref.dtype)


def _mem_kv(mem2, wkv):
    M = mem2.shape[0]
    N = wkv.shape[1]
    tn = 512
    return pl.pallas_call(
        _matmul_kernel,
        grid=(N // tn,),
        in_specs=[pl.BlockSpec((M, D_MODEL), lambda j: (0, 0)),
                  pl.BlockSpec((D_MODEL, tn), lambda j: (0, j))],
        out_specs=pl.BlockSpec((M, tn), lambda j: (0, j)),
        out_shape=jax.ShapeDtypeStruct((M, N), BF16),
        compiler_params=_cparams(("parallel",)),
        name="mem_kv_proj",
    )(mem2, wkv)


def _xattn_kernel(h_ref, wq_ref, kv_ref, o_ref):
    hb = h_ref[...].astype(BF16)
    for hd in range(XA_HEADS):
        c0 = hd * XA_HEAD_DIM
        q = (_dot(hb, wq_ref[:, c0:c0 + XA_HEAD_DIM]) * (XA_HEAD_DIM ** -0.5)).astype(BF16)
        s = _dot_nt(q, kv_ref[:, c0:c0 + XA_HEAD_DIM])
        m = jnp.max(s, -1, keepdims=True)
        p = jnp.exp(s - m)
        o = _dot(p.astype(BF16), kv_ref[:, D_MODEL + c0:D_MODEL + c0 + XA_HEAD_DIM])
        o_ref[:, c0:c0 + XA_HEAD_DIM] = (o / jnp.sum(p, -1, keepdims=True)).astype(o_ref.dtype)


def _xattn(h1, wq, kv, B, S):
    T = B * S
    tm = 512
    n_mem = kv.shape[0] // B
    per_b = S // tm
    return pl.pallas_call(
        _xattn_kernel,
        grid=(T // tm,),
        in_specs=[pl.BlockSpec((tm, D_MODEL), lambda i: (i, 0)),
                  pl.BlockSpec((D_MODEL, D_MODEL), lambda i: (0, 0)),
                  pl.BlockSpec((n_mem, 2 * D_MODEL), lambda i: (i // per_b, 0))],
        out_specs=pl.BlockSpec((tm, D_MODEL), lambda i: (i, 0)),
        out_shape=jax.ShapeDtypeStruct((T, D_MODEL), BF16),
        compiler_params=_cparams(("parallel",)),
        name="mem_xattn",
    )(h1, wq, kv)


def _xa_out_router_kernel(o_ref, wo_ref, h_ref, g_ref, b_ref, rw_ref, rb_ref,
                          h2_ref, idx_ref, wgt_ref):
    xa = _dot(o_ref[...], wo_ref[...])
    h2 = _layer_norm(DN_ALPHA * h_ref[...] + xa, g_ref[...], b_ref[...])
    h2_ref[...] = h2
    hi, lo = _split2(h2)
    logits = _dot(hi, rw_ref[0]) + _dot(hi, rw_ref[1]) + _dot(lo, rw_ref[0])
    scores = jax.nn.sigmoid(logits)
    lane = lax.broadcasted_iota(I32, scores.shape, 1)
    lane_f = lane.astype(F32)
    biased = jnp.where(lane < N_EXPERTS, scores + rb_ref[...], -jnp.inf)
    idx_mat = jnp.zeros(scores.shape, F32)
    w_mat = jnp.zeros(scores.shape, F32)
    for kk in range(TOP_K):
        mx = jnp.max(biased, -1, keepdims=True)
        first = jnp.min(jnp.where(biased == mx, lane_f, float(LANE)), -1, keepdims=True)
        hit = lane_f == first
        top_s = jnp.sum(jnp.where(hit, scores, 0.0), -1, keepdims=True)
        idx_mat = jnp.where(lane == kk, first, idx_mat)
        w_mat = jnp.where(lane == kk, top_s, w_mat)
        biased = jnp.where(hit, -jnp.inf, biased)
    idx_ref[...] = idx_mat.astype(I32)
    wgt_ref[...] = w_mat / jnp.sum(w_mat, -1, keepdims=True) * ROUTED_SCALE


def _xa_out_router(o, wo, h1, g, b, rw2, rb):
    T = h1.shape[0]
    tm = 256
    row = lambda i: (i, 0)
    fixed = lambda i: (0, 0)
    return pl.pallas_call(
        _xa_out_router_kernel,
        grid=(T // tm,),
        in_specs=[pl.BlockSpec((tm, D_MODEL), row),
                  pl.BlockSpec((D_MODEL, D_MODEL), fixed),
                  pl.BlockSpec((tm, D_MODEL), row),
                  pl.BlockSpec((1, D_MODEL), fixed), pl.BlockSpec((1, D_MODEL), fixed),
                  pl.BlockSpec((2, D_MODEL, LANE), lambda i: (0, 0, 0)),
                  pl.BlockSpec((1, LANE), fixed)],
        out_specs=[pl.BlockSpec((tm, D_MODEL), row), pl.BlockSpec((tm, LANE), row),
                   pl.BlockSpec((tm, LANE), row)],
        out_shape=[jax.ShapeDtypeStruct((T, D_MODEL), F32),
                   jax.ShapeDtypeStruct((T, LANE), I32),
                   jax.ShapeDtypeStruct((T, LANE), F32)],
        compiler_params=_cparams(("parallel",)),
        name="xattn_out_ln_router",
    )(o, wo, h1, g, b, rw2, rb)


ROUTE_TM = 512
ZERO_ROWS = 128


def _route_rank_kernel(idx_ref, rank_ref, cnt_ref):
    tm = idx_ref.shape[0]

    @pl.when(pl.program_id(0) == 0)
    def _():
        cnt_ref[...] = jnp.zeros_like(cnt_ref)

    idx = idx_ref[...]
    lane = lax.broadcasted_iota(I32, (tm, LANE), 1)
    hits = [lane == idx[:, kk:kk + 1] for kk in range(TOP_K)]
    onehot = sum(jnp.where(hit, 1.0, 0.0) for hit in hits)
    r_i = lax.broadcasted_iota(I32, (tm, tm), 0)
    c_i = lax.broadcasted_iota(I32, (tm, tm), 1)
    before = jnp.where(r_i > c_i, 1.0, 0.0).astype(BF16)
    rank = _dot(before, onehot.astype(BF16)) + cnt_ref[0:1, :]
    out = jnp.zeros((tm, LANE), F32)
    for kk in range(TOP_K):
        out = jnp.where(lane == kk, jnp.sum(jnp.where(hits[kk], rank, 0.0), -1, keepdims=True), out)
    rank_ref[...] = out.astype(I32)
    cnt_ref[0:1, :] = cnt_ref[0:1, :] + jnp.sum(onehot, 0, keepdims=True)


def _route_rank(top_idx):
    T = top_idx.shape[0]
    tm = ROUTE_TM
    return pl.pallas_call(
        _route_rank_kernel,
        grid=(T // tm,),
        in_specs=[pl.BlockSpec((tm, LANE), lambda i: (i, 0))],
        out_specs=[pl.BlockSpec((tm, LANE), lambda i: (i, 0)),
                   pl.BlockSpec((8, LANE), lambda i: (0, 0))],
        out_shape=[jax.ShapeDtypeStruct((T, LANE), I32), jax.ShapeDtypeStruct((8, LANE), F32)],
        compiler_params=_cparams(("arbitrary",)),
        name="moe_route_rank",
    )(top_idx)


def _load_route(i, dest_hbm, dest_smem, isem):
    cp = pltpu.make_async_copy(dest_hbm.at[i], dest_smem, isem)
    cp.start()
    cp.wait()


_FILL_SIZES = tuple(s for s in (ZERO_ROWS >> n for n in range(ZERO_ROWS.bit_length())) if s >= SUBLANE)


def _experts_kernel(be_ref, nu_ref, x_ref, wg_ref, wu_ref, wd_ref, y_ref):
    used = pl.program_id(0) < nu_ref[0]

    @pl.when(used)
    def _():
        xb = x_ref[...].astype(BF16)
        a = _dot(xb, wg_ref[0])
        act = (a * jax.nn.sigmoid(a)) * _dot(xb, wu_ref[0])
        y_ref[...] = _dot(act.astype(BF16), wd_ref[0])

    @pl.when(jnp.logical_not(used))
    def _():
        y_ref[...] = jnp.zeros_like(y_ref)


def _experts(block_e, n_used, xs, wg, wu, wd):
    bm = EXPERT_BM
    P = xs.shape[0] // bm * bm
    rowmap = lambda j, be, nu: (jnp.minimum(j, nu[0] - 1), 0)
    wmap = lambda j, be, nu: (be[j], 0, 0)
    return pl.pallas_call(
        _experts_kernel,
        grid_spec=pltpu.PrefetchScalarGridSpec(
            num_scalar_prefetch=2,
            grid=(P // bm,),
            in_specs=[pl.BlockSpec((bm, D_MODEL), rowmap),
                      pl.BlockSpec((1, D_MODEL, D_EXPERT), wmap),
                      pl.BlockSpec((1, D_MODEL, D_EXPERT), wmap),
                      pl.BlockSpec((1, D_EXPERT, D_MODEL), wmap)],
            out_specs=pl.BlockSpec((bm, D_MODEL), lambda j, be, nu: (j, 0))),
        out_shape=jax.ShapeDtypeStruct((P, D_MODEL), F32),
        compiler_params=_cparams(("arbitrary",)),
        name="moe_experts",
    )(block_e, n_used, xs, wg, wu, wd)


SHARED_TF = 256
SHARED_STEPS = D_SHARED // SHARED_TF
DISPATCH_SHARE = -(-ROUTE_TM // (SHARED_STEPS * SUBLANE)) * SUBLANE
DUMP_ROWS = (DISPATCH_SHARE * SHARED_STEPS - ROUTE_TM) * TOP_K


def _zero_fill(fs_ref, fn_ref, tail_ref, zero_ref, xs_ref, zsem):
    zero_ref[...] = jnp.zeros_like(zero_ref)

    def fill_copies(e, go):
        n = fn_ref[e]
        first = fs_ref[e]
        lead = jnp.minimum((-first) & (SUBLANE - 1), n)
        for r in range(SUBLANE - 1):
            @pl.when(r < lead)
            def _():
                go(pltpu.make_async_copy(zero_ref.at[pl.ds(0, 1)], xs_ref.at[pl.ds(first + r, 1)], zsem))

        rest = n - lead
        off = first + lead
        for size in _FILL_SIZES:
            take = (rest & size) != 0

            @pl.when(take)
            def _():
                dst = xs_ref.at[pl.ds(pl.multiple_of(off, SUBLANE), size)]
                go(pltpu.make_async_copy(zero_ref.at[pl.ds(0, size)], dst, zsem))
            off = off + jnp.where(take, size, 0)

    def tail_copies(j, go):
        off = pl.multiple_of(tail_ref[0] + j * ZERO_ROWS, ZERO_ROWS)
        go(pltpu.make_async_copy(zero_ref, xs_ref.at[pl.ds(off, ZERO_ROWS)], zsem))

    for go in (lambda cp: cp.start(), lambda cp: cp.wait()):
        lax.fori_loop(0, N_EXPERTS, lambda e, c, go=go: (fill_copies(e, go), c)[1], 0)
        lax.fori_loop(0, tail_ref[1], lambda j, c, go=go: (tail_copies(j, go), c)[1], 0)


def _shared_ffn_kernel(fs_ref, fn_ref, tail_ref, dest_hbm, x_ref, wg_ref, wu_ref, wd_ref,
                       o_ref, xs_ref, xb_ref, dest_smem, zero_ref, isem, sem, zsem, *, dump_base):
    i = pl.program_id(0)
    f = pl.program_id(1)

    @pl.when(f == 0)
    def _():
        _load_route(i, dest_hbm, dest_smem, isem)
        xb_ref[...] = x_ref[...].astype(BF16)
        o_ref[...] = jnp.zeros_like(o_ref)

    @pl.when((f == 0) & (i == 0))
    def _():
        _zero_fill(fs_ref, fn_ref, tail_ref, zero_ref, xs_ref, zsem)

    for rr in range(DISPATCH_SHARE):
        r = f * DISPATCH_SHARE + rr
        live = r < ROUTE_TM
        rc = jnp.minimum(r, ROUTE_TM - 1)
        for kk in range(TOP_K):
            d = jnp.where(live, dest_smem[rc * TOP_K + kk], dump_base + (r - ROUTE_TM) * TOP_K + kk)
            pltpu.make_async_copy(x_ref.at[pl.ds(rc, 1)], xs_ref.at[pl.ds(d, 1)], sem).start()

    xb = xb_ref[...]
    a = _dot(xb, wg_ref[...])
    act = (a * jax.nn.sigmoid(a)) * _dot(xb, wu_ref[...])
    o_ref[...] += _dot(act.astype(BF16), wd_ref[...])

    @pl.when(f == SHARED_STEPS - 1)
    def _():
        for _ in range(TOP_K):
            pltpu.make_async_copy(x_ref, xs_ref.at[pl.ds(0, ROUTE_TM)], sem).wait()
        pltpu.make_async_copy(x_ref.at[pl.ds(0, DUMP_ROWS)], xs_ref.at[pl.ds(0, DUMP_ROWS)], sem).wait()


def _shared_ffn_dispatch(fill_start, fill_n, tail, dest2, h2, wg, wu, wd, P):
    T = h2.shape[0]
    tm, tf = ROUTE_TM, SHARED_TF
    return pl.pallas_call(
        functools.partial(_shared_ffn_kernel, dump_base=P),
        grid_spec=pltpu.PrefetchScalarGridSpec(
            num_scalar_prefetch=3,
            grid=(T // tm, SHARED_STEPS),
            in_specs=[pl.BlockSpec(memory_space=pl.ANY),
                      pl.BlockSpec((tm, D_MODEL), lambda i, f, *_: (i, 0)),
                      pl.BlockSpec((D_MODEL, tf), lambda i, f, *_: (0, f)),
                      pl.BlockSpec((D_MODEL, tf), lambda i, f, *_: (0, f)),
                      pl.BlockSpec((tf, D_MODEL), lambda i, f, *_: (f, 0))],
            out_specs=[pl.BlockSpec((tm, D_MODEL), lambda i, f, *_: (i, 0)),
                       pl.BlockSpec(memory_space=pl.ANY)],
            scratch_shapes=[pltpu.VMEM((tm, D_MODEL), BF16),
                            pltpu.SMEM((tm * TOP_K,), I32),
                            pltpu.VMEM((ZERO_ROWS, D_MODEL), F32),
                            pltpu.SemaphoreType.DMA, pltpu.SemaphoreType.DMA, pltpu.SemaphoreType.DMA]),
        out_shape=[jax.ShapeDtypeStruct((T, D_MODEL), F32),
                   jax.ShapeDtypeStruct((P + DUMP_ROWS, D_MODEL), F32)],
        compiler_params=_cparams(("arbitrary", "arbitrary")),
        name="shared_ffn_dispatch",
    )(fill_start, fill_n, tail, dest2, h2, wg, wu, wd)


COMBINE_SUB = 128
COMBINE_NSUB = ROUTE_TM // COMBINE_SUB


def _combine_kernel(dest_hbm, ys_hbm, w_ref, sh_ref, h_ref, g_ref, b_ref, out_ref,
                    dest_smem, gbuf, isem, sem):
    i = pl.program_id(0)
    _load_route(i, dest_hbm, dest_smem, isem)

    def gather(sub):
        slot = sub % 2

        def issue(r, carry):
            for kk in range(TOP_K):
                d = dest_smem[(sub * COMBINE_SUB + r) * TOP_K + kk]
                pltpu.make_async_copy(ys_hbm.at[pl.ds(d, 1)], gbuf.at[slot, kk, pl.ds(r, 1)],
                                      sem.at[slot]).start()
            return carry

        lax.fori_loop(0, COMBINE_SUB, issue, 0, unroll=True)

    gather(0)
    for sub in range(COMBINE_NSUB):
        slot = sub % 2
        if sub + 1 < COMBINE_NSUB:
            gather(sub + 1)
        for kk in range(TOP_K):
            pltpu.make_async_copy(ys_hbm.at[pl.ds(0, COMBINE_SUB)], gbuf.at[slot, kk], sem.at[slot]).wait()
        rs = slice(sub * COMBINE_SUB, (sub + 1) * COMBINE_SUB)
        w = w_ref[rs, :]
        routed = w[:, 0:1] * gbuf[slot, 0]
        for kk in range(1, TOP_K):
            routed = routed + w[:, kk:kk + 1] * gbuf[slot, kk]
        z = DN_ALPHA * h_ref[rs, :] + (routed + sh_ref[rs, :])
        out_ref[rs, :] = _layer_norm(z, g_ref[...], b_ref[...])


def _combine(dest2, ys, top_w, sh, h2, g, b):
    T = h2.shape[0]
    tm = ROUTE_TM
    row = lambda i: (i, 0)
    fixed = lambda i: (0, 0)
    return pl.pallas_call(
        _combine_kernel,
        grid=(T // tm,),
        in_specs=[pl.BlockSpec(memory_space=pl.ANY), pl.BlockSpec(memory_space=pl.ANY),
                  pl.BlockSpec((tm, LANE), row), pl.BlockSpec((tm, D_MODEL), row),
                  pl.BlockSpec((tm, D_MODEL), row),
                  pl.BlockSpec((1, D_MODEL), fixed), pl.BlockSpec((1, D_MODEL), fixed)],
        out_specs=pl.BlockSpec((tm, D_MODEL), row),
        scratch_shapes=[pltpu.SMEM((tm * TOP_K,), I32),
                        pltpu.VMEM((2, TOP_K, COMBINE_SUB, D_MODEL), F32),
                        pltpu.SemaphoreType.DMA, pltpu.SemaphoreType.DMA((2,))],
        out_shape=jax.ShapeDtypeStruct((T, D_MODEL), F32),
        compiler_params=_cparams(("arbitrary",)),
        name="moe_combine_ln",
    )(dest2, ys, top_w, sh, h2, g, b)


def _route_plan(counts, T):
    bm = EXPERT_BM
    padded = (counts + bm - 1) // bm * bm
    pad_end = jnp.cumsum(padded)
    pad_start = pad_end - padded
    n_blocks = T * TOP_K // bm + N_EXPERTS
    first_row = jnp.arange(n_blocks, dtype=I32) * bm
    block_e = jnp.minimum(jnp.sum((pad_end[None, :] <= first_row[:, None]).astype(I32), 1), N_EXPERTS - 1)
    n_used = (pad_end[-1:] // bm).astype(I32)
    tail = jnp.concatenate([pad_end[-1:], (n_blocks - n_used) * (bm // ZERO_ROWS)]).astype(I32)
    return (pad_start.astype(I32), (pad_start + counts).astype(I32), (padded - counts).astype(I32), tail,
            block_e, n_used, n_blocks * bm)


def _overlap_matrix(S):
    n_cmp_rows = S // CMP_STRIDE
    n_slc = S // SLC_BLOCK
    c_lo = np.arange(n_cmp_rows)[:, None] * CMP_STRIDE
    j_lo = np.arange(n_slc)[None, :] * SLC_BLOCK
    ov = (c_lo <= j_lo + SLC_BLOCK - 1) & (c_lo + CMP_BLOCK - 1 >= j_lo)
    return jnp.asarray(ov.T.astype(np.float32), dtype=BF16)


def _gate_expand_matrix():
    ge = np.zeros((LANE, 3 * NSA_WIDTH), np.float32)
    for hh in range(NSA_HEADS):
        for br in range(3):
            ge[8 + hh * 3 + br, br * NSA_WIDTH + hh * NSA_HEAD_DIM:br * NSA_WIDTH + (hh + 1) * NSA_HEAD_DIM] = 1.0
    return jnp.asarray(ge, dtype=BF16)


def kernel(x, mem, positions, ln0_g, ln0_b, w_in, conv_w, conv_b, igate_b, fgate_b, mlstm_norm_g, cmp_pos, cmp_w1k, cmp_w2k, cmp_w1v, cmp_w2v, nsa_gate_b, w_out, ln1_g, ln1_b, xa_wq, xa_wk, xa_wv, xa_wo, ln2_g, ln2_b, router_w, router_bias, moe_w_gate, moe_w_up, moe_w_down, sh_w_gate, sh_w_up, sh_w_down, ln3_g, ln3_b):
    B, S, D = x.shape
    T = B * S
    assert D == D_MODEL and w_in.shape[0] == DEPTH == 1
    assert S % MLSTM_L == 0 and S % SLC_CHUNK == 0 and T % ROUTE_TM == 0 and (T * TOP_K) % EXPERT_BM == 0
    row = lambda a: a.reshape(1, -1)

    w = w_in[0]
    w_r = jnp.concatenate([w[:, :3072], w[:, 3080:5640]], 1).astype(BF16)
    w_g = jnp.concatenate([w[:, 3072:3080], w[:, 5640:5664],
                           jnp.zeros((D, GATE_COLS - 2 * MLSTM_HEADS - 3 * NSA_HEADS), F32)], 1).astype(BF16)
    gate_b = jnp.concatenate([igate_b[0], fgate_b[0], nsa_gate_b[0],
                              jnp.zeros((LANE - 2 * MLSTM_HEADS - 3 * NSA_HEADS,), F32)]).reshape(1, LANE)
    half = NSA_HEAD_DIM // 2
    inv_freq = ROPE_THETA ** (-jnp.arange(half, dtype=F32) / half)
    invf2 = jnp.concatenate([inv_freq, inv_freq]).reshape(1, LANE)
    sgn = jnp.concatenate([-jnp.ones((half,), F32), jnp.ones((half,), F32)]).reshape(1, LANE)
    pos8 = jnp.zeros((2, 8, CMP_STRIDE * NSA_HEAD_DIM), F32).at[:, 0, :].set(
        cmp_pos[0].reshape(2, CMP_STRIDE * NSA_HEAD_DIM)).astype(BF16)
    w1k = cmp_w1k[0].reshape(2, CMP_STRIDE * NSA_HEAD_DIM, CMP_HIDDEN).astype(BF16)
    w1v = cmp_w1v[0].reshape(2, CMP_STRIDE * NSA_HEAD_DIM, CMP_HIDDEN).astype(BF16)
    rw = jnp.pad(router_w[0], ((0, 0), (0, LANE - N_EXPERTS)))
    rw_hi = rw.astype(BF16)
    rw2 = jnp.stack([rw_hi, (rw - rw_hi.astype(F32)).astype(BF16)])
    rb = jnp.pad(router_bias[0], (0, LANE - N_EXPERTS)).reshape(1, LANE)

    h, u, ug = _ln_inproj(x.reshape(T, D), row(ln0_g), row(ln0_b), w_r, w_g)
    hm = _mlstm(u, ug, conv_w[0], row(conv_b[0]), gate_b, row(mlstm_norm_g[0]), B, S)
    qt, kc, vc, ks, vst, kw, vwt = _nsa_prep(u, positions.reshape(T, 1), invf2, sgn, T)
    kcc = _compress(kc, w1k, pos8, cmp_w2k[0].astype(BF16), B, S, False)
    vcct = _compress(vc, w1v, pos8, cmp_w2v[0].astype(BF16), B, S, True)
    oc, nb, wd_b = _nsa_cmp(qt, kcc, vcct, _overlap_matrix(S), moe_w_down[0], B, S)
    os_, ow, wg_b, wu_b = _nsa_slc(qt, nb, ks, vst, kw, vwt, moe_w_gate[0], moe_w_up[0], B, S)
    h1 = _outproj(hm, oc, os_, ow, ug, gate_b, _gate_expand_matrix(), h, w_out[0].astype(BF16),
                  row(ln1_g[0]), row(ln1_b[0]))

    wkv = jnp.concatenate([xa_wk[0], xa_wv[0]], 1).astype(BF16)
    kv = _mem_kv(mem.reshape(-1, D), wkv)
    xo = _xattn(h1, xa_wq[0].astype(BF16), kv, B, S)
    h2, top_idx, top_w = _xa_out_router(xo, xa_wo[0].astype(BF16), h1, row(ln2_g[0]), row(ln2_b[0]), rw2, rb)

    rank, cnt = _route_rank(top_idx)
    counts = cnt[0, :N_EXPERTS].astype(I32)
    pad_start, fill_start, fill_n, tail, block_e, n_used, P = _route_plan(counts, T)
    slot_e = top_idx[:, :TOP_K, None] == jnp.arange(N_EXPERTS, dtype=I32)[None, None, :]
    dest = rank[:, :TOP_K] + jnp.sum(jnp.where(slot_e, pad_start[None, None, :], 0), -1)
    dest2 = dest.reshape(T // ROUTE_TM, ROUTE_TM * TOP_K)
    sh, xs = _shared_ffn_dispatch(fill_start, fill_n, tail, dest2, h2,
                                  sh_w_gate[0].astype(BF16), sh_w_up[0].astype(BF16),
                                  sh_w_down[0].astype(BF16), P)
    ys = _experts(block_e, n_used, xs, wg_b.reshape(moe_w_gate.shape[1:]), wu_b.reshape(moe_w_up.shape[1:]),
                  wd_b.reshape(moe_w_down.shape[1:]))
    out = _combine(dest2, ys, top_w, sh, h2, row(ln3_g[0]), row(ln3_b[0]))
    return out.reshape(B, S, D)
```

```python
import functools

import numpy as np
import jax
import jax.numpy as jnp
from jax import lax
from jax.experimental import pallas as pl
from jax.experimental.pallas import tpu as pltpu

F32 = jnp.float32
BF16 = jnp.bfloat16
I32 = jnp.int32

D_MODEL = 2048
MLSTM_HEADS = 4
MLSTM_DV = 256
MLSTM_DQK = 128
MLSTM_QK = MLSTM_HEADS * MLSTM_DQK
MLSTM_WIDTH = MLSTM_HEADS * MLSTM_DV
CONV_WIDTH = 4
NSA_HEAD_DIM = 128
NSA_HEADS = 8
NSA_KV_HEADS = 2
NSA_GROUP = 4
NSA_WIDTH = NSA_HEADS * NSA_HEAD_DIM
CMP_BLOCK = 32
CMP_STRIDE = 16
CMP_HIDDEN = 256
SLC_BLOCK = 64
SLC_TOPN = 16
WINDOW = 512
Q_BLOCK = 128
XA_HEADS = 4
XA_HEAD_DIM = 512
N_EXPERTS = 64
TOP_K = 6
D_EXPERT = 1408
D_SHARED = 2816
ROUTED_SCALE = 2.446
ROPE_THETA = 10000.0
LN_EPS = 1e-5
DEPTH = 1
DN_ALPHA = (2.0 * DEPTH) ** 0.25

U_MLSTM = 0
U_NSA = 3072
U_COLS = 5632
GATE_COLS = 128

LANE = 128
SUBLANE = 8
NEG = -1e30
MLSTM_L = 256
VMEM_LIMIT = 56 * 1024 * 1024
EXPERT_BM = 256


def _cparams(sem):
    return pltpu.CompilerParams(dimension_semantics=sem, vmem_limit_bytes=VMEM_LIMIT)


def _dot(a, b):
    return jnp.dot(a, b, preferred_element_type=F32)


def _dot_nt(a, b):
    return lax.dot_general(a, b, (((1,), (1,)), ((), ())), preferred_element_type=F32)


def _layer_norm(z, g, b):
    mu = jnp.mean(z, -1, keepdims=True)
    zc = z - mu
    var = jnp.mean(zc * zc, -1, keepdims=True)
    return zc * lax.rsqrt(var + LN_EPS) * g + b


def _split3(x):
    hi = x.astype(BF16)
    r = x - hi.astype(F32)
    mid = r.astype(BF16)
    lo = (r - mid.astype(F32)).astype(BF16)
    return hi, mid, lo


def _split2(x):
    hi = x.astype(BF16)
    lo = (x - hi.astype(F32)).astype(BF16)
    return hi, lo


def _ln_inproj_kernel(x_ref, g_ref, b_ref, w_ref, wg_ref, h_ref, u_ref, ug_ref, hb_ref):
    @pl.when(pl.program_id(1) == 0)
    def _():
        hn = _layer_norm(x_ref[...], g_ref[...], b_ref[...])
        h_ref[...] = hn
        hb_ref[...] = hn.astype(BF16)
        ug_ref[...] = _dot(hb_ref[...], wg_ref[...])

    u_ref[...] = _dot(hb_ref[...], w_ref[...]).astype(u_ref.dtype)


def _ln_inproj(x2, g, b, w, w_gate):
    T = x2.shape[0]
    tm, tn = 1024, 512
    return pl.pallas_call(
        _ln_inproj_kernel,
        grid=(T // tm, U_COLS // tn),
        in_specs=[pl.BlockSpec((tm, D_MODEL), lambda i, j: (i, 0)),
                  pl.BlockSpec((1, D_MODEL), lambda i, j: (0, 0)),
                  pl.BlockSpec((1, D_MODEL), lambda i, j: (0, 0)),
                  pl.BlockSpec((D_MODEL, tn), lambda i, j: (0, j)),
                  pl.BlockSpec((D_MODEL, GATE_COLS), lambda i, j: (0, 0))],
        out_specs=[pl.BlockSpec((tm, D_MODEL), lambda i, j: (i, 0)),
                   pl.BlockSpec((tm, tn), lambda i, j: (i, j)),
                   pl.BlockSpec((tm, GATE_COLS), lambda i, j: (i, 0))],
        out_shape=[jax.ShapeDtypeStruct((T, D_MODEL), F32),
                   jax.ShapeDtypeStruct((T, U_COLS), BF16),
                   jax.ShapeDtypeStruct((T, GATE_COLS), F32)],
        scratch_shapes=[pltpu.VMEM((tm, D_MODEL), BF16)],
        compiler_params=_cparams(("parallel", "arbitrary")),
        name="ln_inproj",
    )(x2, g, b, w, w_gate)


def _log_sigmoid(x):
    return jnp.minimum(x, 0.0) - jnp.log1p(jnp.exp(-jnp.abs(x)))


def _mlstm_kernel(qk_ref, v_ref, o_ref, gt_ref, cw_ref, cb_ref, gb_ref, ng_ref, out_ref,
                  prev_ref, c_ref, n_ref, m_ref):
    L = MLSTM_L

    @pl.when(pl.program_id(1) == 0)
    def _():
        prev_ref[...] = jnp.zeros_like(prev_ref)
        c_ref[...] = jnp.zeros_like(c_ref)
        n_ref[...] = jnp.zeros_like(n_ref)
        m_ref[...] = jnp.zeros_like(m_ref)

    x = qk_ref[...].astype(F32)
    prev = prev_ref[...]
    row = lax.broadcasted_iota(I32, (L, 1), 0)
    cw = cw_ref[...]
    y = cb_ref[...] + cw[CONV_WIDTH - 1:CONV_WIDTH, :] * x
    for j in range(1, CONV_WIDTH):
        shifted = jnp.where(row < j, pltpu.roll(prev, j, 0), pltpu.roll(x, j, 0))
        y = y + cw[CONV_WIDTH - 1 - j:CONV_WIDTH - j, :] * shifted
    prev_ref[...] = x
    qk = y * jax.nn.sigmoid(y)

    gpre = gt_ref[...] + gb_ref[...]
    gpre_t = gpre.T
    r_i = lax.broadcasted_iota(I32, (L, L), 0)
    c_i = lax.broadcasted_iota(I32, (L, L), 1)
    causal = r_i >= c_i
    tril = jnp.where(causal, 1.0, 0.0).astype(BF16)
    triu = jnp.where(r_i <= c_i, 1.0, 0.0).astype(BF16)
    lf = _log_sigmoid(gpre)
    lf_t = _log_sigmoid(gpre_t)
    b_cols = sum(_dot(tril, part) for part in _split3(lf))
    b_rows = sum(_dot(part, triu) for part in _split3(lf_t))

    for h in range(MLSTM_HEADS):
        i_col = gpre[:, h:h + 1]
        i_row = gpre_t[h:h + 1, :]
        b_col = b_cols[:, MLSTM_HEADS + h:MLSTM_HEADS + h + 1]
        b_row = b_rows[MLSTM_HEADS + h:MLSTM_HEADS + h + 1, :]
        m_prev = m_ref[h:h + 1, 0:1]
        n_prev = n_ref[h:h + 1, :]
        c_prev = c_ref[h]

        q = qk[:, h * MLSTM_DQK:(h + 1) * MLSTM_DQK]
        k = qk[:, MLSTM_QK + h * MLSTM_DQK:MLSTM_QK + (h + 1) * MLSTM_DQK] * (MLSTM_DQK ** -0.5)
        v = v_ref[:, h * MLSTM_DV:(h + 1) * MLSTM_DV].astype(BF16)
        qb = q.astype(BF16)

        dmat = jnp.where(causal, b_col - b_row + i_row, -jnp.inf)
        inter = b_col + m_prev
        m_t = jnp.maximum(inter, jnp.max(dmat, -1, keepdims=True))
        s = _dot_nt(qb, k.astype(BF16)) * jnp.exp(dmat - m_t)
        a_inter = jnp.exp(inter - m_t)
        num = _dot(s.astype(BF16), v) + a_inter * _dot(qb, c_prev.astype(BF16))
        den = jnp.sum(s, -1, keepdims=True) + a_inter * jnp.sum(q * n_prev, -1, keepdims=True)
        hc = num / jnp.maximum(jnp.abs(den), jnp.exp(-m_t))

        mu = jnp.mean(hc, -1, keepdims=True)
        hcc = hc - mu
        var = jnp.mean(hcc * hcc, -1, keepdims=True)
        hn = hcc * lax.rsqrt(var + LN_EPS) * ng_ref[:, h * MLSTM_DV:(h + 1) * MLSTM_DV]
        og = jax.nn.sigmoid(o_ref[:, h * MLSTM_DV:(h + 1) * MLSTM_DV].astype(F32))
        out_ref[:, h * MLSTM_DV:(h + 1) * MLSTM_DV] = (og * hn).astype(out_ref.dtype)

        b_last = b_col[L - 1:L, :]
        g_col = b_last - b_col + i_col
        m_new = jnp.maximum(b_last + m_prev, jnp.max(g_col, 0, keepdims=True))
        kw = k * jnp.exp(g_col - m_new)
        decay = jnp.exp(b_last + m_prev - m_new)
        c_ref[h] = decay * c_prev + _dot(kw.T.astype(BF16), v)
        n_ref[h:h + 1, :] = decay * n_prev + jnp.sum(kw, 0, keepdims=True)
        m_ref[h:h + 1, :] = jnp.broadcast_to(m_new, (1, LANE))


def _mlstm(u, ug, conv_w, conv_b, gate_b, norm_g, B, S):
    T = B * S
    L = MLSTM_L
    nc = S // L
    row = lambda b, c: b * nc + c
    return pl.pallas_call(
        _mlstm_kernel,
        grid=(B, nc),
        in_specs=[pl.BlockSpec((L, 2 * MLSTM_QK), lambda b, c: (row(b, c), 0)),
                  pl.BlockSpec((L, MLSTM_WIDTH), lambda b, c: (row(b, c), 1)),
                  pl.BlockSpec((L, MLSTM_WIDTH), lambda b, c: (row(b, c), 2)),
                  pl.BlockSpec((L, GATE_COLS), lambda b, c: (row(b, c), 0)),
                  pl.BlockSpec((CONV_WIDTH, 2 * MLSTM_QK), lambda b, c: (0, 0)),
                  pl.BlockSpec((1, 2 * MLSTM_QK), lambda b, c: (0, 0)),
                  pl.BlockSpec((1, LANE), lambda b, c: (0, 0)),
                  pl.BlockSpec((1, MLSTM_WIDTH), lambda b, c: (0, 0))],
        out_specs=pl.BlockSpec((L, MLSTM_WIDTH), lambda b, c: (row(b, c), 0)),
        out_shape=jax.ShapeDtypeStruct((T, MLSTM_WIDTH), BF16),
        scratch_shapes=[pltpu.VMEM((L, 2 * MLSTM_QK), F32),
                        pltpu.VMEM((MLSTM_HEADS, MLSTM_DQK, MLSTM_DV), F32),
                        pltpu.VMEM((8, MLSTM_DQK), F32),
                        pltpu.VMEM((8, LANE), F32)],
        compiler_params=_cparams(("parallel", "arbitrary")),
        name="mlstm",
    )(u, u, u, ug, conv_w, conv_b, gate_b, norm_g)


LOG2E = 1.4426950408889634
SLC_CHUNK = 1024
SLC_CHUNK_BLOCKS = SLC_CHUNK // SLC_BLOCK
VT_EXTRA = 16
N_FORCED = 3
CMP_ROW_STEP = 256


def _nsa_prep_kernel(q_ref, kv0_ref, kv1_ref, kv2_ref, pos_ref, invf_ref, sgn_ref,
                     qt_ref, kc_ref, vc_ref, ks_ref, vst_ref, kw_ref, vwt_ref):
    ang = pos_ref[...].astype(F32) * invf_ref[...]
    cos = jnp.cos(ang)
    sin = jnp.sin(ang) * sgn_ref[...]

    def rope(x):
        return x * cos + pltpu.roll(x, NSA_HEAD_DIM // 2, 1) * sin

    scale = NSA_HEAD_DIM ** -0.5 * LOG2E
    ts = q_ref.shape[0]
    tok = pl.program_id(0) * ts + lax.broadcasted_iota(I32, (ts, NSA_HEAD_DIM), 0)
    blk_lane = (tok // SLC_BLOCK) % SLC_CHUNK_BLOCKS
    blk_onehot = jnp.where(lax.broadcasted_iota(I32, (ts, NSA_HEAD_DIM), 1) == blk_lane, 1.0, 0.0).astype(BF16)
    ones_rows = jnp.where(lax.broadcasted_iota(I32, (VT_EXTRA, ts), 0) == 0, 1.0, 0.0).astype(BF16)
    head = lambda ref, c: ref[:, c:c + NSA_HEAD_DIM].astype(F32)
    for g in range(NSA_KV_HEADS):
        for h in range(NSA_GROUP):
            qt_ref[g, h] = (rope(head(q_ref, (g * NSA_GROUP + h) * NSA_HEAD_DIM)) * scale).T.astype(BF16)
        c0 = g * NSA_HEAD_DIM
        c1 = NSA_KV_HEADS * NSA_HEAD_DIM + g * NSA_HEAD_DIM
        kc_ref[g] = rope(head(kv0_ref, c0)).astype(BF16)
        vc_ref[g] = kv0_ref[:, c1:c1 + NSA_HEAD_DIM].astype(BF16)
        ks_ref[g, :, 0:NSA_HEAD_DIM] = rope(head(kv1_ref, c0)).astype(BF16)
        ks_ref[g, :, NSA_HEAD_DIM:2 * NSA_HEAD_DIM] = blk_onehot
        vst_ref[g, 0:NSA_HEAD_DIM, :] = head(kv1_ref, c1).T.astype(BF16)
        vst_ref[g, NSA_HEAD_DIM:NSA_HEAD_DIM + VT_EXTRA, :] = ones_rows
        kw_ref[g] = rope(head(kv2_ref, c0)).astype(BF16)
        vwt_ref[g] = head(kv2_ref, c1).T.astype(BF16)


def _nsa_prep(u, pos_col, invf2, sgn, T):
    ts = 512
    kvw = 2 * NSA_KV_HEADS * NSA_HEAD_DIM
    kv_spec = lambda n: pl.BlockSpec((ts, kvw), lambda i: (i, (U_NSA + NSA_WIDTH) // kvw + n))
    row_out = pl.BlockSpec((NSA_KV_HEADS, ts, NSA_HEAD_DIM), lambda i: (0, i, 0))
    row_shape = jax.ShapeDtypeStruct((NSA_KV_HEADS, T, NSA_HEAD_DIM), BF16)
    col_out = pl.BlockSpec((NSA_KV_HEADS, NSA_HEAD_DIM, ts), lambda i: (0, 0, i))
    col_shape = jax.ShapeDtypeStruct((NSA_KV_HEADS, NSA_HEAD_DIM, T), BF16)
    return pl.pallas_call(
        _nsa_prep_kernel,
        grid=(T // ts,),
        in_specs=[pl.BlockSpec((ts, NSA_WIDTH), lambda i: (i, U_NSA // NSA_WIDTH)),
                  kv_spec(0), kv_spec(1), kv_spec(2),
                  pl.BlockSpec((ts, 1), lambda i: (i, 0)),
                  pl.BlockSpec((1, LANE), lambda i: (0, 0)),
                  pl.BlockSpec((1, LANE), lambda i: (0, 0))],
        out_specs=[pl.BlockSpec((NSA_KV_HEADS, NSA_GROUP, NSA_HEAD_DIM, ts), lambda i: (0, 0, 0, i)),
                   row_out, row_out,
                   pl.BlockSpec((NSA_KV_HEADS, ts, 2 * NSA_HEAD_DIM), lambda i: (0, i, 0)),
                   pl.BlockSpec((NSA_KV_HEADS, NSA_HEAD_DIM + VT_EXTRA, ts), lambda i: (0, 0, i)),
                   row_out, col_out],
        out_shape=[jax.ShapeDtypeStruct((NSA_KV_HEADS, NSA_GROUP, NSA_HEAD_DIM, T), BF16),
                   row_shape, row_shape,
                   jax.ShapeDtypeStruct((NSA_KV_HEADS, T, 2 * NSA_HEAD_DIM), BF16),
                   jax.ShapeDtypeStruct((NSA_KV_HEADS, NSA_HEAD_DIM + VT_EXTRA, T), BF16),
                   row_shape, col_shape],
        compiler_params=_cparams(("parallel",)),
        name="nsa_prep",
    )(u, u, u, u, pos_col, invf2, sgn)


def _gelu_tanh(x):
    return 0.5 * x * (1.0 + jnp.tanh(0.7978845608028654 * (x + 0.044715 * x * x * x)))


def _compress_kernel(r_ref, w1_ref, pos_ref, w2_ref, out_ref, *, transpose_out):
    r = r_ref[0]
    nr = r.shape[0]
    a = _dot(r, w1_ref[0])
    b = _dot(r, w1_ref[1])
    c0 = _dot(pos_ref[0], w1_ref[0]) + _dot(pos_ref[1], w1_ref[1])
    pre = a + pltpu.roll(b, nr - 1, 0) + c0[0:1, :]
    out = _dot(_gelu_tanh(pre).astype(BF16), w2_ref[...])
    row = lax.broadcasted_iota(I32, (nr, 1), 0)
    out = jnp.where(row < nr - 1, out, 0.0)
    out_ref[0] = (out.T if transpose_out else out).astype(out_ref.dtype)


def _compress(kv, w1, pos8, w2, B, S, transpose_out):
    nr = S // CMP_STRIDE
    half = CMP_STRIDE * NSA_HEAD_DIM
    r = kv.reshape(NSA_KV_HEADS, B * nr, half)
    if transpose_out:
        out_spec = pl.BlockSpec((1, NSA_HEAD_DIM, nr), lambda g, b: (g, 0, b))
        out_shape = jax.ShapeDtypeStruct((NSA_KV_HEADS, NSA_HEAD_DIM, B * nr), BF16)
    else:
        out_spec = pl.BlockSpec((1, nr, NSA_HEAD_DIM), lambda g, b: (g, b, 0))
        out_shape = jax.ShapeDtypeStruct((NSA_KV_HEADS, B * nr, NSA_HEAD_DIM), BF16)
    return pl.pallas_call(
        functools.partial(_compress_kernel, transpose_out=transpose_out),
        grid=(NSA_KV_HEADS, B),
        in_specs=[pl.BlockSpec((1, nr, half), lambda g, b: (g, b, 0)),
                  pl.BlockSpec((2, half, CMP_HIDDEN), lambda g, b: (0, 0, 0)),
                  pl.BlockSpec((2, 8, half), lambda g, b: (0, 0, 0)),
                  pl.BlockSpec((CMP_HIDDEN, NSA_HEAD_DIM), lambda g, b: (0, 0))],
        out_specs=out_spec,
        out_shape=out_shape,
        compiler_params=_cparams(("parallel", "parallel")),
        name="nsa_compress",
    )(r, w1, pos8, w2)


NSA_COLS = NSA_GROUP * Q_BLOCK


def _load_qt(q_ref):
    return jnp.concatenate([q_ref[0, h] for h in range(NSA_GROUP)], 1)


def _store_heads(o_t, out_ref):
    for h in range(NSA_GROUP):
        out_ref[:, h * NSA_HEAD_DIM:(h + 1) * NSA_HEAD_DIM] = (
            o_t[:, h * Q_BLOCK:(h + 1) * Q_BLOCK].T.astype(out_ref.dtype))


def _tile_heads(x):
    return jnp.concatenate([x] * NSA_GROUP, 1)


def _nsa_cmp_kernel(q_ref, kc_ref, vct_ref, ovt_ref, wd_ref, oc_ref, nb_ref, wdb_ref, *, n_slc, topn):
    wdb_ref[...] = wd_ref[...].astype(BF16)
    i = pl.program_id(2)
    nc = kc_ref.shape[1]
    qt = _load_qt(q_ref)
    t = i * Q_BLOCK + lax.broadcasted_iota(I32, (1, Q_BLOCK), 1)
    cur = t // SLC_BLOCK
    step = min(CMP_ROW_STEP, nc)
    ratio = SLC_BLOCK // CMP_STRIDE

    def causal_prefix(rows):
        c_idx = lax.broadcasted_iota(I32, (rows, 1), 0)
        valid = (c_idx * CMP_STRIDE + CMP_BLOCK - 1 <= t) & (c_idx < nc - 1)
        s = _dot(kc_ref[0, 0:rows, :], qt) + _tile_heads(jnp.where(valid, 0.0, NEG))
        m = jnp.max(s, 0, keepdims=True)
        p = jnp.where(m > 0.5 * NEG, jnp.exp2(s - m), 0.0)
        p = p * (1.0 / jnp.maximum(jnp.sum(p, 0, keepdims=True), 1e-30))
        _store_heads(_dot(vct_ref[0, :, 0:rows], p.astype(BF16)), oc_ref)
        psum = p[:, 0:Q_BLOCK]
        for h in range(1, NSA_GROUP):
            psum = psum + p[:, h * Q_BLOCK:(h + 1) * Q_BLOCK]
        nb_rows = rows // ratio
        imp = sum(_dot(ovt_ref[0:nb_rows, 0:rows], part) for part in _split2(psum))

        blk = lax.broadcasted_iota(I32, (nb_rows, Q_BLOCK), 0)
        blk_f = blk.astype(F32)
        causal_blk = blk <= cur
        forced = (blk == 0) | (blk == cur) | (blk == cur - 1)
        score = jnp.where(causal_blk & jnp.logical_not(forced), imp, -jnp.inf)
        sel = jnp.where(causal_blk & forced, 1.0, 0.0)
        for _ in range(topn - N_FORCED):
            mx = jnp.max(score, 0, keepdims=True)
            first = jnp.min(jnp.where(score == mx, blk_f, float(n_slc)), 0, keepdims=True)
            hit = blk_f == first
            sel = jnp.where(hit, 1.0, sel)
            score = jnp.where(hit, -jnp.inf, score)
        nb_ref[0, 0, 0:nb_rows, :] = jnp.where(causal_blk & (sel > 0.5), 0.0, NEG)
        if nb_rows < n_slc:
            nb_ref[0, 0, nb_rows:n_slc, :] = jnp.full((n_slc - nb_rows, Q_BLOCK), NEG, F32)

    need = ((i + 1) * Q_BLOCK - CMP_BLOCK) // CMP_STRIDE + 1
    for v in range(nc // step):
        @pl.when((need + step - 1) // step == v + 1)
        def _():
            causal_prefix((v + 1) * step)


def _nsa_cmp(qt, kcc, vcct, ovt, w_down, B, S):
    T = B * S
    nq = S // Q_BLOCK
    nc = S // CMP_STRIDE
    n_slc = S // SLC_BLOCK
    wd2 = w_down.reshape(-1, w_down.shape[-1])
    slab = wd2.shape[0] // (NSA_KV_HEADS * B * nq)
    assert slab * NSA_KV_HEADS * B * nq == wd2.shape[0] and slab % (2 * SUBLANE) == 0
    step = lambda g, b, i: ((g * B + b) * nq + i, 0)
    kern = functools.partial(_nsa_cmp_kernel, n_slc=n_slc, topn=min(SLC_TOPN, n_slc))
    return pl.pallas_call(
        kern,
        grid=(NSA_KV_HEADS, B, nq),
        in_specs=[pl.BlockSpec((1, NSA_GROUP, NSA_HEAD_DIM, Q_BLOCK), lambda g, b, i: (g, 0, 0, b * nq + i)),
                  pl.BlockSpec((1, nc, NSA_HEAD_DIM), lambda g, b, i: (g, b, 0)),
                  pl.BlockSpec((1, NSA_HEAD_DIM, nc), lambda g, b, i: (g, 0, b)),
                  pl.BlockSpec((n_slc, nc), lambda g, b, i: (0, 0)),
                  pl.BlockSpec((slab, wd2.shape[1]), step)],
        out_specs=[pl.BlockSpec((Q_BLOCK, NSA_GROUP * NSA_HEAD_DIM), lambda g, b, i: (b * nq + i, g)),
                   pl.BlockSpec((1, 1, n_slc, Q_BLOCK), lambda g, b, i: (g, b * nq + i, 0, 0)),
                   pl.BlockSpec((slab, wd2.shape[1]), step)],
        out_shape=[jax.ShapeDtypeStruct((T, NSA_WIDTH), BF16),
                   jax.ShapeDtypeStruct((NSA_KV_HEADS, B * nq, n_slc, Q_BLOCK), F32),
                   jax.ShapeDtypeStruct(wd2.shape, BF16)],
        compiler_params=_cparams(("parallel", "parallel", "arbitrary")),
        name="nsa_cmp_select",
    )(qt, kcc, vcct, ovt, wd2)


SLC_QB = 2
SLC_Q = SLC_QB * Q_BLOCK
SLC_COLS = NSA_GROUP * SLC_Q
SLC_WIN_BLOCKS = WINDOW // Q_BLOCK + SLC_QB


def _store_slc_heads(o_t, out_ref):
    for h in range(NSA_GROUP):
        out_ref[:, h * NSA_HEAD_DIM:(h + 1) * NSA_HEAD_DIM] = (
            o_t[:, h * SLC_Q:(h + 1) * SLC_Q].T.astype(out_ref.dtype))


def _nsa_slc_kernel(q_ref, nb_ref, ks_ref, vst_ref, wg_ref, wu_ref, *refs):
    kw_refs = refs[:SLC_WIN_BLOCKS]
    vw_refs = refs[SLC_WIN_BLOCKS:2 * SLC_WIN_BLOCKS]
    os_ref, ow_ref, wgb_ref, wub_ref, m_ref, acc_ref, s_ref = refs[2 * SLC_WIN_BLOCKS:]
    wgb_ref[...] = wg_ref[...].astype(BF16)
    wub_ref[...] = wu_ref[...].astype(BF16)
    i = pl.program_id(2)
    qt = _load_qt(q_ref)
    t = i * SLC_Q + lax.broadcasted_iota(I32, (1, SLC_Q), 1)
    m_ref[...] = jnp.full_like(m_ref, NEG)
    acc_ref[...] = jnp.zeros_like(acc_ref)
    pad = jnp.zeros((NSA_HEAD_DIM - SLC_CHUNK_BLOCKS, SLC_Q), F32)

    def scores(c, slot):
        start = pl.multiple_of(c * SLC_CHUNK, SLC_CHUNK)
        k = ks_ref[0, pl.ds(start, SLC_CHUNK), :]
        blocks = pl.ds(pl.multiple_of(c * SLC_CHUNK_BLOCKS, SLC_CHUNK_BLOCKS), SLC_CHUNK_BLOCKS)
        nb = jnp.concatenate([nb_ref[0, qb, blocks, :] for qb in range(SLC_QB)], 1)
        mask_rows = jnp.concatenate([nb, pad], 0).astype(BF16)
        rhs = jnp.concatenate([qt, _tile_heads(mask_rows)], 0)
        s_ref[slot] = _dot(k, rhs)

    def absorb(c, slot, diagonal):
        start = pl.multiple_of(c * SLC_CHUNK, SLC_CHUNK)
        vt = vst_ref[0, :, pl.ds(start, SLC_CHUNK)]
        s = s_ref[slot]
        if diagonal:
            key = start + lax.broadcasted_iota(I32, (SLC_CHUNK, 1), 0)
            s = s + _tile_heads(jnp.where(key <= t, 0.0, NEG))
        m_old = m_ref[...]
        m_new = jnp.maximum(m_old, jnp.max(s, 0, keepdims=True))
        p = jnp.exp2(s - m_new).astype(BF16)
        acc_ref[...] = jnp.exp2(m_old - m_new) * acc_ref[...] + _dot(vt, p)
        m_ref[...] = m_new

    n_full = (i * SLC_Q) // SLC_CHUNK
    scores(0, 0)

    def window_branch():
        kw = jnp.concatenate([r[0] for r in kw_refs], 0)
        ones_rows = jnp.where(lax.broadcasted_iota(I32, (VT_EXTRA, kw.shape[0]), 0) == 0, 1.0, 0.0).astype(BF16)
        vwt = jnp.concatenate([jnp.concatenate([r[0] for r in vw_refs], 1), ones_rows], 0)
        wpos = (i * SLC_QB - WINDOW // Q_BLOCK) * Q_BLOCK + lax.broadcasted_iota(I32, (kw.shape[0], 1), 0)
        in_window = (wpos <= t) & (wpos > t - WINDOW) & (wpos >= 0)
        s_w = _dot(kw, qt) + _tile_heads(jnp.where(in_window, 0.0, NEG))
        p_w = jnp.exp2(s_w - jnp.max(s_w, 0, keepdims=True)).astype(BF16)
        o_w = _dot(vwt, p_w)
        _store_slc_heads(o_w[0:NSA_HEAD_DIM, :] * (1.0 / o_w[NSA_HEAD_DIM:NSA_HEAD_DIM + 1, :]), ow_ref)

    def body(j, carry):
        scores(2 * j + 1, 1)
        absorb(2 * j, 0, False)
        scores(2 * j + 2, 0)
        absorb(2 * j + 1, 1, False)
        return carry

    lax.fori_loop(0, n_full // 2, body, 0)

    @pl.when(n_full % 2 == 1)
    def _():
        scores(n_full, 1)
        absorb(n_full - 1, 0, False)
        window_branch()
        absorb(n_full, 1, True)

    @pl.when(n_full % 2 == 0)
    def _():
        window_branch()
        absorb(n_full, 0, True)

    ok = m_ref[...] > 0.5 * NEG
    l = acc_ref[NSA_HEAD_DIM:NSA_HEAD_DIM + 1, :]
    inv = jnp.where(ok, 1.0 / jnp.where(ok, l, 1.0), 0.0)
    _store_slc_heads(acc_ref[0:NSA_HEAD_DIM, :] * inv, os_ref)


def _nsa_slc(qt, nb, ks, vst, kw, vwt, w_gate, w_up, B, S):
    T = B * S
    nq = S // SLC_Q
    n_slc = S // SLC_BLOCK
    wg2 = w_gate.reshape(-1, w_gate.shape[-1])
    wu2 = w_up.reshape(-1, w_up.shape[-1])
    slab = wg2.shape[0] // (NSA_KV_HEADS * B * nq)
    assert slab * NSA_KV_HEADS * B * nq == wg2.shape[0] and slab % (2 * SUBLANE) == 0 and wu2.shape == wg2.shape
    step = lambda g, b, i: ((g * B + b) * nq + i, 0)
    resident = dict(pipeline_mode=pl.Buffered(1))
    nblk = S // Q_BLOCK

    def win_block(c):
        return lambda g, b, i: jnp.maximum(i * SLC_QB - WINDOW // Q_BLOCK + c, 0) + b * nblk

    kw_specs = [pl.BlockSpec((1, Q_BLOCK, NSA_HEAD_DIM), lambda g, b, i, f=win_block(c): (g, f(g, b, i), 0))
                for c in range(SLC_WIN_BLOCKS)]
    vw_specs = [pl.BlockSpec((1, NSA_HEAD_DIM, Q_BLOCK), lambda g, b, i, f=win_block(c): (g, 0, f(g, b, i)))
                for c in range(SLC_WIN_BLOCKS)]
    tok_out = pl.BlockSpec((SLC_Q, NSA_GROUP * NSA_HEAD_DIM), lambda g, b, i: (b * nq + i, g))
    return pl.pallas_call(
        _nsa_slc_kernel,
        grid=(NSA_KV_HEADS, B, nq),
        in_specs=[pl.BlockSpec((1, NSA_GROUP, NSA_HEAD_DIM, SLC_Q), lambda g, b, i: (g, 0, 0, b * nq + i)),
                  pl.BlockSpec((1, SLC_QB, n_slc, Q_BLOCK), lambda g, b, i: (g, b * nq + i, 0, 0)),
                  pl.BlockSpec((1, S, 2 * NSA_HEAD_DIM), lambda g, b, i: (g, b, 0), **resident),
                  pl.BlockSpec((1, NSA_HEAD_DIM + VT_EXTRA, S), lambda g, b, i: (g, 0, b), **resident),
                  pl.BlockSpec((slab, wg2.shape[1]), step), pl.BlockSpec((slab, wg2.shape[1]), step)]
        + kw_specs + vw_specs,
        out_specs=[tok_out, tok_out,
                   pl.BlockSpec((slab, wg2.shape[1]), step), pl.BlockSpec((slab, wg2.shape[1]), step)],
        out_shape=[jax.ShapeDtypeStruct((T, NSA_WIDTH), BF16), jax.ShapeDtypeStruct((T, NSA_WIDTH), BF16),
                   jax.ShapeDtypeStruct(wg2.shape, BF16), jax.ShapeDtypeStruct(wg2.shape, BF16)],
        scratch_shapes=[pltpu.VMEM((1, SLC_COLS), F32),
                        pltpu.VMEM((NSA_HEAD_DIM + VT_EXTRA, SLC_COLS), F32),
                        pltpu.VMEM((2, SLC_CHUNK, SLC_COLS), F32)],
        compiler_params=_cparams(("parallel", "parallel", "arbitrary")),
        name="nsa_selected",
    )(qt, nb, ks, vst, wg2, wu2, *([kw] * SLC_WIN_BLOCKS), *([vwt] * SLC_WIN_BLOCKS))


def _outproj_kernel(hm_ref, oc_ref, os_ref, ow_ref, gt_ref, gb_ref, ge_ref, h_ref, w_ref, g_ref, b_ref,
                    out_ref):
    gates = jax.nn.sigmoid(gt_ref[...] + gb_ref[...]).astype(BF16)
    gx = _dot(gates, ge_ref[...])
    hn = (gx[:, 0:NSA_WIDTH] * oc_ref[...].astype(F32)
          + gx[:, NSA_WIDTH:2 * NSA_WIDTH] * os_ref[...].astype(F32)
          + gx[:, 2 * NSA_WIDTH:3 * NSA_WIDTH] * ow_ref[...].astype(F32))
    mix = (_dot(hm_ref[...], w_ref[0:MLSTM_WIDTH, :])
           + _dot(hn.astype(BF16), w_ref[MLSTM_WIDTH:MLSTM_WIDTH + NSA_WIDTH, :]))
    out_ref[...] = _layer_norm(DN_ALPHA * h_ref[...] + mix, g_ref[...], b_ref[...])


def _outproj(hm, oc, os_, ow, ug, gate_b, gate_expand, h, w_out, g, b):
    T = h.shape[0]
    tm = 256
    row = lambda i: (i, 0)
    fixed = lambda i: (0, 0)
    return pl.pallas_call(
        _outproj_kernel,
        grid=(T // tm,),
        in_specs=[pl.BlockSpec((tm, MLSTM_WIDTH), row), pl.BlockSpec((tm, NSA_WIDTH), row),
                  pl.BlockSpec((tm, NSA_WIDTH), row), pl.BlockSpec((tm, NSA_WIDTH), row),
                  pl.BlockSpec((tm, GATE_COLS), row),
                  pl.BlockSpec((1, LANE), fixed),
                  pl.BlockSpec((LANE, 3 * NSA_WIDTH), fixed),
                  pl.BlockSpec((tm, D_MODEL), row),
                  pl.BlockSpec((D_MODEL, D_MODEL), fixed),
                  pl.BlockSpec((1, D_MODEL), fixed), pl.BlockSpec((1, D_MODEL), fixed)],
        out_specs=pl.BlockSpec((tm, D_MODEL), row),
        out_shape=jax.ShapeDtypeStruct((T, D_MODEL), F32),
        compiler_params=_cparams(("parallel",)),
        name="mixer_outproj_ln",
    )(hm, oc, os_, ow, ug, gate_b, gate_expand, h, w_out, g, b)


def _matmul_kernel(x_ref, w_ref, o_ref):
    o_ref[...] = _dot(x_ref[...].astype(BF16), w_ref[...]).astype(o_ref.dtype)


def _mem_kv(mem2, wkv):
    M = mem2.shape[0]
    N = wkv.shape[1]
    tn = 512
    return pl.pallas_call(
        _matmul_kernel,
        grid=(N // tn,),
        in_specs=[pl.BlockSpec((M, D_MODEL), lambda j: (0, 0)),
                  pl.BlockSpec((D_MODEL, tn), lambda j: (0, j))],
        out_specs=pl.BlockSpec((M, tn), lambda j: (0, j)),
        out_shape=jax.ShapeDtypeStruct((M, N), BF16),
        compiler_params=_cparams(("parallel",)),
        name="mem_kv_proj",
    )(mem2, wkv)


def _xattn_kernel(h_ref, wq_ref, kv_ref, o_ref):
    hb = h_ref[...].astype(BF16)
    for hd in range(XA_HEADS):
        c0 = hd * XA_HEAD_DIM
        q = (_dot(hb, wq_ref[:, c0:c0 + XA_HEAD_DIM]) * (XA_HEAD_DIM ** -0.5)).astype(BF16)
        s = _dot_nt(q, kv_ref[:, c0:c0 + XA_HEAD_DIM])
        m = jnp.max(s, -1, keepdims=True)
        p = jnp.exp(s - m)
        o = _dot(p.astype(BF16), kv_ref[:, D_MODEL + c0:D_MODEL + c0 + XA_HEAD_DIM])
        o_ref[:, c0:c0 + XA_HEAD_DIM] = (o / jnp.sum(p, -1, keepdims=True)).astype(o_ref.dtype)


def _xattn(h1, wq, kv, B, S):
    T = B * S
    tm = 512
    n_mem = kv.shape[0] // B
    per_b = S // tm
    return pl.pallas_call(
        _xattn_kernel,
        grid=(T // tm,),
        in_specs=[pl.BlockSpec((tm, D_MODEL), lambda i: (i, 0)),
                  pl.BlockSpec((D_MODEL, D_MODEL), lambda i: (0, 0)),
                  pl.BlockSpec((n_mem, 2 * D_MODEL), lambda i: (i // per_b, 0))],
        out_specs=pl.BlockSpec((tm, D_MODEL), lambda i: (i, 0)),
        out_shape=jax.ShapeDtypeStruct((T, D_MODEL), BF16),
        compiler_params=_cparams(("parallel",)),
        name="mem_xattn",
    )(h1, wq, kv)


def _xa_out_router_kernel(o_ref, wo_ref, h_ref, g_ref, b_ref, rw_ref, rb_ref,
                          h2_ref, idx_ref, wgt_ref):
    xa = _dot(o_ref[...], wo_ref[...])
    h2 = _layer_norm(DN_ALPHA * h_ref[...] + xa, g_ref[...], b_ref[...])
    h2_ref[...] = h2
    hi, lo = _split2(h2)
    logits = _dot(hi, rw_ref[0]) + _dot(hi, rw_ref[1]) + _dot(lo, rw_ref[0])
    scores = jax.nn.sigmoid(logits)
    lane = lax.broadcasted_iota(I32, scores.shape, 1)
    lane_f = lane.astype(F32)
    biased = jnp.where(lane < N_EXPERTS, scores + rb_ref[...], -jnp.inf)
    idx_mat = jnp.zeros(scores.shape, F32)
    w_mat = jnp.zeros(scores.shape, F32)
    for kk in range(TOP_K):
        mx = jnp.max(biased, -1, keepdims=True)
        first = jnp.min(jnp.where(biased == mx, lane_f, float(LANE)), -1, keepdims=True)
        hit = lane_f == first
        top_s = jnp.sum(jnp.where(hit, scores, 0.0), -1, keepdims=True)
        idx_mat = jnp.where(lane == kk, first, idx_mat)
        w_mat = jnp.where(lane == kk, top_s, w_mat)
        biased = jnp.where(hit, -jnp.inf, biased)
    idx_ref[...] = idx_mat.astype(I32)
    wgt_ref[...] = w_mat / jnp.sum(w_mat, -1, keepdims=True) * ROUTED_SCALE


def _xa_out_router(o, wo, h1, g, b, rw2, rb):
    T = h1.shape[0]
    tm = 256
    row = lambda i: (i, 0)
    fixed = lambda i: (0, 0)
    return pl.pallas_call(
        _xa_out_router_kernel,
        grid=(T // tm,),
        in_specs=[pl.BlockSpec((tm, D_MODEL), row),
                  pl.BlockSpec((D_MODEL, D_MODEL), fixed),
                  pl.BlockSpec((tm, D_MODEL), row),
                  pl.BlockSpec((1, D_MODEL), fixed), pl.BlockSpec((1, D_MODEL), fixed),
                  pl.BlockSpec((2, D_MODEL, LANE), lambda i: (0, 0, 0)),
                  pl.BlockSpec((1, LANE), fixed)],
        out_specs=[pl.BlockSpec((tm, D_MODEL), row), pl.BlockSpec((tm, LANE), row),
                   pl.BlockSpec((tm, LANE), row)],
        out_shape=[jax.ShapeDtypeStruct((T, D_MODEL), F32),
                   jax.ShapeDtypeStruct((T, LANE), I32),
                   jax.ShapeDtypeStruct((T, LANE), F32)],
        compiler_params=_cparams(("parallel",)),
        name="xattn_out_ln_router",
    )(o, wo, h1, g, b, rw2, rb)


ROUTE_TM = 512
ZERO_ROWS = 128


def _route_rank_kernel(idx_ref, rank_ref, cnt_ref):
    tm = idx_ref.shape[0]

    @pl.when(pl.program_id(0) == 0)
    def _():
        cnt_ref[...] = jnp.zeros_like(cnt_ref)

    idx = idx_ref[...]
    lane = lax.broadcasted_iota(I32, (tm, LANE), 1)
    hits = [lane == idx[:, kk:kk + 1] for kk in range(TOP_K)]
    onehot = sum(jnp.where(hit, 1.0, 0.0) for hit in hits)
    r_i = lax.broadcasted_iota(I32, (tm, tm), 0)
    c_i = lax.broadcasted_iota(I32, (tm, tm), 1)
    before = jnp.where(r_i > c_i, 1.0, 0.0).astype(BF16)
    rank = _dot(before, onehot.astype(BF16)) + cnt_ref[0:1, :]
    out = jnp.zeros((tm, LANE), F32)
    for kk in range(TOP_K):
        out = jnp.where(lane == kk, jnp.sum(jnp.where(hits[kk], rank, 0.0), -1, keepdims=True), out)
    rank_ref[...] = out.astype(I32)
    cnt_ref[0:1, :] = cnt_ref[0:1, :] + jnp.sum(onehot, 0, keepdims=True)


def _route_rank(top_idx):
    T = top_idx.shape[0]
    tm = ROUTE_TM
    return pl.pallas_call(
        _route_rank_kernel,
        grid=(T // tm,),
        in_specs=[pl.BlockSpec((tm, LANE), lambda i: (i, 0))],
        out_specs=[pl.BlockSpec((tm, LANE), lambda i: (i, 0)),
                   pl.BlockSpec((8, LANE), lambda i: (0, 0))],
        out_shape=[jax.ShapeDtypeStruct((T, LANE), I32), jax.ShapeDtypeStruct((8, LANE), F32)],
        compiler_params=_cparams(("arbitrary",)),
        name="moe_route_rank",
    )(top_idx)


def _load_route(i, dest_hbm, dest_smem, isem):
    cp = pltpu.make_async_copy(dest_hbm.at[i], dest_smem, isem)
    cp.start()
    cp.wait()


_FILL_SIZES = tuple(s for s in (ZERO_ROWS >> n for n in range(ZERO_ROWS.bit_length())) if s >= SUBLANE)


def _experts_kernel(be_ref, nu_ref, x_ref, wg_ref, wu_ref, wd_ref, y_ref):
    used = pl.program_id(0) < nu_ref[0]

    @pl.when(used)
    def _():
        xb = x_ref[...].astype(BF16)
        a = _dot(xb, wg_ref[0])
        act = (a * jax.nn.sigmoid(a)) * _dot(xb, wu_ref[0])
        y_ref[...] = _dot(act.astype(BF16), wd_ref[0])

    @pl.when(jnp.logical_not(used))
    def _():
        y_ref[...] = jnp.zeros_like(y_ref)


def _experts(block_e, n_used, xs, wg, wu, wd):
    bm = EXPERT_BM
    P = xs.shape[0] // bm * bm
    rowmap = lambda j, be, nu: (jnp.minimum(j, nu[0] - 1), 0)
    wmap = lambda j, be, nu: (be[j], 0, 0)
    return pl.pallas_call(
        _experts_kernel,
        grid_spec=pltpu.PrefetchScalarGridSpec(
            num_scalar_prefetch=2,
            grid=(P // bm,),
            in_specs=[pl.BlockSpec((bm, D_MODEL), rowmap),
                      pl.BlockSpec((1, D_MODEL, D_EXPERT), wmap),
                      pl.BlockSpec((1, D_MODEL, D_EXPERT), wmap),
                      pl.BlockSpec((1, D_EXPERT, D_MODEL), wmap)],
            out_specs=pl.BlockSpec((bm, D_MODEL), lambda j, be, nu: (j, 0))),
        out_shape=jax.ShapeDtypeStruct((P, D_MODEL), F32),
        compiler_params=_cparams(("arbitrary",)),
        name="moe_experts",
    )(block_e, n_used, xs, wg, wu, wd)


SHARED_TF = 256
SHARED_STEPS = D_SHARED // SHARED_TF
DISPATCH_SHARE = -(-ROUTE_TM // (SHARED_STEPS * SUBLANE)) * SUBLANE
DUMP_ROWS = (DISPATCH_SHARE * SHARED_STEPS - ROUTE_TM) * TOP_K


def _zero_fill(fs_ref, fn_ref, tail_ref, zero_ref, xs_ref, zsem):
    zero_ref[...] = jnp.zeros_like(zero_ref)

    def fill_copies(e, go):
        n = fn_ref[e]
        first = fs_ref[e]
        lead = jnp.minimum((-first) & (SUBLANE - 1), n)
        for r in range(SUBLANE - 1):
            @pl.when(r < lead)
            def _():
                go(pltpu.make_async_copy(zero_ref.at[pl.ds(0, 1)], xs_ref.at[pl.ds(first + r, 1)], zsem))

        rest = n - lead
        off = first + lead
        for size in _FILL_SIZES:
            take = (rest & size) != 0

            @pl.when(take)
            def _():
                dst = xs_ref.at[pl.ds(pl.multiple_of(off, SUBLANE), size)]
                go(pltpu.make_async_copy(zero_ref.at[pl.ds(0, size)], dst, zsem))
            off = off + jnp.where(take, size, 0)

    def tail_copies(j, go):
        off = pl.multiple_of(tail_ref[0] + j * ZERO_ROWS, ZERO_ROWS)
        go(pltpu.make_async_copy(zero_ref, xs_ref.at[pl.ds(off, ZERO_ROWS)], zsem))

    for go in (lambda cp: cp.start(), lambda cp: cp.wait()):
        lax.fori_loop(0, N_EXPERTS, lambda e, c, go=go: (fill_copies(e, go), c)[1], 0)
        lax.fori_loop(0, tail_ref[1], lambda j, c, go=go: (tail_copies(j, go), c)[1], 0)


def _shared_ffn_kernel(fs_ref, fn_ref, tail_ref, dest_hbm, x_ref, wg_ref, wu_ref, wd_ref,
                       o_ref, xs_ref, xb_ref, dest_smem, zero_ref, isem, sem, zsem, *, dump_base):
    i = pl.program_id(0)
    f = pl.program_id(1)

    @pl.when(f == 0)
    def _():
        _load_route(i, dest_hbm, dest_smem, isem)
        xb_ref[...] = x_ref[...].astype(BF16)
        o_ref[...] = jnp.zeros_like(o_ref)

    @pl.when((f == 0) & (i == 0))
    def _():
        _zero_fill(fs_ref, fn_ref, tail_ref, zero_ref, xs_ref, zsem)

    for rr in range(DISPATCH_SHARE):
        r = f * DISPATCH_SHARE + rr
        live = r < ROUTE_TM
        rc = jnp.minimum(r, ROUTE_TM - 1)
        for kk in range(TOP_K):
            d = jnp.where(live, dest_smem[rc * TOP_K + kk], dump_base + (r - ROUTE_TM) * TOP_K + kk)
            pltpu.make_async_copy(x_ref.at[pl.ds(rc, 1)], xs_ref.at[pl.ds(d, 1)], sem).start()

    xb = xb_ref[...]
    a = _dot(xb, wg_ref[...])
    act = (a * jax.nn.sigmoid(a)) * _dot(xb, wu_ref[...])
    o_ref[...] += _dot(act.astype(BF16), wd_ref[...])

    @pl.when(f == SHARED_STEPS - 1)
    def _():
        for _ in range(TOP_K):
            pltpu.make_async_copy(x_ref, xs_ref.at[pl.ds(0, ROUTE_TM)], sem).wait()
        pltpu.make_async_copy(x_ref.at[pl.ds(0, DUMP_ROWS)], xs_ref.at[pl.ds(0, DUMP_ROWS)], sem).wait()


def _shared_ffn_dispatch(fill_start, fill_n, tail, dest2, h2, wg, wu, wd, P):
    T = h2.shape[0]
    tm, tf = ROUTE_TM, SHARED_TF
    return pl.pallas_call(
        functools.partial(_shared_ffn_kernel, dump_base=P),
        grid_spec=pltpu.PrefetchScalarGridSpec(
            num_scalar_prefetch=3,
            grid=(T // tm, SHARED_STEPS),
            in_specs=[pl.BlockSpec(memory_space=pl.ANY),
                      pl.BlockSpec((tm, D_MODEL), lambda i, f, *_: (i, 0)),
                      pl.BlockSpec((D_MODEL, tf), lambda i, f, *_: (0, f)),
                      pl.BlockSpec((D_MODEL, tf), lambda i, f, *_: (0, f)),
                      pl.BlockSpec((tf, D_MODEL), lambda i, f, *_: (f, 0))],
            out_specs=[pl.BlockSpec((tm, D_MODEL), lambda i, f, *_: (i, 0)),
                       pl.BlockSpec(memory_space=pl.ANY)],
            scratch_shapes=[pltpu.VMEM((tm, D_MODEL), BF16),
                            pltpu.SMEM((tm * TOP_K,), I32),
                            pltpu.VMEM((ZERO_ROWS, D_MODEL), F32),
                            pltpu.SemaphoreType.DMA, pltpu.SemaphoreType.DMA, pltpu.SemaphoreType.DMA]),
        out_shape=[jax.ShapeDtypeStruct((T, D_MODEL), F32),
                   jax.ShapeDtypeStruct((P + DUMP_ROWS, D_MODEL), F32)],
        compiler_params=_cparams(("arbitrary", "arbitrary")),
        name="shared_ffn_dispatch",
    )(fill_start, fill_n, tail, dest2, h2, wg, wu, wd)


COMBINE_SUB = 128
COMBINE_NSUB = ROUTE_TM // COMBINE_SUB


def _combine_kernel(dest_hbm, ys_hbm, w_ref, sh_ref, h_ref, g_ref, b_ref, out_ref,
                    dest_smem, gbuf, isem, sem):
    i = pl.program_id(0)
    _load_route(i, dest_hbm, dest_smem, isem)

    def gather(sub):
        slot = sub % 2

        def issue(r, carry):
            for kk in range(TOP_K):
                d = dest_smem[(sub * COMBINE_SUB + r) * TOP_K + kk]
                pltpu.make_async_copy(ys_hbm.at[pl.ds(d, 1)], gbuf.at[slot, kk, pl.ds(r, 1)],
                                      sem.at[slot]).start()
            return carry

        lax.fori_loop(0, COMBINE_SUB, issue, 0, unroll=True)

    gather(0)
    for sub in range(COMBINE_NSUB):
        slot = sub % 2
        if sub + 1 < COMBINE_NSUB:
            gather(sub + 1)
        for kk in range(TOP_K):
            pltpu.make_async_copy(ys_hbm.at[pl.ds(0, COMBINE_SUB)], gbuf.at[slot, kk], sem.at[slot]).wait()
        rs = slice(sub * COMBINE_SUB, (sub + 1) * COMBINE_SUB)
        w = w_ref[rs, :]
        routed = w[:, 0:1] * gbuf[slot, 0]
        for kk in range(1, TOP_K):
            routed = routed + w[:, kk:kk + 1] * gbuf[slot, kk]
        z = DN_ALPHA * h_ref[rs, :] + (routed + sh_ref[rs, :])
        out_ref[rs, :] = _layer_norm(z, g_ref[...], b_ref[...])


def _combine(dest2, ys, top_w, sh, h2, g, b):
    T = h2.shape[0]
    tm = ROUTE_TM
    row = lambda i: (i, 0)
    fixed = lambda i: (0, 0)
    return pl.pallas_call(
        _combine_kernel,
        grid=(T // tm,),
        in_specs=[pl.BlockSpec(memory_space=pl.ANY), pl.BlockSpec(memory_space=pl.ANY),
                  pl.BlockSpec((tm, LANE), row), pl.BlockSpec((tm, D_MODEL), row),
                  pl.BlockSpec((tm, D_MODEL), row),
                  pl.BlockSpec((1, D_MODEL), fixed), pl.BlockSpec((1, D_MODEL), fixed)],
        out_specs=pl.BlockSpec((tm, D_MODEL), row),
        scratch_shapes=[pltpu.SMEM((tm * TOP_K,), I32),
                        pltpu.VMEM((2, TOP_K, COMBINE_SUB, D_MODEL), F32),
                        pltpu.SemaphoreType.DMA, pltpu.SemaphoreType.DMA((2,))],
        out_shape=jax.ShapeDtypeStruct((T, D_MODEL), F32),
        compiler_params=_cparams(("arbitrary",)),
        name="moe_combine_ln",
    )(dest2, ys, top_w, sh, h2, g, b)


def _route_plan(counts, T):
    bm = EXPERT_BM
    padded = (counts + bm - 1) // bm * bm
    pad_end = jnp.cumsum(padded)
    pad_start = pad_end - padded
    n_blocks = T * TOP_K // bm + N_EXPERTS
    first_row = jnp.arange(n_blocks, dtype=I32) * bm
    block_e = jnp.minimum(jnp.sum((pad_end[None, :] <= first_row[:, None]).astype(I32), 1), N_EXPERTS - 1)
    n_used = (pad_end[-1:] // bm).astype(I32)
    tail = jnp.concatenate([pad_end[-1:], (n_blocks - n_used) * (bm // ZERO_ROWS)]).astype(I32)
    return (pad_start.astype(I32), (pad_start + counts).astype(I32), (padded - counts).astype(I32), tail,
            block_e, n_used, n_blocks * bm)


def _overlap_matrix(S):
    n_cmp_rows = S // CMP_STRIDE
    n_slc = S // SLC_BLOCK
    c_lo = np.arange(n_cmp_rows)[:, None] * CMP_STRIDE
    j_lo = np.arange(n_slc)[None, :] * SLC_BLOCK
    ov = (c_lo <= j_lo + SLC_BLOCK - 1) & (c_lo + CMP_BLOCK - 1 >= j_lo)
    return jnp.asarray(ov.T.astype(np.float32), dtype=BF16)


def _gate_expand_matrix():
    ge = np.zeros((LANE, 3 * NSA_WIDTH), np.float32)
    for hh in range(NSA_HEADS):
        for br in range(3):
            ge[8 + hh * 3 + br, br * NSA_WIDTH + hh * NSA_HEAD_DIM:br * NSA_WIDTH + (hh + 1) * NSA_HEAD_DIM] = 1.0
    return jnp.asarray(ge, dtype=BF16)


def kernel(x, mem, positions, ln0_g, ln0_b, w_in, conv_w, conv_b, igate_b, fgate_b, mlstm_norm_g, cmp_pos, cmp_w1k, cmp_w2k, cmp_w1v, cmp_w2v, nsa_gate_b, w_out, ln1_g, ln1_b, xa_wq, xa_wk, xa_wv, xa_wo, ln2_g, ln2_b, router_w, router_bias, moe_w_gate, moe_w_up, moe_w_down, sh_w_gate, sh_w_up, sh_w_down, ln3_g, ln3_b):
    B, S, D = x.shape
    T = B * S
    assert D == D_MODEL and w_in.shape[0] == DEPTH == 1
    assert S % MLSTM_L == 0 and S % SLC_CHUNK == 0 and T % ROUTE_TM == 0 and (T * TOP_K) % EXPERT_BM == 0
    row = lambda a: a.reshape(1, -1)

    w = w_in[0]
    w_r = jnp.concatenate([w[:, :3072], w[:, 3080:5640]], 1).astype(BF16)
    w_g = jnp.concatenate([w[:, 3072:3080], w[:, 5640:5664],
                           jnp.zeros((D, GATE_COLS - 2 * MLSTM_HEADS - 3 * NSA_HEADS), F32)], 1).astype(BF16)
    gate_b = jnp.concatenate([igate_b[0], fgate_b[0], nsa_gate_b[0],
                              jnp.zeros((LANE - 2 * MLSTM_HEADS - 3 * NSA_HEADS,), F32)]).reshape(1, LANE)
    half = NSA_HEAD_DIM // 2
    inv_freq = ROPE_THETA ** (-jnp.arange(half, dtype=F32) / half)
    invf2 = jnp.concatenate([inv_freq, inv_freq]).reshape(1, LANE)
    sgn = jnp.concatenate([-jnp.ones((half,), F32), jnp.ones((half,), F32)]).reshape(1, LANE)
    pos8 = jnp.zeros((2, 8, CMP_STRIDE * NSA_HEAD_DIM), F32).at[:, 0, :].set(
        cmp_pos[0].reshape(2, CMP_STRIDE * NSA_HEAD_DIM)).astype(BF16)
    w1k = cmp_w1k[0].reshape(2, CMP_STRIDE * NSA_HEAD_DIM, CMP_HIDDEN).astype(BF16)
    w1v = cmp_w1v[0].reshape(2, CMP_STRIDE * NSA_HEAD_DIM, CMP_HIDDEN).astype(BF16)
    rw = jnp.pad(router_w[0], ((0, 0), (0, LANE - N_EXPERTS)))
    rw_hi = rw.astype(BF16)
    rw2 = jnp.stack([rw_hi, (rw - rw_hi.astype(F32)).astype(BF16)])
    rb = jnp.pad(router_bias[0], (0, LANE - N_EXPERTS)).reshape(1, LANE)

    h, u, ug = _ln_inproj(x.reshape(T, D), row(ln0_g), row(ln0_b), w_r, w_g)
    hm = _mlstm(u, ug, conv_w[0], row(conv_b[0]), gate_b, row(mlstm_norm_g[0]), B, S)
    qt, kc, vc, ks, vst, kw, vwt = _nsa_prep(u, positions.reshape(T, 1), invf2, sgn, T)
    kcc = _compress(kc, w1k, pos8, cmp_w2k[0].astype(BF16), B, S, False)
    vcct = _compress(vc, w1v, pos8, cmp_w2v[0].astype(BF16), B, S, True)
    oc, nb, wd_b = _nsa_cmp(qt, kcc, vcct, _overlap_matrix(S), moe_w_down[0], B, S)
    os_, ow, wg_b, wu_b = _nsa_slc(qt, nb, ks, vst, kw, vwt, moe_w_gate[0], moe_w_up[0], B, S)
    h1 = _outproj(hm, oc, os_, ow, ug, gate_b, _gate_expand_matrix(), h, w_out[0].astype(BF16),
                  row(ln1_g[0]), row(ln1_b[0]))

    wkv = jnp.concatenate([xa_wk[0], xa_wv[0]], 1).astype(BF16)
    kv = _mem_kv(mem.reshape(-1, D), wkv)
    xo = _xattn(h1, xa_wq[0].astype(BF16), kv, B, S)
    h2, top_idx, top_w = _xa_out_router(xo, xa_wo[0].astype(BF16), h1, row(ln2_g[0]), row(ln2_b[0]), rw2, rb)

    rank, cnt = _route_rank(top_idx)
    counts = cnt[0, :N_EXPERTS].astype(I32)
    pad_start, fill_start, fill_n, tail, block_e, n_used, P = _route_plan(counts, T)
    slot_e = top_idx[:, :TOP_K, None] == jnp.arange(N_EXPERTS, dtype=I32)[None, None, :]
    dest = rank[:, :TOP_K] + jnp.sum(jnp.where(slot_e, pad_start[None, None, :], 0), -1)
    dest2 = dest.reshape(T // ROUTE_TM, ROUTE_TM * TOP_K)
    sh, xs = _shared_ffn_dispatch(fill_start, fill_n, tail, dest2, h2,
                                  sh_w_gate[0].astype(BF16), sh_w_up[0].astype(BF16),
                                  sh_w_down[0].astype(BF16), P)
    ys = _experts(block_e, n_used, xs, wg_b.reshape(moe_w_gate.shape[1:]), wu_b.reshape(moe_w_up.shape[1:]),
                  wd_b.reshape(moe_w_down.shape[1:]))
    out = _combine(dest2, ys, top_w, sh, h2, row(ln3_g[0]), row(ln3_b[0]))
    return out.reshape(B, S, D)
```

```python
import functools

import numpy as np
import jax
import jax.numpy as jnp
from jax import lax
from jax.experimental import pallas as pl
from jax.experimental.pallas import tpu as pltpu

F32 = jnp.float32
BF16 = jnp.bfloat16
I32 = jnp.int32

D_MODEL = 2048
MLSTM_HEADS = 4
MLSTM_DV = 256
MLSTM_DQK = 128
MLSTM_QK = MLSTM_HEADS * MLSTM_DQK
MLSTM_WIDTH = MLSTM_HEADS * MLSTM_DV
CONV_WIDTH = 4
NSA_HEAD_DIM = 128
NSA_HEADS = 8
NSA_KV_HEADS = 2
NSA_GROUP = 4
NSA_WIDTH = NSA_HEADS * NSA_HEAD_DIM
CMP_BLOCK = 32
CMP_STRIDE = 16
CMP_HIDDEN = 256
SLC_BLOCK = 64
SLC_TOPN = 16
WINDOW = 512
Q_BLOCK = 128
XA_HEADS = 4
XA_HEAD_DIM = 512
N_EXPERTS = 64
TOP_K = 6
D_EXPERT = 1408
D_SHARED = 2816
ROUTED_SCALE = 2.446
ROPE_THETA = 10000.0
LN_EPS = 1e-5
DEPTH = 1
DN_ALPHA = (2.0 * DEPTH) ** 0.25

U_MLSTM = 0
U_NSA = 3072
U_COLS = 5632
GATE_COLS = 128

LANE = 128
SUBLANE = 8
NEG = -1e30
MLSTM_L = 256
VMEM_LIMIT = 56 * 1024 * 1024
EXPERT_BM = 256


def _cparams(sem):
    return pltpu.CompilerParams(dimension_semantics=sem, vmem_limit_bytes=VMEM_LIMIT)


def _dot(a, b):
    return jnp.dot(a, b, preferred_element_type=F32)


def _dot_nt(a, b):
    return lax.dot_general(a, b, (((1,), (1,)), ((), ())), preferred_element_type=F32)


def _layer_norm(z, g, b):
    mu = jnp.mean(z, -1, keepdims=True)
    zc = z - mu
    var = jnp.mean(zc * zc, -1, keepdims=True)
    return zc * lax.rsqrt(var + LN_EPS) * g + b


def _split3(x):
    hi = x.astype(BF16)
    r = x - hi.astype(F32)
    mid = r.astype(BF16)
    lo = (r - mid.astype(F32)).astype(BF16)
    return hi, mid, lo


def _split2(x):
    hi = x.astype(BF16)
    lo = (x - hi.astype(F32)).astype(BF16)
    return hi, lo


def _ln_inproj_kernel(x_ref, g_ref, b_ref, w_ref, wg_ref, h_ref, u_ref, ug_ref, hb_ref):
    @pl.when(pl.program_id(1) == 0)
    def _():
        hn = _layer_norm(x_ref[...], g_ref[...], b_ref[...])
        h_ref[...] = hn
        hb_ref[...] = hn.astype(BF16)
        ug_ref[...] = _dot(hb_ref[...], wg_ref[...])

    u_ref[...] = _dot(hb_ref[...], w_ref[...]).astype(u_ref.dtype)


def _ln_inproj(x2, g, b, w, w_gate):
    T = x2.shape[0]
    tm, tn = 1024, 512
    return pl.pallas_call(
        _ln_inproj_kernel,
        grid=(T // tm, U_COLS // tn),
        in_specs=[pl.BlockSpec((tm, D_MODEL), lambda i, j: (i, 0)),
                  pl.BlockSpec((1, D_MODEL), lambda i, j: (0, 0)),
                  pl.BlockSpec((1, D_MODEL), lambda i, j: (0, 0)),
                  pl.BlockSpec((D_MODEL, tn), lambda i, j: (0, j)),
                  pl.BlockSpec((D_MODEL, GATE_COLS), lambda i, j: (0, 0))],
        out_specs=[pl.BlockSpec((tm, D_MODEL), lambda i, j: (i, 0)),
                   pl.BlockSpec((tm, tn), lambda i, j: (i, j)),
                   pl.BlockSpec((tm, GATE_COLS), lambda i, j: (i, 0))],
        out_shape=[jax.ShapeDtypeStruct((T, D_MODEL), F32),
                   jax.ShapeDtypeStruct((T, U_COLS), BF16),
                   jax.ShapeDtypeStruct((T, GATE_COLS), F32)],
        scratch_shapes=[pltpu.VMEM((tm, D_MODEL), BF16)],
        compiler_params=_cparams(("parallel", "arbitrary")),
        name="ln_inproj",
    )(x2, g, b, w, w_gate)


def _log_sigmoid(x):
    return jnp.minimum(x, 0.0) - jnp.log1p(jnp.exp(-jnp.abs(x)))


def _mlstm_kernel(qk_ref, v_ref, o_ref, gt_ref, cw_ref, cb_ref, gb_ref, ng_ref, out_ref,
                  prev_ref, c_ref, n_ref, m_ref):
    L = MLSTM_L

    @pl.when(pl.program_id(1) == 0)
    def _():
        prev_ref[...] = jnp.zeros_like(prev_ref)
        c_ref[...] = jnp.zeros_like(c_ref)
        n_ref[...] = jnp.zeros_like(n_ref)
        m_ref[...] = jnp.zeros_like(m_ref)

    x = qk_ref[...].astype(F32)
    prev = prev_ref[...]
    row = lax.broadcasted_iota(I32, (L, 1), 0)
    cw = cw_ref[...]
    y = cb_ref[...] + cw[CONV_WIDTH - 1:CONV_WIDTH, :] * x
    for j in range(1, CONV_WIDTH):
        shifted = jnp.where(row < j, pltpu.roll(prev, j, 0), pltpu.roll(x, j, 0))
        y = y + cw[CONV_WIDTH - 1 - j:CONV_WIDTH - j, :] * shifted
    prev_ref[...] = x
    qk = y * jax.nn.sigmoid(y)

    gpre = gt_ref[...] + gb_ref[...]
    gpre_t = gpre.T
    r_i = lax.broadcasted_iota(I32, (L, L), 0)
    c_i = lax.broadcasted_iota(I32, (L, L), 1)
    causal = r_i >= c_i
    tril = jnp.where(causal, 1.0, 0.0).astype(BF16)
    triu = jnp.where(r_i <= c_i, 1.0, 0.0).astype(BF16)
    lf = _log_sigmoid(gpre)
    lf_t = _log_sigmoid(gpre_t)
    b_cols = sum(_dot(tril, part) for part in _split3(lf))
    b_rows = sum(_dot(part, triu) for part in _split3(lf_t))

    for h in range(MLSTM_HEADS):
        i_col = gpre[:, h:h + 1]
        i_row = gpre_t[h:h + 1, :]
        b_col = b_cols[:, MLSTM_HEADS + h:MLSTM_HEADS + h + 1]
        b_row = b_rows[MLSTM_HEADS + h:MLSTM_HEADS + h + 1, :]
        m_prev = m_ref[h:h + 1, 0:1]
        n_prev = n_ref[h:h + 1, :]
        c_prev = c_ref[h]

        q = qk[:, h * MLSTM_DQK:(h + 1) * MLSTM_DQK]
        k = qk[:, MLSTM_QK + h * MLSTM_DQK:MLSTM_QK + (h + 1) * MLSTM_DQK] * (MLSTM_DQK ** -0.5)
        v = v_ref[:, h * MLSTM_DV:(h + 1) * MLSTM_DV].astype(BF16)
        qb = q.astype(BF16)

        dmat = jnp.where(causal, b_col - b_row + i_row, -jnp.inf)
        inter = b_col + m_prev
        m_t = jnp.maximum(inter, jnp.max(dmat, -1, keepdims=True))
        s = _dot_nt(qb, k.astype(BF16)) * jnp.exp(dmat - m_t)
        a_inter = jnp.exp(inter - m_t)
        num = _dot(s.astype(BF16), v) + a_inter * _dot(qb, c_prev.astype(BF16))
        den = jnp.sum(s, -1, keepdims=True) + a_inter * jnp.sum(q * n_prev, -1, keepdims=True)
        hc = num / jnp.maximum(jnp.abs(den), jnp.exp(-m_t))

        mu = jnp.mean(hc, -1, keepdims=True)
        hcc = hc - mu
        var = jnp.mean(hcc * hcc, -1, keepdims=True)
        hn = hcc * lax.rsqrt(var + LN_EPS) * ng_ref[:, h * MLSTM_DV:(h + 1) * MLSTM_DV]
        og = jax.nn.sigmoid(o_ref[:, h * MLSTM_DV:(h + 1) * MLSTM_DV].astype(F32))
        out_ref[:, h * MLSTM_DV:(h + 1) * MLSTM_DV] = (og * hn).astype(out_ref.dtype)

        b_last = b_col[L - 1:L, :]
        g_col = b_last - b_col + i_col
        m_new = jnp.maximum(b_last + m_prev, jnp.max(g_col, 0, keepdims=True))
        kw = k * jnp.exp(g_col - m_new)
        decay = jnp.exp(b_last + m_prev - m_new)
        c_ref[h] = decay * c_prev + _dot(kw.T.astype(BF16), v)
        n_ref[h:h + 1, :] = decay * n_prev + jnp.sum(kw, 0, keepdims=True)
        m_ref[h:h + 1, :] = jnp.broadcast_to(m_new, (1, LANE))


def _mlstm(u, ug, conv_w, conv_b, gate_b, norm_g, B, S):
    T = B * S
    L = MLSTM_L
    nc = S // L
    row = lambda b, c: b * nc + c
    return pl.pallas_call(
        _mlstm_kernel,
        grid=(B, nc),
        in_specs=[pl.BlockSpec((L, 2 * MLSTM_QK), lambda b, c: (row(b, c), 0)),
                  pl.BlockSpec((L, MLSTM_WIDTH), lambda b, c: (row(b, c), 1)),
                  pl.BlockSpec((L, MLSTM_WIDTH), lambda b, c: (row(b, c), 2)),
                  pl.BlockSpec((L, GATE_COLS), lambda b, c: (row(b, c), 0)),
                  pl.BlockSpec((CONV_WIDTH, 2 * MLSTM_QK), lambda b, c: (0, 0)),
                  pl.BlockSpec((1, 2 * MLSTM_QK), lambda b, c: (0, 0)),
                  pl.BlockSpec((1, LANE), lambda b, c: (0, 0)),
                  pl.BlockSpec((1, MLSTM_WIDTH), lambda b, c: (0, 0))],
        out_specs=pl.BlockSpec((L, MLSTM_WIDTH), lambda b, c: (row(b, c), 0)),
        out_shape=jax.ShapeDtypeStruct((T, MLSTM_WIDTH), BF16),
        scratch_shapes=[pltpu.VMEM((L, 2 * MLSTM_QK), F32),
                        pltpu.VMEM((MLSTM_HEADS, MLSTM_DQK, MLSTM_DV), F32),
                        pltpu.VMEM((8, MLSTM_DQK), F32),
                        pltpu.VMEM((8, LANE), F32)],
        compiler_params=_cparams(("parallel", "arbitrary")),
        name="mlstm",
    )(u, u, u, ug, conv_w, conv_b, gate_b, norm_g)


LOG2E = 1.4426950408889634
SLC_CHUNK = 1024
SLC_CHUNK_BLOCKS = SLC_CHUNK // SLC_BLOCK
VT_EXTRA = 16
N_FORCED = 3
CMP_ROW_STEP = 256


def _nsa_prep_kernel(q_ref, kv0_ref, kv1_ref, kv2_ref, pos_ref, invf_ref, sgn_ref,
                     qt_ref, kc_ref, vc_ref, ks_ref, vst_ref, kw_ref, vwt_ref):
    ang = pos_ref[...].astype(F32) * invf_ref[...]
    cos = jnp.cos(ang)
    sin = jnp.sin(ang) * sgn_ref[...]

    def rope(x):
        return x * cos + pltpu.roll(x, NSA_HEAD_DIM // 2, 1) * sin

    scale = NSA_HEAD_DIM ** -0.5 * LOG2E
    ts = q_ref.shape[0]
    tok = pl.program_id(0) * ts + lax.broadcasted_iota(I32, (ts, NSA_HEAD_DIM), 0)
    blk_lane = (tok // SLC_BLOCK) % SLC_CHUNK_BLOCKS
    blk_onehot = jnp.where(lax.broadcasted_iota(I32, (ts, NSA_HEAD_DIM), 1) == blk_lane, 1.0, 0.0).astype(BF16)
    ones_rows = jnp.where(lax.broadcasted_iota(I32, (VT_EXTRA, ts), 0) == 0, 1.0, 0.0).astype(BF16)
    head = lambda ref, c: ref[:, c:c + NSA_HEAD_DIM].astype(F32)
    for g in range(NSA_KV_HEADS):
        for h in range(NSA_GROUP):
            qt_ref[g, h] = (rope(head(q_ref, (g * NSA_GROUP + h) * NSA_HEAD_DIM)) * scale).T.astype(BF16)
        c0 = g * NSA_HEAD_DIM
        c1 = NSA_KV_HEADS * NSA_HEAD_DIM + g * NSA_HEAD_DIM
        kc_ref[g] = rope(head(kv0_ref, c0)).astype(BF16)
        vc_ref[g] = kv0_ref[:, c1:c1 + NSA_HEAD_DIM].astype(BF16)
        ks_ref[g, :, 0:NSA_HEAD_DIM] = rope(head(kv1_ref, c0)).astype(BF16)
        ks_ref[g, :, NSA_HEAD_DIM:2 * NSA_HEAD_DIM] = blk_onehot
        vst_ref[g, 0:NSA_HEAD_DIM, :] = head(kv1_ref, c1).T.astype(BF16)
        vst_ref[g, NSA_HEAD_DIM:NSA_HEAD_DIM + VT_EXTRA, :] = ones_rows
        kw_ref[g] = rope(head(kv2_ref, c0)).astype(BF16)
        vwt_ref[g] = head(kv2_ref, c1).T.astype(BF16)


def _nsa_prep(u, pos_col, invf2, sgn, T):
    ts = 512
    kvw = 2 * NSA_KV_HEADS * NSA_HEAD_DIM
    kv_spec = lambda n: pl.BlockSpec((ts, kvw), lambda i: (i, (U_NSA + NSA_WIDTH) // kvw + n))
    row_out = pl.BlockSpec((NSA_KV_HEADS, ts, NSA_HEAD_DIM), lambda i: (0, i, 0))
    row_shape = jax.ShapeDtypeStruct((NSA_KV_HEADS, T, NSA_HEAD_DIM), BF16)
    col_out = pl.BlockSpec((NSA_KV_HEADS, NSA_HEAD_DIM, ts), lambda i: (0, 0, i))
    col_shape = jax.ShapeDtypeStruct((NSA_KV_HEADS, NSA_HEAD_DIM, T), BF16)
    return pl.pallas_call(
        _nsa_prep_kernel,
        grid=(T // ts,),
        in_specs=[pl.BlockSpec((ts, NSA_WIDTH), lambda i: (i, U_NSA // NSA_WIDTH)),
                  kv_spec(0), kv_spec(1), kv_spec(2),
                  pl.BlockSpec((ts, 1), lambda i: (i, 0)),
                  pl.BlockSpec((1, LANE), lambda i: (0, 0)),
                  pl.BlockSpec((1, LANE), lambda i: (0, 0))],
        out_specs=[pl.BlockSpec((NSA_KV_HEADS, NSA_GROUP, NSA_HEAD_DIM, ts), lambda i: (0, 0, 0, i)),
                   row_out, row_out,
                   pl.BlockSpec((NSA_KV_HEADS, ts, 2 * NSA_HEAD_DIM), lambda i: (0, i, 0)),
                   pl.BlockSpec((NSA_KV_HEADS, NSA_HEAD_DIM + VT_EXTRA, ts), lambda i: (0, 0, i)),
                   row_out, col_out],
        out_shape=[jax.ShapeDtypeStruct((NSA_KV_HEADS, NSA_GROUP, NSA_HEAD_DIM, T), BF16),
                   row_shape, row_shape,
                   jax.ShapeDtypeStruct((NSA_KV_HEADS, T, 2 * NSA_HEAD_DIM), BF16),
                   jax.ShapeDtypeStruct((NSA_KV_HEADS, NSA_HEAD_DIM + VT_EXTRA, T), BF16),
                   row_shape, col_shape],
        compiler_params=_cparams(("parallel",)),
        name="nsa_prep",
    )(u, u, u, u, pos_col, invf2, sgn)


def _gelu_tanh(x):
    return 0.5 * x * (1.0 + jnp.tanh(0.7978845608028654 * (x + 0.044715 * x * x * x)))


def _compress_kernel(r_ref, w1_ref, pos_ref, w2_ref, out_ref, *, transpose_out):
    r = r_ref[0]
    nr = r.shape[0]
    a = _dot(r, w1_ref[0])
    b = _dot(r, w1_ref[1])
    c0 = _dot(pos_ref[0], w1_ref[0]) + _dot(pos_ref[1], w1_ref[1])
    pre = a + pltpu.roll(b, nr - 1, 0) + c0[0:1, :]
    out = _dot(_gelu_tanh(pre).astype(BF16), w2_ref[...])
    row = lax.broadcasted_iota(I32, (nr, 1), 0)
    out = jnp.where(row < nr - 1, out, 0.0)
    out_ref[0] = (out.T if transpose_out else out).astype(out_ref.dtype)


def _compress(kv, w1, pos8, w2, B, S, transpose_out):
    nr = S // CMP_STRIDE
    half = CMP_STRIDE * NSA_HEAD_DIM
    r = kv.reshape(NSA_KV_HEADS, B * nr, half)
    if transpose_out:
        out_spec = pl.BlockSpec((1, NSA_HEAD_DIM, nr), lambda g, b: (g, 0, b))
        out_shape = jax.ShapeDtypeStruct((NSA_KV_HEADS, NSA_HEAD_DIM, B * nr), BF16)
    else:
        out_spec = pl.BlockSpec((1, nr, NSA_HEAD_DIM), lambda g, b: (g, b, 0))
        out_shape = jax.ShapeDtypeStruct((NSA_KV_HEADS, B * nr, NSA_HEAD_DIM), BF16)
    return pl.pallas_call(
        functools.partial(_compress_kernel, transpose_out=transpose_out),
        grid=(NSA_KV_HEADS, B),
        in_specs=[pl.BlockSpec((1, nr, half), lambda g, b: (g, b, 0)),
                  pl.BlockSpec((2, half, CMP_HIDDEN), lambda g, b: (0, 0, 0)),
                  pl.BlockSpec((2, 8, half), lambda g, b: (0, 0, 0)),
                  pl.BlockSpec((CMP_HIDDEN, NSA_HEAD_DIM), lambda g, b: (0, 0))],
        out_specs=out_spec,
        out_shape=out_shape,
        compiler_params=_cparams(("parallel", "parallel")),
        name="nsa_compress",
    )(r, w1, pos8, w2)


NSA_COLS = NSA_GROUP * Q_BLOCK


def _load_qt(q_ref):
    return jnp.concatenate([q_ref[0, h] for h in range(NSA_GROUP)], 1)


def _store_heads(o_t, out_ref):
    for h in range(NSA_GROUP):
        out_ref[:, h * NSA_HEAD_DIM:(h + 1) * NSA_HEAD_DIM] = (
            o_t[:, h * Q_BLOCK:(h + 1) * Q_BLOCK].T.astype(out_ref.dtype))


def _tile_heads(x):
    return jnp.concatenate([x] * NSA_GROUP, 1)


def _nsa_cmp_kernel(q_ref, kc_ref, vct_ref, ovt_ref, wd_ref, oc_ref, nb_ref, wdb_ref, *, n_slc, topn):
    wdb_ref[...] = wd_ref[...].astype(BF16)
    i = pl.program_id(2)
    nc = kc_ref.shape[1]
    qt = _load_qt(q_ref)
    t = i * Q_BLOCK + lax.broadcasted_iota(I32, (1, Q_BLOCK), 1)
    cur = t // SLC_BLOCK
    step = min(CMP_ROW_STEP, nc)
    ratio = SLC_BLOCK // CMP_STRIDE

    def causal_prefix(rows):
        c_idx = lax.broadcasted_iota(I32, (rows, 1), 0)
        valid = (c_idx * CMP_STRIDE + CMP_BLOCK - 1 <= t) & (c_idx < nc - 1)
        s = _dot(kc_ref[0, 0:rows, :], qt) + _tile_heads(jnp.where(valid, 0.0, NEG))
        m = jnp.max(s, 0, keepdims=True)
        p = jnp.where(m > 0.5 * NEG, jnp.exp2(s - m), 0.0)
        p = p * (1.0 / jnp.maximum(jnp.sum(p, 0, keepdims=True), 1e-30))
        _store_heads(_dot(vct_ref[0, :, 0:rows], p.astype(BF16)), oc_ref)
        psum = p[:, 0:Q_BLOCK]
        for h in range(1, NSA_GROUP):
            psum = psum + p[:, h * Q_BLOCK:(h + 1) * Q_BLOCK]
        nb_rows = rows // ratio
        imp = sum(_dot(ovt_ref[0:nb_rows, 0:rows], part) for part in _split2(psum))

        blk = lax.broadcasted_iota(I32, (nb_rows, Q_BLOCK), 0)
        blk_f = blk.astype(F32)
        causal_blk = blk <= cur
        forced = (blk == 0) | (blk == cur) | (blk == cur - 1)
        score = jnp.where(causal_blk & jnp.logical_not(forced), imp, -jnp.inf)
        sel = jnp.where(causal_blk & forced, 1.0, 0.0)
        for _ in range(topn - N_FORCED):
            mx = jnp.max(score, 0, keepdims=True)
            first = jnp.min(jnp.where(score == mx, blk_f, float(n_slc)), 0, keepdims=True)
            hit = blk_f == first
            sel = jnp.where(hit, 1.0, sel)
            score = jnp.where(hit, -jnp.inf, score)
        nb_ref[0, 0, 0:nb_rows, :] = jnp.where(causal_blk & (sel > 0.5), 0.0, NEG)
        if nb_rows < n_slc:
            nb_ref[0, 0, nb_rows:n_slc, :] = jnp.full((n_slc - nb_rows, Q_BLOCK), NEG, F32)

    need = ((i + 1) * Q_BLOCK - CMP_BLOCK) // CMP_STRIDE + 1
    for v in range(nc // step):
        @pl.when((need + step - 1) // step == v + 1)
        def _():
            causal_prefix((v + 1) * step)


def _nsa_cmp(qt, kcc, vcct, ovt, w_down, B, S):
    T = B * S
    nq = S // Q_BLOCK
    nc = S // CMP_STRIDE
    n_slc = S // SLC_BLOCK
    wd2 = w_down.reshape(-1, w_down.shape[-1])
    slab = wd2.shape[0] // (NSA_KV_HEADS * B * nq)
    assert slab * NSA_KV_HEADS * B * nq == wd2.shape[0] and slab % (2 * SUBLANE) == 0
    step = lambda g, b, i: ((g * B + b) * nq + i, 0)
    kern = functools.partial(_nsa_cmp_kernel, n_slc=n_slc, topn=min(SLC_TOPN, n_slc))
    return pl.pallas_call(
        kern,
        grid=(NSA_KV_HEADS, B, nq),
        in_specs=[pl.BlockSpec((1, NSA_GROUP, NSA_HEAD_DIM, Q_BLOCK), lambda g, b, i: (g, 0, 0, b * nq + i)),
                  pl.BlockSpec((1, nc, NSA_HEAD_DIM), lambda g, b, i: (g, b, 0)),
                  pl.BlockSpec((1, NSA_HEAD_DIM, nc), lambda g, b, i: (g, 0, b)),
                  pl.BlockSpec((n_slc, nc), lambda g, b, i: (0, 0)),
                  pl.BlockSpec((slab, wd2.shape[1]), step)],
        out_specs=[pl.BlockSpec((Q_BLOCK, NSA_GROUP * NSA_HEAD_DIM), lambda g, b, i: (b * nq + i, g)),
                   pl.BlockSpec((1, 1, n_slc, Q_BLOCK), lambda g, b, i: (g, b * nq + i, 0, 0)),
                   pl.BlockSpec((slab, wd2.shape[1]), step)],
        out_shape=[jax.ShapeDtypeStruct((T, NSA_WIDTH), BF16),
                   jax.ShapeDtypeStruct((NSA_KV_HEADS, B * nq, n_slc, Q_BLOCK), F32),
                   jax.ShapeDtypeStruct(wd2.shape, BF16)],
        compiler_params=_cparams(("parallel", "parallel", "arbitrary")),
        name="nsa_cmp_select",
    )(qt, kcc, vcct, ovt, wd2)


SLC_QB = 2
SLC_Q = SLC_QB * Q_BLOCK
SLC_COLS = NSA_GROUP * SLC_Q
SLC_WIN_BLOCKS = WINDOW // Q_BLOCK + SLC_QB


def _store_slc_heads(o_t, out_ref):
    for h in range(NSA_GROUP):
        out_ref[:, h * NSA_HEAD_DIM:(h + 1) * NSA_HEAD_DIM] = (
            o_t[:, h * SLC_Q:(h + 1) * SLC_Q].T.astype(out_ref.dtype))


def _nsa_slc_kernel(q_ref, nb_ref, ks_ref, vst_ref, wg_ref, wu_ref, *refs):
    kw_refs = refs[:SLC_WIN_BLOCKS]
    vw_refs = refs[SLC_WIN_BLOCKS:2 * SLC_WIN_BLOCKS]
    os_ref, ow_ref, wgb_ref, wub_ref, m_ref, acc_ref, s_ref = refs[2 * SLC_WIN_BLOCKS:]
    wgb_ref[...] = wg_ref[...].astype(BF16)
    wub_ref[...] = wu_ref[...].astype(BF16)
    i = pl.program_id(2)
    qt = _load_qt(q_ref)
    t = i * SLC_Q + lax.broadcasted_iota(I32, (1, SLC_Q), 1)
    m_ref[...] = jnp.full_like(m_ref, NEG)
    acc_ref[...] = jnp.zeros_like(acc_ref)
    pad = jnp.zeros((NSA_HEAD_DIM - SLC_CHUNK_BLOCKS, SLC_Q), F32)

    def scores(c, slot):
        start = pl.multiple_of(c * SLC_CHUNK, SLC_CHUNK)
        k = ks_ref[0, pl.ds(start, SLC_CHUNK), :]
        blocks = pl.ds(pl.multiple_of(c * SLC_CHUNK_BLOCKS, SLC_CHUNK_BLOCKS), SLC_CHUNK_BLOCKS)
        nb = jnp.concatenate([nb_ref[0, qb, blocks, :] for qb in range(SLC_QB)], 1)
        mask_rows = jnp.concatenate([nb, pad], 0).astype(BF16)
        rhs = jnp.concatenate([qt, _tile_heads(mask_rows)], 0)
        s_ref[slot] = _dot(k, rhs)

    def absorb(c, slot, diagonal):
        start = pl.multiple_of(c * SLC_CHUNK, SLC_CHUNK)
        vt = vst_ref[0, :, pl.ds(start, SLC_CHUNK)]
        s = s_ref[slot]
        if diagonal:
            key = start + lax.broadcasted_iota(I32, (SLC_CHUNK, 1), 0)
            s = s + _tile_heads(jnp.where(key <= t, 0.0, NEG))
        m_old = m_ref[...]
        m_new = jnp.maximum(m_old, jnp.max(s, 0, keepdims=True))
        p = jnp.exp2(s - m_new).astype(BF16)
        acc_ref[...] = jnp.exp2(m_old - m_new) * acc_ref[...] + _dot(vt, p)
        m_ref[...] = m_new

    n_full = (i * SLC_Q) // SLC_CHUNK
    scores(0, 0)

    def window_branch():
        kw = jnp.concatenate([r[0] for r in kw_refs], 0)
        ones_rows = jnp.where(lax.broadcasted_iota(I32, (VT_EXTRA, kw.shape[0]), 0) == 0, 1.0, 0.0).astype(BF16)
        vwt = jnp.concatenate([jnp.concatenate([r[0] for r in vw_refs], 1), ones_rows], 0)
        wpos = (i * SLC_QB - WINDOW // Q_BLOCK) * Q_BLOCK + lax.broadcasted_iota(I32, (kw.shape[0], 1), 0)
        in_window = (wpos <= t) & (wpos > t - WINDOW) & (wpos >= 0)
        s_w = _dot(kw, qt) + _tile_heads(jnp.where(in_window, 0.0, NEG))
        p_w = jnp.exp2(s_w - jnp.max(s_w, 0, keepdims=True)).astype(BF16)
        o_w = _dot(vwt, p_w)
        _store_slc_heads(o_w[0:NSA_HEAD_DIM, :] * (1.0 / o_w[NSA_HEAD_DIM:NSA_HEAD_DIM + 1, :]), ow_ref)

    def body(j, carry):
        scores(2 * j + 1, 1)
        absorb(2 * j, 0, False)
        scores(2 * j + 2, 0)
        absorb(2 * j + 1, 1, False)
        return carry

    lax.fori_loop(0, n_full // 2, body, 0)

    @pl.when(n_full % 2 == 1)
    def _():
        scores(n_full, 1)
        absorb(n_full - 1, 0, False)
        window_branch()
        absorb(n_full, 1, True)

    @pl.when(n_full % 2 == 0)
    def _():
        window_branch()
        absorb(n_full, 0, True)

    ok = m_ref[...] > 0.5 * NEG
    l = acc_ref[NSA_HEAD_DIM:NSA_HEAD_DIM + 1, :]
    inv = jnp.where(ok, 1.0 / jnp.where(ok, l, 1.0), 0.0)
    _store_slc_heads(acc_ref[0:NSA_HEAD_DIM, :] * inv, os_ref)


def _nsa_slc(qt, nb, ks, vst, kw, vwt, w_gate, w_up, B, S):
    T = B * S
    nq = S // SLC_Q
    n_slc = S // SLC_BLOCK
    wg2 = w_gate.reshape(-1, w_gate.shape[-1])
    wu2 = w_up.reshape(-1, w_up.shape[-1])
    slab = wg2.shape[0] // (NSA_KV_HEADS * B * nq)
    assert slab * NSA_KV_HEADS * B * nq == wg2.shape[0] and slab % (2 * SUBLANE) == 0 and wu2.shape == wg2.shape
    step = lambda g, b, i: ((g * B + b) * nq + i, 0)
    resident = dict(pipeline_mode=pl.Buffered(1))
    nblk = S // Q_BLOCK

    def win_block(c):
        return lambda g, b, i: jnp.maximum(i * SLC_QB - WINDOW // Q_BLOCK + c, 0) + b * nblk

    kw_specs = [pl.BlockSpec((1, Q_BLOCK, NSA_HEAD_DIM), lambda g, b, i, f=win_block(c): (g, f(g, b, i), 0))
                for c in range(SLC_WIN_BLOCKS)]
    vw_specs = [pl.BlockSpec((1, NSA_HEAD_DIM, Q_BLOCK), lambda g, b, i, f=win_block(c): (g, 0, f(g, b, i)))
                for c in range(SLC_WIN_BLOCKS)]
    tok_out = pl.BlockSpec((SLC_Q, NSA_GROUP * NSA_HEAD_DIM), lambda g, b, i: (b * nq + i, g))
    return pl.pallas_call(
        _nsa_slc_kernel,
        grid=(NSA_KV_HEADS, B, nq),
        in_specs=[pl.BlockSpec((1, NSA_GROUP, NSA_HEAD_DIM, SLC_Q), lambda g, b, i: (g, 0, 0, b * nq + i)),
                  pl.BlockSpec((1, SLC_QB, n_slc, Q_BLOCK), lambda g, b, i: (g, b * nq + i, 0, 0)),
                  pl.BlockSpec((1, S, 2 * NSA_HEAD_DIM), lambda g, b, i: (g, b, 0), **resident),
                  pl.BlockSpec((1, NSA_HEAD_DIM + VT_EXTRA, S), lambda g, b, i: (g, 0, b), **resident),
                  pl.BlockSpec((slab, wg2.shape[1]), step), pl.BlockSpec((slab, wg2.shape[1]), step)]
        + kw_specs + vw_specs,
        out_specs=[tok_out, tok_out,
                   pl.BlockSpec((slab, wg2.shape[1]), step), pl.BlockSpec((slab, wg2.shape[1]), step)],
        out_shape=[jax.ShapeDtypeStruct((T, NSA_WIDTH), BF16), jax.ShapeDtypeStruct((T, NSA_WIDTH), BF16),
                   jax.ShapeDtypeStruct(wg2.shape, BF16), jax.ShapeDtypeStruct(wg2.shape, BF16)],
        scratch_shapes=[pltpu.VMEM((1, SLC_COLS), F32),
                        pltpu.VMEM((NSA_HEAD_DIM + VT_EXTRA, SLC_COLS), F32),
                        pltpu.VMEM((2, SLC_CHUNK, SLC_COLS), F32)],
        compiler_params=_cparams(("parallel", "parallel", "arbitrary")),
        name="nsa_selected",
    )(qt, nb, ks, vst, wg2, wu2, *([kw] * SLC_WIN_BLOCKS), *([vwt] * SLC_WIN_BLOCKS))


def _outproj_kernel(hm_ref, oc_ref, os_ref, ow_ref, gt_ref, gb_ref, ge_ref, h_ref, w_ref, g_ref, b_ref,
                    out_ref):
    gates = jax.nn.sigmoid(gt_ref[...] + gb_ref[...]).astype(BF16)
    gx = _dot(gates, ge_ref[...])
    hn = (gx[:, 0:NSA_WIDTH] * oc_ref[...].astype(F32)
          + gx[:, NSA_WIDTH:2 * NSA_WIDTH] * os_ref[...].astype(F32)
          + gx[:, 2 * NSA_WIDTH:3 * NSA_WIDTH] * ow_ref[...].astype(F32))
    mix = (_dot(hm_ref[...], w_ref[0:MLSTM_WIDTH, :])
           + _dot(hn.astype(BF16), w_ref[MLSTM_WIDTH:MLSTM_WIDTH + NSA_WIDTH, :]))
    out_ref[...] = _layer_norm(DN_ALPHA * h_ref[...] + mix, g_ref[...], b_ref[...])


def _outproj(hm, oc, os_, ow, ug, gate_b, gate_expand, h, w_out, g, b):
    T = h.shape[0]
    tm = 256
    row = lambda i: (i, 0)
    fixed = lambda i: (0, 0)
    return pl.pallas_call(
        _outproj_kernel,
        grid=(T // tm,),
        in_specs=[pl.BlockSpec((tm, MLSTM_WIDTH), row), pl.BlockSpec((tm, NSA_WIDTH), row),
                  pl.BlockSpec((tm, NSA_WIDTH), row), pl.BlockSpec((tm, NSA_WIDTH), row),
                  pl.BlockSpec((tm, GATE_COLS), row),
                  pl.BlockSpec((1, LANE), fixed),
                  pl.BlockSpec((LANE, 3 * NSA_WIDTH), fixed),
                  pl.BlockSpec((tm, D_MODEL), row),
                  pl.BlockSpec((D_MODEL, D_MODEL), fixed),
                  pl.BlockSpec((1, D_MODEL), fixed), pl.BlockSpec((1, D_MODEL), fixed)],
        out_specs=pl.BlockSpec((tm, D_MODEL), row),
        out_shape=jax.ShapeDtypeStruct((T, D_MODEL), F32),
        compiler_params=_cparams(("parallel",)),
        name="mixer_outproj_ln",
    )(hm, oc, os_, ow, ug, gate_b, gate_expand, h, w_out, g, b)


def _matmul_kernel(x_ref, w_ref, o_ref):
    o_ref[...] = _dot(x_ref[...].astype(BF16), w_ref[...]).astype(o_ref.dtype)


def _mem_kv(mem2, wkv):
    M = mem2.shape[0]
    N = wkv.shape[1]
    tn = 512
    return pl.pallas_call(
        _matmul_kernel,
        grid=(N // tn,),
        in_specs=[pl.BlockSpec((M, D_MODEL), lambda j: (0, 0)),
                  pl.BlockSpec((D_MODEL, tn), lambda j: (0, j))],
        out_specs=pl.BlockSpec((M, tn), lambda j: (0, j)),
        out_shape=jax.ShapeDtypeStruct((M, N), BF16),
        compiler_params=_cparams(("parallel",)),
        name="mem_kv_proj",
    )(mem2, wkv)


def _xattn_kernel(h_ref, wq_ref, kv_ref, o_ref):
    hb = h_ref[...].astype(BF16)
    for hd in range(XA_HEADS):
        c0 = hd * XA_HEAD_DIM
        q = (_dot(hb, wq_ref[:, c0:c0 + XA_HEAD_DIM]) * (XA_HEAD_DIM ** -0.5)).astype(BF16)
        s = _dot_nt(q, kv_ref[:, c0:c0 + XA_HEAD_DIM])
        m = jnp.max(s, -1, keepdims=True)
        p = jnp.exp(s - m)
        o = _dot(p.astype(BF16), kv_ref[:, D_MODEL + c0:D_MODEL + c0 + XA_HEAD_DIM])
        o_ref[:, c0:c0 + XA_HEAD_DIM] = (o / jnp.sum(p, -1, keepdims=True)).astype(o_ref.dtype)


def _xattn(h1, wq, kv, B, S):
    T = B * S
    tm = 512
    n_mem = kv.shape[0] // B
    per_b = S // tm
    return pl.pallas_call(
        _xattn_kernel,
        grid=(T // tm,),
        in_specs=[pl.BlockSpec((tm, D_MODEL), lambda i: (i, 0)),
                  pl.BlockSpec((D_MODEL, D_MODEL), lambda i: (0, 0)),
                  pl.BlockSpec((n_mem, 2 * D_MODEL), lambda i: (i // per_b, 0))],
        out_specs=pl.BlockSpec((tm, D_MODEL), lambda i: (i, 0)),
        out_shape=jax.ShapeDtypeStruct((T, D_MODEL), BF16),
        compiler_params=_cparams(("parallel",)),
        name="mem_xattn",
    )(h1, wq, kv)


def _xa_out_router_kernel(o_ref, wo_ref, h_ref, g_ref, b_ref, rw_ref, rb_ref,
                          h2_ref, idx_ref, wgt_ref):
    xa = _dot(o_ref[...], wo_ref[...])
    h2 = _layer_norm(DN_ALPHA * h_ref[...] + xa, g_ref[...], b_ref[...])
    h2_ref[...] = h2
    hi, lo = _split2(h2)
    logits = _dot(hi, rw_ref[0]) + _dot(hi, rw_ref[1]) + _dot(lo, rw_ref[0])
    scores = jax.nn.sigmoid(logits)
    lane = lax.broadcasted_iota(I32, scores.shape, 1)
    lane_f = lane.astype(F32)
    biased = jnp.where(lane < N_EXPERTS, scores + rb_ref[...], -jnp.inf)
    idx_mat = jnp.zeros(scores.shape, F32)
    w_mat = jnp.zeros(scores.shape, F32)
    for kk in range(TOP_K):
        mx = jnp.max(biased, -1, keepdims=True)
        first = jnp.min(jnp.where(biased == mx, lane_f, float(LANE)), -1, keepdims=True)
        hit = lane_f == first
        top_s = jnp.sum(jnp.where(hit, scores, 0.0), -1, keepdims=True)
        idx_mat = jnp.where(lane == kk, first, idx_mat)
        w_mat = jnp.where(lane == kk, top_s, w_mat)
        biased = jnp.where(hit, -jnp.inf, biased)
    idx_ref[...] = idx_mat.astype(I32)
    wgt_ref[...] = w_mat / jnp.sum(w_mat, -1, keepdims=True) * ROUTED_SCALE


def _xa_out_router(o, wo, h1, g, b, rw2, rb):
    T = h1.shape[0]
    tm = 256
    row = lambda i: (i, 0)
    fixed = lambda i: (0, 0)
    return pl.pallas_call(
        _xa_out_router_kernel,
        grid=(T // tm,),
        in_specs=[pl.BlockSpec((tm, D_MODEL), row),
                  pl.BlockSpec((D_MODEL, D_MODEL), fixed),
                  pl.BlockSpec((tm, D_MODEL), row),
                  pl.BlockSpec((1, D_MODEL), fixed), pl.BlockSpec((1, D_MODEL), fixed),
                  pl.BlockSpec((2, D_MODEL, LANE), lambda i: (0, 0, 0)),
                  pl.BlockSpec((1, LANE), fixed)],
        out_specs=[pl.BlockSpec((tm, D_MODEL), row), pl.BlockSpec((tm, LANE), row),
                   pl.BlockSpec((tm, LANE), row)],
        out_shape=[jax.ShapeDtypeStruct((T, D_MODEL), F32),
                   jax.ShapeDtypeStruct((T, LANE), I32),
                   jax.ShapeDtypeStruct((T, LANE), F32)],
        compiler_params=_cparams(("parallel",)),
        name="xattn_out_ln_router",
    )(o, wo, h1, g, b, rw2, rb)


ROUTE_TM = 512
ZERO_ROWS = 128


def _route_rank_kernel(idx_ref, rank_ref, cnt_ref):
    tm = idx_ref.shape[0]

    @pl.when(pl.program_id(0) == 0)
    def _():
        cnt_ref[...] = jnp.zeros_like(cnt_ref)

    idx = idx_ref[...]
    lane = lax.broadcasted_iota(I32, (tm, LANE), 1)
    hits = [lane == idx[:, kk:kk + 1] for kk in range(TOP_K)]
    onehot = sum(jnp.where(hit, 1.0, 0.0) for hit in hits)
    r_i = lax.broadcasted_iota(I32, (tm, tm), 0)
    c_i = lax.broadcasted_iota(I32, (tm, tm), 1)
    before = jnp.where(r_i > c_i, 1.0, 0.0).astype(BF16)
    rank = _dot(before, onehot.astype(BF16)) + cnt_ref[0:1, :]
    out = jnp.zeros((tm, LANE), F32)
    for kk in range(TOP_K):
        out = jnp.where(lane == kk, jnp.sum(jnp.where(hits[kk], rank, 0.0), -1, keepdims=True), out)
    rank_ref[...] = out.astype(I32)
    cnt_ref[0:1, :] = cnt_ref[0:1, :] + jnp.sum(onehot, 0, keepdims=True)


def _route_rank(top_idx):
    T = top_idx.shape[0]
    tm = ROUTE_TM
    return pl.pallas_call(
        _route_rank_kernel,
        grid=(T // tm,),
        in_specs=[pl.BlockSpec((tm, LANE), lambda i: (i, 0))],
        out_specs=[pl.BlockSpec((tm, LANE), lambda i: (i, 0)),
                   pl.BlockSpec((8, LANE), lambda i: (0, 0))],
        out_shape=[jax.ShapeDtypeStruct((T, LANE), I32), jax.ShapeDtypeStruct((8, LANE), F32)],
        compiler_params=_cparams(("arbitrary",)),
        name="moe_route_rank",
    )(top_idx)


def _load_route(i, dest_hbm, dest_smem, isem):
    cp = pltpu.make_async_copy(dest_hbm.at[i], dest_smem, isem)
    cp.start()
    cp.wait()


_FILL_SIZES = tuple(s for s in (ZERO_ROWS >> n for n in range(ZERO_ROWS.bit_length())) if s >= SUBLANE)


def _experts_kernel(be_ref, nu_ref, x_ref, wg_ref, wu_ref, wd_ref, y_ref):
    used = pl.program_id(0) < nu_ref[0]

    @pl.when(used)
    def _():
        xb = x_ref[...].astype(BF16)
        a = _dot(xb, wg_ref[0])
        act = (a * jax.nn.sigmoid(a)) * _dot(xb, wu_ref[0])
        y_ref[...] = _dot(act.astype(BF16), wd_ref[0])

    @pl.when(jnp.logical_not(used))
    def _():
        y_ref[...] = jnp.zeros_like(y_ref)


def _experts(block_e, n_used, xs, wg, wu, wd):
    bm = EXPERT_BM
    P = xs.shape[0]
    rowmap = lambda j, be, nu: (jnp.minimum(j, nu[0] - 1), 0)
    wmap = lambda j, be, nu: (be[j], 0, 0)
    return pl.pallas_call(
        _experts_kernel,
        grid_spec=pltpu.PrefetchScalarGridSpec(
            num_scalar_prefetch=2,
            grid=(P // bm,),
            in_specs=[pl.BlockSpec((bm, D_MODEL), rowmap),
                      pl.BlockSpec((1, D_MODEL, D_EXPERT), wmap),
                      pl.BlockSpec((1, D_MODEL, D_EXPERT), wmap),
                      pl.BlockSpec((1, D_EXPERT, D_MODEL), wmap)],
            out_specs=pl.BlockSpec((bm, D_MODEL), lambda j, be, nu: (j, 0))),
        out_shape=jax.ShapeDtypeStruct((P, D_MODEL), F32),
        compiler_params=_cparams(("arbitrary",)),
        name="moe_experts",
    )(block_e, n_used, xs, wg, wu, wd)


SHARED_TF = 256
SHARED_STEPS = D_SHARED // SHARED_TF
DISPATCH_SHARE = -(-ROUTE_TM // (SHARED_STEPS * SUBLANE)) * SUBLANE
DISPATCH_LAST = ROUTE_TM - DISPATCH_SHARE * (SHARED_STEPS - 1)
assert 0 < DISPATCH_LAST <= DISPATCH_SHARE


def _zero_fill(fs_ref, fn_ref, tail_ref, zero_ref, xs_ref, zsem):
    zero_ref[...] = jnp.zeros_like(zero_ref)

    def fill_copies(e, go):
        n = fn_ref[e]
        first = fs_ref[e]
        lead = jnp.minimum((-first) & (SUBLANE - 1), n)
        for r in range(SUBLANE - 1):
            @pl.when(r < lead)
            def _():
                go(pltpu.make_async_copy(zero_ref.at[pl.ds(0, 1)], xs_ref.at[pl.ds(first + r, 1)], zsem))

        rest = n - lead
        off = first + lead
        for size in _FILL_SIZES:
            take = (rest & size) != 0

            @pl.when(take)
            def _():
                dst = xs_ref.at[pl.ds(pl.multiple_of(off, SUBLANE), size)]
                go(pltpu.make_async_copy(zero_ref.at[pl.ds(0, size)], dst, zsem))
            off = off + jnp.where(take, size, 0)

    def tail_copies(j, go):
        off = pl.multiple_of(tail_ref[0] + j * ZERO_ROWS, ZERO_ROWS)
        go(pltpu.make_async_copy(zero_ref, xs_ref.at[pl.ds(off, ZERO_ROWS)], zsem))

    for go in (lambda cp: cp.start(), lambda cp: cp.wait()):
        lax.fori_loop(0, N_EXPERTS, lambda e, c, go=go: (fill_copies(e, go), c)[1], 0)
        lax.fori_loop(0, tail_ref[1], lambda j, c, go=go: (tail_copies(j, go), c)[1], 0)


def _shared_ffn_kernel(fs_ref, fn_ref, tail_ref, dest_hbm, x_ref, wg_ref, wu_ref, wd_ref,
                       o_ref, xs_ref, xb_ref, dest_smem, zero_ref, isem, sem, zsem):
    i = pl.program_id(0)
    f = pl.program_id(1)

    @pl.when(f == 0)
    def _():
        _load_route(i, dest_hbm, dest_smem, isem)
        xb_ref[...] = x_ref[...].astype(BF16)
        o_ref[...] = jnp.zeros_like(o_ref)

    @pl.when((f == 0) & (i == 0))
    def _():
        _zero_fill(fs_ref, fn_ref, tail_ref, zero_ref, xs_ref, zsem)

    def send_rows(count):
        for rr in range(count):
            r = f * DISPATCH_SHARE + rr
            for kk in range(TOP_K):
                d = dest_smem[r * TOP_K + kk]
                pltpu.make_async_copy(x_ref.at[pl.ds(r, 1)], xs_ref.at[pl.ds(d, 1)], sem).start()

    @pl.when(f < SHARED_STEPS - 1)
    def _():
        send_rows(DISPATCH_SHARE)

    @pl.when(f == SHARED_STEPS - 1)
    def _():
        send_rows(DISPATCH_LAST)

    xb = xb_ref[...]
    a = _dot(xb, wg_ref[...])
    act = (a * jax.nn.sigmoid(a)) * _dot(xb, wu_ref[...])
    o_ref[...] += _dot(act.astype(BF16), wd_ref[...])

    @pl.when(f == SHARED_STEPS - 1)
    def _():
        for _ in range(TOP_K):
            pltpu.make_async_copy(x_ref, xs_ref.at[pl.ds(0, ROUTE_TM)], sem).wait()


def _shared_ffn_dispatch(fill_start, fill_n, tail, dest2, h2, wg, wu, wd, P):
    T = h2.shape[0]
    tm, tf = ROUTE_TM, SHARED_TF
    return pl.pallas_call(
        _shared_ffn_kernel,
        grid_spec=pltpu.PrefetchScalarGridSpec(
            num_scalar_prefetch=3,
            grid=(T // tm, SHARED_STEPS),
            in_specs=[pl.BlockSpec(memory_space=pl.ANY),
                      pl.BlockSpec((tm, D_MODEL), lambda i, f, *_: (i, 0)),
                      pl.BlockSpec((D_MODEL, tf), lambda i, f, *_: (0, f)),
                      pl.BlockSpec((D_MODEL, tf), lambda i, f, *_: (0, f)),
                      pl.BlockSpec((tf, D_MODEL), lambda i, f, *_: (f, 0))],
            out_specs=[pl.BlockSpec((tm, D_MODEL), lambda i, f, *_: (i, 0)),
                       pl.BlockSpec(memory_space=pl.ANY)],
            scratch_shapes=[pltpu.VMEM((tm, D_MODEL), BF16),
                            pltpu.SMEM((tm * TOP_K,), I32),
                            pltpu.VMEM((ZERO_ROWS, D_MODEL), F32),
                            pltpu.SemaphoreType.DMA, pltpu.SemaphoreType.DMA, pltpu.SemaphoreType.DMA]),
        out_shape=[jax.ShapeDtypeStruct((T, D_MODEL), F32),
                   jax.ShapeDtypeStruct((P, D_MODEL), F32)],
        compiler_params=_cparams(("arbitrary", "arbitrary")),
        name="shared_ffn_dispatch",
    )(fill_start, fill_n, tail, dest2, h2, wg, wu, wd)


COMBINE_SUB = 128
COMBINE_NSUB = ROUTE_TM // COMBINE_SUB


def _combine_kernel(dest_hbm, ys_hbm, w_ref, sh_ref, h_ref, g_ref, b_ref, out_ref,
                    dest_smem, gbuf, isem, sem):
    i = pl.program_id(0)
    _load_route(i, dest_hbm, dest_smem, isem)

    def gather(sub):
        slot = sub % 2

        def issue(r, carry):
            for kk in range(TOP_K):
                d = dest_smem[(sub * COMBINE_SUB + r) * TOP_K + kk]
                pltpu.make_async_copy(ys_hbm.at[pl.ds(d, 1)], gbuf.at[slot, kk, pl.ds(r, 1)],
                                      sem.at[slot]).start()
            return carry

        lax.fori_loop(0, COMBINE_SUB, issue, 0, unroll=True)

    gather(0)
    for sub in range(COMBINE_NSUB):
        slot = sub % 2
        if sub + 1 < COMBINE_NSUB:
            gather(sub + 1)
        for kk in range(TOP_K):
            pltpu.make_async_copy(ys_hbm.at[pl.ds(0, COMBINE_SUB)], gbuf.at[slot, kk], sem.at[slot]).wait()
        rs = slice(sub * COMBINE_SUB, (sub + 1) * COMBINE_SUB)
        w = w_ref[rs, :]
        routed = w[:, 0:1] * gbuf[slot, 0]
        for kk in range(1, TOP_K):
            routed = routed + w[:, kk:kk + 1] * gbuf[slot, kk]
        z = DN_ALPHA * h_ref[rs, :] + (routed + sh_ref[rs, :])
        out_ref[rs, :] = _layer_norm(z, g_ref[...], b_ref[...])


def _combine(dest2, ys, top_w, sh, h2, g, b):
    T = h2.shape[0]
    tm = ROUTE_TM
    row = lambda i: (i, 0)
    fixed = lambda i: (0, 0)
    return pl.pallas_call(
        _combine_kernel,
        grid=(T // tm,),
        in_specs=[pl.BlockSpec(memory_space=pl.ANY), pl.BlockSpec(memory_space=pl.ANY),
                  pl.BlockSpec((tm, LANE), row), pl.BlockSpec((tm, D_MODEL), row),
                  pl.BlockSpec((tm, D_MODEL), row),
                  pl.BlockSpec((1, D_MODEL), fixed), pl.BlockSpec((1, D_MODEL), fixed)],
        out_specs=pl.BlockSpec((tm, D_MODEL), row),
        scratch_shapes=[pltpu.SMEM((tm * TOP_K,), I32),
                        pltpu.VMEM((2, TOP_K, COMBINE_SUB, D_MODEL), F32),
                        pltpu.SemaphoreType.DMA, pltpu.SemaphoreType.DMA((2,))],
        out_shape=jax.ShapeDtypeStruct((T, D_MODEL), F32),
        compiler_params=_cparams(("arbitrary",)),
        name="moe_combine_ln",
    )(dest2, ys, top_w, sh, h2, g, b)


def _route_plan(counts, T):
    bm = EXPERT_BM
    padded = (counts + bm - 1) // bm * bm
    pad_end = jnp.cumsum(padded)
    pad_start = pad_end - padded
    n_blocks = T * TOP_K // bm + N_EXPERTS
    first_row = jnp.arange(n_blocks, dtype=I32) * bm
    block_e = jnp.minimum(jnp.sum((pad_end[None, :] <= first_row[:, None]).astype(I32), 1), N_EXPERTS - 1)
    n_used = (pad_end[-1:] // bm).astype(I32)
    tail = jnp.concatenate([pad_end[-1:], (n_blocks - n_used) * (bm // ZERO_ROWS)]).astype(I32)
    return (pad_start.astype(I32), (pad_start + counts).astype(I32), (padded - counts).astype(I32), tail,
            block_e, n_used, n_blocks * bm)


def _overlap_matrix(S):
    n_cmp_rows = S // CMP_STRIDE
    n_slc = S // SLC_BLOCK
    c_lo = np.arange(n_cmp_rows)[:, None] * CMP_STRIDE
    j_lo = np.arange(n_slc)[None, :] * SLC_BLOCK
    ov = (c_lo <= j_lo + SLC_BLOCK - 1) & (c_lo + CMP_BLOCK - 1 >= j_lo)
    return jnp.asarray(ov.T.astype(np.float32), dtype=BF16)


def _gate_expand_matrix():
    ge = np.zeros((LANE, 3 * NSA_WIDTH), np.float32)
    for hh in range(NSA_HEADS):
        for br in range(3):
            ge[8 + hh * 3 + br, br * NSA_WIDTH + hh * NSA_HEAD_DIM:br * NSA_WIDTH + (hh + 1) * NSA_HEAD_DIM] = 1.0
    return jnp.asarray(ge, dtype=BF16)


def kernel(x, mem, positions, ln0_g, ln0_b, w_in, conv_w, conv_b, igate_b, fgate_b, mlstm_norm_g, cmp_pos, cmp_w1k, cmp_w2k, cmp_w1v, cmp_w2v, nsa_gate_b, w_out, ln1_g, ln1_b, xa_wq, xa_wk, xa_wv, xa_wo, ln2_g, ln2_b, router_w, router_bias, moe_w_gate, moe_w_up, moe_w_down, sh_w_gate, sh_w_up, sh_w_down, ln3_g, ln3_b):
    B, S, D = x.shape
    T = B * S
    assert D == D_MODEL and w_in.shape[0] == DEPTH == 1
    assert S % MLSTM_L == 0 and S % SLC_CHUNK == 0 and T % ROUTE_TM == 0 and (T * TOP_K) % EXPERT_BM == 0
    row = lambda a: a.reshape(1, -1)

    w = w_in[0]
    w_r = jnp.concatenate([w[:, :3072], w[:, 3080:5640]], 1).astype(BF16)
    w_g = jnp.concatenate([w[:, 3072:3080], w[:, 5640:5664],
                           jnp.zeros((D, GATE_COLS - 2 * MLSTM_HEADS - 3 * NSA_HEADS), F32)], 1).astype(BF16)
    gate_b = jnp.concatenate([igate_b[0], fgate_b[0], nsa_gate_b[0],
                              jnp.zeros((LANE - 2 * MLSTM_HEADS - 3 * NSA_HEADS,), F32)]).reshape(1, LANE)
    half = NSA_HEAD_DIM // 2
    inv_freq = ROPE_THETA ** (-jnp.arange(half, dtype=F32) / half)
    invf2 = jnp.concatenate([inv_freq, inv_freq]).reshape(1, LANE)
    sgn = jnp.concatenate([-jnp.ones((half,), F32), jnp.ones((half,), F32)]).reshape(1, LANE)
    pos8 = jnp.zeros((2, 8, CMP_STRIDE * NSA_HEAD_DIM), F32).at[:, 0, :].set(
        cmp_pos[0].reshape(2, CMP_STRIDE * NSA_HEAD_DIM)).astype(BF16)
    w1k = cmp_w1k[0].reshape(2, CMP_STRIDE * NSA_HEAD_DIM, CMP_HIDDEN).astype(BF16)
    w1v = cmp_w1v[0].reshape(2, CMP_STRIDE * NSA_HEAD_DIM, CMP_HIDDEN).astype(BF16)
    rw = jnp.pad(router_w[0], ((0, 0), (0, LANE - N_EXPERTS)))
    rw_hi = rw.astype(BF16)
    rw2 = jnp.stack([rw_hi, (rw - rw_hi.astype(F32)).astype(BF16)])
    rb = jnp.pad(router_bias[0], (0, LANE - N_EXPERTS)).reshape(1, LANE)

    h, u, ug = _ln_inproj(x.reshape(T, D), row(ln0_g), row(ln0_b), w_r, w_g)
    hm = _mlstm(u, ug, conv_w[0], row(conv_b[0]), gate_b, row(mlstm_norm_g[0]), B, S)
    qt, kc, vc, ks, vst, kw, vwt = _nsa_prep(u, positions.reshape(T, 1), invf2, sgn, T)
    kcc = _compress(kc, w1k, pos8, cmp_w2k[0].astype(BF16), B, S, False)
    vcct = _compress(vc, w1v, pos8, cmp_w2v[0].astype(BF16), B, S, True)
    oc, nb, wd_b = _nsa_cmp(qt, kcc, vcct, _overlap_matrix(S), moe_w_down[0], B, S)
    os_, ow, wg_b, wu_b = _nsa_slc(qt, nb, ks, vst, kw, vwt, moe_w_gate[0], moe_w_up[0], B, S)
    h1 = _outproj(hm, oc, os_, ow, ug, gate_b, _gate_expand_matrix(), h, w_out[0].astype(BF16),
                  row(ln1_g[0]), row(ln1_b[0]))

    wkv = jnp.concatenate([xa_wk[0], xa_wv[0]], 1).astype(BF16)
    kv = _mem_kv(mem.reshape(-1, D), wkv)
    xo = _xattn(h1, xa_wq[0].astype(BF16), kv, B, S)
    h2, top_idx, top_w = _xa_out_router(xo, xa_wo[0].astype(BF16), h1, row(ln2_g[0]), row(ln2_b[0]), rw2, rb)

    rank, cnt = _route_rank(top_idx)
    counts = cnt[0, :N_EXPERTS].astype(I32)
    pad_start, fill_start, fill_n, tail, block_e, n_used, P = _route_plan(counts, T)
    slot_e = top_idx[:, :TOP_K, None] == jnp.arange(N_EXPERTS, dtype=I32)[None, None, :]
    dest = rank[:, :TOP_K] + jnp.sum(jnp.where(slot_e, pad_start[None, None, :], 0), -1)
    dest2 = dest.reshape(T // ROUTE_TM, ROUTE_TM * TOP_K)
    sh, xs = _shared_ffn_dispatch(fill_start, fill_n, tail, dest2, h2,
                                  sh_w_gate[0].astype(BF16), sh_w_up[0].astype(BF16),
                                  sh_w_down[0].astype(BF16), P)
    ys = _experts(block_e, n_used, xs, wg_b.reshape(moe_w_gate.shape[1:]), wu_b.reshape(moe_w_up.shape[1:]),
                  wd_b.reshape(moe_w_down.shape[1:]))
    out = _combine(dest2, ys, top_w, sh, h2, row(ln3_g[0]), row(ln3_b[0]))
    return out.reshape(B, S, D)
```

```python
import functools

import numpy as np
import jax
import jax.numpy as jnp
from jax import lax
from jax.experimental import pallas as pl
from jax.experimental.pallas import tpu as pltpu

F32 = jnp.float32
BF16 = jnp.bfloat16
I32 = jnp.int32

D_MODEL = 2048
MLSTM_HEADS = 4
MLSTM_DV = 256
MLSTM_DQK = 128
MLSTM_QK = MLSTM_HEADS * MLSTM_DQK
MLSTM_WIDTH = MLSTM_HEADS * MLSTM_DV
CONV_WIDTH = 4
NSA_HEAD_DIM = 128
NSA_HEADS = 8
NSA_KV_HEADS = 2
NSA_GROUP = 4
NSA_WIDTH = NSA_HEADS * NSA_HEAD_DIM
CMP_BLOCK = 32
CMP_STRIDE = 16
CMP_HIDDEN = 256
SLC_BLOCK = 64
SLC_TOPN = 16
WINDOW = 512
Q_BLOCK = 128
XA_HEADS = 4
XA_HEAD_DIM = 512
N_EXPERTS = 64
TOP_K = 6
D_EXPERT = 1408
D_SHARED = 2816
ROUTED_SCALE = 2.446
ROPE_THETA = 10000.0
LN_EPS = 1e-5
DEPTH = 1
DN_ALPHA = (2.0 * DEPTH) ** 0.25

U_MLSTM = 0
U_NSA = 3072
U_COLS = 5632
GATE_COLS = 128

LANE = 128
SUBLANE = 8
NEG = -1e30
MLSTM_L = 256
VMEM_LIMIT = 56 * 1024 * 1024
EXPERT_BM = 256


def _cparams(sem):
    return pltpu.CompilerParams(dimension_semantics=sem, vmem_limit_bytes=VMEM_LIMIT)


def _dot(a, b):
    return jnp.dot(a, b, preferred_element_type=F32)


def _dot_nt(a, b):
    return lax.dot_general(a, b, (((1,), (1,)), ((), ())), preferred_element_type=F32)


def _layer_norm(z, g, b):
    mu = jnp.mean(z, -1, keepdims=True)
    zc = z - mu
    var = jnp.mean(zc * zc, -1, keepdims=True)
    return zc * lax.rsqrt(var + LN_EPS) * g + b


def _split3(x):
    hi = x.astype(BF16)
    r = x - hi.astype(F32)
    mid = r.astype(BF16)
    lo = (r - mid.astype(F32)).astype(BF16)
    return hi, mid, lo


def _split2(x):
    hi = x.astype(BF16)
    lo = (x - hi.astype(F32)).astype(BF16)
    return hi, lo


def _ln_inproj_kernel(x_ref, g_ref, b_ref, w_ref, wg_ref, h_ref, u_ref, ug_ref, hb_ref):
    @pl.when(pl.program_id(1) == 0)
    def _():
        hn = _layer_norm(x_ref[...], g_ref[...], b_ref[...])
        h_ref[...] = hn
        hb_ref[...] = hn.astype(BF16)
        ug_ref[...] = _dot(hb_ref[...], wg_ref[...])

    u_ref[...] = _dot(hb_ref[...], w_ref[...]).astype(u_ref.dtype)


def _ln_inproj(x2, g, b, w, w_gate):
    T = x2.shape[0]
    tm, tn = 1024, 512
    return pl.pallas_call(
        _ln_inproj_kernel,
        grid=(T // tm, U_COLS // tn),
        in_specs=[pl.BlockSpec((tm, D_MODEL), lambda i, j: (i, 0)),
                  pl.BlockSpec((1, D_MODEL), lambda i, j: (0, 0)),
                  pl.BlockSpec((1, D_MODEL), lambda i, j: (0, 0)),
                  pl.BlockSpec((D_MODEL, tn), lambda i, j: (0, j)),
                  pl.BlockSpec((D_MODEL, GATE_COLS), lambda i, j: (0, 0))],
        out_specs=[pl.BlockSpec((tm, D_MODEL), lambda i, j: (i, 0)),
                   pl.BlockSpec((tm, tn), lambda i, j: (i, j)),
                   pl.BlockSpec((tm, GATE_COLS), lambda i, j: (i, 0))],
        out_shape=[jax.ShapeDtypeStruct((T, D_MODEL), F32),
                   jax.ShapeDtypeStruct((T, U_COLS), BF16),
                   jax.ShapeDtypeStruct((T, GATE_COLS), F32)],
        scratch_shapes=[pltpu.VMEM((tm, D_MODEL), BF16)],
        compiler_params=_cparams(("parallel", "arbitrary")),
        name="ln_inproj",
    )(x2, g, b, w, w_gate)


def _log_sigmoid(x):
    return jnp.minimum(x, 0.0) - jnp.log1p(jnp.exp(-jnp.abs(x)))


def _mlstm_kernel(qk_ref, v_ref, o_ref, gt_ref, cw_ref, cb_ref, gb_ref, ng_ref, out_ref,
                  prev_ref, c_ref, n_ref, m_ref):
    L = MLSTM_L

    @pl.when(pl.program_id(1) == 0)
    def _():
        prev_ref[...] = jnp.zeros_like(prev_ref)
        c_ref[...] = jnp.zeros_like(c_ref)
        n_ref[...] = jnp.zeros_like(n_ref)
        m_ref[...] = jnp.zeros_like(m_ref)

    x = qk_ref[...].astype(F32)
    prev = prev_ref[...]
    row = lax.broadcasted_iota(I32, (L, 1), 0)
    cw = cw_ref[...]
    y = cb_ref[...] + cw[CONV_WIDTH - 1:CONV_WIDTH, :] * x
    for j in range(1, CONV_WIDTH):
        shifted = jnp.where(row < j, pltpu.roll(prev, j, 0), pltpu.roll(x, j, 0))
        y = y + cw[CONV_WIDTH - 1 - j:CONV_WIDTH - j, :] * shifted
    prev_ref[...] = x
    qk = y * jax.nn.sigmoid(y)

    gpre = gt_ref[...] + gb_ref[...]
    gpre_t = gpre.T
    r_i = lax.broadcasted_iota(I32, (L, L), 0)
    c_i = lax.broadcasted_iota(I32, (L, L), 1)
    causal = r_i >= c_i
    tril = jnp.where(causal, 1.0, 0.0).astype(BF16)
    triu = jnp.where(r_i <= c_i, 1.0, 0.0).astype(BF16)
    lf = _log_sigmoid(gpre)
    lf_t = _log_sigmoid(gpre_t)
    b_cols = sum(_dot(tril, part) for part in _split3(lf))
    b_rows = sum(_dot(part, triu) for part in _split3(lf_t))

    for h in range(MLSTM_HEADS):
        i_col = gpre[:, h:h + 1]
        i_row = gpre_t[h:h + 1, :]
        b_col = b_cols[:, MLSTM_HEADS + h:MLSTM_HEADS + h + 1]
        b_row = b_rows[MLSTM_HEADS + h:MLSTM_HEADS + h + 1, :]
        m_prev = m_ref[h:h + 1, 0:1]
        n_prev = n_ref[h:h + 1, :]
        c_prev = c_ref[h]

        q = qk[:, h * MLSTM_DQK:(h + 1) * MLSTM_DQK]
        k = qk[:, MLSTM_QK + h * MLSTM_DQK:MLSTM_QK + (h + 1) * MLSTM_DQK] * (MLSTM_DQK ** -0.5)
        v = v_ref[:, h * MLSTM_DV:(h + 1) * MLSTM_DV].astype(BF16)
        qb = q.astype(BF16)

        dmat = jnp.where(causal, b_col - b_row + i_row, -jnp.inf)
        inter = b_col + m_prev
        m_t = jnp.maximum(inter, jnp.max(dmat, -1, keepdims=True))
        s = _dot_nt(qb, k.astype(BF16)) * jnp.exp(dmat - m_t)
        a_inter = jnp.exp(inter - m_t)
        num = _dot(s.astype(BF16), v) + a_inter * _dot(qb, c_prev.astype(BF16))
        den = jnp.sum(s, -1, keepdims=True) + a_inter * jnp.sum(q * n_prev, -1, keepdims=True)
        hc = num / jnp.maximum(jnp.abs(den), jnp.exp(-m_t))

        mu = jnp.mean(hc, -1, keepdims=True)
        hcc = hc - mu
        var = jnp.mean(hcc * hcc, -1, keepdims=True)
        hn = hcc * lax.rsqrt(var + LN_EPS) * ng_ref[:, h * MLSTM_DV:(h + 1) * MLSTM_DV]
        og = jax.nn.sigmoid(o_ref[:, h * MLSTM_DV:(h + 1) * MLSTM_DV].astype(F32))
        out_ref[:, h * MLSTM_DV:(h + 1) * MLSTM_DV] = (og * hn).astype(out_ref.dtype)

        b_last = b_col[L - 1:L, :]
        g_col = b_last - b_col + i_col
        m_new = jnp.maximum(b_last + m_prev, jnp.max(g_col, 0, keepdims=True))
        kw = k * jnp.exp(g_col - m_new)
        decay = jnp.exp(b_last + m_prev - m_new)
        c_ref[h] = decay * c_prev + _dot(kw.T.astype(BF16), v)
        n_ref[h:h + 1, :] = decay * n_prev + jnp.sum(kw, 0, keepdims=True)
        m_ref[h:h + 1, :] = jnp.broadcast_to(m_new, (1, LANE))


def _mlstm(u, ug, conv_w, conv_b, gate_b, norm_g, B, S):
    T = B * S
    L = MLSTM_L
    nc = S // L
    row = lambda b, c: b * nc + c
    return pl.pallas_call(
        _mlstm_kernel,
        grid=(B, nc),
        in_specs=[pl.BlockSpec((L, 2 * MLSTM_QK), lambda b, c: (row(b, c), 0)),
                  pl.BlockSpec((L, MLSTM_WIDTH), lambda b, c: (row(b, c), 1)),
                  pl.BlockSpec((L, MLSTM_WIDTH), lambda b, c: (row(b, c), 2)),
                  pl.BlockSpec((L, GATE_COLS), lambda b, c: (row(b, c), 0)),
                  pl.BlockSpec((CONV_WIDTH, 2 * MLSTM_QK), lambda b, c: (0, 0)),
                  pl.BlockSpec((1, 2 * MLSTM_QK), lambda b, c: (0, 0)),
                  pl.BlockSpec((1, LANE), lambda b, c: (0, 0)),
                  pl.BlockSpec((1, MLSTM_WIDTH), lambda b, c: (0, 0))],
        out_specs=pl.BlockSpec((L, MLSTM_WIDTH), lambda b, c: (row(b, c), 0)),
        out_shape=jax.ShapeDtypeStruct((T, MLSTM_WIDTH), BF16),
        scratch_shapes=[pltpu.VMEM((L, 2 * MLSTM_QK), F32),
                        pltpu.VMEM((MLSTM_HEADS, MLSTM_DQK, MLSTM_DV), F32),
                        pltpu.VMEM((8, MLSTM_DQK), F32),
                        pltpu.VMEM((8, LANE), F32)],
        compiler_params=_cparams(("parallel", "arbitrary")),
        name="mlstm",
    )(u, u, u, ug, conv_w, conv_b, gate_b, norm_g)


LOG2E = 1.4426950408889634
SLC_CHUNK = 1024
SLC_CHUNK_BLOCKS = SLC_CHUNK // SLC_BLOCK
VT_EXTRA = 16
N_FORCED = 3
CMP_ROW_STEP = 128


def _nsa_prep_kernel(q_ref, kv0_ref, kv1_ref, kv2_ref, pos_ref, invf_ref, sgn_ref,
                     qt_ref, kc_ref, vc_ref, ks_ref, vst_ref, kw_ref, vwt_ref):
    ang = pos_ref[...].astype(F32) * invf_ref[...]
    cos = jnp.cos(ang)
    sin = jnp.sin(ang) * sgn_ref[...]

    def rope(x):
        return x * cos + pltpu.roll(x, NSA_HEAD_DIM // 2, 1) * sin

    scale = NSA_HEAD_DIM ** -0.5 * LOG2E
    ts = q_ref.shape[0]
    tok = pl.program_id(0) * ts + lax.broadcasted_iota(I32, (ts, NSA_HEAD_DIM), 0)
    blk_lane = (tok // SLC_BLOCK) % SLC_CHUNK_BLOCKS
    blk_onehot = jnp.where(lax.broadcasted_iota(I32, (ts, NSA_HEAD_DIM), 1) == blk_lane, 1.0, 0.0).astype(BF16)
    ones_rows = jnp.where(lax.broadcasted_iota(I32, (VT_EXTRA, ts), 0) == 0, 1.0, 0.0).astype(BF16)
    head = lambda ref, c: ref[:, c:c + NSA_HEAD_DIM].astype(F32)
    for g in range(NSA_KV_HEADS):
        for h in range(NSA_GROUP):
            qt_ref[g, h] = (rope(head(q_ref, (g * NSA_GROUP + h) * NSA_HEAD_DIM)) * scale).T.astype(BF16)
        c0 = g * NSA_HEAD_DIM
        c1 = NSA_KV_HEADS * NSA_HEAD_DIM + g * NSA_HEAD_DIM
        kc_ref[g] = rope(head(kv0_ref, c0)).astype(BF16)
        vc_ref[g] = kv0_ref[:, c1:c1 + NSA_HEAD_DIM].astype(BF16)
        ks_ref[g, :, 0:NSA_HEAD_DIM] = rope(head(kv1_ref, c0)).astype(BF16)
        ks_ref[g, :, NSA_HEAD_DIM:2 * NSA_HEAD_DIM] = blk_onehot
        vst_ref[g, 0:NSA_HEAD_DIM, :] = head(kv1_ref, c1).T.astype(BF16)
        vst_ref[g, NSA_HEAD_DIM:NSA_HEAD_DIM + VT_EXTRA, :] = ones_rows
        kw_ref[g] = rope(head(kv2_ref, c0)).astype(BF16)
        vwt_ref[g] = head(kv2_ref, c1).T.astype(BF16)


def _nsa_prep(u, pos_col, invf2, sgn, T):
    ts = 512
    kvw = 2 * NSA_KV_HEADS * NSA_HEAD_DIM
    kv_spec = lambda n: pl.BlockSpec((ts, kvw), lambda i: (i, (U_NSA + NSA_WIDTH) // kvw + n))
    row_out = pl.BlockSpec((NSA_KV_HEADS, ts, NSA_HEAD_DIM), lambda i: (0, i, 0))
    row_shape = jax.ShapeDtypeStruct((NSA_KV_HEADS, T, NSA_HEAD_DIM), BF16)
    col_out = pl.BlockSpec((NSA_KV_HEADS, NSA_HEAD_DIM, ts), lambda i: (0, 0, i))
    col_shape = jax.ShapeDtypeStruct((NSA_KV_HEADS, NSA_HEAD_DIM, T), BF16)
    return pl.pallas_call(
        _nsa_prep_kernel,
        grid=(T // ts,),
        in_specs=[pl.BlockSpec((ts, NSA_WIDTH), lambda i: (i, U_NSA // NSA_WIDTH)),
                  kv_spec(0), kv_spec(1), kv_spec(2),
                  pl.BlockSpec((ts, 1), lambda i: (i, 0)),
                  pl.BlockSpec((1, LANE), lambda i: (0, 0)),
                  pl.BlockSpec((1, LANE), lambda i: (0, 0))],
        out_specs=[pl.BlockSpec((NSA_KV_HEADS, NSA_GROUP, NSA_HEAD_DIM, ts), lambda i: (0, 0, 0, i)),
                   row_out, row_out,
                   pl.BlockSpec((NSA_KV_HEADS, ts, 2 * NSA_HEAD_DIM), lambda i: (0, i, 0)),
                   pl.BlockSpec((NSA_KV_HEADS, NSA_HEAD_DIM + VT_EXTRA, ts), lambda i: (0, 0, i)),
                   row_out, col_out],
        out_shape=[jax.ShapeDtypeStruct((NSA_KV_HEADS, NSA_GROUP, NSA_HEAD_DIM, T), BF16),
                   row_shape, row_shape,
                   jax.ShapeDtypeStruct((NSA_KV_HEADS, T, 2 * NSA_HEAD_DIM), BF16),
                   jax.ShapeDtypeStruct((NSA_KV_HEADS, NSA_HEAD_DIM + VT_EXTRA, T), BF16),
                   row_shape, col_shape],
        compiler_params=_cparams(("parallel",)),
        name="nsa_prep",
    )(u, u, u, u, pos_col, invf2, sgn)


def _gelu_tanh(x):
    return 0.5 * x * (1.0 + jnp.tanh(0.7978845608028654 * (x + 0.044715 * x * x * x)))


def _compress_kernel(r_ref, w1_ref, pos_ref, w2_ref, out_ref, *, transpose_out):
    r = r_ref[0]
    nr = r.shape[0]
    a = _dot(r, w1_ref[0])
    b = _dot(r, w1_ref[1])
    c0 = _dot(pos_ref[0], w1_ref[0]) + _dot(pos_ref[1], w1_ref[1])
    pre = a + pltpu.roll(b, nr - 1, 0) + c0[0:1, :]
    out = _dot(_gelu_tanh(pre).astype(BF16), w2_ref[...])
    row = lax.broadcasted_iota(I32, (nr, 1), 0)
    out = jnp.where(row < nr - 1, out, 0.0)
    out_ref[0] = (out.T if transpose_out else out).astype(out_ref.dtype)


def _compress(kv, w1, pos8, w2, B, S, transpose_out):
    nr = S // CMP_STRIDE
    half = CMP_STRIDE * NSA_HEAD_DIM
    r = kv.reshape(NSA_KV_HEADS, B * nr, half)
    if transpose_out:
        out_spec = pl.BlockSpec((1, NSA_HEAD_DIM, nr), lambda g, b: (g, 0, b))
        out_shape = jax.ShapeDtypeStruct((NSA_KV_HEADS, NSA_HEAD_DIM, B * nr), BF16)
    else:
        out_spec = pl.BlockSpec((1, nr, NSA_HEAD_DIM), lambda g, b: (g, b, 0))
        out_shape = jax.ShapeDtypeStruct((NSA_KV_HEADS, B * nr, NSA_HEAD_DIM), BF16)
    return pl.pallas_call(
        functools.partial(_compress_kernel, transpose_out=transpose_out),
        grid=(NSA_KV_HEADS, B),
        in_specs=[pl.BlockSpec((1, nr, half), lambda g, b: (g, b, 0)),
                  pl.BlockSpec((2, half, CMP_HIDDEN), lambda g, b: (0, 0, 0)),
                  pl.BlockSpec((2, 8, half), lambda g, b: (0, 0, 0)),
                  pl.BlockSpec((CMP_HIDDEN, NSA_HEAD_DIM), lambda g, b: (0, 0))],
        out_specs=out_spec,
        out_shape=out_shape,
        compiler_params=_cparams(("parallel", "parallel")),
        name="nsa_compress",
    )(r, w1, pos8, w2)


NSA_COLS = NSA_GROUP * Q_BLOCK


def _load_qt(q_ref):
    return jnp.concatenate([q_ref[0, h] for h in range(NSA_GROUP)], 1)


def _store_heads(o_t, out_ref):
    for h in range(NSA_GROUP):
        out_ref[:, h * NSA_HEAD_DIM:(h + 1) * NSA_HEAD_DIM] = (
            o_t[:, h * Q_BLOCK:(h + 1) * Q_BLOCK].T.astype(out_ref.dtype))


def _tile_heads(x):
    return jnp.concatenate([x] * NSA_GROUP, 1)


def _nsa_cmp_kernel(q_ref, kc_ref, vct_ref, ovt_ref, wd_ref, oc_ref, nb_ref, wdb_ref, *, n_slc, topn):
    wdb_ref[...] = wd_ref[...].astype(BF16)
    i = pl.program_id(2)
    nc = kc_ref.shape[1]
    qt = _load_qt(q_ref)
    t = i * Q_BLOCK + lax.broadcasted_iota(I32, (1, Q_BLOCK), 1)
    cur = t // SLC_BLOCK
    step = min(CMP_ROW_STEP, nc)
    ratio = SLC_BLOCK // CMP_STRIDE

    def causal_prefix(rows):
        c_idx = lax.broadcasted_iota(I32, (rows, 1), 0)
        valid = (c_idx * CMP_STRIDE + CMP_BLOCK - 1 <= t) & (c_idx < nc - 1)
        s = _dot(kc_ref[0, 0:rows, :], qt) + _tile_heads(jnp.where(valid, 0.0, NEG))
        m = jnp.max(s, 0, keepdims=True)
        p = jnp.where(m > 0.5 * NEG, jnp.exp2(s - m), 0.0)
        p = p * (1.0 / jnp.maximum(jnp.sum(p, 0, keepdims=True), 1e-30))
        _store_heads(_dot(vct_ref[0, :, 0:rows], p.astype(BF16)), oc_ref)
        psum = p[:, 0:Q_BLOCK]
        for h in range(1, NSA_GROUP):
            psum = psum + p[:, h * Q_BLOCK:(h + 1) * Q_BLOCK]
        nb_rows = rows // ratio
        imp = sum(_dot(ovt_ref[0:nb_rows, 0:rows], part) for part in _split2(psum))

        blk = lax.broadcasted_iota(I32, (nb_rows, Q_BLOCK), 0)
        blk_f = blk.astype(F32)
        causal_blk = blk <= cur
        forced = (blk == 0) | (blk == cur) | (blk == cur - 1)
        score = jnp.where(causal_blk & jnp.logical_not(forced), imp, -jnp.inf)
        sel = jnp.where(causal_blk & forced, 1.0, 0.0)
        for _ in range(topn - N_FORCED):
            mx = jnp.max(score, 0, keepdims=True)
            first = jnp.min(jnp.where(score == mx, blk_f, float(n_slc)), 0, keepdims=True)
            hit = blk_f == first
            sel = jnp.where(hit, 1.0, sel)
            score = jnp.where(hit, -jnp.inf, score)
        nb_ref[0, 0, 0:nb_rows, :] = jnp.where(causal_blk & (sel > 0.5), 0.0, NEG)
        if nb_rows < n_slc:
            nb_ref[0, 0, nb_rows:n_slc, :] = jnp.full((n_slc - nb_rows, Q_BLOCK), NEG, F32)

    need = ((i + 1) * Q_BLOCK - CMP_BLOCK) // CMP_STRIDE + 1
    for v in range(nc // step):
        @pl.when((need + step - 1) // step == v + 1)
        def _():
            causal_prefix((v + 1) * step)


def _nsa_cmp(qt, kcc, vcct, ovt, w_down, B, S):
    T = B * S
    nq = S // Q_BLOCK
    nc = S // CMP_STRIDE
    n_slc = S // SLC_BLOCK
    wd2 = w_down.reshape(-1, w_down.shape[-1])
    slab = wd2.shape[0] // (NSA_KV_HEADS * B * nq)
    assert slab * NSA_KV_HEADS * B * nq == wd2.shape[0] and slab % (2 * SUBLANE) == 0
    step = lambda g, b, i: ((g * B + b) * nq + i, 0)
    kern = functools.partial(_nsa_cmp_kernel, n_slc=n_slc, topn=min(SLC_TOPN, n_slc))
    return pl.pallas_call(
        kern,
        grid=(NSA_KV_HEADS, B, nq),
        in_specs=[pl.BlockSpec((1, NSA_GROUP, NSA_HEAD_DIM, Q_BLOCK), lambda g, b, i: (g, 0, 0, b * nq + i)),
                  pl.BlockSpec((1, nc, NSA_HEAD_DIM), lambda g, b, i: (g, b, 0)),
                  pl.BlockSpec((1, NSA_HEAD_DIM, nc), lambda g, b, i: (g, 0, b)),
                  pl.BlockSpec((n_slc, nc), lambda g, b, i: (0, 0)),
                  pl.BlockSpec((slab, wd2.shape[1]), step)],
        out_specs=[pl.BlockSpec((Q_BLOCK, NSA_GROUP * NSA_HEAD_DIM), lambda g, b, i: (b * nq + i, g)),
                   pl.BlockSpec((1, 1, n_slc, Q_BLOCK), lambda g, b, i: (g, b * nq + i, 0, 0)),
                   pl.BlockSpec((slab, wd2.shape[1]), step)],
        out_shape=[jax.ShapeDtypeStruct((T, NSA_WIDTH), BF16),
                   jax.ShapeDtypeStruct((NSA_KV_HEADS, B * nq, n_slc, Q_BLOCK), F32),
                   jax.ShapeDtypeStruct(wd2.shape, BF16)],
        compiler_params=_cparams(("parallel", "parallel", "arbitrary")),
        name="nsa_cmp_select",
    )(qt, kcc, vcct, ovt, wd2)


SLC_QB = 2
SLC_Q = SLC_QB * Q_BLOCK
SLC_COLS = NSA_GROUP * SLC_Q
SLC_WIN_BLOCKS = WINDOW // Q_BLOCK + SLC_QB


def _store_slc_heads(o_t, out_ref):
    for h in range(NSA_GROUP):
        out_ref[:, h * NSA_HEAD_DIM:(h + 1) * NSA_HEAD_DIM] = (
            o_t[:, h * SLC_Q:(h + 1) * SLC_Q].T.astype(out_ref.dtype))


def _nsa_slc_kernel(q_ref, nb_ref, ks_ref, vst_ref, wg_ref, wu_ref, *refs):
    kw_refs = refs[:SLC_WIN_BLOCKS]
    vw_refs = refs[SLC_WIN_BLOCKS:2 * SLC_WIN_BLOCKS]
    os_ref, ow_ref, wgb_ref, wub_ref, m_ref, acc_ref, s_ref = refs[2 * SLC_WIN_BLOCKS:]
    wgb_ref[...] = wg_ref[...].astype(BF16)
    wub_ref[...] = wu_ref[...].astype(BF16)
    i = pl.program_id(2)
    qt = _load_qt(q_ref)
    t = i * SLC_Q + lax.broadcasted_iota(I32, (1, SLC_Q), 1)
    m_ref[...] = jnp.full_like(m_ref, NEG)
    acc_ref[...] = jnp.zeros_like(acc_ref)
    pad = jnp.zeros((NSA_HEAD_DIM - SLC_CHUNK_BLOCKS, SLC_Q), F32)

    def scores(c, slot):
        start = pl.multiple_of(c * SLC_CHUNK, SLC_CHUNK)
        k = ks_ref[0, pl.ds(start, SLC_CHUNK), :]
        blocks = pl.ds(pl.multiple_of(c * SLC_CHUNK_BLOCKS, SLC_CHUNK_BLOCKS), SLC_CHUNK_BLOCKS)
        nb = jnp.concatenate([nb_ref[0, qb, blocks, :] for qb in range(SLC_QB)], 1)
        mask_rows = jnp.concatenate([nb, pad], 0).astype(BF16)
        rhs = jnp.concatenate([qt, _tile_heads(mask_rows)], 0)
        s_ref[slot] = _dot(k, rhs)

    def absorb(c, slot, diagonal):
        start = pl.multiple_of(c * SLC_CHUNK, SLC_CHUNK)
        vt = vst_ref[0, :, pl.ds(start, SLC_CHUNK)]
        s = s_ref[slot]
        if diagonal:
            key = start + lax.broadcasted_iota(I32, (SLC_CHUNK, 1), 0)
            s = s + _tile_heads(jnp.where(key <= t, 0.0, NEG))
        m_old = m_ref[...]
        m_new = jnp.maximum(m_old, jnp.max(s, 0, keepdims=True))
        p = jnp.exp2(s - m_new).astype(BF16)
        acc_ref[...] = jnp.exp2(m_old - m_new) * acc_ref[...] + _dot(vt, p)
        m_ref[...] = m_new

    n_full = (i * SLC_Q) // SLC_CHUNK
    scores(0, 0)

    def window_branch():
        kw = jnp.concatenate([r[0] for r in kw_refs], 0)
        ones_rows = jnp.where(lax.broadcasted_iota(I32, (VT_EXTRA, kw.shape[0]), 0) == 0, 1.0, 0.0).astype(BF16)
        vwt = jnp.concatenate([jnp.concatenate([r[0] for r in vw_refs], 1), ones_rows], 0)
        wpos = (i * SLC_QB - WINDOW // Q_BLOCK) * Q_BLOCK + lax.broadcasted_iota(I32, (kw.shape[0], 1), 0)
        in_window = (wpos <= t) & (wpos > t - WINDOW) & (wpos >= 0)
        s_w = _dot(kw, qt) + _tile_heads(jnp.where(in_window, 0.0, NEG))
        p_w = jnp.exp2(s_w - jnp.max(s_w, 0, keepdims=True)).astype(BF16)
        o_w = _dot(vwt, p_w)
        _store_slc_heads(o_w[0:NSA_HEAD_DIM, :] * (1.0 / o_w[NSA_HEAD_DIM:NSA_HEAD_DIM + 1, :]), ow_ref)

    def body(j, carry):
        scores(2 * j + 1, 1)
        absorb(2 * j, 0, False)
        scores(2 * j + 2, 0)
        absorb(2 * j + 1, 1, False)
        return carry

    lax.fori_loop(0, n_full // 2, body, 0)

    @pl.when(n_full % 2 == 1)
    def _():
        scores(n_full, 1)
        absorb(n_full - 1, 0, False)
        window_branch()
        absorb(n_full, 1, True)

    @pl.when(n_full % 2 == 0)
    def _():
        window_branch()
        absorb(n_full, 0, True)

    ok = m_ref[...] > 0.5 * NEG
    l = acc_ref[NSA_HEAD_DIM:NSA_HEAD_DIM + 1, :]
    inv = jnp.where(ok, 1.0 / jnp.where(ok, l, 1.0), 0.0)
    _store_slc_heads(acc_ref[0:NSA_HEAD_DIM, :] * inv, os_ref)


def _nsa_slc(qt, nb, ks, vst, kw, vwt, w_gate, w_up, B, S):
    T = B * S
    nq = S // SLC_Q
    n_slc = S // SLC_BLOCK
    wg2 = w_gate.reshape(-1, w_gate.shape[-1])
    wu2 = w_up.reshape(-1, w_up.shape[-1])
    slab = wg2.shape[0] // (NSA_KV_HEADS * B * nq)
    assert slab * NSA_KV_HEADS * B * nq == wg2.shape[0] and slab % (2 * SUBLANE) == 0 and wu2.shape == wg2.shape
    step = lambda g, b, i: ((g * B + b) * nq + i, 0)
    resident = dict(pipeline_mode=pl.Buffered(1))
    nblk = S // Q_BLOCK

    def win_block(c):
        return lambda g, b, i: jnp.maximum(i * SLC_QB - WINDOW // Q_BLOCK + c, 0) + b * nblk

    kw_specs = [pl.BlockSpec((1, Q_BLOCK, NSA_HEAD_DIM), lambda g, b, i, f=win_block(c): (g, f(g, b, i), 0))
                for c in range(SLC_WIN_BLOCKS)]
    vw_specs = [pl.BlockSpec((1, NSA_HEAD_DIM, Q_BLOCK), lambda g, b, i, f=win_block(c): (g, 0, f(g, b, i)))
                for c in range(SLC_WIN_BLOCKS)]
    tok_out = pl.BlockSpec((SLC_Q, NSA_GROUP * NSA_HEAD_DIM), lambda g, b, i: (b * nq + i, g))
    return pl.pallas_call(
        _nsa_slc_kernel,
        grid=(NSA_KV_HEADS, B, nq),
        in_specs=[pl.BlockSpec((1, NSA_GROUP, NSA_HEAD_DIM, SLC_Q), lambda g, b, i: (g, 0, 0, b * nq + i)),
                  pl.BlockSpec((1, SLC_QB, n_slc, Q_BLOCK), lambda g, b, i: (g, b * nq + i, 0, 0)),
                  pl.BlockSpec((1, S, 2 * NSA_HEAD_DIM), lambda g, b, i: (g, b, 0), **resident),
                  pl.BlockSpec((1, NSA_HEAD_DIM + VT_EXTRA, S), lambda g, b, i: (g, 0, b), **resident),
                  pl.BlockSpec((slab, wg2.shape[1]), step), pl.BlockSpec((slab, wg2.shape[1]), step)]
        + kw_specs + vw_specs,
        out_specs=[tok_out, tok_out,
                   pl.BlockSpec((slab, wg2.shape[1]), step), pl.BlockSpec((slab, wg2.shape[1]), step)],
        out_shape=[jax.ShapeDtypeStruct((T, NSA_WIDTH), BF16), jax.ShapeDtypeStruct((T, NSA_WIDTH), BF16),
                   jax.ShapeDtypeStruct(wg2.shape, BF16), jax.ShapeDtypeStruct(wg2.shape, BF16)],
        scratch_shapes=[pltpu.VMEM((1, SLC_COLS), F32),
                        pltpu.VMEM((NSA_HEAD_DIM + VT_EXTRA, SLC_COLS), F32),
                        pltpu.VMEM((2, SLC_CHUNK, SLC_COLS), F32)],
        compiler_params=_cparams(("parallel", "parallel", "arbitrary")),
        name="nsa_selected",
    )(qt, nb, ks, vst, wg2, wu2, *([kw] * SLC_WIN_BLOCKS), *([vwt] * SLC_WIN_BLOCKS))


def _outproj_kernel(hm_ref, oc_ref, os_ref, ow_ref, gt_ref, gb_ref, ge_ref, h_ref, w_ref, g_ref, b_ref,
                    out_ref):
    gates = jax.nn.sigmoid(gt_ref[...] + gb_ref[...]).astype(BF16)
    gx = _dot(gates, ge_ref[...])
    hn = (gx[:, 0:NSA_WIDTH] * oc_ref[...].astype(F32)
          + gx[:, NSA_WIDTH:2 * NSA_WIDTH] * os_ref[...].astype(F32)
          + gx[:, 2 * NSA_WIDTH:3 * NSA_WIDTH] * ow_ref[...].astype(F32))
    mix = (_dot(hm_ref[...], w_ref[0:MLSTM_WIDTH, :])
           + _dot(hn.astype(BF16), w_ref[MLSTM_WIDTH:MLSTM_WIDTH + NSA_WIDTH, :]))
    out_ref[...] = _layer_norm(DN_ALPHA * h_ref[...] + mix, g_ref[...], b_ref[...])


def _outproj(hm, oc, os_, ow, ug, gate_b, gate_expand, h, w_out, g, b):
    T = h.shape[0]
    tm = 256
    row = lambda i: (i, 0)
    fixed = lambda i: (0, 0)
    return pl.pallas_call(
        _outproj_kernel,
        grid=(T // tm,),
        in_specs=[pl.BlockSpec((tm, MLSTM_WIDTH), row), pl.BlockSpec((tm, NSA_WIDTH), row),
                  pl.BlockSpec((tm, NSA_WIDTH), row), pl.BlockSpec((tm, NSA_WIDTH), row),
                  pl.BlockSpec((tm, GATE_COLS), row),
                  pl.BlockSpec((1, LANE), fixed),
                  pl.BlockSpec((LANE, 3 * NSA_WIDTH), fixed),
                  pl.BlockSpec((tm, D_MODEL), row),
                  pl.BlockSpec((D_MODEL, D_MODEL), fixed),
                  pl.BlockSpec((1, D_MODEL), fixed), pl.BlockSpec((1, D_MODEL), fixed)],
        out_specs=pl.BlockSpec((tm, D_MODEL), row),
        out_shape=jax.ShapeDtypeStruct((T, D_MODEL), F32),
        compiler_params=_cparams(("parallel",)),
        name="mixer_outproj_ln",
    )(hm, oc, os_, ow, ug, gate_b, gate_expand, h, w_out, g, b)


def _matmul_kernel(x_ref, w_ref, o_ref):
    o_ref[...] = _dot(x_ref[...].astype(BF16), w_ref[...]).astype(o_ref.dtype)


def _mem_kv(mem2, wkv):
    M = mem2.shape[0]
    N = wkv.shape[1]
    tn = 512
    return pl.pallas_call(
        _matmul_kernel,
        grid=(N // tn,),
        in_specs=[pl.BlockSpec((M, D_MODEL), lambda j: (0, 0)),
                  pl.BlockSpec((D_MODEL, tn), lambda j: (0, j))],
        out_specs=pl.BlockSpec((M, tn), lambda j: (0, j)),
        out_shape=jax.ShapeDtypeStruct((M, N), BF16),
        compiler_params=_cparams(("parallel",)),
        name="mem_kv_proj",
    )(mem2, wkv)


def _xattn_kernel(h_ref, wq_ref, kv_ref, o_ref):
    hb = h_ref[...].astype(BF16)
    for hd in range(XA_HEADS):
        c0 = hd * XA_HEAD_DIM
        q = (_dot(hb, wq_ref[:, c0:c0 + XA_HEAD_DIM]) * (XA_HEAD_DIM ** -0.5)).astype(BF16)
        s = _dot_nt(q, kv_ref[:, c0:c0 + XA_HEAD_DIM])
        m = jnp.max(s, -1, keepdims=True)
        p = jnp.exp(s - m)
        o = _dot(p.astype(BF16), kv_ref[:, D_MODEL + c0:D_MODEL + c0 + XA_HEAD_DIM])
        o_ref[:, c0:c0 + XA_HEAD_DIM] = (o / jnp.sum(p, -1, keepdims=True)).astype(o_ref.dtype)


def _xattn(h1, wq, kv, B, S):
    T = B * S
    tm = 512
    n_mem = kv.shape[0] // B
    per_b = S // tm
    return pl.pallas_call(
        _xattn_kernel,
        grid=(T // tm,),
        in_specs=[pl.BlockSpec((tm, D_MODEL), lambda i: (i, 0)),
                  pl.BlockSpec((D_MODEL, D_MODEL), lambda i: (0, 0)),
                  pl.BlockSpec((n_mem, 2 * D_MODEL), lambda i: (i // per_b, 0))],
        out_specs=pl.BlockSpec((tm, D_MODEL), lambda i: (i, 0)),
        out_shape=jax.ShapeDtypeStruct((T, D_MODEL), BF16),
        compiler_params=_cparams(("parallel",)),
        name="mem_xattn",
    )(h1, wq, kv)


def _xa_out_router_kernel(o_ref, wo_ref, h_ref, g_ref, b_ref, rw_ref, rb_ref,
                          h2_ref, idx_ref, wgt_ref):
    xa = _dot(o_ref[...], wo_ref[...])
    h2 = _layer_norm(DN_ALPHA * h_ref[...] + xa, g_ref[...], b_ref[...])
    h2_ref[...] = h2
    hi, lo = _split2(h2)
    logits = _dot(hi, rw_ref[0]) + _dot(hi, rw_ref[1]) + _dot(lo, rw_ref[0])
    scores = jax.nn.sigmoid(logits)
    lane = lax.broadcasted_iota(I32, scores.shape, 1)
    lane_f = lane.astype(F32)
    biased = jnp.where(lane < N_EXPERTS, scores + rb_ref[...], -jnp.inf)
    idx_mat = jnp.zeros(scores.shape, F32)
    w_mat = jnp.zeros(scores.shape, F32)
    for kk in range(TOP_K):
        mx = jnp.max(biased, -1, keepdims=True)
        first = jnp.min(jnp.where(biased == mx, lane_f, float(LANE)), -1, keepdims=True)
        hit = lane_f == first
        top_s = jnp.sum(jnp.where(hit, scores, 0.0), -1, keepdims=True)
        idx_mat = jnp.where(lane == kk, first, idx_mat)
        w_mat = jnp.where(lane == kk, top_s, w_mat)
        biased = jnp.where(hit, -jnp.inf, biased)
    idx_ref[...] = idx_mat.astype(I32)
    wgt_ref[...] = w_mat / jnp.sum(w_mat, -1, keepdims=True) * ROUTED_SCALE


def _xa_out_router(o, wo, h1, g, b, rw2, rb):
    T = h1.shape[0]
    tm = 256
    row = lambda i: (i, 0)
    fixed = lambda i: (0, 0)
    return pl.pallas_call(
        _xa_out_router_kernel,
        grid=(T // tm,),
        in_specs=[pl.BlockSpec((tm, D_MODEL), row),
                  pl.BlockSpec((D_MODEL, D_MODEL), fixed),
                  pl.BlockSpec((tm, D_MODEL), row),
                  pl.BlockSpec((1, D_MODEL), fixed), pl.BlockSpec((1, D_MODEL), fixed),
                  pl.BlockSpec((2, D_MODEL, LANE), lambda i: (0, 0, 0)),
                  pl.BlockSpec((1, LANE), fixed)],
        out_specs=[pl.BlockSpec((tm, D_MODEL), row), pl.BlockSpec((tm, LANE), row),
                   pl.BlockSpec((tm, LANE), row)],
        out_shape=[jax.ShapeDtypeStruct((T, D_MODEL), F32),
                   jax.ShapeDtypeStruct((T, LANE), I32),
                   jax.ShapeDtypeStruct((T, LANE), F32)],
        compiler_params=_cparams(("parallel",)),
        name="xattn_out_ln_router",
    )(o, wo, h1, g, b, rw2, rb)


ROUTE_TM = 512
ZERO_ROWS = 128


def _route_rank_kernel(idx_ref, rank_ref, cnt_ref):
    tm = idx_ref.shape[0]

    @pl.when(pl.program_id(0) == 0)
    def _():
        cnt_ref[...] = jnp.zeros_like(cnt_ref)

    idx = idx_ref[...]
    lane = lax.broadcasted_iota(I32, (tm, LANE), 1)
    hits = [lane == idx[:, kk:kk + 1] for kk in range(TOP_K)]
    onehot = sum(jnp.where(hit, 1.0, 0.0) for hit in hits)
    r_i = lax.broadcasted_iota(I32, (tm, tm), 0)
    c_i = lax.broadcasted_iota(I32, (tm, tm), 1)
    before = jnp.where(r_i > c_i, 1.0, 0.0).astype(BF16)
    rank = _dot(before, onehot.astype(BF16)) + cnt_ref[0:1, :]
    out = jnp.zeros((tm, LANE), F32)
    for kk in range(TOP_K):
        out = jnp.where(lane == kk, jnp.sum(jnp.where(hits[kk], rank, 0.0), -1, keepdims=True), out)
    rank_ref[...] = out.astype(I32)
    cnt_ref[0:1, :] = cnt_ref[0:1, :] + jnp.sum(onehot, 0, keepdims=True)


def _route_rank(top_idx):
    T = top_idx.shape[0]
    tm = ROUTE_TM
    return pl.pallas_call(
        _route_rank_kernel,
        grid=(T // tm,),
        in_specs=[pl.BlockSpec((tm, LANE), lambda i: (i, 0))],
        out_specs=[pl.BlockSpec((tm, LANE), lambda i: (i, 0)),
                   pl.BlockSpec((8, LANE), lambda i: (0, 0))],
        out_shape=[jax.ShapeDtypeStruct((T, LANE), I32), jax.ShapeDtypeStruct((8, LANE), F32)],
        compiler_params=_cparams(("arbitrary",)),
        name="moe_route_rank",
    )(top_idx)


def _load_route(i, dest_hbm, dest_smem, isem):
    cp = pltpu.make_async_copy(dest_hbm.at[i], dest_smem, isem)
    cp.start()
    cp.wait()


_FILL_SIZES = tuple(s for s in (ZERO_ROWS >> n for n in range(ZERO_ROWS.bit_length())) if s >= SUBLANE)


def _experts_kernel(be_ref, nu_ref, x_ref, wg_ref, wu_ref, wd_ref, y_ref):
    used = pl.program_id(0) < nu_ref[0]

    @pl.when(used)
    def _():
        xb = x_ref[...].astype(BF16)
        a = _dot(xb, wg_ref[0])
        act = (a * jax.nn.sigmoid(a)) * _dot(xb, wu_ref[0])
        y_ref[...] = _dot(act.astype(BF16), wd_ref[0])

    @pl.when(jnp.logical_not(used))
    def _():
        y_ref[...] = jnp.zeros_like(y_ref)


def _experts(block_e, n_used, xs, wg, wu, wd):
    bm = EXPERT_BM
    P = xs.shape[0]
    rowmap = lambda j, be, nu: (jnp.minimum(j, nu[0] - 1), 0)
    wmap = lambda j, be, nu: (be[j], 0, 0)
    return pl.pallas_call(
        _experts_kernel,
        grid_spec=pltpu.PrefetchScalarGridSpec(
            num_scalar_prefetch=2,
            grid=(P // bm,),
            in_specs=[pl.BlockSpec((bm, D_MODEL), rowmap),
                      pl.BlockSpec((1, D_MODEL, D_EXPERT), wmap),
                      pl.BlockSpec((1, D_MODEL, D_EXPERT), wmap),
                      pl.BlockSpec((1, D_EXPERT, D_MODEL), wmap)],
            out_specs=pl.BlockSpec((bm, D_MODEL), lambda j, be, nu: (j, 0))),
        out_shape=jax.ShapeDtypeStruct((P, D_MODEL), F32),
        compiler_params=_cparams(("arbitrary",)),
        name="moe_experts",
    )(block_e, n_used, xs, wg, wu, wd)


SHARED_TF = 256
SHARED_STEPS = D_SHARED // SHARED_TF
DISPATCH_SHARE = -(-ROUTE_TM // (SHARED_STEPS * SUBLANE)) * SUBLANE
DISPATCH_LAST = ROUTE_TM - DISPATCH_SHARE * (SHARED_STEPS - 1)
assert 0 < DISPATCH_LAST <= DISPATCH_SHARE


def _zero_fill(fs_ref, fn_ref, tail_ref, zero_ref, xs_ref, zsem):
    zero_ref[...] = jnp.zeros_like(zero_ref)

    def fill_copies(e, go):
        n = fn_ref[e]
        first = fs_ref[e]
        lead = jnp.minimum((-first) & (SUBLANE - 1), n)
        for r in range(SUBLANE - 1):
            @pl.when(r < lead)
            def _():
                go(pltpu.make_async_copy(zero_ref.at[pl.ds(0, 1)], xs_ref.at[pl.ds(first + r, 1)], zsem))

        rest = n - lead
        off = first + lead
        for size in _FILL_SIZES:
            take = (rest & size) != 0

            @pl.when(take)
            def _():
                dst = xs_ref.at[pl.ds(pl.multiple_of(off, SUBLANE), size)]
                go(pltpu.make_async_copy(zero_ref.at[pl.ds(0, size)], dst, zsem))
            off = off + jnp.where(take, size, 0)

    def tail_copies(j, go):
        off = pl.multiple_of(tail_ref[0] + j * ZERO_ROWS, ZERO_ROWS)
        go(pltpu.make_async_copy(zero_ref, xs_ref.at[pl.ds(off, ZERO_ROWS)], zsem))

    for go in (lambda cp: cp.start(), lambda cp: cp.wait()):
        lax.fori_loop(0, N_EXPERTS, lambda e, c, go=go: (fill_copies(e, go), c)[1], 0)
        lax.fori_loop(0, tail_ref[1], lambda j, c, go=go: (tail_copies(j, go), c)[1], 0)


def _shared_ffn_kernel(fs_ref, fn_ref, tail_ref, dest_hbm, x_ref, wg_ref, wu_ref, wd_ref,
                       o_ref, xs_ref, xb_ref, dest_smem, zero_ref, isem, sem, zsem):
    i = pl.program_id(0)
    f = pl.program_id(1)

    @pl.when(f == 0)
    def _():
        _load_route(i, dest_hbm, dest_smem, isem)
        xb_ref[...] = x_ref[...].astype(BF16)
        o_ref[...] = jnp.zeros_like(o_ref)

    @pl.when((f == 0) & (i == 0))
    def _():
        _zero_fill(fs_ref, fn_ref, tail_ref, zero_ref, xs_ref, zsem)

    def send_rows(count):
        for rr in range(count):
            r = f * DISPATCH_SHARE + rr
            for kk in range(TOP_K):
                d = dest_smem[r * TOP_K + kk]
                pltpu.make_async_copy(x_ref.at[pl.ds(r, 1)], xs_ref.at[pl.ds(d, 1)], sem).start()

    @pl.when(f < SHARED_STEPS - 1)
    def _():
        send_rows(DISPATCH_SHARE)

    @pl.when(f == SHARED_STEPS - 1)
    def _():
        send_rows(DISPATCH_LAST)

    xb = xb_ref[...]
    a = _dot(xb, wg_ref[...])
    act = (a * jax.nn.sigmoid(a)) * _dot(xb, wu_ref[...])
    o_ref[...] += _dot(act.astype(BF16), wd_ref[...])

    @pl.when(f == SHARED_STEPS - 1)
    def _():
        for _ in range(TOP_K):
            pltpu.make_async_copy(x_ref, xs_ref.at[pl.ds(0, ROUTE_TM)], sem).wait()


def _shared_ffn_dispatch(fill_start, fill_n, tail, dest2, h2, wg, wu, wd, P):
    T = h2.shape[0]
    tm, tf = ROUTE_TM, SHARED_TF
    return pl.pallas_call(
        _shared_ffn_kernel,
        grid_spec=pltpu.PrefetchScalarGridSpec(
            num_scalar_prefetch=3,
            grid=(T // tm, SHARED_STEPS),
            in_specs=[pl.BlockSpec(memory_space=pl.ANY),
                      pl.BlockSpec((tm, D_MODEL), lambda i, f, *_: (i, 0)),
                      pl.BlockSpec((D_MODEL, tf), lambda i, f, *_: (0, f)),
                      pl.BlockSpec((D_MODEL, tf), lambda i, f, *_: (0, f)),
                      pl.BlockSpec((tf, D_MODEL), lambda i, f, *_: (f, 0))],
            out_specs=[pl.BlockSpec((tm, D_MODEL), lambda i, f, *_: (i, 0)),
                       pl.BlockSpec(memory_space=pl.ANY)],
            scratch_shapes=[pltpu.VMEM((tm, D_MODEL), BF16),
                            pltpu.SMEM((tm * TOP_K,), I32),
                            pltpu.VMEM((ZERO_ROWS, D_MODEL), F32),
                            pltpu.SemaphoreType.DMA, pltpu.SemaphoreType.DMA, pltpu.SemaphoreType.DMA]),
        out_shape=[jax.ShapeDtypeStruct((T, D_MODEL), F32),
                   jax.ShapeDtypeStruct((P, D_MODEL), F32)],
        compiler_params=_cparams(("arbitrary", "arbitrary")),
        name="shared_ffn_dispatch",
    )(fill_start, fill_n, tail, dest2, h2, wg, wu, wd)


COMBINE_SUB = 128
COMBINE_NSUB = ROUTE_TM // COMBINE_SUB


def _combine_kernel(dest_hbm, ys_hbm, w_ref, sh_ref, h_ref, g_ref, b_ref, out_ref,
                    dest_smem, gbuf, isem, sem):
    i = pl.program_id(0)
    _load_route(i, dest_hbm, dest_smem, isem)

    def gather(sub):
        slot = sub % 2

        def issue(r, carry):
            for kk in range(TOP_K):
                d = dest_smem[(sub * COMBINE_SUB + r) * TOP_K + kk]
                pltpu.make_async_copy(ys_hbm.at[pl.ds(d, 1)], gbuf.at[slot, kk, pl.ds(r, 1)],
                                      sem.at[slot]).start()
            return carry

        lax.fori_loop(0, COMBINE_SUB, issue, 0, unroll=True)

    gather(0)
    for sub in range(COMBINE_NSUB):
        slot = sub % 2
        if sub + 1 < COMBINE_NSUB:
            gather(sub + 1)
        for kk in range(TOP_K):
            pltpu.make_async_copy(ys_hbm.at[pl.ds(0, COMBINE_SUB)], gbuf.at[slot, kk], sem.at[slot]).wait()
        rs = slice(sub * COMBINE_SUB, (sub + 1) * COMBINE_SUB)
        w = w_ref[rs, :]
        routed = w[:, 0:1] * gbuf[slot, 0]
        for kk in range(1, TOP_K):
            routed = routed + w[:, kk:kk + 1] * gbuf[slot, kk]
        z = DN_ALPHA * h_ref[rs, :] + (routed + sh_ref[rs, :])
        out_ref[rs, :] = _layer_norm(z, g_ref[...], b_ref[...])


def _combine(dest2, ys, top_w, sh, h2, g, b):
    T = h2.shape[0]
    tm = ROUTE_TM
    row = lambda i: (i, 0)
    fixed = lambda i: (0, 0)
    return pl.pallas_call(
        _combine_kernel,
        grid=(T // tm,),
        in_specs=[pl.BlockSpec(memory_space=pl.ANY), pl.BlockSpec(memory_space=pl.ANY),
                  pl.BlockSpec((tm, LANE), row), pl.BlockSpec((tm, D_MODEL), row),
                  pl.BlockSpec((tm, D_MODEL), row),
                  pl.BlockSpec((1, D_MODEL), fixed), pl.BlockSpec((1, D_MODEL), fixed)],
        out_specs=pl.BlockSpec((tm, D_MODEL), row),
        scratch_shapes=[pltpu.SMEM((tm * TOP_K,), I32),
                        pltpu.VMEM((2, TOP_K, COMBINE_SUB, D_MODEL), F32),
                        pltpu.SemaphoreType.DMA, pltpu.SemaphoreType.DMA((2,))],
        out_shape=jax.ShapeDtypeStruct((T, D_MODEL), F32),
        compiler_params=_cparams(("arbitrary",)),
        name="moe_combine_ln",
    )(dest2, ys, top_w, sh, h2, g, b)


def _route_plan(counts, T):
    bm = EXPERT_BM
    padded = (counts + bm - 1) // bm * bm
    pad_end = jnp.cumsum(padded)
    pad_start = pad_end - padded
    n_blocks = T * TOP_K // bm + N_EXPERTS
    first_row = jnp.arange(n_blocks, dtype=I32) * bm
    block_e = jnp.minimum(jnp.sum((pad_end[None, :] <= first_row[:, None]).astype(I32), 1), N_EXPERTS - 1)
    n_used = (pad_end[-1:] // bm).astype(I32)
    tail = jnp.concatenate([pad_end[-1:], (n_blocks - n_used) * (bm // ZERO_ROWS)]).astype(I32)
    return (pad_start.astype(I32), (pad_start + counts).astype(I32), (padded - counts).astype(I32), tail,
            block_e, n_used, n_blocks * bm)


def _overlap_matrix(S):
    n_cmp_rows = S // CMP_STRIDE
    n_slc = S // SLC_BLOCK
    c_lo = np.arange(n_cmp_rows)[:, None] * CMP_STRIDE
    j_lo = np.arange(n_slc)[None, :] * SLC_BLOCK
    ov = (c_lo <= j_lo + SLC_BLOCK - 1) & (c_lo + CMP_BLOCK - 1 >= j_lo)
    return jnp.asarray(ov.T.astype(np.float32), dtype=BF16)


def _gate_expand_matrix():
    ge = np.zeros((LANE, 3 * NSA_WIDTH), np.float32)
    for hh in range(NSA_HEADS):
        for br in range(3):
            ge[8 + hh * 3 + br, br * NSA_WIDTH + hh * NSA_HEAD_DIM:br * NSA_WIDTH + (hh + 1) * NSA_HEAD_DIM] = 1.0
    return jnp.asarray(ge, dtype=BF16)


def kernel(x, mem, positions, ln0_g, ln0_b, w_in, conv_w, conv_b, igate_b, fgate_b, mlstm_norm_g, cmp_pos, cmp_w1k, cmp_w2k, cmp_w1v, cmp_w2v, nsa_gate_b, w_out, ln1_g, ln1_b, xa_wq, xa_wk, xa_wv, xa_wo, ln2_g, ln2_b, router_w, router_bias, moe_w_gate, moe_w_up, moe_w_down, sh_w_gate, sh_w_up, sh_w_down, ln3_g, ln3_b):
    B, S, D = x.shape
    T = B * S
    assert D == D_MODEL and w_in.shape[0] == DEPTH == 1
    assert S % MLSTM_L == 0 and S % SLC_CHUNK == 0 and T % ROUTE_TM == 0 and (T * TOP_K) % EXPERT_BM == 0
    row = lambda a: a.reshape(1, -1)

    w = w_in[0]
    w_r = jnp.concatenate([w[:, :3072], w[:, 3080:5640]], 1).astype(BF16)
    w_g = jnp.concatenate([w[:, 3072:3080], w[:, 5640:5664],
                           jnp.zeros((D, GATE_COLS - 2 * MLSTM_HEADS - 3 * NSA_HEADS), F32)], 1).astype(BF16)
    gate_b = jnp.concatenate([igate_b[0], fgate_b[0], nsa_gate_b[0],
                              jnp.zeros((LANE - 2 * MLSTM_HEADS - 3 * NSA_HEADS,), F32)]).reshape(1, LANE)
    half = NSA_HEAD_DIM // 2
    inv_freq = ROPE_THETA ** (-jnp.arange(half, dtype=F32) / half)
    invf2 = jnp.concatenate([inv_freq, inv_freq]).reshape(1, LANE)
    sgn = jnp.concatenate([-jnp.ones((half,), F32), jnp.ones((half,), F32)]).reshape(1, LANE)
    pos8 = jnp.zeros((2, 8, CMP_STRIDE * NSA_HEAD_DIM), F32).at[:, 0, :].set(
        cmp_pos[0].reshape(2, CMP_STRIDE * NSA_HEAD_DIM)).astype(BF16)
    w1k = cmp_w1k[0].reshape(2, CMP_STRIDE * NSA_HEAD_DIM, CMP_HIDDEN).astype(BF16)
    w1v = cmp_w1v[0].reshape(2, CMP_STRIDE * NSA_HEAD_DIM, CMP_HIDDEN).astype(BF16)
    rw = jnp.pad(router_w[0], ((0, 0), (0, LANE - N_EXPERTS)))
    rw_hi = rw.astype(BF16)
    rw2 = jnp.stack([rw_hi, (rw - rw_hi.astype(F32)).astype(BF16)])
    rb = jnp.pad(router_bias[0], (0, LANE - N_EXPERTS)).reshape(1, LANE)

    h, u, ug = _ln_inproj(x.reshape(T, D), row(ln0_g), row(ln0_b), w_r, w_g)
    hm = _mlstm(u, ug, conv_w[0], row(conv_b[0]), gate_b, row(mlstm_norm_g[0]), B, S)
    qt, kc, vc, ks, vst, kw, vwt = _nsa_prep(u, positions.reshape(T, 1), invf2, sgn, T)
    kcc = _compress(kc, w1k, pos8, cmp_w2k[0].astype(BF16), B, S, False)
    vcct = _compress(vc, w1v, pos8, cmp_w2v[0].astype(BF16), B, S, True)
    oc, nb, wd_b = _nsa_cmp(qt, kcc, vcct, _overlap_matrix(S), moe_w_down[0], B, S)
    os_, ow, wg_b, wu_b = _nsa_slc(qt, nb, ks, vst, kw, vwt, moe_w_gate[0], moe_w_up[0], B, S)
    h1 = _outproj(hm, oc, os_, ow, ug, gate_b, _gate_expand_matrix(), h, w_out[0].astype(BF16),
                  row(ln1_g[0]), row(ln1_b[0]))

    wkv = jnp.concatenate([xa_wk[0], xa_wv[0]], 1).astype(BF16)
    kv = _mem_kv(mem.reshape(-1, D), wkv)
    xo = _xattn(h1, xa_wq[0].astype(BF16), kv, B, S)
    h2, top_idx, top_w = _xa_out_router(xo, xa_wo[0].astype(BF16), h1, row(ln2_g[0]), row(ln2_b[0]), rw2, rb)

    rank, cnt = _route_rank(top_idx)
    counts = cnt[0, :N_EXPERTS].astype(I32)
    pad_start, fill_start, fill_n, tail, block_e, n_used, P = _route_plan(counts, T)
    slot_e = top_idx[:, :TOP_K, None] == jnp.arange(N_EXPERTS, dtype=I32)[None, None, :]
    dest = rank[:, :TOP_K] + jnp.sum(jnp.where(slot_e, pad_start[None, None, :], 0), -1)
    dest2 = dest.reshape(T // ROUTE_TM, ROUTE_TM * TOP_K)
    sh, xs = _shared_ffn_dispatch(fill_start, fill_n, tail, dest2, h2,
                                  sh_w_gate[0].astype(BF16), sh_w_up[0].astype(BF16),
                                  sh_w_down[0].astype(BF16), P)
    ys = _experts(block_e, n_used, xs, wg_b.reshape(moe_w_gate.shape[1:]), wu_b.reshape(moe_w_up.shape[1:]),
                  wd_b.reshape(moe_w_down.shape[1:]))
    out = _combine(dest2, ys, top_w, sh, h2, row(ln3_g[0]), row(ln3_b[0]))
    return out.reshape(B, S, D)
```
